```python
import jax
import jax.numpy as jnp
from jax import lax
import numpy as np

D_MODEL = 1024
BATCH = 2
SEQ = 16384
DEPTH = 1
DEC_BATCH = 4
DEC_SEQ = 4096
PAST_LEN = 128

M_HEADS = 4
M_DH = D_MODEL // 8
M_WIDTH = M_HEADS * M_DH
CONV_K = 5
HG_HEADS = 4
HG_DK = D_MODEL // 8
HG_DV = D_MODEL // 8
HG_KW = HG_HEADS * HG_DK
HG_VW = HG_HEADS * HG_DV
MIX_W = M_WIDTH + HG_VW
IN_W = 4 * M_WIDTH + 4 * M_HEADS + 3 * HG_KW + 2 * HG_VW
CHUNK = 64
N_GROUPS = 4
EXPERTS_PER_GROUP = 8
N_EXPERTS = N_GROUPS * EXPERTS_PER_GROUP
TOP_K = 2
EXPERT_FF = D_MODEL // 2
MOE_BLOCK = 128
NORM_EPS = 1e-6

kernel_name = 'hybrid_mlstm_hgrn2_hmoe_encoder'


def rmsnorm(x, g):
    xf = x.astype(jnp.float32)
    y = xf * lax.rsqrt(jnp.mean(xf * xf, axis=-1, keepdims=True) + NORM_EPS)
    return (y * g.astype(jnp.float32)).astype(x.dtype)


def head_rmsnorm(h, g):
    y = h * lax.rsqrt(jnp.mean(h * h, axis=-1, keepdims=True) + NORM_EPS)
    B, H, T, D = h.shape
    return y.transpose(0, 2, 1, 3).reshape(B, T, H * D) * g.astype(jnp.float32)


def to_heads(u, n_heads):
    B, T, C = u.shape
    return u.reshape(B, T, n_heads, C // n_heads).transpose(0, 2, 1, 3)


def flip_t(u):
    return jnp.flip(u, axis=2)


def to_chunks(u):
    B, H, T = u.shape[:3]
    u = u.reshape((B, H, T // CHUNK, CHUNK) + u.shape[3:])
    return jnp.moveaxis(u, 2, 0)


def from_chunks(u):
    u = jnp.moveaxis(u, 0, 2)
    B, H, NC, L = u.shape[:4]
    return u.reshape((B, H, NC * L) + u.shape[4:])


def split_columns(proj):
    sizes = [M_WIDTH] * 4 + [M_HEADS] * 4 + [HG_KW] * 3 + [HG_VW] * 2
    parts = []
    off = 0
    for s in sizes:
        parts.append(proj[..., off:off + s])
        off += s
    return parts


def centred_dwconv(u, w, b):
    T = u.shape[1]
    pad = CONV_K // 2
    up = jnp.pad(u, ((0, 0), (pad, pad), (0, 0)))
    out = up[:, 0:T, :] * w[0] + b
    for j in range(1, CONV_K):
        out = out + up[:, j:j + T, :] * w[j]
    return out


def mlstm_chunkwise(q, k, v, log_i, log_f):
    B, H, T, D = q.shape
    mask = jnp.tril(jnp.ones((CHUNK, CHUNK), dtype=bool))

    def step(carry, inp):
        C, n, m = carry
        qc, kc, vc, ic, fc = inp
        b = jnp.cumsum(fc, axis=-1)
        log_d = jnp.where(mask, b[..., :, None] - b[..., None, :] + ic[..., None, :], -jnp.inf)
        m_inter = b + m[..., None]
        m_t = jnp.maximum(m_inter, jnp.max(log_d, axis=-1))
        scores = jnp.einsum('bhtd,bhsd->bhts', qc, kc) * jnp.exp(log_d - m_t[..., None])
        inter_scale = jnp.exp(m_inter - m_t)
        num = jnp.einsum('bhts,bhse->bhte', scores, vc) + inter_scale[..., None] * jnp.einsum('bhtd,bhde->bhte', qc, C)
        den = jnp.sum(scores, axis=-1) + inter_scale * jnp.einsum('bhtd,bhd->bht', qc, n)
        h = num / jnp.maximum(jnp.abs(den), jnp.exp(-m_t))[..., None]
        b_last = b[..., -1]
        log_w = b_last[..., None] - b + ic
        m_new = jnp.maximum(b_last + m, jnp.max(log_w, axis=-1))
        w = jnp.exp(log_w - m_new[..., None])
        decay = jnp.exp(b_last + m - m_new)
        C_new = decay[..., None, None] * C + jnp.einsum('bhs,bhsd,bhse->bhde', w, kc, vc)
        n_new = decay[..., None] * n + jnp.einsum('bhs,bhsd->bhd', w, kc)
        return (C_new, n_new, m_new), h

    init = (jnp.zeros((B, H, D, D), q.dtype), jnp.zeros((B, H, D), q.dtype), jnp.zeros((B, H), q.dtype))
    xs = (to_chunks(q * (D ** -0.5)), to_chunks(k), to_chunks(v), to_chunks(log_i), to_chunks(log_f))
    _, h = lax.scan(step, init, xs)
    return from_chunks(h)


def hgrn2_chunkwise(q, k, v, log_g):
    B, H, T, DK = q.shape
    DV = v.shape[-1]
    mask = jnp.tril(jnp.ones((CHUNK, CHUNK), dtype=bool))[:, :, None]

    def step(S, inp):
        qc, kc, vc, gc = inp
        b = jnp.cumsum(gc, axis=2)
        inter = jnp.einsum('bhtd,bhde->bhte', qc * jnp.exp(b), S)
        diff = jnp.where(mask, b[:, :, :, None, :] - b[:, :, None, :, :], -jnp.inf)
        a = jnp.einsum('bhtd,bhsd,bhtsd->bhts', qc, kc, jnp.exp(diff))
        intra = jnp.einsum('bhts,bhse->bhte', a, vc)
        b_last = b[:, :, -1]
        S_new = jnp.exp(b_last)[..., None] * S + jnp.einsum('bhsd,bhse->bhde', kc * jnp.exp(b_last[:, :, None] - b), vc)
        return S_new, inter + intra

    init = jnp.zeros((B, H, DK, DV), q.dtype)
    _, o = lax.scan(step, init, (to_chunks(q), to_chunks(k), to_chunks(v), to_chunks(log_g)))
    return from_chunks(o)


def expert_dispatch(xf, expert_ids, weights, w1, w3, w2):
    N, D = xf.shape
    A = N * TOP_K
    flat_e = expert_ids.reshape(-1)
    flat_tok = jnp.arange(A, dtype=jnp.int32) // TOP_K
    order = jnp.argsort(flat_e)
    sorted_e = flat_e[order]
    counts = jnp.bincount(flat_e, length=N_EXPERTS)
    seg_start = jnp.cumsum(counts) - counts
    padded = ((counts + MOE_BLOCK - 1) // MOE_BLOCK) * MOE_BLOCK
    pad_end = jnp.cumsum(padded)
    pad_start = pad_end - padded
    dest = (pad_start[sorted_e] + jnp.arange(A, dtype=jnp.int32) - seg_start[sorted_e]).astype(jnp.int32)
    P = ((A + N_EXPERTS * (MOE_BLOCK - 1) + MOE_BLOCK - 1) // MOE_BLOCK) * MOE_BLOCK
    n_blocks = P // MOE_BLOCK
    row_tok = jnp.zeros((P,), jnp.int32).at[dest].set(flat_tok[order])
    block_start = jnp.arange(n_blocks, dtype=jnp.int32) * MOE_BLOCK
    block_e = jnp.clip(jnp.searchsorted(pad_end, block_start, side='right'), 0, N_EXPERTS - 1)

    def run_block(args):
        tok, e = args
        xb = xf[tok]
        hb = jax.nn.silu(xb @ w1[e]) * (xb @ w3[e])
        return hb @ w2[e]

    out_rows = lax.map(run_block, (row_tok.reshape(n_blocks, MOE_BLOCK), block_e)).reshape(P, D)
    dest_orig = jnp.zeros((A,), jnp.int32).at[order].set(dest)
    y_assign = out_rows[dest_orig].reshape(N, TOP_K, D)
    return jnp.einsum('nk,nkd->nd', weights, y_assign)


def hierarchical_moe(h, w_rg, b_rg, w_re, b_re, w1, w3, w2):
    B, T, D = h.shape
    N = B * T
    xf = h.reshape(N, D)
    p_group = jax.nn.softmax((xf @ w_rg + b_rg).astype(jnp.float32), axis=-1)
    g_val, g_idx = lax.top_k(p_group, 1)
    e_logits = (xf @ w_re + b_re).astype(jnp.float32).reshape(N, N_GROUPS, EXPERTS_PER_GROUP)
    e_logits = e_logits[jnp.arange(N), g_idx[:, 0]]
    p_exp = jax.nn.softmax(e_logits, axis=-1)
    e_val, e_idx = lax.top_k(p_exp, TOP_K)
    weights = g_val * e_val / jnp.sum(e_val, axis=-1, keepdims=True)
    expert_ids = (g_idx * EXPERTS_PER_GROUP + e_idx).astype(jnp.int32)
    y = expert_dispatch(xf, expert_ids, weights.astype(h.dtype), w1, w3, w2)
    return y.reshape(B, T, D)


def encoder_trunk(x, norm1, w_in, b_in, conv_w, conv_b, m_fgate_bias, m_norm, hg_lb_logits, hg_norm,
                  w_out, norm2, w_router_group, b_router_group, w_router_expert, b_router_expert,
                  w1, w3, w2, norm_f):
    f32 = jnp.float32
    lb_all = jnp.cumsum(jax.nn.softmax(hg_lb_logits.astype(f32), axis=0), axis=0)
    for l in range(DEPTH):
        h = rmsnorm(x, norm1[l])
        proj = (h @ w_in[l] + b_in[l]).astype(f32)
        (m_q, m_k, m_v, m_o, m_if, m_ib, m_ff, m_fb,
         hg_q, hg_ff, hg_fb, hg_i, hg_g) = split_columns(proj)
        qk = jax.nn.silu(centred_dwconv(jnp.concatenate([m_q, m_k], axis=-1),
                                        conv_w[l].astype(f32), conv_b[l].astype(f32)))
        q = to_heads(qk[..., :M_WIDTH], M_HEADS)
        k = to_heads(qk[..., M_WIDTH:], M_HEADS)
        v = to_heads(m_v, M_HEADS)
        fb = m_fgate_bias[l].astype(f32)
        i_f = m_if.transpose(0, 2, 1)
        i_b = m_ib.transpose(0, 2, 1)
        lf_f = jax.nn.log_sigmoid(m_ff + fb[0]).transpose(0, 2, 1)
        lf_b = jax.nn.log_sigmoid(m_fb + fb[1]).transpose(0, 2, 1)
        h_fwd = mlstm_chunkwise(q, k, v, i_f, lf_f)
        h_bwd = flip_t(mlstm_chunkwise(flip_t(q), flip_t(k), flip_t(v), flip_t(i_b), flip_t(lf_b)))
        m_out = head_rmsnorm(h_fwd + h_bwd, m_norm[l]) * jax.nn.sigmoid(m_o)
        lb = lb_all[l]
        g_f = lb + (1.0 - lb) * jax.nn.sigmoid(hg_ff)
        g_b = lb + (1.0 - lb) * jax.nn.sigmoid(hg_fb)
        hq = to_heads(jax.nn.silu(hg_q), HG_HEADS)
        hv = to_heads(hg_i, HG_HEADS)
        o_fwd = hgrn2_chunkwise(hq, to_heads(1.0 - g_f, HG_HEADS), hv, to_heads(jnp.log(g_f), HG_HEADS))
        o_bwd = flip_t(hgrn2_chunkwise(flip_t(hq), flip_t(to_heads(1.0 - g_b, HG_HEADS)), flip_t(hv),
                                       flip_t(to_heads(jnp.log(g_b), HG_HEADS))))
        hg_out = head_rmsnorm(o_fwd + o_bwd, hg_norm[l]) * jax.nn.silu(hg_g)
        mixed = jnp.concatenate([m_out, hg_out], axis=-1).astype(x.dtype) @ w_out[l]
        x = x + mixed
        h2 = rmsnorm(x, norm2[l])
        x = x + hierarchical_moe(h2, w_router_group[l], b_router_group[l], w_router_expert[l],
                                 b_router_expert[l], w1[l], w3[l], w2[l])
    return rmsnorm(x, norm_f)


def setup_inputs(seed: int = 0) -> dict:
    key = jax.random.key(seed)
    ks = jax.random.split(key, 22)

    def nrm(k, shape, scale):
        return scale * jax.random.normal(k, shape, jnp.float32)

    return {
        'x_prompt': nrm(ks[0], (BATCH, SEQ, D_MODEL), 1.0),
        'x_sample': nrm(ks[1], (DEC_BATCH, DEC_SEQ, D_MODEL), 1.0),
        'norm1': 1.0 + nrm(ks[2], (DEPTH, D_MODEL), 0.02),
        'w_in': nrm(ks[3], (DEPTH, D_MODEL, IN_W), D_MODEL ** -0.5),
        'b_in': nrm(ks[4], (DEPTH, IN_W), 0.02),
        'conv_w': nrm(ks[5], (DEPTH, CONV_K, 2 * M_WIDTH), CONV_K ** -0.5),
        'conv_b': nrm(ks[6], (DEPTH, 2 * M_WIDTH), 0.02),
        'm_fgate_bias': jnp.linspace(3.0, 6.0, M_HEADS)[None, None, :] + nrm(ks[7], (DEPTH, 2, M_HEADS), 0.1),
        'm_norm': 1.0 + nrm(ks[8], (DEPTH, M_WIDTH), 0.02),
        'hg_lb_logits': nrm(ks[9], (DEPTH + 1, HG_KW), 0.5),
        'hg_norm': 1.0 + nrm(ks[10], (DEPTH, HG_VW), 0.02),
        'w_out': nrm(ks[11], (DEPTH, MIX_W, D_MODEL), MIX_W ** -0.5),
        'norm2': 1.0 + nrm(ks[12], (DEPTH, D_MODEL), 0.02),
        'w_router_group': nrm(ks[13], (DEPTH, D_MODEL, N_GROUPS), D_MODEL ** -0.5),
        'b_router_group': nrm(ks[14], (DEPTH, N_GROUPS), 0.01),
        'w_router_expert': nrm(ks[15], (DEPTH, D_MODEL, N_EXPERTS), D_MODEL ** -0.5),
        'b_router_expert': nrm(ks[16], (DEPTH, N_EXPERTS), 0.01),
        'w1': nrm(ks[17], (DEPTH, N_EXPERTS, D_MODEL, EXPERT_FF), D_MODEL ** -0.5),
        'w3': nrm(ks[18], (DEPTH, N_EXPERTS, D_MODEL, EXPERT_FF), D_MODEL ** -0.5),
        'w2': nrm(ks[19], (DEPTH, N_EXPERTS, EXPERT_FF, D_MODEL), EXPERT_FF ** -0.5),
        'norm_f': 1.0 + nrm(ks[20], (D_MODEL,), 0.02),
    }


def reference(x_prompt, x_sample, norm1, w_in, b_in, conv_w, conv_b, m_fgate_bias, m_norm, hg_lb_logits,
              hg_norm, w_out, norm2, w_router_group, b_router_group, w_router_expert, b_router_expert,
              w1, w3, w2, norm_f):
    y_prompt = encoder_trunk(x_prompt, norm1, w_in, b_in, conv_w, conv_b, m_fgate_bias, m_norm, hg_lb_logits,
                             hg_norm, w_out, norm2, w_router_group, b_router_group, w_router_expert,
                             b_router_expert, w1, w3, w2, norm_f)
    y_sample = encoder_trunk(x_sample, norm1, w_in, b_in, conv_w, conv_b, m_fgate_bias, m_norm, hg_lb_logits,
                             hg_norm, w_out, norm2, w_router_group, b_router_group, w_router_expert,
                             b_router_expert, w1, w3, w2, norm_f)
    return (y_prompt, y_sample)
```

```python
import functools

import jax
import jax.numpy as jnp
from jax import lax
from jax.experimental import pallas as pl
from jax.experimental.pallas import tpu as pltpu

F32 = jnp.float32
BF16 = jnp.bfloat16

D_MODEL = 1024
N_HEADS = 4
D_HEAD = 128
WIDTH = N_HEADS * D_HEAD
CONV_K = 5
CONV_PAD = CONV_K // 2
N_GROUPS = 4
EXPERTS_PER_GROUP = 8
N_EXPERTS = N_GROUPS * EXPERTS_PER_GROUP
TOP_K = 2
EXPERT_FF = D_MODEL // 2
NORM_EPS = 1e-6

LANES = 128
SUBLANES = 8
CHUNK = 128
PROJ_ROWS = 256
HALO = SUBLANES
EXPERT_ROWS = 256
N_GATES = 4 * N_HEADS
VMEM_LIMIT = 56 * 1024 * 1024


def _dot(a, b):
    return jnp.dot(a, b, preferred_element_type=F32)


def _dot_nt(a, b):
    return lax.dot_general(a, b, (((1,), (1,)), ((), ())), preferred_element_type=F32)


def _dot_tn(a, b):
    return lax.dot_general(a, b, (((0,), (0,)), ((), ())), preferred_element_type=F32)


def _split3(x):
    hi = x.astype(BF16)
    r1 = x - hi.astype(F32)
    mid = r1.astype(BF16)
    lo = (r1 - mid.astype(F32)).astype(BF16)
    return hi, mid, lo


def _silu(x):
    return x * jax.nn.sigmoid(x)


def _log_sigmoid(x):
    return -(jnp.maximum(-x, 0.0) + jnp.log1p(jnp.exp(-jnp.abs(x))))


def _rms(x, gain):
    return x * lax.rsqrt(jnp.mean(x * x, axis=-1, keepdims=True) + NORM_EPS) * gain


def _in_proj_kernel(x_ref, xp_ref, xn_ref, n1_ref, wa_ref, ba_ref, wg_ref, bg_ref, wgt_ref, bgt_ref,
                    fbrow_ref, fbcol_ref, wh_ref, bh_ref, cw_ref, cb_ref, lbl_ref,
                    q_ref, k_ref, v_ref, mo_ref, gcol_ref, grow_ref, hq_ref, kf_ref, lgf_ref,
                    kb_ref, lgb_ref, hv_ref, hgg_ref, ext_ref):
    t = pl.program_id(1)
    nt = pl.num_programs(1)
    rows = x_ref.shape[1]
    gain = n1_ref[...]

    h = _rms(x_ref[0], gain).astype(BF16)
    hp = _rms(xp_ref[0], gain).astype(BF16)
    hn = _rms(xn_ref[0], gain).astype(BF16)

    pa = _dot(h, wa_ref[...]) + ba_ref[...]
    wqk = wa_ref[:, 0:2 * WIDTH]
    bqk = ba_ref[:, 0:2 * WIDTH]
    has_prev = (t > 0).astype(F32)
    has_next = (t < nt - 1).astype(F32)
    ext_ref[0:HALO, :] = (_dot(hp, wqk) + bqk) * has_prev
    ext_ref[HALO:HALO + rows, :] = pa[:, 0:2 * WIDTH]
    ext_ref[HALO + rows:2 * HALO + rows, :] = (_dot(hn, wqk) + bqk) * has_next
    acc = cb_ref[...] + ext_ref[pl.ds(HALO - CONV_PAD, rows), :] * cw_ref[0:1, :]
    for j in range(1, CONV_K):
        acc = acc + ext_ref[pl.ds(HALO - CONV_PAD + j, rows), :] * cw_ref[j:j + 1, :]
    qk = _silu(acc)
    q_ref[0] = qk[:, 0:WIDTH] * (D_HEAD ** -0.5)
    k_ref[0] = qk[:, WIDTH:2 * WIDTH]
    v_ref[0] = pa[:, 2 * WIDTH:3 * WIDTH]
    mo_ref[0] = jax.nn.sigmoid(pa[:, 3 * WIDTH:4 * WIDTH])

    gc = _dot(h, wg_ref[...]) + bg_ref[...]
    lane = lax.broadcasted_iota(jnp.int32, gc.shape, 1)
    is_f = (lane >= 2 * N_HEADS) & (lane < N_GATES)
    gcol_ref[0] = jnp.where(is_f, _log_sigmoid(gc + fbrow_ref[...]), gc)
    gr = _dot_nt(wgt_ref[...], h) + bgt_ref[...]
    sub = lax.broadcasted_iota(jnp.int32, gr.shape, 0)
    grow_ref[0] = jnp.where(sub >= 2 * N_HEADS, _log_sigmoid(gr + fbcol_ref[...]), gr)

    ph = _dot(h, wh_ref[...]) + bh_ref[...]
    lbl = lbl_ref[...]
    lmax = jnp.max(lbl, axis=0, keepdims=True)
    le = jnp.exp(lbl - lmax)
    lb = le[0:1, :] / jnp.sum(le, axis=0, keepdims=True)
    hq_ref[0] = _silu(ph[:, 0:WIDTH])
    g_f = lb + (1.0 - lb) * jax.nn.sigmoid(ph[:, WIDTH:2 * WIDTH])
    kf_ref[0] = 1.0 - g_f
    lgf_ref[0] = jnp.log(g_f)
    g_b = lb + (1.0 - lb) * jax.nn.sigmoid(ph[:, 2 * WIDTH:3 * WIDTH])
    kb_ref[0] = 1.0 - g_b
    lgb_ref[0] = jnp.log(g_b)
    hv_ref[0] = ph[:, 3 * WIDTH:4 * WIDTH]
    hgg_ref[0] = _silu(ph[:, 4 * WIDTH:5 * WIDTH])


def _in_proj(x, norm1, w_in, b_in, conv_w, conv_b, fgate_bias, lb_logits):
    B, T, D = x.shape
    rows = PROJ_ROWS
    nt = T // rows
    a_w = 4 * WIDTH
    wa = w_in[:, 0:a_w].astype(BF16)
    ba = b_in[None, 0:a_w]
    wg32 = jnp.pad(w_in[:, a_w:a_w + N_GATES], ((0, 0), (0, LANES - N_GATES)))
    bg = jnp.pad(b_in[a_w:a_w + N_GATES], (0, LANES - N_GATES))[None, :]
    wg = wg32.astype(BF16)
    wgt = w_in[:, a_w:a_w + N_GATES].T.astype(BF16)
    bgt = b_in[a_w:a_w + N_GATES][:, None]
    fb = fgate_bias.reshape(2 * N_HEADS)
    fbrow = jnp.zeros((1, LANES), F32).at[0, 2 * N_HEADS:N_GATES].set(fb)
    fbcol = jnp.zeros((N_GATES, 1), F32).at[2 * N_HEADS:N_GATES, 0].set(fb)
    wh = w_in[:, a_w + N_GATES:].astype(BF16)
    bh = b_in[None, a_w + N_GATES:]

    tiles_per_halo = rows // HALO
    n_halo = T // HALO

    def full(arr):
        nd = arr.ndim
        return pl.BlockSpec(arr.shape, lambda b, t: (0,) * nd)

    def tok(width):
        return pl.BlockSpec((1, rows, width), lambda b, t: (b, t, 0))

    in_specs = [
        tok(D),
        pl.BlockSpec((1, HALO, D), lambda b, t: (b, jnp.maximum(t * tiles_per_halo - 1, 0), 0)),
        pl.BlockSpec((1, HALO, D), lambda b, t: (b, jnp.minimum((t + 1) * tiles_per_halo, n_halo - 1), 0)),
    ]
    consts = [norm1[None, :], wa, ba, wg, bg, wgt, bgt, fbrow, fbcol, wh, bh, conv_w, conv_b[None, :], lb_logits]
    in_specs += [full(c) for c in consts]
    tok_out = jax.ShapeDtypeStruct((B, T, WIDTH), F32)
    out_shape = [tok_out, tok_out, tok_out, tok_out,
                 jax.ShapeDtypeStruct((B, T, LANES), F32),
                 jax.ShapeDtypeStruct((B, N_GATES, T), F32),
                 tok_out, tok_out, tok_out, tok_out, tok_out, tok_out, tok_out]
    out_specs = [tok(WIDTH)] * 4 + [tok(LANES), pl.BlockSpec((1, N_GATES, rows), lambda b, t: (b, 0, t))] + [tok(WIDTH)] * 7
    return pl.pallas_call(
        _in_proj_kernel,
        grid=(B, nt),
        in_specs=in_specs,
        out_specs=out_specs,
        out_shape=out_shape,
        scratch_shapes=[pltpu.VMEM((rows + 2 * HALO, 2 * WIDTH), F32)],
        compiler_params=pltpu.CompilerParams(
            dimension_semantics=("parallel", "parallel"), vmem_limit_bytes=VMEM_LIMIT),
        name="in_proj",
    )(x, x, x, *consts)


def _cumsum_rows(tri_bf, x):
    hi, mid, lo = _split3(x)
    return _dot(tri_bf, hi) + _dot(tri_bf, mid) + _dot(tri_bf, lo)


def _cumsum_lanes(x, tri_bf):
    hi, mid, lo = _split3(x)
    return _dot(hi, tri_bf) + _dot(mid, tri_bf) + _dot(lo, tri_bf)


def _mlstm_chunk(q, k, vext, i_col, b_col, i_row, b_row, seen, last, c_ref, m_ref, idx):
    m_prev = m_ref[idx][:, 0:1]
    c_prev = c_ref[idx]
    q_bf = q.astype(BF16)
    log_d = jnp.where(seen, b_col - b_row + i_row, -jnp.inf)
    m_inter = b_col + m_prev
    m_t = jnp.maximum(m_inter, jnp.max(log_d, axis=-1, keepdims=True))
    scores = (_dot_nt(q_bf, k.astype(BF16)) * jnp.exp(log_d - m_t)).astype(BF16)
    inter_scale = jnp.exp(m_inter - m_t)
    numden = _dot(scores, vext) + inter_scale * _dot(q_bf, c_prev.astype(BF16))
    num = numden[:, 0:D_HEAD]
    den = numden[:, D_HEAD:2 * D_HEAD]
    h = num / jnp.maximum(jnp.abs(den), jnp.exp(-m_t))

    b_last = b_col[last:last + 1, :]
    log_w = b_last - b_col + i_col
    m_new = jnp.maximum(b_last + m_prev, jnp.max(log_w, axis=0, keepdims=True))
    w = jnp.exp(log_w - m_new)
    decay = jnp.exp(b_last + m_prev - m_new)
    c_ref[idx] = decay * c_prev + _dot_tn((k * w).astype(BF16), vext)
    m_ref[idx] = jnp.broadcast_to(m_new, (1, LANES))
    return h


def _sibling_total(c, half, rev, sub_iota):
    L = c.shape[0]
    if half >= SUBLANES:
        n = L // (2 * half)
        c3 = c.reshape(n, 2 * half, LANES)
        r = half if rev else half - 1
        return jnp.broadcast_to(c3[:, r:r + 1, :], (n, 2 * half, LANES)).reshape(L, LANES)
    c3 = c.reshape(L // SUBLANES, SUBLANES, LANES)
    src_row = half if rev else half - 1
    y = jnp.where((sub_iota & (2 * half - 1)) == src_row, c3, 0.0)
    step = 1 if rev else -1
    span = 1
    while span < half:
        y = y + pltpu.roll(y, (step * span) % SUBLANES, 1)
        span *= 2
    y = y + pltpu.roll(y, (-step * half) % SUBLANES, 1)
    return y.reshape(L, LANES)


def _hgrn2_chunk(q, k, v_bf, lg, rev, level, diag, row_iota, sub_iota, st_ref, idx):
    L = q.shape[0]
    att = jnp.where(diag, _dot_nt(q.astype(BF16), k.astype(BF16)), 0.0)
    c = lg
    half = 1
    bit = 0
    while half < L:
        total = _sibling_total(c, half, rev, sub_iota)
        q_hat = (q * jnp.exp(c)).astype(BF16)
        k_hat = (k * jnp.exp(jnp.minimum(total - c, 0.0))).astype(BF16)
        att = jnp.where(level == bit, _dot_nt(q_hat, k_hat), att)
        second = ((row_iota & half) == 0) if rev else ((row_iota & half) != 0)
        c = c + jnp.where(second, total, 0.0)
        half *= 2
        bit += 1
    last = 0 if rev else L - 1
    st_prev = st_ref[idx]
    b_last = c[last:last + 1, :]
    o = _dot_nt((q * jnp.exp(c)).astype(BF16), st_prev.astype(BF16)) + _dot(att.astype(BF16), v_bf)
    k_dec = (k * jnp.exp(b_last - c)).astype(BF16)
    st_ref[idx] = st_prev * jnp.exp(b_last) + _dot_tn(v_bf, k_dec)
    return o


def _mixer_kernel(qf_ref, kf_ref, vf_ref, gcf_ref, grf_ref, hqf_ref, hkf_ref, hlf_ref, hvf_ref,
                  qb_ref, kb_ref, vb_ref, gcb_ref, grb_ref, hqb_ref, hkb_ref, hlb_ref, hvb_ref,
                  hf_ref, of_ref, hb_ref, ob_ref, c_ref, m_ref, st_ref):
    L = CHUNK

    @pl.when(pl.program_id(1) == 0)
    def _():
        c_ref[...] = jnp.zeros_like(c_ref)
        m_ref[...] = jnp.zeros_like(m_ref)
        st_ref[...] = jnp.zeros_like(st_ref)

    row = lax.broadcasted_iota(jnp.int32, (L, L), 0)
    col = lax.broadcasted_iota(jnp.int32, (L, L), 1)
    row_iota = lax.broadcasted_iota(jnp.int32, (L, LANES), 0)
    sub_iota = lax.broadcasted_iota(jnp.int32, (L // SUBLANES, SUBLANES, LANES), 1)
    diag = row == col
    diff = row ^ col
    high_bit = jnp.zeros((L, L), jnp.int32)
    half = 2
    while half < L:
        high_bit = high_bit + (diff >= half).astype(jnp.int32)
        half *= 2
    ones = jnp.ones((L, D_HEAD), BF16)

    dirs = (
        (0, qf_ref, kf_ref, vf_ref, gcf_ref, grf_ref, hqf_ref, hkf_ref, hlf_ref, hvf_ref, hf_ref, of_ref),
        (1, qb_ref, kb_ref, vb_ref, gcb_ref, grb_ref, hqb_ref, hkb_ref, hlb_ref, hvb_ref, hb_ref, ob_ref),
    )
    for d, q_ref, k_ref, v_ref, gc_ref, gr_ref, hq_ref, hk_ref, hl_ref, hv_ref, h_out, o_out in dirs:
        rev = d == 1
        seen = (col >= row) if rev else (col <= row)
        before = (col > row) if rev else (col < row)
        level = jnp.where(before, high_bit, -1)
        tri = seen.astype(BF16)
        tri_t = (row >= col if rev else row <= col).astype(BF16)
        last = 0 if rev else L - 1
        gc = gc_ref[0]
        gr = gr_ref[0]
        gc_cum = _cumsum_rows(tri, gc)
        gr_cum = _cumsum_lanes(gr, tri_t)
        for hd in range(N_HEADS):
            sl = slice(hd * D_HEAD, (hd + 1) * D_HEAD)
            gi = d * N_HEADS + hd
            gf = 2 * N_HEADS + gi
            idx = d * N_HEADS + hd
            vext = jnp.concatenate([v_ref[0, :, sl].astype(BF16), ones], axis=1)
            h_out[0, :, sl] = _mlstm_chunk(
                q_ref[0, :, sl], k_ref[0, :, sl], vext,
                gc[:, gi:gi + 1], gc_cum[:, gf:gf + 1], gr[gi:gi + 1, :], gr_cum[gf:gf + 1, :],
                seen, last, c_ref, m_ref, idx)
            o_out[0, :, sl] = _hgrn2_chunk(
                hq_ref[0, :, sl], hk_ref[0, :, sl], hv_ref[0, :, sl].astype(BF16), hl_ref[0, :, sl],
                rev, level, diag, row_iota, sub_iota, st_ref, idx)


def _mixer(q, k, v, gcol, grow, hq, kf, lgf, kb, lgb, hv):
    B, T, _ = q.shape
    L = CHUNK
    nc = T // L

    def fwd(width):
        return pl.BlockSpec((1, L, width), lambda b, c: (b, c, 0))

    def bwd(width):
        return pl.BlockSpec((1, L, width), lambda b, c: (b, nc - 1 - c, 0))

    grow_f = pl.BlockSpec((1, N_GATES, L), lambda b, c: (b, 0, c))
    grow_b = pl.BlockSpec((1, N_GATES, L), lambda b, c: (b, 0, nc - 1 - c))
    in_specs = ([fwd(WIDTH)] * 3 + [fwd(LANES), grow_f] + [fwd(WIDTH)] * 4
                + [bwd(WIDTH)] * 3 + [bwd(LANES), grow_b] + [bwd(WIDTH)] * 4)
    out = jax.ShapeDtypeStruct((B, T, WIDTH), F32)
    n_state = 2 * N_HEADS
    return pl.pallas_call(
        _mixer_kernel,
        grid=(B, nc),
        in_specs=in_specs,
        out_specs=[fwd(WIDTH), fwd(WIDTH), bwd(WIDTH), bwd(WIDTH)],
        out_shape=[out, out, out, out],
        scratch_shapes=[
            pltpu.VMEM((n_state, D_HEAD, 2 * D_HEAD), F32),
            pltpu.VMEM((n_state, 1, LANES), F32),
            pltpu.VMEM((n_state, D_HEAD, D_HEAD), F32),
        ],
        compiler_params=pltpu.CompilerParams(
            dimension_semantics=("parallel", "arbitrary"), vmem_limit_bytes=VMEM_LIMIT),
        name="mixer",
    )(q, k, v, gcol, grow, hq, kf, lgf, hv, q, k, v, gcol, grow, hq, kb, lgb, hv)


def _head_norm(hsum, gain):
    parts = []
    for hd in range(N_HEADS):
        hh = hsum[:, hd * D_HEAD:(hd + 1) * D_HEAD]
        parts.append(hh * lax.rsqrt(jnp.mean(hh * hh, axis=-1, keepdims=True) + NORM_EPS))
    return jnp.concatenate(parts, axis=1) * gain


def _merge_kernel(hf_ref, hb_ref, of_ref, ob_ref, mo_ref, hgg_ref, x_ref, mn_ref, hn_ref, wo_ref,
                  n2_ref, wr_ref, br_ref, x1_ref, h2_ref, route_ref):
    m_out = _head_norm(hf_ref[0] + hb_ref[0], mn_ref[...]) * mo_ref[0]
    hg_out = _head_norm(of_ref[0] + ob_ref[0], hn_ref[...]) * hgg_ref[0]
    mixed = jnp.concatenate([m_out, hg_out], axis=1).astype(BF16)
    x1 = x_ref[0] + _dot(mixed, wo_ref[...])
    x1_ref[0] = x1
    h2 = _rms(x1, n2_ref[...])
    h2_ref[0] = h2

    logits = jnp.dot(h2, wr_ref[...], preferred_element_type=F32, precision=lax.Precision.HIGHEST) + br_ref[...]
    lane = lax.broadcasted_iota(jnp.int32, logits.shape, 1)
    big = jnp.int32(LANES)
    neg = -jnp.inf
    g_log = jnp.where(lane < N_GROUPS, logits, neg)
    g_max = jnp.max(g_log, axis=-1, keepdims=True)
    g_idx = jnp.min(jnp.where(g_log == g_max, lane, big), axis=-1, keepdims=True)
    g_val = 1.0 / jnp.sum(jnp.exp(g_log - g_max), axis=-1, keepdims=True)
    e_lo = N_GROUPS + g_idx * EXPERTS_PER_GROUP
    e_log = jnp.where((lane >= e_lo) & (lane < e_lo + EXPERTS_PER_GROUP), logits, neg)
    m1 = jnp.max(e_log, axis=-1, keepdims=True)
    i1 = jnp.min(jnp.where(e_log == m1, lane, big), axis=-1, keepdims=True)
    e_log2 = jnp.where(lane == i1, neg, e_log)
    m2 = jnp.max(e_log2, axis=-1, keepdims=True)
    i2 = jnp.min(jnp.where(e_log2 == m2, lane, big), axis=-1, keepdims=True)
    r2 = jnp.exp(m2 - m1)
    w1 = g_val / (1.0 + r2)
    w2 = g_val * r2 / (1.0 + r2)
    route = jnp.where(lane == 0, (i1 - N_GROUPS).astype(F32),
                      jnp.where(lane == 1, (i2 - N_GROUPS).astype(F32),
                                jnp.where(lane == 2, w1, jnp.where(lane == 3, w2, 0.0))))
    route_ref[0] = route


def _merge(h_f, h_b, o_f, o_b, mo, hgg, x, m_norm, hg_norm, w_out, norm2, w_rg, b_rg, w_re, b_re):
    B, T, D = x.shape
    rows = PROJ_ROWS
    n_log = N_GROUPS + N_EXPERTS
    wr = jnp.pad(jnp.concatenate([w_rg, w_re], axis=1), ((0, 0), (0, LANES - n_log)))
    br = jnp.pad(jnp.concatenate([b_rg, b_re]), (0, LANES - n_log))[None, :]
    consts = [m_norm[None, :], hg_norm[None, :], w_out.astype(BF16), norm2[None, :], wr, br]

    def full(arr):
        nd = arr.ndim
        return pl.BlockSpec(arr.shape, lambda b, t: (0,) * nd)

    def tok(width):
        return pl.BlockSpec((1, rows, width), lambda b, t: (b, t, 0))

    return pl.pallas_call(
        _merge_kernel,
        grid=(B, T // rows),
        in_specs=[tok(WIDTH)] * 6 + [tok(D)] + [full(c) for c in consts],
        out_specs=[tok(D), tok(D), tok(LANES)],
        out_shape=[jax.ShapeDtypeStruct((B, T, D), F32), jax.ShapeDtypeStruct((B, T, D), F32),
                   jax.ShapeDtypeStruct((B, T, LANES), F32)],
        compiler_params=pltpu.CompilerParams(
            dimension_semantics=("parallel", "parallel"), vmem_limit_bytes=VMEM_LIMIT),
        name="merge",
    )(h_f, h_b, o_f, o_b, mo, hgg, x, *consts)


def _gather_rows(src_hbm, idx_ref, dst_ref, sem, n_rows):
    def issue(r, carry):
        pltpu.make_async_copy(src_hbm.at[pl.ds(idx_ref[0, 0, r], 1)], dst_ref.at[pl.ds(r, 1)], sem).start()
        return carry

    lax.fori_loop(0, n_rows, issue, 0)

    def drain(r, carry):
        pltpu.make_async_copy(src_hbm.at[pl.ds(0, 1)], dst_ref.at[pl.ds(r, 1)], sem).wait()
        return carry

    lax.fori_loop(0, n_rows, drain, 0)


def _expert_kernel(be_ref, nu_ref, tok_ref, h2_hbm, w1_ref, w3_ref, w2_ref, out_ref, xbuf, sem):
    i = pl.program_id(0)

    @pl.when(i < nu_ref[0])
    def _():
        _gather_rows(h2_hbm, tok_ref, xbuf, sem, EXPERT_ROWS)
        xb = xbuf[...].astype(BF16)
        hid = _silu(_dot(xb, w1_ref[0])) * _dot(xb, w3_ref[0])
        out_ref[...] = _dot(hid.astype(BF16), w2_ref[0])

    @pl.when(i >= nu_ref[0])
    def _():
        out_ref[...] = jnp.zeros_like(out_ref)


def _experts(h2, block_e, n_used, row_tok, w1, w3, w2):
    n_blocks = block_e.shape[0]
    rows = EXPERT_ROWS
    D = h2.shape[1]
    grid_spec = pltpu.PrefetchScalarGridSpec(
        num_scalar_prefetch=2,
        grid=(n_blocks,),
        in_specs=[
            pl.BlockSpec((1, 1, rows), lambda i, be, nu: (i, 0, 0), memory_space=pltpu.SMEM),
            pl.BlockSpec(memory_space=pl.ANY),
            pl.BlockSpec((1, D, EXPERT_FF), lambda i, be, nu: (be[i], 0, 0)),
            pl.BlockSpec((1, D, EXPERT_FF), lambda i, be, nu: (be[i], 0, 0)),
            pl.BlockSpec((1, EXPERT_FF, D), lambda i, be, nu: (be[i], 0, 0)),
        ],
        out_specs=pl.BlockSpec((rows, D), lambda i, be, nu: (i, 0)),
        scratch_shapes=[pltpu.VMEM((rows, D), F32), pltpu.SemaphoreType.DMA(())],
    )
    return pl.pallas_call(
        _expert_kernel,
        grid_spec=grid_spec,
        out_shape=jax.ShapeDtypeStruct((n_blocks * rows, D), F32),
        compiler_params=pltpu.CompilerParams(
            dimension_semantics=("arbitrary",), vmem_limit_bytes=VMEM_LIMIT),
        name="experts",
    )(block_e, n_used, row_tok.reshape(n_blocks, 1, rows), h2, w1, w3, w2)


def _combine_kernel(dest_ref, rows_hbm, x1_ref, route_ref, nf_ref, y_ref, buf, sem):
    n = x1_ref.shape[0]
    _gather_rows(rows_hbm, dest_ref, buf, sem, TOP_K * n)
    route = route_ref[...]
    y = x1_ref[...] + route[:, 2:3] * buf[0:n, :] + route[:, 3:4] * buf[n:2 * n, :]
    y_ref[...] = _rms(y, nf_ref[...])


def _combine(x1, route, out_rows, dest_kmajor, norm_f):
    N, D = x1.shape
    rows = PROJ_ROWS
    nt = N // rows
    return pl.pallas_call(
        _combine_kernel,
        grid=(nt,),
        in_specs=[
            pl.BlockSpec((1, 1, TOP_K * rows), lambda i: (i, 0, 0), memory_space=pltpu.SMEM),
            pl.BlockSpec(memory_space=pl.ANY),
            pl.BlockSpec((rows, D), lambda i: (i, 0)),
            pl.BlockSpec((rows, LANES), lambda i: (i, 0)),
            pl.BlockSpec((1, D), lambda i: (0, 0)),
        ],
        out_specs=pl.BlockSpec((rows, D), lambda i: (i, 0)),
        out_shape=jax.ShapeDtypeStruct((N, D), F32),
        scratch_shapes=[pltpu.VMEM((TOP_K * rows, D), F32), pltpu.SemaphoreType.DMA(())],
        compiler_params=pltpu.CompilerParams(
            dimension_semantics=("arbitrary",), vmem_limit_bytes=VMEM_LIMIT),
        name="combine",
    )(dest_kmajor, out_rows, x1, route, norm_f[None, :])


def _dispatch_plan(expert_ids):
    N = expert_ids.shape[0]
    A = N * TOP_K
    blk = EXPERT_ROWS
    flat_e = expert_ids.reshape(A)
    onehot = (flat_e[:, None] == jnp.arange(N_EXPERTS, dtype=jnp.int32)[None, :]).astype(jnp.int32)
    csum = jnp.cumsum(onehot, axis=0)
    rank = jnp.take_along_axis(csum, flat_e[:, None], axis=1)[:, 0] - 1
    counts = csum[-1]
    padded = ((counts + blk - 1) // blk) * blk
    pad_end = jnp.cumsum(padded)
    pad_start = pad_end - padded
    dest = (pad_start[flat_e] + rank).astype(jnp.int32)
    n_blocks = (A + N_EXPERTS * (blk - 1) + blk - 1) // blk
    row_tok = jnp.zeros((n_blocks * blk,), jnp.int32).at[dest].set(jnp.arange(A, dtype=jnp.int32) // TOP_K)
    block_start = jnp.arange(n_blocks, dtype=jnp.int32) * blk
    block_e = jnp.clip(jnp.searchsorted(pad_end, block_start, side="right"), 0, N_EXPERTS - 1).astype(jnp.int32)
    n_used = (pad_end[-1] // blk).astype(jnp.int32).reshape(1)
    return dest, row_tok, block_e, n_used


def _trunk(x, norm1, w_in, b_in, conv_w, conv_b, m_fgate_bias, m_norm, hg_lb_logits, hg_norm, w_out, norm2,
           w_rg, b_rg, w_re, b_re, w1_bf, w3_bf, w2_bf, norm_f):
    B, T, D = x.shape
    N = B * T
    q, k, v, mo, gcol, grow, hq, kf, lgf, kb, lgb, hv, hgg = _in_proj(
        x, norm1, w_in, b_in, conv_w, conv_b, m_fgate_bias, hg_lb_logits)
    h_f, o_f, h_b, o_b = _mixer(q, k, v, gcol, grow, hq, kf, lgf, kb, lgb, hv)
    x1, h2, route = _merge(h_f, h_b, o_f, o_b, mo, hgg, x, m_norm, hg_norm, w_out, norm2, w_rg, b_rg, w_re, b_re)
    x1 = x1.reshape(N, D)
    h2 = h2.reshape(N, D)
    route = route.reshape(N, LANES)
    expert_ids = route[:, 0:TOP_K].astype(jnp.int32)
    dest, row_tok, block_e, n_used = _dispatch_plan(expert_ids)
    out_rows = _experts(h2, block_e, n_used, row_tok, w1_bf, w3_bf, w2_bf)
    rows = PROJ_ROWS
    dest_kmajor = dest.reshape(N // rows, rows, TOP_K).transpose(0, 2, 1).reshape(N // rows, 1, TOP_K * rows)
    y = _combine(x1, route, out_rows, dest_kmajor, norm_f)
    return y.reshape(B, T, D)


def kernel(x_prompt, x_sample, norm1, w_in, b_in, conv_w, conv_b, m_fgate_bias, m_norm, hg_lb_logits, hg_norm,
           w_out, norm2, w_router_group, b_router_group, w_router_expert, b_router_expert, w1, w3, w2, norm_f):
    layer = 0
    w1_bf = w1[layer].astype(BF16)
    w3_bf = w3[layer].astype(BF16)
    w2_bf = w2[layer].astype(BF16)
    args = (norm1[layer], w_in[layer], b_in[layer], conv_w[layer], conv_b[layer], m_fgate_bias[layer],
            m_norm[layer], hg_lb_logits, hg_norm[layer], w_out[layer], norm2[layer],
            w_router_group[layer], b_router_group[layer], w_router_expert[layer], b_router_expert[layer],
            w1_bf, w3_bf, w2_bf, norm_f)
    return (_trunk(x_prompt, *args), _trunk(x_sample, *args))
```

```python
import functools

import jax
import jax.numpy as jnp
from jax import lax
from jax.experimental import pallas as pl
from jax.experimental.pallas import tpu as pltpu

F32 = jnp.float32
BF16 = jnp.bfloat16

D_MODEL = 1024
N_HEADS = 4
D_HEAD = 128
WIDTH = N_HEADS * D_HEAD
CONV_K = 5
CONV_PAD = CONV_K // 2
N_GROUPS = 4
EXPERTS_PER_GROUP = 8
N_EXPERTS = N_GROUPS * EXPERTS_PER_GROUP
TOP_K = 2
EXPERT_FF = D_MODEL // 2
NORM_EPS = 1e-6

LANES = 128
SUBLANES = 8
CHUNK = 128
PROJ_ROWS = 256
HALO = SUBLANES
EXPERT_ROWS = 256
N_GATES = 4 * N_HEADS
VMEM_LIMIT = 56 * 1024 * 1024


def _dot(a, b):
    return jnp.dot(a, b, preferred_element_type=F32)


def _dot_nt(a, b):
    return lax.dot_general(a, b, (((1,), (1,)), ((), ())), preferred_element_type=F32)


def _dot_tn(a, b):
    return lax.dot_general(a, b, (((0,), (0,)), ((), ())), preferred_element_type=F32)


def _split3(x):
    hi = x.astype(BF16)
    r1 = x - hi.astype(F32)
    mid = r1.astype(BF16)
    lo = (r1 - mid.astype(F32)).astype(BF16)
    return hi, mid, lo


def _silu(x):
    return x * jax.nn.sigmoid(x)


def _log_sigmoid(x):
    return -(jnp.maximum(-x, 0.0) + jnp.log1p(jnp.exp(-jnp.abs(x))))


def _rms(x, gain):
    return x * lax.rsqrt(jnp.mean(x * x, axis=-1, keepdims=True) + NORM_EPS) * gain


def _in_proj_kernel(x_ref, xp_ref, xn_ref, n1_ref, wa_ref, ba_ref, wg_ref, bg_ref, wgt_ref, bgt_ref,
                    fbrow_ref, fbcol_ref, wh_ref, bh_ref, cw_ref, cb_ref, lbl_ref,
                    q_ref, k_ref, v_ref, mo_ref, gcol_ref, grow_ref, hq_ref, kf_ref, lgf_ref,
                    kb_ref, lgb_ref, hv_ref, hgg_ref, ext_ref):
    t = pl.program_id(1)
    nt = pl.num_programs(1)
    rows = x_ref.shape[1]
    gain = n1_ref[...]

    h = _rms(x_ref[0], gain).astype(BF16)
    hp = _rms(xp_ref[0], gain).astype(BF16)
    hn = _rms(xn_ref[0], gain).astype(BF16)

    pa = _dot(h, wa_ref[...]) + ba_ref[...]
    wqk = wa_ref[:, 0:2 * WIDTH]
    bqk = ba_ref[:, 0:2 * WIDTH]
    has_prev = (t > 0).astype(F32)
    has_next = (t < nt - 1).astype(F32)
    ext_ref[0:HALO, :] = (_dot(hp, wqk) + bqk) * has_prev
    ext_ref[HALO:HALO + rows, :] = pa[:, 0:2 * WIDTH]
    ext_ref[HALO + rows:2 * HALO + rows, :] = (_dot(hn, wqk) + bqk) * has_next
    acc = cb_ref[...] + ext_ref[pl.ds(HALO - CONV_PAD, rows), :] * cw_ref[0:1, :]
    for j in range(1, CONV_K):
        acc = acc + ext_ref[pl.ds(HALO - CONV_PAD + j, rows), :] * cw_ref[j:j + 1, :]
    qk = _silu(acc)
    q_ref[0] = qk[:, 0:WIDTH] * (D_HEAD ** -0.5)
    k_ref[0] = qk[:, WIDTH:2 * WIDTH]
    v_ref[0] = pa[:, 2 * WIDTH:3 * WIDTH]
    mo_ref[0] = jax.nn.sigmoid(pa[:, 3 * WIDTH:4 * WIDTH])

    gc = _dot(h, wg_ref[...]) + bg_ref[...]
    lane = lax.broadcasted_iota(jnp.int32, gc.shape, 1)
    is_f = (lane >= 2 * N_HEADS) & (lane < N_GATES)
    gcol_ref[0] = jnp.where(is_f, _log_sigmoid(gc + fbrow_ref[...]), gc)
    gr = _dot_nt(wgt_ref[...], h) + bgt_ref[...]
    sub = lax.broadcasted_iota(jnp.int32, gr.shape, 0)
    grow_ref[0] = jnp.where(sub >= 2 * N_HEADS, _log_sigmoid(gr + fbcol_ref[...]), gr)

    ph = _dot(h, wh_ref[...]) + bh_ref[...]
    lbl = lbl_ref[...]
    lmax = jnp.max(lbl, axis=0, keepdims=True)
    le = jnp.exp(lbl - lmax)
    lb = le[0:1, :] / jnp.sum(le, axis=0, keepdims=True)
    hq_ref[0] = _silu(ph[:, 0:WIDTH])
    g_f = lb + (1.0 - lb) * jax.nn.sigmoid(ph[:, WIDTH:2 * WIDTH])
    kf_ref[0] = 1.0 - g_f
    lgf_ref[0] = jnp.log(g_f)
    g_b = lb + (1.0 - lb) * jax.nn.sigmoid(ph[:, 2 * WIDTH:3 * WIDTH])
    kb_ref[0] = 1.0 - g_b
    lgb_ref[0] = jnp.log(g_b)
    hv_ref[0] = ph[:, 3 * WIDTH:4 * WIDTH]
    hgg_ref[0] = _silu(ph[:, 4 * WIDTH:5 * WIDTH])


def _in_proj(x, norm1, w_in, b_in, conv_w, conv_b, fgate_bias, lb_logits):
    B, T, D = x.shape
    rows = PROJ_ROWS
    nt = T // rows
    a_w = 4 * WIDTH
    wa = w_in[:, 0:a_w].astype(BF16)
    ba = b_in[None, 0:a_w]
    wg32 = jnp.pad(w_in[:, a_w:a_w + N_GATES], ((0, 0), (0, LANES - N_GATES)))
    bg = jnp.pad(b_in[a_w:a_w + N_GATES], (0, LANES - N_GATES))[None, :]
    wg = wg32.astype(BF16)
    wgt = w_in[:, a_w:a_w + N_GATES].T.astype(BF16)
    bgt = b_in[a_w:a_w + N_GATES][:, None]
    fb = fgate_bias.reshape(2 * N_HEADS)
    fbrow = jnp.zeros((1, LANES), F32).at[0, 2 * N_HEADS:N_GATES].set(fb)
    fbcol = jnp.zeros((N_GATES, 1), F32).at[2 * N_HEADS:N_GATES, 0].set(fb)
    wh = w_in[:, a_w + N_GATES:].astype(BF16)
    bh = b_in[None, a_w + N_GATES:]

    tiles_per_halo = rows // HALO
    n_halo = T // HALO

    def full(arr):
        nd = arr.ndim
        return pl.BlockSpec(arr.shape, lambda b, t: (0,) * nd)

    def tok(width):
        return pl.BlockSpec((1, rows, width), lambda b, t: (b, t, 0))

    in_specs = [
        tok(D),
        pl.BlockSpec((1, HALO, D), lambda b, t: (b, jnp.maximum(t * tiles_per_halo - 1, 0), 0)),
        pl.BlockSpec((1, HALO, D), lambda b, t: (b, jnp.minimum((t + 1) * tiles_per_halo, n_halo - 1), 0)),
    ]
    consts = [norm1[None, :], wa, ba, wg, bg, wgt, bgt, fbrow, fbcol, wh, bh, conv_w, conv_b[None, :], lb_logits]
    in_specs += [full(c) for c in consts]
    tok_out = jax.ShapeDtypeStruct((B, T, WIDTH), F32)
    out_shape = [tok_out, tok_out, tok_out, tok_out,
                 jax.ShapeDtypeStruct((B, T, LANES), F32),
                 jax.ShapeDtypeStruct((B, N_GATES, T), F32),
                 tok_out, tok_out, tok_out, tok_out, tok_out, tok_out, tok_out]
    out_specs = [tok(WIDTH)] * 4 + [tok(LANES), pl.BlockSpec((1, N_GATES, rows), lambda b, t: (b, 0, t))] + [tok(WIDTH)] * 7
    return pl.pallas_call(
        _in_proj_kernel,
        grid=(B, nt),
        in_specs=in_specs,
        out_specs=out_specs,
        out_shape=out_shape,
        scratch_shapes=[pltpu.VMEM((rows + 2 * HALO, 2 * WIDTH), F32)],
        compiler_params=pltpu.CompilerParams(
            dimension_semantics=("parallel", "parallel"), vmem_limit_bytes=VMEM_LIMIT),
        name="in_proj",
    )(x, x, x, *consts)


def _cumsum_rows(tri_bf, x):
    hi, mid, lo = _split3(x)
    return _dot(tri_bf, hi) + _dot(tri_bf, mid) + _dot(tri_bf, lo)


def _cumsum_lanes(x, tri_bf):
    hi, mid, lo = _split3(x)
    return _dot(hi, tri_bf) + _dot(mid, tri_bf) + _dot(lo, tri_bf)


def _mlstm_chunk(q, k, vext, i_col, b_col, i_row, b_row, seen, last, c_ref, m_ref, idx):
    m_prev = m_ref[idx][:, 0:1]
    c_prev = c_ref[idx]
    q_bf = q.astype(BF16)
    log_d = jnp.where(seen, b_col - b_row + i_row, -jnp.inf)
    m_inter = b_col + m_prev
    m_t = jnp.maximum(m_inter, jnp.max(log_d, axis=-1, keepdims=True))
    scores = (_dot_nt(q_bf, k.astype(BF16)) * jnp.exp(log_d - m_t)).astype(BF16)
    inter_scale = jnp.exp(m_inter - m_t)
    numden = _dot(scores, vext) + inter_scale * _dot(q_bf, c_prev.astype(BF16))
    num = numden[:, 0:D_HEAD]
    den = numden[:, D_HEAD:2 * D_HEAD]
    h = num / jnp.maximum(jnp.abs(den), jnp.exp(-m_t))

    b_last = b_col[last:last + 1, :]
    log_w = b_last - b_col + i_col
    m_new = jnp.maximum(b_last + m_prev, jnp.max(log_w, axis=0, keepdims=True))
    w = jnp.exp(log_w - m_new)
    decay = jnp.exp(b_last + m_prev - m_new)
    c_ref[idx] = decay * c_prev + _dot_tn((k * w).astype(BF16), vext)
    m_ref[idx] = jnp.broadcast_to(m_new, (1, LANES))
    return h


def _sibling_total(c, half, rev, sub_iota):
    L = c.shape[0]
    if half >= SUBLANES:
        n = L // (2 * half)
        c3 = c.reshape(n, 2 * half, LANES)
        r = half if rev else half - 1
        return jnp.broadcast_to(c3[:, r:r + 1, :], (n, 2 * half, LANES)).reshape(L, LANES)
    c3 = c.reshape(L // SUBLANES, SUBLANES, LANES)
    src_row = half if rev else half - 1
    y = jnp.where((sub_iota & (2 * half - 1)) == src_row, c3, 0.0)
    step = 1 if rev else -1
    span = 1
    while span < half:
        y = y + pltpu.roll(y, (step * span) % SUBLANES, 1)
        span *= 2
    y = y + pltpu.roll(y, (-step * half) % SUBLANES, 1)
    return y.reshape(L, LANES)


def _hgrn2_chunk(q, k, v_bf, lg, rev, level, diag, row_iota, sub_iota, st_ref, idx):
    L = q.shape[0]
    att = jnp.where(diag, _dot_nt(q.astype(BF16), k.astype(BF16)), 0.0)
    c = lg
    half = 1
    bit = 0
    while half < L:
        total = _sibling_total(c, half, rev, sub_iota)
        q_hat = (q * jnp.exp(c)).astype(BF16)
        k_hat = (k * jnp.exp(jnp.minimum(total - c, 0.0))).astype(BF16)
        att = jnp.where(level == bit, _dot_nt(q_hat, k_hat), att)
        second = ((row_iota & half) == 0) if rev else ((row_iota & half) != 0)
        c = c + jnp.where(second, total, 0.0)
        half *= 2
        bit += 1
    last = 0 if rev else L - 1
    st_prev = st_ref[idx]
    b_last = c[last:last + 1, :]
    o = _dot_nt((q * jnp.exp(c)).astype(BF16), st_prev.astype(BF16)) + _dot(att.astype(BF16), v_bf)
    k_dec = (k * jnp.exp(b_last - c)).astype(BF16)
    st_ref[idx] = st_prev * jnp.exp(b_last) + _dot_tn(v_bf, k_dec)
    return o


def _mixer_kernel(qf_ref, kf_ref, vf_ref, gcf_ref, grf_ref, hqf_ref, hkf_ref, hlf_ref, hvf_ref,
                  qb_ref, kb_ref, vb_ref, gcb_ref, grb_ref, hqb_ref, hkb_ref, hlb_ref, hvb_ref,
                  hf_ref, of_ref, hb_ref, ob_ref, c_ref, m_ref, st_ref):
    L = CHUNK

    @pl.when(pl.program_id(1) == 0)
    def _():
        c_ref[...] = jnp.zeros_like(c_ref)
        m_ref[...] = jnp.zeros_like(m_ref)
        st_ref[...] = jnp.zeros_like(st_ref)

    row = lax.broadcasted_iota(jnp.int32, (L, L), 0)
    col = lax.broadcasted_iota(jnp.int32, (L, L), 1)
    row_iota = lax.broadcasted_iota(jnp.int32, (L, LANES), 0)
    sub_iota = lax.broadcasted_iota(jnp.int32, (L // SUBLANES, SUBLANES, LANES), 1)
    diag = row == col
    diff = row ^ col
    high_bit = jnp.zeros((L, L), jnp.int32)
    half = 2
    while half < L:
        high_bit = high_bit + (diff >= half).astype(jnp.int32)
        half *= 2
    ones = jnp.ones((L, D_HEAD), BF16)

    dirs = (
        (0, qf_ref, kf_ref, vf_ref, gcf_ref, grf_ref, hqf_ref, hkf_ref, hlf_ref, hvf_ref, hf_ref, of_ref),
        (1, qb_ref, kb_ref, vb_ref, gcb_ref, grb_ref, hqb_ref, hkb_ref, hlb_ref, hvb_ref, hb_ref, ob_ref),
    )
    for d, q_ref, k_ref, v_ref, gc_ref, gr_ref, hq_ref, hk_ref, hl_ref, hv_ref, h_out, o_out in dirs:
        rev = d == 1
        seen = (col >= row) if rev else (col <= row)
        before = (col > row) if rev else (col < row)
        level = jnp.where(before, high_bit, -1)
        tri = seen.astype(BF16)
        tri_t = (row >= col if rev else row <= col).astype(BF16)
        last = 0 if rev else L - 1
        gc = gc_ref[0]
        gr = gr_ref[0]
        gc_cum = _cumsum_rows(tri, gc)
        gr_cum = _cumsum_lanes(gr, tri_t)
        for hd in range(N_HEADS):
            sl = slice(hd * D_HEAD, (hd + 1) * D_HEAD)
            gi = d * N_HEADS + hd
            gf = 2 * N_HEADS + gi
            idx = d * N_HEADS + hd
            vext = jnp.concatenate([v_ref[0, :, sl].astype(BF16), ones], axis=1)
            h_out[0, :, sl] = _mlstm_chunk(
                q_ref[0, :, sl], k_ref[0, :, sl], vext,
                gc[:, gi:gi + 1], gc_cum[:, gf:gf + 1], gr[gi:gi + 1, :], gr_cum[gf:gf + 1, :],
                seen, last, c_ref, m_ref, idx)
            o_out[0, :, sl] = _hgrn2_chunk(
                hq_ref[0, :, sl], hk_ref[0, :, sl], hv_ref[0, :, sl].astype(BF16), hl_ref[0, :, sl],
                rev, level, diag, row_iota, sub_iota, st_ref, idx)


def _mixer(q, k, v, gcol, grow, hq, kf, lgf, kb, lgb, hv):
    B, T, _ = q.shape
    L = CHUNK
    nc = T // L

    def fwd(width):
        return pl.BlockSpec((1, L, width), lambda b, c: (b, c, 0))

    def bwd(width):
        return pl.BlockSpec((1, L, width), lambda b, c: (b, nc - 1 - c, 0))

    grow_f = pl.BlockSpec((1, N_GATES, L), lambda b, c: (b, 0, c))
    grow_b = pl.BlockSpec((1, N_GATES, L), lambda b, c: (b, 0, nc - 1 - c))
    in_specs = ([fwd(WIDTH)] * 3 + [fwd(LANES), grow_f] + [fwd(WIDTH)] * 4
                + [bwd(WIDTH)] * 3 + [bwd(LANES), grow_b] + [bwd(WIDTH)] * 4)
    out = jax.ShapeDtypeStruct((B, T, WIDTH), F32)
    n_state = 2 * N_HEADS
    return pl.pallas_call(
        _mixer_kernel,
        grid=(B, nc),
        in_specs=in_specs,
        out_specs=[fwd(WIDTH), fwd(WIDTH), bwd(WIDTH), bwd(WIDTH)],
        out_shape=[out, out, out, out],
        scratch_shapes=[
            pltpu.VMEM((n_state, D_HEAD, 2 * D_HEAD), F32),
            pltpu.VMEM((n_state, 1, LANES), F32),
            pltpu.VMEM((n_state, D_HEAD, D_HEAD), F32),
        ],
        compiler_params=pltpu.CompilerParams(
            dimension_semantics=("parallel", "arbitrary"), vmem_limit_bytes=VMEM_LIMIT),
        name="mixer",
    )(q, k, v, gcol, grow, hq, kf, lgf, hv, q, k, v, gcol, grow, hq, kb, lgb, hv)


def _head_norm(hsum, gain):
    parts = []
    for hd in range(N_HEADS):
        hh = hsum[:, hd * D_HEAD:(hd + 1) * D_HEAD]
        parts.append(hh * lax.rsqrt(jnp.mean(hh * hh, axis=-1, keepdims=True) + NORM_EPS))
    return jnp.concatenate(parts, axis=1) * gain


def _merge_kernel(hf_ref, hb_ref, of_ref, ob_ref, mo_ref, hgg_ref, x_ref, mn_ref, hn_ref, wo_ref,
                  n2_ref, wr_ref, br_ref, x1_ref, h2_ref, route_ref):
    m_out = _head_norm(hf_ref[0] + hb_ref[0], mn_ref[...]) * mo_ref[0]
    hg_out = _head_norm(of_ref[0] + ob_ref[0], hn_ref[...]) * hgg_ref[0]
    mixed = jnp.concatenate([m_out, hg_out], axis=1).astype(BF16)
    x1 = x_ref[0] + _dot(mixed, wo_ref[...])
    x1_ref[0] = x1
    h2 = _rms(x1, n2_ref[...])
    h2_ref[0] = h2

    logits = jnp.dot(h2, wr_ref[...], preferred_element_type=F32, precision=lax.Precision.HIGHEST) + br_ref[...]
    lane = lax.broadcasted_iota(jnp.int32, logits.shape, 1)
    big = jnp.int32(LANES)
    neg = -jnp.inf
    g_log = jnp.where(lane < N_GROUPS, logits, neg)
    g_max = jnp.max(g_log, axis=-1, keepdims=True)
    g_idx = jnp.min(jnp.where(g_log == g_max, lane, big), axis=-1, keepdims=True)
    g_val = 1.0 / jnp.sum(jnp.exp(g_log - g_max), axis=-1, keepdims=True)
    e_lo = N_GROUPS + g_idx * EXPERTS_PER_GROUP
    e_log = jnp.where((lane >= e_lo) & (lane < e_lo + EXPERTS_PER_GROUP), logits, neg)
    m1 = jnp.max(e_log, axis=-1, keepdims=True)
    i1 = jnp.min(jnp.where(e_log == m1, lane, big), axis=-1, keepdims=True)
    e_log2 = jnp.where(lane == i1, neg, e_log)
    m2 = jnp.max(e_log2, axis=-1, keepdims=True)
    i2 = jnp.min(jnp.where(e_log2 == m2, lane, big), axis=-1, keepdims=True)
    r2 = jnp.exp(m2 - m1)
    w1 = g_val / (1.0 + r2)
    w2 = g_val * r2 / (1.0 + r2)
    route = jnp.where(lane == 0, (i1 - N_GROUPS).astype(F32),
                      jnp.where(lane == 1, (i2 - N_GROUPS).astype(F32),
                                jnp.where(lane == 2, w1, jnp.where(lane == 3, w2, 0.0))))
    route_ref[0] = route


def _merge(h_f, h_b, o_f, o_b, mo, hgg, x, m_norm, hg_norm, w_out, norm2, w_rg, b_rg, w_re, b_re):
    B, T, D = x.shape
    rows = PROJ_ROWS
    n_log = N_GROUPS + N_EXPERTS
    wr = jnp.pad(jnp.concatenate([w_rg, w_re], axis=1), ((0, 0), (0, LANES - n_log)))
    br = jnp.pad(jnp.concatenate([b_rg, b_re]), (0, LANES - n_log))[None, :]
    consts = [m_norm[None, :], hg_norm[None, :], w_out.astype(BF16), norm2[None, :], wr, br]

    def full(arr):
        nd = arr.ndim
        return pl.BlockSpec(arr.shape, lambda b, t: (0,) * nd)

    def tok(width):
        return pl.BlockSpec((1, rows, width), lambda b, t: (b, t, 0))

    return pl.pallas_call(
        _merge_kernel,
        grid=(B, T // rows),
        in_specs=[tok(WIDTH)] * 6 + [tok(D)] + [full(c) for c in consts],
        out_specs=[tok(D), tok(D), tok(LANES)],
        out_shape=[jax.ShapeDtypeStruct((B, T, D), F32), jax.ShapeDtypeStruct((B, T, D), F32),
                   jax.ShapeDtypeStruct((B, T, LANES), F32)],
        compiler_params=pltpu.CompilerParams(
            dimension_semantics=("parallel", "parallel"), vmem_limit_bytes=VMEM_LIMIT),
        name="merge",
    )(h_f, h_b, o_f, o_b, mo, hgg, x, *consts)


def _start_row_gather(src_hbm, idx_ref, dst_ref, sem):
    for r in range(dst_ref.shape[0]):
        pltpu.make_async_copy(src_hbm.at[pl.ds(idx_ref[0, 0, r], 1)], dst_ref.at[pl.ds(r, 1)], sem).start()


def _start_row_scatter(src_ref, idx_ref, dst_hbm, sem):
    for r in range(src_ref.shape[0]):
        pltpu.make_async_copy(src_ref.at[pl.ds(r, 1)], dst_hbm.at[pl.ds(idx_ref[0, 0, r], 1)], sem).start()


def _expert_kernel(be_ref, nu_ref, tok_cur, tok_nxt, dst_prev, h2_hbm, w1_ref, w3_ref, w2_ref, y_hbm,
                   x_a, x_b, o_a, o_b, gsem, ssem):
    i = pl.program_id(0)
    n_used = nu_ref[0]
    rows = x_a.shape[0]

    @pl.when(i == 0)
    def _():
        _start_row_gather(h2_hbm, tok_cur, x_a, gsem.at[0])
        o_b[...] = jnp.zeros_like(o_b)

    def step(x_cur, x_nxt, o_cur, o_prev, s_cur, s_nxt):
        pltpu.make_async_copy(h2_hbm.at[pl.ds(0, rows)], x_cur, gsem.at[s_cur]).wait()
        _start_row_gather(h2_hbm, tok_nxt, x_nxt, gsem.at[s_nxt])
        _start_row_scatter(o_prev, dst_prev, y_hbm, ssem)
        xb = x_cur[...].astype(BF16)
        hid = _silu(_dot(xb, w1_ref[0])) * _dot(xb, w3_ref[0])
        o_cur[...] = _dot(hid.astype(BF16), w2_ref[0])
        pltpu.make_async_copy(o_prev, y_hbm.at[pl.ds(0, rows)], ssem).wait()

    even = (i % 2) == 0

    @pl.when((i <= n_used) & even)
    def _():
        step(x_a, x_b, o_a, o_b, 0, 1)

    @pl.when((i <= n_used) & jnp.logical_not(even))
    def _():
        step(x_b, x_a, o_b, o_a, 1, 0)

    @pl.when((i == n_used + 1) & even)
    def _():
        pltpu.make_async_copy(h2_hbm.at[pl.ds(0, rows)], x_a, gsem.at[0]).wait()

    @pl.when((i == n_used + 1) & jnp.logical_not(even))
    def _():
        pltpu.make_async_copy(h2_hbm.at[pl.ds(0, rows)], x_b, gsem.at[1]).wait()


def _experts(h2, block_e, n_used, row_tok, row_dst, n_out_rows, w1, w3, w2):
    steps = block_e.shape[0]
    rows = EXPERT_ROWS
    D = h2.shape[1]
    smem_blk = functools.partial(pl.BlockSpec, (1, 1, rows), memory_space=pltpu.SMEM)
    grid_spec = pltpu.PrefetchScalarGridSpec(
        num_scalar_prefetch=2,
        grid=(steps,),
        in_specs=[
            smem_blk(lambda i, be, nu: (i, 0, 0)),
            smem_blk(lambda i, be, nu: (i + 1, 0, 0)),
            smem_blk(lambda i, be, nu: (i, 0, 0)),
            pl.BlockSpec(memory_space=pl.ANY),
            pl.BlockSpec((1, D, EXPERT_FF), lambda i, be, nu: (be[i], 0, 0)),
            pl.BlockSpec((1, D, EXPERT_FF), lambda i, be, nu: (be[i], 0, 0)),
            pl.BlockSpec((1, EXPERT_FF, D), lambda i, be, nu: (be[i], 0, 0)),
        ],
        out_specs=pl.BlockSpec(memory_space=pl.ANY),
        scratch_shapes=[pltpu.VMEM((rows, D), F32)] * 4 + [pltpu.SemaphoreType.DMA((2,)), pltpu.SemaphoreType.DMA(())],
    )
    return pl.pallas_call(
        _expert_kernel,
        grid_spec=grid_spec,
        out_shape=jax.ShapeDtypeStruct((n_out_rows, D), F32),
        compiler_params=pltpu.CompilerParams(
            dimension_semantics=("arbitrary",), vmem_limit_bytes=VMEM_LIMIT),
        name="experts",
    )(block_e, n_used, row_tok, row_tok, row_dst, h2, w1, w3, w2)


def _combine_kernel(y0_ref, y1_ref, x1_ref, route_ref, nf_ref, y_ref):
    route = route_ref[...]
    y = x1_ref[...] + route[:, 2:3] * y0_ref[...] + route[:, 3:4] * y1_ref[...]
    y_ref[...] = _rms(y, nf_ref[...])


def _combine(x1, route, y_rows, norm_f):
    N, D = x1.shape
    rows = PROJ_ROWS
    nt = N // rows
    return pl.pallas_call(
        _combine_kernel,
        grid=(nt,),
        in_specs=[
            pl.BlockSpec((rows, D), lambda i: (i, 0)),
            pl.BlockSpec((rows, D), lambda i: (i + nt, 0)),
            pl.BlockSpec((rows, D), lambda i: (i, 0)),
            pl.BlockSpec((rows, LANES), lambda i: (i, 0)),
            pl.BlockSpec((1, D), lambda i: (0, 0)),
        ],
        out_specs=pl.BlockSpec((rows, D), lambda i: (i, 0)),
        out_shape=jax.ShapeDtypeStruct((N, D), F32),
        compiler_params=pltpu.CompilerParams(
            dimension_semantics=("parallel",), vmem_limit_bytes=VMEM_LIMIT),
        name="combine",
    )(y_rows, y_rows, x1, route, norm_f[None, :])


def _dispatch_plan(expert_ids):
    N = expert_ids.shape[0]
    A = N * TOP_K
    blk = EXPERT_ROWS
    flat_e = expert_ids.T.reshape(A)
    sorted_e, order = lax.sort((flat_e, jnp.arange(A, dtype=jnp.int32)), num_keys=1)
    seg_end = jnp.searchsorted(sorted_e, jnp.arange(N_EXPERTS, dtype=jnp.int32), side="right").astype(jnp.int32)
    seg_start = jnp.concatenate([jnp.zeros((1,), jnp.int32), seg_end[:-1]])
    counts = seg_end - seg_start
    padded = ((counts + blk - 1) // blk) * blk
    pad_end = jnp.cumsum(padded)
    pad_start = pad_end - padded
    n_blocks = (A + N_EXPERTS * (blk - 1) + blk - 1) // blk
    block_start = jnp.arange(n_blocks, dtype=jnp.int32) * blk
    block_e = jnp.clip(jnp.searchsorted(pad_end, block_start, side="right"), 0, N_EXPERTS - 1).astype(jnp.int32)
    within = jnp.arange(blk, dtype=jnp.int32)[None, :]
    j = block_start[:, None] + within - pad_start[block_e][:, None]
    valid = j < counts[block_e][:, None]
    src = jnp.clip(seg_start[block_e][:, None] + j, 0, A - 1)
    assign = order[src]
    row_dst = jnp.where(valid, assign, A + within)
    row_tok = jnp.where(valid, assign % N, 0)
    n_used = (pad_end[-1] // blk).astype(jnp.int32).reshape(1)
    steps = n_blocks + 2
    spare = jnp.broadcast_to(A + within, (1, blk))
    row_tok = jnp.concatenate([row_tok, jnp.zeros((steps + 1 - n_blocks, blk), jnp.int32)], axis=0)
    row_dst = jnp.concatenate([spare, row_dst, spare], axis=0)
    block_e = jnp.concatenate([block_e, jnp.full((2,), N_EXPERTS - 1, jnp.int32)])
    return row_tok[:, None, :], row_dst[:, None, :], block_e, n_used, A + blk


def _trunk(x, norm1, w_in, b_in, conv_w, conv_b, m_fgate_bias, m_norm, hg_lb_logits, hg_norm, w_out, norm2,
           w_rg, b_rg, w_re, b_re, w1_bf, w3_bf, w2_bf, norm_f):
    B, T, D = x.shape
    N = B * T
    q, k, v, mo, gcol, grow, hq, kf, lgf, kb, lgb, hv, hgg = _in_proj(
        x, norm1, w_in, b_in, conv_w, conv_b, m_fgate_bias, hg_lb_logits)
    h_f, o_f, h_b, o_b = _mixer(q, k, v, gcol, grow, hq, kf, lgf, kb, lgb, hv)
    x1, h2, route = _merge(h_f, h_b, o_f, o_b, mo, hgg, x, m_norm, hg_norm, w_out, norm2, w_rg, b_rg, w_re, b_re)
    x1 = x1.reshape(N, D)
    h2 = h2.reshape(N, D)
    route = route.reshape(N, LANES)
    expert_ids = route[:, 0:TOP_K].astype(jnp.int32)
    row_tok, row_dst, block_e, n_used, n_out_rows = _dispatch_plan(expert_ids)
    y_rows = _experts(h2, block_e, n_used, row_tok, row_dst, n_out_rows, w1_bf, w3_bf, w2_bf)
    y = _combine(x1, route, y_rows, norm_f)
    return y.reshape(B, T, D)


def kernel(x_prompt, x_sample, norm1, w_in, b_in, conv_w, conv_b, m_fgate_bias, m_norm, hg_lb_logits, hg_norm,
           w_out, norm2, w_router_group, b_router_group, w_router_expert, b_router_expert, w1, w3, w2, norm_f):
    layer = 0
    w1_bf = w1[layer].astype(BF16)
    w3_bf = w3[layer].astype(BF16)
    w2_bf = w2[layer].astype(BF16)
    args = (norm1[layer], w_in[layer], b_in[layer], conv_w[layer], conv_b[layer], m_fgate_bias[layer],
            m_norm[layer], hg_lb_logits, hg_norm[layer], w_out[layer], norm2[layer],
            w_router_group[layer], b_router_group[layer], w_router_expert[layer], b_router_expert[layer],
            w1_bf, w3_bf, w2_bf, norm_f)
    return (_trunk(x_prompt, *args), _trunk(x_sample, *args))
```

```python
import functools

import jax
import jax.numpy as jnp
from jax import lax
from jax.experimental import pallas as pl
from jax.experimental.pallas import tpu as pltpu

F32 = jnp.float32
BF16 = jnp.bfloat16

D_MODEL = 1024
N_HEADS = 4
D_HEAD = 128
WIDTH = N_HEADS * D_HEAD
CONV_K = 5
CONV_PAD = CONV_K // 2
N_GROUPS = 4
EXPERTS_PER_GROUP = 8
N_EXPERTS = N_GROUPS * EXPERTS_PER_GROUP
TOP_K = 2
EXPERT_FF = D_MODEL // 2
NORM_EPS = 1e-6

LANES = 128
SUBLANES = 8
CHUNK = 128
PROJ_ROWS = 256
HALO = SUBLANES
EXPERT_ROWS = 256
N_GATES = 4 * N_HEADS
N_DMA_PRIORITIES = 2
VMEM_LIMIT = 56 * 1024 * 1024


def _dot(a, b):
    return jnp.dot(a, b, preferred_element_type=F32)


def _dot_nt(a, b):
    return lax.dot_general(a, b, (((1,), (1,)), ((), ())), preferred_element_type=F32)


def _dot_tn(a, b):
    return lax.dot_general(a, b, (((0,), (0,)), ((), ())), preferred_element_type=F32)


def _split3(x):
    hi = x.astype(BF16)
    r1 = x - hi.astype(F32)
    mid = r1.astype(BF16)
    lo = (r1 - mid.astype(F32)).astype(BF16)
    return hi, mid, lo


def _silu(x):
    return x * jax.nn.sigmoid(x)


def _log_sigmoid(x):
    return -(jnp.maximum(-x, 0.0) + jnp.log1p(jnp.exp(-jnp.abs(x))))


def _rms(x, gain):
    return x * lax.rsqrt(jnp.mean(x * x, axis=-1, keepdims=True) + NORM_EPS) * gain


def _in_proj_kernel(x_ref, xp_ref, xn_ref, n1_ref, wa_ref, ba_ref, wg_ref, bg_ref, wgt_ref, bgt_ref,
                    fbrow_ref, fbcol_ref, wh_ref, bh_ref, cw_ref, cb_ref, lbl_ref,
                    q_ref, k_ref, v_ref, mo_ref, gcol_ref, grow_ref, hq_ref, gf_ref, gb_ref, hv_ref, hgg_ref,
                    ext_ref):
    t = pl.program_id(1)
    nt = pl.num_programs(1)
    rows = x_ref.shape[1]
    gain = n1_ref[...]

    h = _rms(x_ref[0], gain).astype(BF16)
    hp = _rms(xp_ref[0], gain).astype(BF16)
    hn = _rms(xn_ref[0], gain).astype(BF16)

    pa = _dot(h, wa_ref[...]) + ba_ref[...]
    wqk = wa_ref[:, 0:2 * WIDTH]
    bqk = ba_ref[:, 0:2 * WIDTH]
    has_prev = (t > 0).astype(F32)
    has_next = (t < nt - 1).astype(F32)
    ext_ref[0:HALO, :] = (_dot(hp, wqk) + bqk) * has_prev
    ext_ref[HALO:HALO + rows, :] = pa[:, 0:2 * WIDTH]
    ext_ref[HALO + rows:2 * HALO + rows, :] = (_dot(hn, wqk) + bqk) * has_next
    acc = cb_ref[...] + ext_ref[pl.ds(HALO - CONV_PAD, rows), :] * cw_ref[0:1, :]
    for j in range(1, CONV_K):
        acc = acc + ext_ref[pl.ds(HALO - CONV_PAD + j, rows), :] * cw_ref[j:j + 1, :]
    qk = _silu(acc)
    q_ref[0] = qk[:, 0:WIDTH] * (D_HEAD ** -0.5)
    k_ref[0] = qk[:, WIDTH:2 * WIDTH]
    v_ref[0] = pa[:, 2 * WIDTH:3 * WIDTH]
    mo_ref[0] = jax.nn.sigmoid(pa[:, 3 * WIDTH:4 * WIDTH])

    gc = _dot(h, wg_ref[...]) + bg_ref[...]
    lane = lax.broadcasted_iota(jnp.int32, gc.shape, 1)
    is_f = (lane >= 2 * N_HEADS) & (lane < N_GATES)
    gcol_ref[0] = jnp.where(is_f, _log_sigmoid(gc + fbrow_ref[...]), gc)
    gr = _dot_nt(wgt_ref[...], h) + bgt_ref[...]
    sub = lax.broadcasted_iota(jnp.int32, gr.shape, 0)
    grow_ref[0] = jnp.where(sub >= 2 * N_HEADS, _log_sigmoid(gr + fbcol_ref[...]), gr)

    ph = _dot(h, wh_ref[...]) + bh_ref[...]
    lbl = lbl_ref[...]
    lmax = jnp.max(lbl, axis=0, keepdims=True)
    le = jnp.exp(lbl - lmax)
    lb = le[0:1, :] / jnp.sum(le, axis=0, keepdims=True)
    hq_ref[0] = _silu(ph[:, 0:WIDTH])
    gf_ref[0] = lb + (1.0 - lb) * jax.nn.sigmoid(ph[:, WIDTH:2 * WIDTH])
    gb_ref[0] = lb + (1.0 - lb) * jax.nn.sigmoid(ph[:, 2 * WIDTH:3 * WIDTH])
    hv_ref[0] = ph[:, 3 * WIDTH:4 * WIDTH]
    hgg_ref[0] = _silu(ph[:, 4 * WIDTH:5 * WIDTH])


def _in_proj(x, norm1, w_in, b_in, conv_w, conv_b, fgate_bias, lb_logits):
    B, T, D = x.shape
    rows = PROJ_ROWS
    nt = T // rows
    a_w = 4 * WIDTH
    wa = w_in[:, 0:a_w].astype(BF16)
    ba = b_in[None, 0:a_w]
    wg32 = jnp.pad(w_in[:, a_w:a_w + N_GATES], ((0, 0), (0, LANES - N_GATES)))
    bg = jnp.pad(b_in[a_w:a_w + N_GATES], (0, LANES - N_GATES))[None, :]
    wg = wg32.astype(BF16)
    wgt = w_in[:, a_w:a_w + N_GATES].T.astype(BF16)
    bgt = b_in[a_w:a_w + N_GATES][:, None]
    fb = fgate_bias.reshape(2 * N_HEADS)
    fbrow = jnp.zeros((1, LANES), F32).at[0, 2 * N_HEADS:N_GATES].set(fb)
    fbcol = jnp.zeros((N_GATES, 1), F32).at[2 * N_HEADS:N_GATES, 0].set(fb)
    wh = w_in[:, a_w + N_GATES:].astype(BF16)
    bh = b_in[None, a_w + N_GATES:]

    tiles_per_halo = rows // HALO
    n_halo = T // HALO

    def full(arr):
        nd = arr.ndim
        return pl.BlockSpec(arr.shape, lambda b, t: (0,) * nd)

    def tok(width):
        return pl.BlockSpec((1, rows, width), lambda b, t: (b, t, 0))

    in_specs = [
        tok(D),
        pl.BlockSpec((1, HALO, D), lambda b, t: (b, jnp.maximum(t * tiles_per_halo - 1, 0), 0)),
        pl.BlockSpec((1, HALO, D), lambda b, t: (b, jnp.minimum((t + 1) * tiles_per_halo, n_halo - 1), 0)),
    ]
    consts = [norm1[None, :], wa, ba, wg, bg, wgt, bgt, fbrow, fbcol, wh, bh, conv_w, conv_b[None, :], lb_logits]
    in_specs += [full(c) for c in consts]
    tok_out = jax.ShapeDtypeStruct((B, T, WIDTH), F32)
    out_shape = [tok_out, tok_out, tok_out, tok_out,
                 jax.ShapeDtypeStruct((B, T, LANES), F32),
                 jax.ShapeDtypeStruct((B, N_GATES, T), F32),
                 tok_out, tok_out, tok_out, tok_out, tok_out]
    out_specs = [tok(WIDTH)] * 4 + [tok(LANES), pl.BlockSpec((1, N_GATES, rows), lambda b, t: (b, 0, t))] + [tok(WIDTH)] * 5
    return pl.pallas_call(
        _in_proj_kernel,
        grid=(B, nt),
        in_specs=in_specs,
        out_specs=out_specs,
        out_shape=out_shape,
        scratch_shapes=[pltpu.VMEM((rows + 2 * HALO, 2 * WIDTH), F32)],
        compiler_params=pltpu.CompilerParams(
            dimension_semantics=("parallel", "parallel"), vmem_limit_bytes=VMEM_LIMIT),
        name="in_proj",
    )(x, x, x, *consts)


def _cumsum_rows(tri_bf, x):
    hi, mid, lo = _split3(x)
    return _dot(tri_bf, hi) + _dot(tri_bf, mid) + _dot(tri_bf, lo)


def _cumsum_lanes(x, tri_bf):
    hi, mid, lo = _split3(x)
    return _dot(hi, tri_bf) + _dot(mid, tri_bf) + _dot(lo, tri_bf)


def _mlstm_chunk(q, k, vext, i_col, b_col, i_row, b_row, seen, last, c_ref, m_ref, idx):
    m_prev = m_ref[idx][:, 0:1]
    c_prev = c_ref[idx]
    q_bf = q.astype(BF16)
    log_d = jnp.where(seen, b_col - b_row + i_row, -jnp.inf)
    m_inter = b_col + m_prev
    m_t = jnp.maximum(m_inter, jnp.max(log_d, axis=-1, keepdims=True))
    scores = (_dot_nt(q_bf, k.astype(BF16)) * jnp.exp(log_d - m_t)).astype(BF16)
    inter_scale = jnp.exp(m_inter - m_t)
    numden = _dot(scores, vext) + inter_scale * _dot(q_bf, c_prev.astype(BF16))
    num = numden[:, 0:D_HEAD]
    den = numden[:, D_HEAD:2 * D_HEAD]
    h = num / jnp.maximum(jnp.abs(den), jnp.exp(-m_t))

    b_last = b_col[last:last + 1, :]
    log_w = b_last - b_col + i_col
    m_new = jnp.maximum(b_last + m_prev, jnp.max(log_w, axis=0, keepdims=True))
    w = jnp.exp(log_w - m_new)
    decay = jnp.exp(b_last + m_prev - m_new)
    c_ref[idx] = decay * c_prev + _dot_tn((k * w).astype(BF16), vext)
    m_ref[idx] = jnp.broadcast_to(m_new, (1, LANES))
    return h


def _hgrn2_level_small(q3, k3, pre3, suf3, half, rev, sub_iota):
    upper = (sub_iota & half) != 0
    second = jnp.logical_not(upper) if rev else upper
    end = 0 if rev else half - 1
    y = jnp.where((sub_iota & (half - 1)) == end, pre3, 0.0)
    step = 1 if rev else -1
    span = 1
    while span < half:
        y = y + pltpu.roll(y, (step * span) % SUBLANES, 1)
        span *= 2
    if 2 * half == SUBLANES:
        other = pltpu.roll(y, half, 1)
    else:
        other = jnp.where(upper, pltpu.roll(y, half, 1), pltpu.roll(y, SUBLANES - half, 1))
    z = jnp.where(second, q3 * pre3, k3 * suf3)
    return z, pre3 * jnp.where(second, other, 1.0), suf3 * jnp.where(second, 1.0, other)


def _hgrn2_level_big(q, k, pre, suf, half, rev):
    L = q.shape[0]
    shape = (L // (2 * half), 2, half, LANES)
    q4, k4, pre4, suf4 = (a.reshape(shape) for a in (q, k, pre, suf))
    first = 1 if rev else 0
    second = 1 - first
    end = 0 if rev else half - 1
    total_first = pre4[:, first, end:end + 1, :]
    total_second = pre4[:, second, end:end + 1, :]

    def join(at_first, at_second):
        parts = (at_second, at_first) if rev else (at_first, at_second)
        return jnp.stack(parts, axis=1).reshape(L, LANES)

    z = join(k4[:, first] * suf4[:, first], q4[:, second] * pre4[:, second])
    pre_new = join(pre4[:, first], pre4[:, second] * total_first)
    suf_new = join(suf4[:, first] * total_second, suf4[:, second])
    return z, pre_new, suf_new


def _hgrn2_chunk(q, g, v_bf, rev, level, diag, sub_iota, st_ref, idx):
    L = q.shape[0]
    k = 1.0 - g
    att = jnp.where(diag, _dot_nt(q.astype(BF16), k.astype(BF16)), 0.0)
    small = (L // SUBLANES, SUBLANES, LANES)
    q3, k3, pre, suf = q.reshape(small), k.reshape(small), g.reshape(small), jnp.ones(small, F32)
    half = 1
    bit = 0
    while half < L:
        if half == SUBLANES:
            pre, suf = pre.reshape(L, LANES), suf.reshape(L, LANES)
        if half < SUBLANES:
            z, pre, suf = _hgrn2_level_small(q3, k3, pre, suf, half, rev, sub_iota)
            z = z.reshape(L, LANES)
        else:
            z, pre, suf = _hgrn2_level_big(q, k, pre, suf, half, rev)
        z = z.astype(BF16)
        att = jnp.where(level == bit, _dot_nt(z, z), att)
        half *= 2
        bit += 1
    last = 0 if rev else L - 1
    st_prev = st_ref[idx]
    o = _dot_nt((q * pre).astype(BF16), st_prev.astype(BF16)) + _dot(att.astype(BF16), v_bf)
    st_ref[idx] = st_prev * pre[last:last + 1, :] + _dot_tn(v_bf, (k * suf).astype(BF16))
    return o


def _mixer_kernel(qf_ref, kf_ref, vf_ref, gcf_ref, grf_ref, hqf_ref, hgf_ref, hvf_ref,
                  qb_ref, kb_ref, vb_ref, gcb_ref, grb_ref, hqb_ref, hgb_ref, hvb_ref,
                  hf_ref, of_ref, hb_ref, ob_ref, c_ref, m_ref, st_ref):
    L = CHUNK

    @pl.when(pl.program_id(1) == 0)
    def _():
        c_ref[...] = jnp.zeros_like(c_ref)
        m_ref[...] = jnp.zeros_like(m_ref)
        st_ref[...] = jnp.zeros_like(st_ref)

    row = lax.broadcasted_iota(jnp.int32, (L, L), 0)
    col = lax.broadcasted_iota(jnp.int32, (L, L), 1)
    sub_iota = lax.broadcasted_iota(jnp.int32, (L // SUBLANES, SUBLANES, LANES), 1)
    diag = row == col
    diff = row ^ col
    high_bit = jnp.zeros((L, L), jnp.int32)
    half = 2
    while half < L:
        high_bit = high_bit + (diff >= half).astype(jnp.int32)
        half *= 2
    ones = jnp.ones((L, D_HEAD), BF16)

    dirs = (
        (0, qf_ref, kf_ref, vf_ref, gcf_ref, grf_ref, hqf_ref, hgf_ref, hvf_ref, hf_ref, of_ref),
        (1, qb_ref, kb_ref, vb_ref, gcb_ref, grb_ref, hqb_ref, hgb_ref, hvb_ref, hb_ref, ob_ref),
    )
    for d, q_ref, k_ref, v_ref, gc_ref, gr_ref, hq_ref, hg_ref, hv_ref, h_out, o_out in dirs:
        rev = d == 1
        seen = (col >= row) if rev else (col <= row)
        before = (col > row) if rev else (col < row)
        level = jnp.where(before, high_bit, -1)
        tri = seen.astype(BF16)
        tri_t = (row >= col if rev else row <= col).astype(BF16)
        last = 0 if rev else L - 1
        gc = gc_ref[0]
        gr = gr_ref[0]
        gc_cum = _cumsum_rows(tri, gc)
        gr_cum = _cumsum_lanes(gr, tri_t)
        for hd in range(N_HEADS):
            sl = slice(hd * D_HEAD, (hd + 1) * D_HEAD)
            gi = d * N_HEADS + hd
            gf = 2 * N_HEADS + gi
            idx = d * N_HEADS + hd
            vext = jnp.concatenate([v_ref[0, :, sl].astype(BF16), ones], axis=1)
            h_out[0, :, sl] = _mlstm_chunk(
                q_ref[0, :, sl], k_ref[0, :, sl], vext,
                gc[:, gi:gi + 1], gc_cum[:, gf:gf + 1], gr[gi:gi + 1, :], gr_cum[gf:gf + 1, :],
                seen, last, c_ref, m_ref, idx)
            o_out[0, :, sl] = _hgrn2_chunk(
                hq_ref[0, :, sl], hg_ref[0, :, sl], hv_ref[0, :, sl].astype(BF16),
                rev, level, diag, sub_iota, st_ref, idx)


def _mixer(q, k, v, gcol, grow, hq, g_f, g_b, hv):
    B, T, _ = q.shape
    L = CHUNK
    nc = T // L

    def fwd(width):
        return pl.BlockSpec((1, L, width), lambda b, c: (b, c, 0))

    def bwd(width):
        return pl.BlockSpec((1, L, width), lambda b, c: (b, nc - 1 - c, 0))

    grow_f = pl.BlockSpec((1, N_GATES, L), lambda b, c: (b, 0, c))
    grow_b = pl.BlockSpec((1, N_GATES, L), lambda b, c: (b, 0, nc - 1 - c))
    in_specs = ([fwd(WIDTH)] * 3 + [fwd(LANES), grow_f] + [fwd(WIDTH)] * 3
                + [bwd(WIDTH)] * 3 + [bwd(LANES), grow_b] + [bwd(WIDTH)] * 3)
    out = jax.ShapeDtypeStruct((B, T, WIDTH), F32)
    n_state = 2 * N_HEADS
    return pl.pallas_call(
        _mixer_kernel,
        grid=(B, nc),
        in_specs=in_specs,
        out_specs=[fwd(WIDTH), fwd(WIDTH), bwd(WIDTH), bwd(WIDTH)],
        out_shape=[out, out, out, out],
        scratch_shapes=[
            pltpu.VMEM((n_state, D_HEAD, 2 * D_HEAD), F32),
            pltpu.VMEM((n_state, 1, LANES), F32),
            pltpu.VMEM((n_state, D_HEAD, D_HEAD), F32),
        ],
        compiler_params=pltpu.CompilerParams(
            dimension_semantics=("parallel", "arbitrary"), vmem_limit_bytes=VMEM_LIMIT),
        name="mixer",
    )(q, k, v, gcol, grow, hq, g_f, hv, q, k, v, gcol, grow, hq, g_b, hv)


def _head_norm(hsum, gain):
    parts = []
    for hd in range(N_HEADS):
        hh = hsum[:, hd * D_HEAD:(hd + 1) * D_HEAD]
        parts.append(hh * lax.rsqrt(jnp.mean(hh * hh, axis=-1, keepdims=True) + NORM_EPS))
    return jnp.concatenate(parts, axis=1) * gain


def _merge_kernel(hf_ref, hb_ref, of_ref, ob_ref, mo_ref, hgg_ref, x_ref, mn_ref, hn_ref, wo_ref,
                  n2_ref, wr_ref, br_ref, x1_ref, h2_ref, route_ref):
    m_out = _head_norm(hf_ref[0] + hb_ref[0], mn_ref[...]) * mo_ref[0]
    hg_out = _head_norm(of_ref[0] + ob_ref[0], hn_ref[...]) * hgg_ref[0]
    mixed = jnp.concatenate([m_out, hg_out], axis=1).astype(BF16)
    x1 = x_ref[0] + _dot(mixed, wo_ref[...])
    x1_ref[0] = x1
    h2 = _rms(x1, n2_ref[...])
    h2_ref[0] = h2

    logits = jnp.dot(h2, wr_ref[...], preferred_element_type=F32, precision=lax.Precision.HIGHEST) + br_ref[...]
    lane = lax.broadcasted_iota(jnp.int32, logits.shape, 1)
    big = jnp.int32(LANES)
    neg = -jnp.inf
    g_log = jnp.where(lane < N_GROUPS, logits, neg)
    g_max = jnp.max(g_log, axis=-1, keepdims=True)
    g_idx = jnp.min(jnp.where(g_log == g_max, lane, big), axis=-1, keepdims=True)
    g_val = 1.0 / jnp.sum(jnp.exp(g_log - g_max), axis=-1, keepdims=True)
    e_lo = N_GROUPS + g_idx * EXPERTS_PER_GROUP
    e_log = jnp.where((lane >= e_lo) & (lane < e_lo + EXPERTS_PER_GROUP), logits, neg)
    m1 = jnp.max(e_log, axis=-1, keepdims=True)
    i1 = jnp.min(jnp.where(e_log == m1, lane, big), axis=-1, keepdims=True)
    e_log2 = jnp.where(lane == i1, neg, e_log)
    m2 = jnp.max(e_log2, axis=-1, keepdims=True)
    i2 = jnp.min(jnp.where(e_log2 == m2, lane, big), axis=-1, keepdims=True)
    r2 = jnp.exp(m2 - m1)
    w1 = g_val / (1.0 + r2)
    w2 = g_val * r2 / (1.0 + r2)
    route = jnp.where(lane == 0, (i1 - N_GROUPS).astype(F32),
                      jnp.where(lane == 1, (i2 - N_GROUPS).astype(F32),
                                jnp.where(lane == 2, w1, jnp.where(lane == 3, w2, 0.0))))
    route_ref[0] = route


def _merge(h_f, h_b, o_f, o_b, mo, hgg, x, m_norm, hg_norm, w_out, norm2, w_rg, b_rg, w_re, b_re):
    B, T, D = x.shape
    rows = PROJ_ROWS
    n_log = N_GROUPS + N_EXPERTS
    wr = jnp.pad(jnp.concatenate([w_rg, w_re], axis=1), ((0, 0), (0, LANES - n_log)))
    br = jnp.pad(jnp.concatenate([b_rg, b_re]), (0, LANES - n_log))[None, :]
    consts = [m_norm[None, :], hg_norm[None, :], w_out.astype(BF16), norm2[None, :], wr, br]

    def full(arr):
        nd = arr.ndim
        return pl.BlockSpec(arr.shape, lambda b, t: (0,) * nd)

    def tok(width):
        return pl.BlockSpec((1, rows, width), lambda b, t: (b, t, 0))

    return pl.pallas_call(
        _merge_kernel,
        grid=(B, T // rows),
        in_specs=[tok(WIDTH)] * 6 + [tok(D)] + [full(c) for c in consts],
        out_specs=[tok(D), tok(D), tok(LANES)],
        out_shape=[jax.ShapeDtypeStruct((B, T, D), F32), jax.ShapeDtypeStruct((B, T, D), F32),
                   jax.ShapeDtypeStruct((B, T, LANES), F32)],
        compiler_params=pltpu.CompilerParams(
            dimension_semantics=("parallel", "parallel"), vmem_limit_bytes=VMEM_LIMIT),
        name="merge",
    )(h_f, h_b, o_f, o_b, mo, hgg, x, *consts)


def _start_row_gather(src_hbm, idx_ref, dst_ref, sem):
    for r in range(dst_ref.shape[0]):
        pltpu.make_async_copy(src_hbm.at[pl.ds(idx_ref[0, 0, r], 1)], dst_ref.at[pl.ds(r, 1)], sem).start(
            priority=r % N_DMA_PRIORITIES)


def _start_row_scatter(src_ref, idx_ref, dst_hbm, sem):
    for r in range(src_ref.shape[0]):
        pltpu.make_async_copy(src_ref.at[pl.ds(r, 1)], dst_hbm.at[pl.ds(idx_ref[0, 0, r], 1)], sem).start(
            priority=r % N_DMA_PRIORITIES)


def _expert_kernel(be_ref, nu_ref, tok_cur, tok_nxt, dst_prev, h2_hbm, w1_ref, w3_ref, w2_ref, y_hbm,
                   x_a, x_b, o_a, o_b, gsem, ssem):
    i = pl.program_id(0)
    n_used = nu_ref[0]
    rows = x_a.shape[0]

    @pl.when(i == 0)
    def _():
        _start_row_gather(h2_hbm, tok_cur, x_a, gsem.at[0])
        o_b[...] = jnp.zeros_like(o_b)

    def step(x_cur, x_nxt, o_cur, o_prev, s_cur, s_nxt):
        pltpu.make_async_copy(h2_hbm.at[pl.ds(0, rows)], x_cur, gsem.at[s_cur]).wait()
        _start_row_gather(h2_hbm, tok_nxt, x_nxt, gsem.at[s_nxt])
        _start_row_scatter(o_prev, dst_prev, y_hbm, ssem)
        xb = x_cur[...].astype(BF16)
        hid = _silu(_dot(xb, w1_ref[0])) * _dot(xb, w3_ref[0])
        o_cur[...] = _dot(hid.astype(BF16), w2_ref[0])
        pltpu.make_async_copy(o_prev, y_hbm.at[pl.ds(0, rows)], ssem).wait()

    even = (i % 2) == 0

    @pl.when((i <= n_used) & even)
    def _():
        step(x_a, x_b, o_a, o_b, 0, 1)

    @pl.when((i <= n_used) & jnp.logical_not(even))
    def _():
        step(x_b, x_a, o_b, o_a, 1, 0)

    @pl.when((i == n_used + 1) & even)
    def _():
        pltpu.make_async_copy(h2_hbm.at[pl.ds(0, rows)], x_a, gsem.at[0]).wait()

    @pl.when((i == n_used + 1) & jnp.logical_not(even))
    def _():
        pltpu.make_async_copy(h2_hbm.at[pl.ds(0, rows)], x_b, gsem.at[1]).wait()


def _experts(h2, block_e, n_used, row_tok, row_dst, n_out_rows, w1, w3, w2):
    steps = block_e.shape[0]
    rows = EXPERT_ROWS
    D = h2.shape[1]
    smem_blk = functools.partial(pl.BlockSpec, (1, 1, rows), memory_space=pltpu.SMEM)
    grid_spec = pltpu.PrefetchScalarGridSpec(
        num_scalar_prefetch=2,
        grid=(steps,),
        in_specs=[
            smem_blk(lambda i, be, nu: (i, 0, 0)),
            smem_blk(lambda i, be, nu: (i + 1, 0, 0)),
            smem_blk(lambda i, be, nu: (i, 0, 0)),
            pl.BlockSpec(memory_space=pl.ANY),
            pl.BlockSpec((1, D, EXPERT_FF), lambda i, be, nu: (be[i], 0, 0)),
            pl.BlockSpec((1, D, EXPERT_FF), lambda i, be, nu: (be[i], 0, 0)),
            pl.BlockSpec((1, EXPERT_FF, D), lambda i, be, nu: (be[i], 0, 0)),
        ],
        out_specs=pl.BlockSpec(memory_space=pl.ANY),
        scratch_shapes=[pltpu.VMEM((rows, D), F32)] * 4 + [pltpu.SemaphoreType.DMA((2,)), pltpu.SemaphoreType.DMA(())],
    )
    return pl.pallas_call(
        _expert_kernel,
        grid_spec=grid_spec,
        out_shape=jax.ShapeDtypeStruct((n_out_rows, D), F32),
        compiler_params=pltpu.CompilerParams(
            dimension_semantics=("arbitrary",), vmem_limit_bytes=VMEM_LIMIT),
        name="experts",
    )(block_e, n_used, row_tok, row_tok, row_dst, h2, w1, w3, w2)


def _combine_kernel(y0_ref, y1_ref, x1_ref, route_ref, nf_ref, y_ref):
    route = route_ref[...]
    y = x1_ref[...] + route[:, 2:3] * y0_ref[...] + route[:, 3:4] * y1_ref[...]
    y_ref[...] = _rms(y, nf_ref[...])


def _combine(x1, route, y_rows, norm_f):
    N, D = x1.shape
    rows = PROJ_ROWS
    nt = N // rows
    return pl.pallas_call(
        _combine_kernel,
        grid=(nt,),
        in_specs=[
            pl.BlockSpec((rows, D), lambda i: (i, 0)),
            pl.BlockSpec((rows, D), lambda i: (i + nt, 0)),
            pl.BlockSpec((rows, D), lambda i: (i, 0)),
            pl.BlockSpec((rows, LANES), lambda i: (i, 0)),
            pl.BlockSpec((1, D), lambda i: (0, 0)),
        ],
        out_specs=pl.BlockSpec((rows, D), lambda i: (i, 0)),
        out_shape=jax.ShapeDtypeStruct((N, D), F32),
        compiler_params=pltpu.CompilerParams(
            dimension_semantics=("parallel",), vmem_limit_bytes=VMEM_LIMIT),
        name="combine",
    )(y_rows, y_rows, x1, route, norm_f[None, :])


def _dispatch_plan(expert_ids):
    N = expert_ids.shape[0]
    A = N * TOP_K
    blk = EXPERT_ROWS
    flat_e = expert_ids.T.reshape(A)
    _, order = lax.sort((flat_e, jnp.arange(A, dtype=jnp.int32)), num_keys=1)
    experts = jnp.arange(N_EXPERTS, dtype=jnp.int32)
    counts = jnp.sum((flat_e[None, :] == experts[:, None]).astype(jnp.int32), axis=1)
    seg_end = jnp.cumsum(counts)
    seg_start = seg_end - counts
    padded = ((counts + blk - 1) // blk) * blk
    pad_end = jnp.cumsum(padded)
    pad_start = pad_end - padded
    n_blocks = (A + N_EXPERTS * (blk - 1) + blk - 1) // blk
    block_start = jnp.arange(n_blocks, dtype=jnp.int32) * blk
    block_e = jnp.sum((pad_end[None, :] <= block_start[:, None]).astype(jnp.int32), axis=1)
    block_e = jnp.minimum(block_e, N_EXPERTS - 1)
    within =jnp.arange(blk, dtype=jnp.int32)[None, :]
    j = block_start[:, None] + within - pad_start[block_e][:, None]
    valid = j < counts[block_e][:, None]
    src = jnp.clip(seg_start[block_e][:, None] + j, 0, A - 1)
    assign = order[src]
    row_dst = jnp.where(valid, assign, A + within)
    row_tok = jnp.where(valid, assign % N, 0)
    n_used = (pad_end[-1] // blk).astype(jnp.int32).reshape(1)
    steps = n_blocks + 2
    spare = jnp.broadcast_to(A + within, (1, blk))
    row_tok = jnp.concatenate([row_tok, jnp.zeros((steps + 1 - n_blocks, blk), jnp.int32)], axis=0)
    row_dst = jnp.concatenate([spare, row_dst, spare], axis=0)
    block_e = jnp.concatenate([block_e, jnp.full((2,), N_EXPERTS - 1, jnp.int32)])
    return row_tok[:, None, :], row_dst[:, None, :], block_e, n_used, A + blk


def _trunk(x, norm1, w_in, b_in, conv_w, conv_b, m_fgate_bias, m_norm, hg_lb_logits, hg_norm, w_out, norm2,
           w_rg, b_rg, w_re, b_re, w1_bf, w3_bf, w2_bf, norm_f):
    B, T, D = x.shape
    N = B * T
    q, k, v, mo, gcol, grow, hq, g_f, g_b, hv, hgg = _in_proj(
        x, norm1, w_in, b_in, conv_w, conv_b, m_fgate_bias, hg_lb_logits)
    h_f, o_f, h_b, o_b = _mixer(q, k, v, gcol, grow, hq, g_f, g_b, hv)
    x1, h2, route = _merge(h_f, h_b, o_f, o_b, mo, hgg, x, m_norm, hg_norm, w_out, norm2, w_rg, b_rg, w_re, b_re)
    x1 = x1.reshape(N, D)
    h2 = h2.reshape(N, D)
    route = route.reshape(N, LANES)
    expert_ids = route[:, 0:TOP_K].astype(jnp.int32)
    row_tok, row_dst, block_e, n_used, n_out_rows = _dispatch_plan(expert_ids)
    y_rows = _experts(h2, block_e, n_used, row_tok, row_dst, n_out_rows, w1_bf, w3_bf, w2_bf)
    y = _combine(x1, route, y_rows, norm_f)
    return y.reshape(B, T, D)


def kernel(x_prompt, x_sample, norm1, w_in, b_in, conv_w, conv_b, m_fgate_bias, m_norm, hg_lb_logits, hg_norm,
           w_out, norm2, w_router_group, b_router_group, w_router_expert, b_router_expert, w1, w3, w2, norm_f):
    layer = 0
    w1_bf = w1[layer].astype(BF16)
    w3_bf = w3[layer].astype(BF16)
    w2_bf = w2[layer].astype(BF16)
    args = (norm1[layer], w_in[layer], b_in[layer], conv_w[layer], conv_b[layer], m_fgate_bias[layer],
            m_norm[layer], hg_lb_logits, hg_norm[layer], w_out[layer], norm2[layer],
            w_router_group[layer], b_router_group[layer], w_router_expert[layer], b_router_expert[layer],
            w1_bf, w3_bf, w2_bf, norm_f)
    return (_trunk(x_prompt, *args), _trunk(x_sample, *args))
```

```python
import functools

import jax
import jax.numpy as jnp
from jax import lax
from jax.experimental import pallas as pl
from jax.experimental.pallas import tpu as pltpu
from jax.experimental.pallas import tpu_sc as plsc

F32 = jnp.float32
BF16 = jnp.bfloat16

D_MODEL = 1024
N_HEADS = 4
D_HEAD = 128
WIDTH = N_HEADS * D_HEAD
CONV_K = 5
CONV_PAD = CONV_K // 2
N_GROUPS = 4
EXPERTS_PER_GROUP = 8
N_EXPERTS = N_GROUPS * EXPERTS_PER_GROUP
TOP_K = 2
EXPERT_FF = D_MODEL // 2
NORM_EPS = 1e-6

LANES = 128
SUBLANES = 8
CHUNK = 128
PROJ_ROWS = 256
HALO = SUBLANES
EXPERT_ROWS = 256
N_GATES = 4 * N_HEADS
SC_CORES = 2
SC_SUBCORES = 16
SC_WINDOW = 32
VMEM_LIMIT = 56 * 1024 * 1024


def _dot(a, b):
    return jnp.dot(a, b, preferred_element_type=F32)


def _dot_nt(a, b):
    return lax.dot_general(a, b, (((1,), (1,)), ((), ())), preferred_element_type=F32)


def _dot_tn(a, b):
    return lax.dot_general(a, b, (((0,), (0,)), ((), ())), preferred_element_type=F32)


def _split3(x):
    hi = x.astype(BF16)
    r1 = x - hi.astype(F32)
    mid = r1.astype(BF16)
    lo = (r1 - mid.astype(F32)).astype(BF16)
    return hi, mid, lo


def _silu(x):
    return x * jax.nn.sigmoid(x)


def _log_sigmoid(x):
    return -(jnp.maximum(-x, 0.0) + jnp.log1p(jnp.exp(-jnp.abs(x))))


def _rms(x, gain):
    return x * lax.rsqrt(jnp.mean(x * x, axis=-1, keepdims=True) + NORM_EPS) * gain


def _in_proj_kernel(x_ref, xp_ref, xn_ref, n1_ref, wa_ref, ba_ref, wg_ref, bg_ref, wgt_ref, bgt_ref,
                    fbrow_ref, fbcol_ref, wh_ref, bh_ref, cw_ref, cb_ref, lbl_ref,
                    q_ref, k_ref, v_ref, mo_ref, gcol_ref, grow_ref, hq_ref, gf_ref, gb_ref, hv_ref, hgg_ref,
                    ext_ref):
    t = pl.program_id(1)
    nt = pl.num_programs(1)
    rows = x_ref.shape[1]
    gain = n1_ref[...]

    h = _rms(x_ref[0], gain).astype(BF16)
    hp = _rms(xp_ref[0], gain).astype(BF16)
    hn = _rms(xn_ref[0], gain).astype(BF16)

    pa = _dot(h, wa_ref[...]) + ba_ref[...]
    wqk = wa_ref[:, 0:2 * WIDTH]
    bqk = ba_ref[:, 0:2 * WIDTH]
    has_prev = (t > 0).astype(F32)
    has_next = (t < nt - 1).astype(F32)
    ext_ref[0:HALO, :] = (_dot(hp, wqk) + bqk) * has_prev
    ext_ref[HALO:HALO + rows, :] = pa[:, 0:2 * WIDTH]
    ext_ref[HALO + rows:2 * HALO + rows, :] = (_dot(hn, wqk) + bqk) * has_next
    acc = cb_ref[...] + ext_ref[pl.ds(HALO - CONV_PAD, rows), :] * cw_ref[0:1, :]
    for j in range(1, CONV_K):
        acc = acc + ext_ref[pl.ds(HALO - CONV_PAD + j, rows), :] * cw_ref[j:j + 1, :]
    qk = _silu(acc)
    q_ref[0] = qk[:, 0:WIDTH] * (D_HEAD ** -0.5)
    k_ref[0] = qk[:, WIDTH:2 * WIDTH]
    v_ref[0] = pa[:, 2 * WIDTH:3 * WIDTH]
    mo_ref[0] = jax.nn.sigmoid(pa[:, 3 * WIDTH:4 * WIDTH])

    gc = _dot(h, wg_ref[...]) + bg_ref[...]
    lane = lax.broadcasted_iota(jnp.int32, gc.shape, 1)
    is_f = (lane >= 2 * N_HEADS) & (lane < N_GATES)
    gcol_ref[0] = jnp.where(is_f, _log_sigmoid(gc + fbrow_ref[...]), gc)
    gr = _dot_nt(wgt_ref[...], h) + bgt_ref[...]
    sub = lax.broadcasted_iota(jnp.int32, gr.shape, 0)
    grow_ref[0] = jnp.where(sub >= 2 * N_HEADS, _log_sigmoid(gr + fbcol_ref[...]), gr)

    ph = _dot(h, wh_ref[...]) + bh_ref[...]
    lbl = lbl_ref[...]
    lmax = jnp.max(lbl, axis=0, keepdims=True)
    le = jnp.exp(lbl - lmax)
    lb = le[0:1, :] / jnp.sum(le, axis=0, keepdims=True)
    hq_ref[0] = _silu(ph[:, 0:WIDTH])
    gf_ref[0] = lb + (1.0 - lb) * jax.nn.sigmoid(ph[:, WIDTH:2 * WIDTH])
    gb_ref[0] = lb + (1.0 - lb) * jax.nn.sigmoid(ph[:, 2 * WIDTH:3 * WIDTH])
    hv_ref[0] = ph[:, 3 * WIDTH:4 * WIDTH]
    hgg_ref[0] = _silu(ph[:, 4 * WIDTH:5 * WIDTH])


def _in_proj(x, norm1, w_in, b_in, conv_w, conv_b, fgate_bias, lb_logits):
    B, T, D = x.shape
    rows = PROJ_ROWS
    nt = T // rows
    a_w = 4 * WIDTH
    wa = w_in[:, 0:a_w].astype(BF16)
    ba = b_in[None, 0:a_w]
    wg32 = jnp.pad(w_in[:, a_w:a_w + N_GATES], ((0, 0), (0, LANES - N_GATES)))
    bg = jnp.pad(b_in[a_w:a_w + N_GATES], (0, LANES - N_GATES))[None, :]
    wg = wg32.astype(BF16)
    wgt = w_in[:, a_w:a_w + N_GATES].T.astype(BF16)
    bgt = b_in[a_w:a_w + N_GATES][:, None]
    fb = fgate_bias.reshape(2 * N_HEADS)
    fbrow = jnp.zeros((1, LANES), F32).at[0, 2 * N_HEADS:N_GATES].set(fb)
    fbcol = jnp.zeros((N_GATES, 1), F32).at[2 * N_HEADS:N_GATES, 0].set(fb)
    wh = w_in[:, a_w + N_GATES:].astype(BF16)
    bh = b_in[None, a_w + N_GATES:]

    tiles_per_halo = rows // HALO
    n_halo = T // HALO

    def full(arr):
        nd = arr.ndim
        return pl.BlockSpec(arr.shape, lambda b, t: (0,) * nd)

    def tok(width):
        return pl.BlockSpec((1, rows, width), lambda b, t: (b, t, 0))

    in_specs = [
        tok(D),
        pl.BlockSpec((1, HALO, D), lambda b, t: (b, jnp.maximum(t * tiles_per_halo - 1, 0), 0)),
        pl.BlockSpec((1, HALO, D), lambda b, t: (b, jnp.minimum((t + 1) * tiles_per_halo, n_halo - 1), 0)),
    ]
    consts = [norm1[None, :], wa, ba, wg, bg, wgt, bgt, fbrow, fbcol, wh, bh, conv_w, conv_b[None, :], lb_logits]
    in_specs += [full(c) for c in consts]
    tok_out = jax.ShapeDtypeStruct((B, T, WIDTH), F32)
    out_shape = [tok_out, tok_out, tok_out, tok_out,
                 jax.ShapeDtypeStruct((B, T, LANES), F32),
                 jax.ShapeDtypeStruct((B, N_GATES, T), F32),
                 tok_out, tok_out, tok_out, tok_out, tok_out]
    out_specs = [tok(WIDTH)] * 4 + [tok(LANES), pl.BlockSpec((1, N_GATES, rows), lambda b, t: (b, 0, t))] + [tok(WIDTH)] * 5
    return pl.pallas_call(
        _in_proj_kernel,
        grid=(B, nt),
        in_specs=in_specs,
        out_specs=out_specs,
        out_shape=out_shape,
        scratch_shapes=[pltpu.VMEM((rows + 2 * HALO, 2 * WIDTH), F32)],
        compiler_params=pltpu.CompilerParams(
            dimension_semantics=("parallel", "parallel"), vmem_limit_bytes=VMEM_LIMIT),
        name="in_proj",
    )(x, x, x, *consts)


def _cumsum_rows(tri_bf, x):
    hi, mid, lo = _split3(x)
    return _dot(tri_bf, hi) + _dot(tri_bf, mid) + _dot(tri_bf, lo)


def _cumsum_lanes(x, tri_bf):
    hi, mid, lo = _split3(x)
    return _dot(hi, tri_bf) + _dot(mid, tri_bf) + _dot(lo, tri_bf)


def _mlstm_chunk(q, k, vext, i_col, b_col, i_row, b_row, seen, last, c_ref, m_ref, idx):
    m_prev = m_ref[idx][:, 0:1]
    c_prev = c_ref[idx]
    q_bf = q.astype(BF16)
    log_d = jnp.where(seen, b_col - b_row + i_row, -jnp.inf)
    m_inter = b_col + m_prev
    m_t = jnp.maximum(m_inter, jnp.max(log_d, axis=-1, keepdims=True))
    scores = (_dot_nt(q_bf, k.astype(BF16)) * jnp.exp(log_d - m_t)).astype(BF16)
    inter_scale = jnp.exp(m_inter - m_t)
    numden = _dot(scores, vext) + inter_scale * _dot(q_bf, c_prev.astype(BF16))
    num = numden[:, 0:D_HEAD]
    den = numden[:, D_HEAD:2 * D_HEAD]
    h = num / jnp.maximum(jnp.abs(den), jnp.exp(-m_t))

    b_last = b_col[last:last + 1, :]
    log_w = b_last - b_col + i_col
    m_new = jnp.maximum(b_last + m_prev, jnp.max(log_w, axis=0, keepdims=True))
    w = jnp.exp(log_w - m_new)
    decay = jnp.exp(b_last + m_prev - m_new)
    c_ref[idx] = decay * c_prev + _dot_tn((k * w).astype(BF16), vext)
    m_ref[idx] = jnp.broadcast_to(m_new, (1, LANES))
    return h


def _hgrn2_level_small(q3, k3, pre3, suf3, half, rev, sub_iota):
    upper = (sub_iota & half) != 0
    second = jnp.logical_not(upper) if rev else upper
    end = 0 if rev else half - 1
    y = jnp.where((sub_iota & (half - 1)) == end, pre3, 0.0)
    step = 1 if rev else -1
    span = 1
    while span < half:
        y = y + pltpu.roll(y, (step * span) % SUBLANES, 1)
        span *= 2
    if 2 * half == SUBLANES:
        other = pltpu.roll(y, half, 1)
    else:
        other = jnp.where(upper, pltpu.roll(y, half, 1), pltpu.roll(y, SUBLANES - half, 1))
    z = jnp.where(second, q3 * pre3, k3 * suf3)
    return z, pre3 * jnp.where(second, other, 1.0), suf3 * jnp.where(second, 1.0, other)


def _hgrn2_level_big(q, k, pre, suf, half, rev):
    L = q.shape[0]
    shape = (L // (2 * half), 2, half, LANES)
    q4, k4, pre4, suf4 = (a.reshape(shape) for a in (q, k, pre, suf))
    first = 1 if rev else 0
    second = 1 - first
    end = 0 if rev else half - 1
    total_first = pre4[:, first, end:end + 1, :]
    total_second = pre4[:, second, end:end + 1, :]

    def join(at_first, at_second):
        parts = (at_second, at_first) if rev else (at_first, at_second)
        return jnp.stack(parts, axis=1).reshape(L, LANES)

    z = join(k4[:, first] * suf4[:, first], q4[:, second] * pre4[:, second])
    pre_new = join(pre4[:, first], pre4[:, second] * total_first)
    suf_new = join(suf4[:, first] * total_second, suf4[:, second])
    return z, pre_new, suf_new


def _hgrn2_chunk(q, g, v_bf, rev, level, diag, sub_iota, st_ref, idx):
    L = q.shape[0]
    k = 1.0 - g
    att = jnp.where(diag, _dot_nt(q.astype(BF16), k.astype(BF16)), 0.0)
    small = (L // SUBLANES, SUBLANES, LANES)
    q3, k3, pre, suf = q.reshape(small), k.reshape(small), g.reshape(small), jnp.ones(small, F32)
    half = 1
    bit = 0
    while half < L:
        if half == SUBLANES:
            pre, suf = pre.reshape(L, LANES), suf.reshape(L, LANES)
        if half < SUBLANES:
            z, pre, suf = _hgrn2_level_small(q3, k3, pre, suf, half, rev, sub_iota)
            z = z.reshape(L, LANES)
        else:
            z, pre, suf = _hgrn2_level_big(q, k, pre, suf, half, rev)
        z = z.astype(BF16)
        att = jnp.where(level == bit, _dot_nt(z, z), att)
        half *= 2
        bit += 1
    last = 0 if rev else L - 1
    st_prev = st_ref[idx]
    o = _dot_nt((q * pre).astype(BF16), st_prev.astype(BF16)) + _dot(att.astype(BF16), v_bf)
    st_ref[idx] = st_prev * pre[last:last + 1, :] + _dot_tn(v_bf, (k * suf).astype(BF16))
    return o


def _mixer_kernel(qf_ref, kf_ref, vf_ref, gcf_ref, grf_ref, hqf_ref, hgf_ref, hvf_ref,
                  qb_ref, kb_ref, vb_ref, gcb_ref, grb_ref, hqb_ref, hgb_ref, hvb_ref,
                  hf_ref, of_ref, hb_ref, ob_ref, c_ref, m_ref, st_ref):
    L = CHUNK

    @pl.when(pl.program_id(1) == 0)
    def _():
        c_ref[...] = jnp.zeros_like(c_ref)
        m_ref[...] = jnp.zeros_like(m_ref)
        st_ref[...] = jnp.zeros_like(st_ref)

    row = lax.broadcasted_iota(jnp.int32, (L, L), 0)
    col = lax.broadcasted_iota(jnp.int32, (L, L), 1)
    sub_iota = lax.broadcasted_iota(jnp.int32, (L // SUBLANES, SUBLANES, LANES), 1)
    diag = row == col
    diff = row ^ col
    high_bit = jnp.zeros((L, L), jnp.int32)
    half = 2
    while half < L:
        high_bit = high_bit + (diff >= half).astype(jnp.int32)
        half *= 2
    ones = jnp.ones((L, D_HEAD), BF16)

    dirs = (
        (0, qf_ref, kf_ref, vf_ref, gcf_ref, grf_ref, hqf_ref, hgf_ref, hvf_ref, hf_ref, of_ref),
        (1, qb_ref, kb_ref, vb_ref, gcb_ref, grb_ref, hqb_ref, hgb_ref, hvb_ref, hb_ref, ob_ref),
    )
    for d, q_ref, k_ref, v_ref, gc_ref, gr_ref, hq_ref, hg_ref, hv_ref, h_out, o_out in dirs:
        rev = d == 1
        seen = (col >= row) if rev else (col <= row)
        before = (col > row) if rev else (col < row)
        level = jnp.where(before, high_bit, -1)
        tri = seen.astype(BF16)
        tri_t = (row >= col if rev else row <= col).astype(BF16)
        last = 0 if rev else L - 1
        gc = gc_ref[0]
        gr = gr_ref[0]
        gc_cum = _cumsum_rows(tri, gc)
        gr_cum = _cumsum_lanes(gr, tri_t)
        for hd in range(N_HEADS):
            sl = slice(hd * D_HEAD, (hd + 1) * D_HEAD)
            gi = d * N_HEADS + hd
            gf = 2 * N_HEADS + gi
            idx = d * N_HEADS + hd
            vext = jnp.concatenate([v_ref[0, :, sl].astype(BF16), ones], axis=1)
            h_out[0, :, sl] = _mlstm_chunk(
                q_ref[0, :, sl], k_ref[0, :, sl], vext,
                gc[:, gi:gi + 1], gc_cum[:, gf:gf + 1], gr[gi:gi + 1, :], gr_cum[gf:gf + 1, :],
                seen, last, c_ref, m_ref, idx)
            o_out[0, :, sl] = _hgrn2_chunk(
                hq_ref[0, :, sl], hg_ref[0, :, sl], hv_ref[0, :, sl].astype(BF16),
                rev, level, diag, sub_iota, st_ref, idx)


def _mixer(q, k, v, gcol, grow, hq, g_f, g_b, hv):
    B, T, _ = q.shape
    L = CHUNK
    nc = T // L

    def fwd(width):
        return pl.BlockSpec((1, L, width), lambda b, c: (b, c, 0))

    def bwd(width):
        return pl.BlockSpec((1, L, width), lambda b, c: (b, nc - 1 - c, 0))

    grow_f = pl.BlockSpec((1, N_GATES, L), lambda b, c: (b, 0, c))
    grow_b = pl.BlockSpec((1, N_GATES, L), lambda b, c: (b, 0, nc - 1 - c))
    in_specs = ([fwd(WIDTH)] * 3 + [fwd(LANES), grow_f] + [fwd(WIDTH)] * 3
                + [bwd(WIDTH)] * 3 + [bwd(LANES), grow_b] + [bwd(WIDTH)] * 3)
    out = jax.ShapeDtypeStruct((B, T, WIDTH), F32)
    n_state = 2 * N_HEADS
    return pl.pallas_call(
        _mixer_kernel,
        grid=(B, nc),
        in_specs=in_specs,
        out_specs=[fwd(WIDTH), fwd(WIDTH), bwd(WIDTH), bwd(WIDTH)],
        out_shape=[out, out, out, out],
        scratch_shapes=[
            pltpu.VMEM((n_state, D_HEAD, 2 * D_HEAD), F32),
            pltpu.VMEM((n_state, 1, LANES), F32),
            pltpu.VMEM((n_state, D_HEAD, D_HEAD), F32),
        ],
        compiler_params=pltpu.CompilerParams(
            dimension_semantics=("parallel", "arbitrary"), vmem_limit_bytes=VMEM_LIMIT),
        name="mixer",
    )(q, k, v, gcol, grow, hq, g_f, hv, q, k, v, gcol, grow, hq, g_b, hv)


def _head_norm(hsum, gain):
    parts = []
    for hd in range(N_HEADS):
        hh = hsum[:, hd * D_HEAD:(hd + 1) * D_HEAD]
        parts.append(hh * lax.rsqrt(jnp.mean(hh * hh, axis=-1, keepdims=True) + NORM_EPS))
    return jnp.concatenate(parts, axis=1) * gain


def _merge_kernel(hf_ref, hb_ref, of_ref, ob_ref, mo_ref, hgg_ref, x_ref, mn_ref, hn_ref, wo_ref,
                  n2_ref, wrh_ref, wrl_ref, br_ref, x1_ref, h2_ref, route_ref):
    m_out = _head_norm(hf_ref[0] + hb_ref[0], mn_ref[...]) * mo_ref[0]
    hg_out = _head_norm(of_ref[0] + ob_ref[0], hn_ref[...]) * hgg_ref[0]
    mixed = jnp.concatenate([m_out, hg_out], axis=1).astype(BF16)
    x1 = x_ref[0] + _dot(mixed, wo_ref[...])
    x1_ref[0] = x1
    h2 = _rms(x1, n2_ref[...])
    h2_ref[0] = h2

    h_hi = h2.astype(BF16)
    h_lo = (h2 - h_hi.astype(F32)).astype(BF16)
    logits = _dot(h_hi, wrh_ref[...]) + _dot(h_lo, wrh_ref[...]) + _dot(h_hi, wrl_ref[...]) + br_ref[...]
    lane = lax.broadcasted_iota(jnp.int32, logits.shape, 1)
    big = jnp.int32(LANES)
    neg = -jnp.inf
    g_log = jnp.where(lane < N_GROUPS, logits, neg)
    g_max = jnp.max(g_log, axis=-1, keepdims=True)
    g_idx = jnp.min(jnp.where(g_log == g_max, lane, big), axis=-1, keepdims=True)
    g_val = 1.0 / jnp.sum(jnp.exp(g_log - g_max), axis=-1, keepdims=True)
    e_lo = N_GROUPS + g_idx * EXPERTS_PER_GROUP
    e_log = jnp.where((lane >= e_lo) & (lane < e_lo + EXPERTS_PER_GROUP), logits, neg)
    m1 = jnp.max(e_log, axis=-1, keepdims=True)
    i1 = jnp.min(jnp.where(e_log == m1, lane, big), axis=-1, keepdims=True)
    e_log2 = jnp.where(lane == i1, neg, e_log)
    m2 = jnp.max(e_log2, axis=-1, keepdims=True)
    i2 = jnp.min(jnp.where(e_log2 == m2, lane, big), axis=-1, keepdims=True)
    r2 = jnp.exp(m2 - m1)
    w1 = g_val / (1.0 + r2)
    w2 = g_val * r2 / (1.0 + r2)
    route = jnp.where(lane == 0, (i1 - N_GROUPS).astype(F32),
                      jnp.where(lane == 1, (i2 - N_GROUPS).astype(F32),
                                jnp.where(lane == 2, w1, jnp.where(lane == 3, w2, 0.0))))
    route_ref[0] = route


def _merge(h_f, h_b, o_f, o_b, mo, hgg, x, m_norm, hg_norm, w_out, norm2, w_rg, b_rg, w_re, b_re):
    B, T, D = x.shape
    rows = PROJ_ROWS
    n_log = N_GROUPS + N_EXPERTS
    wr = jnp.pad(jnp.concatenate([w_rg, w_re], axis=1), ((0, 0), (0, LANES - n_log)))
    br = jnp.pad(jnp.concatenate([b_rg, b_re]), (0, LANES - n_log))[None, :]
    wr_hi = wr.astype(BF16)
    wr_lo = (wr - wr_hi.astype(F32)).astype(BF16)
    consts = [m_norm[None, :], hg_norm[None, :], w_out.astype(BF16), norm2[None, :], wr_hi, wr_lo, br]

    def full(arr):
        nd = arr.ndim
        return pl.BlockSpec(arr.shape, lambda b, t: (0,) * nd)

    def tok(width):
        return pl.BlockSpec((1, rows, width), lambda b, t: (b, t, 0))

    return pl.pallas_call(
        _merge_kernel,
        grid=(B, T // rows),
        in_specs=[tok(WIDTH)] * 6 + [tok(D)] + [full(c) for c in consts],
        out_specs=[tok(D), tok(D), tok(LANES)],
        out_shape=[jax.ShapeDtypeStruct((B, T, D), F32), jax.ShapeDtypeStruct((B, T, D), F32),
                   jax.ShapeDtypeStruct((B, T, LANES), F32)],
        compiler_params=pltpu.CompilerParams(
            dimension_semantics=("parallel", "parallel"), vmem_limit_bytes=VMEM_LIMIT),
        name="merge",
    )(h_f, h_b, o_f, o_b, mo, hgg, x, *consts)


def _sc_gather_rows(src, idx):
    n_out = idx.shape[0]
    D = src.shape[1]
    n_sub = SC_CORES * SC_SUBCORES
    per = n_out // n_sub
    assert per * n_sub == n_out and per % SC_WINDOW == 0, (n_out, per)
    mesh = plsc.VectorSubcoreMesh(core_axis_name="c", subcore_axis_name="s",
                                  num_cores=SC_CORES, num_subcores=SC_SUBCORES)

    def body(src_hbm, idx_hbm, out_hbm, idx_v, buf):
        base = (lax.axis_index("c") * SC_SUBCORES + lax.axis_index("s")) * per
        pltpu.sync_copy(idx_hbm.at[pl.ds(base, per)], idx_v)

        @pl.loop(0, per // SC_WINDOW)
        def _(j):
            pltpu.sync_copy(src_hbm.at[idx_v.at[pl.ds(j * SC_WINDOW, SC_WINDOW)]], buf)
            pltpu.sync_copy(buf, out_hbm.at[pl.ds(base + j * SC_WINDOW, SC_WINDOW)])

    return pl.kernel(
        body,
        out_type=jax.ShapeDtypeStruct((n_out, D), src.dtype),
        mesh=mesh,
        scratch_types=[pltpu.VMEM((per,), jnp.int32), pltpu.VMEM((SC_WINDOW, D), src.dtype)],
        name="sc_gather_rows",
    )(src, idx)


def _expert_kernel(be_ref, nu_ref, x_ref, w1_ref, w3_ref, w2_ref, o_ref):
    @pl.when(pl.program_id(0) < nu_ref[0])
    def _():
        xb = x_ref[...].astype(BF16)
        hid = _silu(_dot(xb, w1_ref[0])) * _dot(xb, w3_ref[0])
        o_ref[...] = _dot(hid.astype(BF16), w2_ref[0])


def _experts(xs, block_e, n_used, w1, w3, w2):
    rows = EXPERT_ROWS
    n_blocks = xs.shape[0] // rows
    D = xs.shape[1]

    def blk(i, be, nu):
        return jnp.minimum(i, nu[0] - 1)

    grid_spec = pltpu.PrefetchScalarGridSpec(
        num_scalar_prefetch=2,
        grid=(n_blocks,),
        in_specs=[
            pl.BlockSpec((rows, D), lambda i, be, nu: (blk(i, be, nu), 0)),
            pl.BlockSpec((1, D, EXPERT_FF), lambda i, be, nu: (be[blk(i, be, nu)], 0, 0)),
            pl.BlockSpec((1, D, EXPERT_FF), lambda i, be, nu: (be[blk(i, be, nu)], 0, 0)),
            pl.BlockSpec((1, EXPERT_FF, D), lambda i, be, nu: (be[blk(i, be, nu)], 0, 0)),
        ],
        out_specs=pl.BlockSpec((rows, D), lambda i, be, nu: (blk(i, be, nu), 0)),
    )
    return pl.pallas_call(
        _expert_kernel,
        grid_spec=grid_spec,
        out_shape=jax.ShapeDtypeStruct(xs.shape, F32),
        compiler_params=pltpu.CompilerParams(
            dimension_semantics=("arbitrary",), vmem_limit_bytes=VMEM_LIMIT),
        name="experts",
    )(block_e, n_used, xs, w1, w3, w2)


def _combine_kernel(y0_ref, y1_ref, x1_ref, route_ref, nf_ref, y_ref):
    route = route_ref[...]
    y = x1_ref[...] + route[:, 2:3] * y0_ref[...] + route[:, 3:4] * y1_ref[...]
    y_ref[...] = _rms(y, nf_ref[...])


def _combine(x1, route, y_rows, norm_f):
    N, D = x1.shape
    rows = PROJ_ROWS
    nt = N // rows
    return pl.pallas_call(
        _combine_kernel,
        grid=(nt,),
        in_specs=[
            pl.BlockSpec((rows, D), lambda i: (i, 0)),
            pl.BlockSpec((rows, D), lambda i: (i + nt, 0)),
            pl.BlockSpec((rows, D), lambda i: (i, 0)),
            pl.BlockSpec((rows, LANES), lambda i: (i, 0)),
            pl.BlockSpec((1, D), lambda i: (0, 0)),
        ],
        out_specs=pl.BlockSpec((rows, D), lambda i: (i, 0)),
        out_shape=jax.ShapeDtypeStruct((N, D), F32),
        compiler_params=pltpu.CompilerParams(
            dimension_semantics=("parallel",), vmem_limit_bytes=VMEM_LIMIT),
        name="combine",
    )(y_rows, y_rows, x1, route, norm_f[None, :])


def _dispatch_plan(expert_ids):
    N = expert_ids.shape[0]
    A = N * TOP_K
    blk = EXPERT_ROWS
    flat_e = expert_ids.T.reshape(A)
    iota_a = jnp.arange(A, dtype=jnp.int32)
    sorted_e, order = lax.sort((flat_e, iota_a), num_keys=1)
    experts = jnp.arange(N_EXPERTS, dtype=jnp.int32)
    counts = jnp.sum((flat_e[None, :] == experts[:, None]).astype(jnp.int32), axis=1)
    seg_end = jnp.cumsum(counts)
    seg_start = seg_end - counts
    padded = ((counts + blk - 1) // blk) * blk
    pad_end = jnp.cumsum(padded)
    pad_start = pad_end - padded
    n_blocks = (A + N_EXPERTS * (blk - 1) + blk - 1) // blk
    block_start = jnp.arange(n_blocks, dtype=jnp.int32) * blk
    block_e = jnp.sum((pad_end[None, :] <= block_start[:, None]).astype(jnp.int32), axis=1)
    block_e = jnp.minimum(block_e, N_EXPERTS - 1)
    within =jnp.arange(blk, dtype=jnp.int32)[None, :]
    j = block_start[:, None] + within - pad_start[block_e][:, None]
    valid = j < counts[block_e][:, None]
    src = jnp.clip(seg_start[block_e][:, None] + j, 0, A - 1)
    row_tok = jnp.where(valid, order[src] % N, 0).reshape(n_blocks * blk)
    n_used = (pad_end[-1] // blk).astype(jnp.int32).reshape(1)
    shift = pad_start - seg_start
    pos = iota_a + jnp.sum(jnp.where(sorted_e[:, None] == experts[None, :], shift[None, :], 0), axis=1)
    _, dest = lax.sort((order, pos.astype(jnp.int32)), num_keys=1)
    return row_tok, dest, block_e, n_used


def _trunk(x, norm1, w_in, b_in, conv_w, conv_b, m_fgate_bias, m_norm, hg_lb_logits, hg_norm, w_out, norm2,
           w_rg, b_rg, w_re, b_re, w1_bf, w3_bf, w2_bf, norm_f):
    B, T, D = x.shape
    N = B * T
    q, k, v, mo, gcol, grow, hq, g_f, g_b, hv, hgg = _in_proj(
        x, norm1, w_in, b_in, conv_w, conv_b, m_fgate_bias, hg_lb_logits)
    h_f, o_f, h_b, o_b = _mixer(q, k, v, gcol, grow, hq, g_f, g_b, hv)
    x1, h2, route = _merge(h_f, h_b, o_f, o_b, mo, hgg, x, m_norm, hg_norm, w_out, norm2, w_rg, b_rg, w_re, b_re)
    x1 = x1.reshape(N, D)
    h2 = h2.reshape(N, D)
    route = route.reshape(N, LANES)
    expert_ids = route[:, 0:TOP_K].astype(jnp.int32)
    row_tok, dest, block_e, n_used = _dispatch_plan(expert_ids)
    xs = _sc_gather_rows(h2, row_tok)
    out_rows = _experts(xs, block_e, n_used, w1_bf, w3_bf, w2_bf)
    y_rows = _sc_gather_rows(out_rows, dest)
    y = _combine(x1, route, y_rows, norm_f)
    return y.reshape(B, T, D)


def kernel(x_prompt, x_sample, norm1, w_in, b_in, conv_w, conv_b, m_fgate_bias, m_norm, hg_lb_logits, hg_norm,
           w_out, norm2, w_router_group, b_router_group, w_router_expert, b_router_expert, w1, w3, w2, norm_f):
    layer = 0
    w1_bf = w1[layer].astype(BF16)
    w3_bf = w3[layer].astype(BF16)
    w2_bf = w2[layer].astype(BF16)
    args = (norm1[layer], w_in[layer], b_in[layer], conv_w[layer], conv_b[layer], m_fgate_bias[layer],
            m_norm[layer], hg_lb_logits, hg_norm[layer], w_out[layer], norm2[layer],
            w_router_group[layer], b_router_group[layer], w_router_expert[layer], b_router_expert[layer],
            w1_bf, w3_bf, w2_bf, norm_f)
    return (_trunk(x_prompt, *args), _trunk(x_sample, *args))
```

```python
import functools

import jax
import jax.numpy as jnp
from jax import lax
from jax.experimental import pallas as pl
from jax.experimental.pallas import tpu as pltpu
from jax.experimental.pallas import tpu_sc as plsc

F32 = jnp.float32
BF16 = jnp.bfloat16

D_MODEL = 1024
N_HEADS = 4
D_HEAD = 128
WIDTH = N_HEADS * D_HEAD
CONV_K = 5
CONV_PAD = CONV_K // 2
N_GROUPS = 4
EXPERTS_PER_GROUP = 8
N_EXPERTS = N_GROUPS * EXPERTS_PER_GROUP
TOP_K = 2
EXPERT_FF = D_MODEL // 2
NORM_EPS = 1e-6

LANES = 128
SUBLANES = 8
CHUNK = 128
PROJ_ROWS = 256
HALO = SUBLANES
EXPERT_ROWS = 256
N_GATES = 4 * N_HEADS
SC_CORES = 2
SC_SUBCORES = 16
SC_WINDOW_BYTES = 128 * 1024
VMEM_LIMIT = 56 * 1024 * 1024


def _dot(a, b):
    return jnp.dot(a, b, preferred_element_type=F32)


def _dot_nt(a, b):
    return lax.dot_general(a, b, (((1,), (1,)), ((), ())), preferred_element_type=F32)


def _dot_tn(a, b):
    return lax.dot_general(a, b, (((0,), (0,)), ((), ())), preferred_element_type=F32)


def _split3(x):
    hi = x.astype(BF16)
    r1 = x - hi.astype(F32)
    mid = r1.astype(BF16)
    lo = (r1 - mid.astype(F32)).astype(BF16)
    return hi, mid, lo


def _silu(x):
    return x * jax.nn.sigmoid(x)


def _log_sigmoid(x):
    return -(jnp.maximum(-x, 0.0) + jnp.log1p(jnp.exp(-jnp.abs(x))))


def _rms(x, gain):
    return x * lax.rsqrt(jnp.mean(x * x, axis=-1, keepdims=True) + NORM_EPS) * gain


def _in_proj_kernel(x_ref, xp_ref, xn_ref, n1_ref, wa_ref, ba_ref, wg_ref, bg_ref, wgt_ref, bgt_ref,
                    fbrow_ref, fbcol_ref, wh_ref, bh_ref, cw_ref, cb_ref, lbl_ref,
                    q_ref, k_ref, v_ref, mo_ref, gcol_ref, grow_ref, hq_ref, gf_ref, gb_ref, hv_ref, hgg_ref,
                    ext_ref):
    t = pl.program_id(1)
    nt = pl.num_programs(1)
    rows = x_ref.shape[1]
    gain = n1_ref[...]

    h = _rms(x_ref[0], gain).astype(BF16)
    hp = _rms(xp_ref[0], gain).astype(BF16)
    hn = _rms(xn_ref[0], gain).astype(BF16)

    pa = _dot(h, wa_ref[...]) + ba_ref[...]
    wqk = wa_ref[:, 0:2 * WIDTH]
    bqk = ba_ref[:, 0:2 * WIDTH]
    has_prev = (t > 0).astype(F32)
    has_next = (t < nt - 1).astype(F32)
    ext_ref[0:HALO, :] = (_dot(hp, wqk) + bqk) * has_prev
    ext_ref[HALO:HALO + rows, :] = pa[:, 0:2 * WIDTH]
    ext_ref[HALO + rows:2 * HALO + rows, :] = (_dot(hn, wqk) + bqk) * has_next
    acc = cb_ref[...] + ext_ref[pl.ds(HALO - CONV_PAD, rows), :] * cw_ref[0:1, :]
    for j in range(1, CONV_K):
        acc = acc + ext_ref[pl.ds(HALO - CONV_PAD + j, rows), :] * cw_ref[j:j + 1, :]
    qk = _silu(acc)
    q_ref[0] = qk[:, 0:WIDTH] * (D_HEAD ** -0.5)
    k_ref[0] = qk[:, WIDTH:2 * WIDTH]
    v_ref[0] = pa[:, 2 * WIDTH:3 * WIDTH]
    mo_ref[0] = jax.nn.sigmoid(pa[:, 3 * WIDTH:4 * WIDTH])

    gc = _dot(h, wg_ref[...]) + bg_ref[...]
    lane = lax.broadcasted_iota(jnp.int32, gc.shape, 1)
    is_f = (lane >= 2 * N_HEADS) & (lane < N_GATES)
    gcol_ref[0] = jnp.where(is_f, _log_sigmoid(gc + fbrow_ref[...]), gc)
    gr = _dot_nt(wgt_ref[...], h) + bgt_ref[...]
    sub = lax.broadcasted_iota(jnp.int32, gr.shape, 0)
    grow_ref[0] = jnp.where(sub >= 2 * N_HEADS, _log_sigmoid(gr + fbcol_ref[...]), gr)

    ph = _dot(h, wh_ref[...]) + bh_ref[...]
    lbl = lbl_ref[...]
    lmax = jnp.max(lbl, axis=0, keepdims=True)
    le = jnp.exp(lbl - lmax)
    lb = le[0:1, :] / jnp.sum(le, axis=0, keepdims=True)
    hq_ref[0] = _silu(ph[:, 0:WIDTH])
    gf_ref[0] = lb + (1.0 - lb) * jax.nn.sigmoid(ph[:, WIDTH:2 * WIDTH])
    gb_ref[0] = lb + (1.0 - lb) * jax.nn.sigmoid(ph[:, 2 * WIDTH:3 * WIDTH])
    hv_ref[0] = ph[:, 3 * WIDTH:4 * WIDTH]
    hgg_ref[0] = _silu(ph[:, 4 * WIDTH:5 * WIDTH])


def _in_proj(x, norm1, w_in, b_in, conv_w, conv_b, fgate_bias, lb_logits):
    B, T, D = x.shape
    rows = PROJ_ROWS
    nt = T // rows
    a_w = 4 * WIDTH
    wa = w_in[:, 0:a_w].astype(BF16)
    ba = b_in[None, 0:a_w]
    wg32 = jnp.pad(w_in[:, a_w:a_w + N_GATES], ((0, 0), (0, LANES - N_GATES)))
    bg = jnp.pad(b_in[a_w:a_w + N_GATES], (0, LANES - N_GATES))[None, :]
    wg = wg32.astype(BF16)
    wgt = w_in[:, a_w:a_w + N_GATES].T.astype(BF16)
    bgt = b_in[a_w:a_w + N_GATES][:, None]
    fb = fgate_bias.reshape(2 * N_HEADS)
    fbrow = jnp.zeros((1, LANES), F32).at[0, 2 * N_HEADS:N_GATES].set(fb)
    fbcol = jnp.zeros((N_GATES, 1), F32).at[2 * N_HEADS:N_GATES, 0].set(fb)
    wh = w_in[:, a_w + N_GATES:].astype(BF16)
    bh = b_in[None, a_w + N_GATES:]

    tiles_per_halo = rows // HALO
    n_halo = T // HALO

    def full(arr):
        nd = arr.ndim
        return pl.BlockSpec(arr.shape, lambda b, t: (0,) * nd)

    def tok(width):
        return pl.BlockSpec((1, rows, width), lambda b, t: (b, t, 0))

    in_specs = [
        tok(D),
        pl.BlockSpec((1, HALO, D), lambda b, t: (b, jnp.maximum(t * tiles_per_halo - 1, 0), 0)),
        pl.BlockSpec((1, HALO, D), lambda b, t: (b, jnp.minimum((t + 1) * tiles_per_halo, n_halo - 1), 0)),
    ]
    consts = [norm1[None, :], wa, ba, wg, bg, wgt, bgt, fbrow, fbcol, wh, bh, conv_w, conv_b[None, :], lb_logits]
    in_specs += [full(c) for c in consts]
    tok_out = jax.ShapeDtypeStruct((B, T, WIDTH), F32)
    out_shape = [tok_out, tok_out, tok_out, tok_out,
                 jax.ShapeDtypeStruct((B, T, LANES), F32),
                 jax.ShapeDtypeStruct((B, N_GATES, T), F32),
                 tok_out, tok_out, tok_out, tok_out, tok_out]
    out_specs = [tok(WIDTH)] * 4 + [tok(LANES), pl.BlockSpec((1, N_GATES, rows), lambda b, t: (b, 0, t))] + [tok(WIDTH)] * 5
    return pl.pallas_call(
        _in_proj_kernel,
        grid=(B, nt),
        in_specs=in_specs,
        out_specs=out_specs,
        out_shape=out_shape,
        scratch_shapes=[pltpu.VMEM((rows + 2 * HALO, 2 * WIDTH), F32)],
        compiler_params=pltpu.CompilerParams(
            dimension_semantics=("parallel", "parallel"), vmem_limit_bytes=VMEM_LIMIT),
        name="in_proj",
    )(x, x, x, *consts)


def _cumsum_rows(tri_bf, x):
    hi, mid, lo = _split3(x)
    return _dot(tri_bf, hi) + _dot(tri_bf, mid) + _dot(tri_bf, lo)


def _cumsum_lanes(x, tri_bf):
    hi, mid, lo = _split3(x)
    return _dot(hi, tri_bf) + _dot(mid, tri_bf) + _dot(lo, tri_bf)


def _mlstm_chunk(q, k, vext, i_col, b_col, i_row, b_row, seen, last, c_ref, m_ref, idx):
    m_prev = m_ref[idx][:, 0:1]
    c_prev = c_ref[idx]
    q_bf = q.astype(BF16)
    log_d = jnp.where(seen, b_col - b_row + i_row, -jnp.inf)
    m_inter = b_col + m_prev
    m_t = jnp.maximum(m_inter, jnp.max(log_d, axis=-1, keepdims=True))
    scores = (_dot_nt(q_bf, k.astype(BF16)) * jnp.exp(log_d - m_t)).astype(BF16)
    inter_scale = jnp.exp(m_inter - m_t)
    numden = _dot(scores, vext) + inter_scale * _dot(q_bf, c_prev.astype(BF16))
    num = numden[:, 0:D_HEAD]
    den = numden[:, D_HEAD:2 * D_HEAD]
    h = num / jnp.maximum(jnp.abs(den), jnp.exp(-m_t))

    b_last = b_col[last:last + 1, :]
    log_w = b_last - b_col + i_col
    m_new = jnp.maximum(b_last + m_prev, jnp.max(log_w, axis=0, keepdims=True))
    w = jnp.exp(log_w - m_new)
    decay = jnp.exp(b_last + m_prev - m_new)
    c_ref[idx] = decay * c_prev + _dot_tn((k * w).astype(BF16), vext)
    m_ref[idx] = jnp.broadcast_to(m_new, (1, LANES))
    return h


def _hgrn2_level_small(q3, k3, pre3, suf3, half, rev, sub_iota):
    upper = (sub_iota & half) != 0
    second = jnp.logical_not(upper) if rev else upper
    end = 0 if rev else half - 1
    y = jnp.where((sub_iota & (half - 1)) == end, pre3, 0.0)
    step = 1 if rev else -1
    span = 1
    while span < half:
        y = y + pltpu.roll(y, (step * span) % SUBLANES, 1)
        span *= 2
    if 2 * half == SUBLANES:
        other = pltpu.roll(y, half, 1)
    else:
        other = jnp.where(upper, pltpu.roll(y, half, 1), pltpu.roll(y, SUBLANES - half, 1))
    z = jnp.where(second, q3 * pre3, k3 * suf3)
    return z, pre3 * jnp.where(second, other, 1.0), suf3 * jnp.where(second, 1.0, other)


def _hgrn2_level_big(q, k, pre, suf, half, rev):
    L = q.shape[0]
    shape = (L // (2 * half), 2, half, LANES)
    q4, k4, pre4, suf4 = (a.reshape(shape) for a in (q, k, pre, suf))
    first = 1 if rev else 0
    second = 1 - first
    end = 0 if rev else half - 1
    total_first = pre4[:, first, end:end + 1, :]
    total_second = pre4[:, second, end:end + 1, :]

    def join(at_first, at_second):
        parts = (at_second, at_first) if rev else (at_first, at_second)
        return jnp.stack(parts, axis=1).reshape(L, LANES)

    z = join(k4[:, first] * suf4[:, first], q4[:, second] * pre4[:, second])
    pre_new = join(pre4[:, first], pre4[:, second] * total_first)
    suf_new = join(suf4[:, first] * total_second, suf4[:, second])
    return z, pre_new, suf_new


def _hgrn2_chunk(q, g, v_bf, rev, level, diag, sub_iota, st_ref, idx):
    L = q.shape[0]
    k = 1.0 - g
    att = jnp.where(diag, _dot_nt(q.astype(BF16), k.astype(BF16)), 0.0)
    small = (L // SUBLANES, SUBLANES, LANES)
    q3, k3, pre, suf = q.reshape(small), k.reshape(small), g.reshape(small), jnp.ones(small, F32)
    half = 1
    bit = 0
    while half < L:
        if half == SUBLANES:
            pre, suf = pre.reshape(L, LANES), suf.reshape(L, LANES)
        if half < SUBLANES:
            z, pre, suf = _hgrn2_level_small(q3, k3, pre, suf, half, rev, sub_iota)
            z = z.reshape(L, LANES)
        else:
            z, pre, suf = _hgrn2_level_big(q, k, pre, suf, half, rev)
        z = z.astype(BF16)
        att = jnp.where(level == bit, _dot_nt(z, z), att)
        half *= 2
        bit += 1
    last = 0 if rev else L - 1
    st_prev = st_ref[idx]
    o = _dot_nt((q * pre).astype(BF16), st_prev.astype(BF16)) + _dot(att.astype(BF16), v_bf)
    st_ref[idx] = st_prev * pre[last:last + 1, :] + _dot_tn(v_bf, (k * suf).astype(BF16))
    return o


def _mixer_kernel(qf_ref, kf_ref, vf_ref, gcf_ref, grf_ref, hqf_ref, hgf_ref, hvf_ref,
                  qb_ref, kb_ref, vb_ref, gcb_ref, grb_ref, hqb_ref, hgb_ref, hvb_ref,
                  hf_ref, of_ref, hb_ref, ob_ref, c_ref, m_ref, st_ref):
    L = CHUNK

    @pl.when(pl.program_id(1) == 0)
    def _():
        c_ref[...] = jnp.zeros_like(c_ref)
        m_ref[...] = jnp.zeros_like(m_ref)
        st_ref[...] = jnp.zeros_like(st_ref)

    row = lax.broadcasted_iota(jnp.int32, (L, L), 0)
    col = lax.broadcasted_iota(jnp.int32, (L, L), 1)
    sub_iota = lax.broadcasted_iota(jnp.int32, (L // SUBLANES, SUBLANES, LANES), 1)
    diag = row == col
    diff = row ^ col
    high_bit = jnp.zeros((L, L), jnp.int32)
    half = 2
    while half < L:
        high_bit = high_bit + (diff >= half).astype(jnp.int32)
        half *= 2
    ones = jnp.ones((L, D_HEAD), BF16)

    dirs = (
        (0, qf_ref, kf_ref, vf_ref, gcf_ref, grf_ref, hqf_ref, hgf_ref, hvf_ref, hf_ref, of_ref),
        (1, qb_ref, kb_ref, vb_ref, gcb_ref, grb_ref, hqb_ref, hgb_ref, hvb_ref, hb_ref, ob_ref),
    )
    for d, q_ref, k_ref, v_ref, gc_ref, gr_ref, hq_ref, hg_ref, hv_ref, h_out, o_out in dirs:
        rev = d == 1
        seen = (col >= row) if rev else (col <= row)
        before = (col > row) if rev else (col < row)
        level = jnp.where(before, high_bit, -1)
        tri = seen.astype(BF16)
        tri_t = (row >= col if rev else row <= col).astype(BF16)
        last = 0 if rev else L - 1
        gc = gc_ref[0]
        gr = gr_ref[0]
        gc_cum = _cumsum_rows(tri, gc)
        gr_cum = _cumsum_lanes(gr, tri_t)
        for hd in range(N_HEADS):
            sl = slice(hd * D_HEAD, (hd + 1) * D_HEAD)
            gi = d * N_HEADS + hd
            gf = 2 * N_HEADS + gi
            idx = d * N_HEADS + hd
            vext = jnp.concatenate([v_ref[0, :, sl].astype(BF16), ones], axis=1)
            h_out[0, :, sl] = _mlstm_chunk(
                q_ref[0, :, sl], k_ref[0, :, sl], vext,
                gc[:, gi:gi + 1], gc_cum[:, gf:gf + 1], gr[gi:gi + 1, :], gr_cum[gf:gf + 1, :],
                seen, last, c_ref, m_ref, idx)
            o_out[0, :, sl] = _hgrn2_chunk(
                hq_ref[0, :, sl], hg_ref[0, :, sl], hv_ref[0, :, sl].astype(BF16),
                rev, level, diag, sub_iota, st_ref, idx)


def _mixer(q, k, v, gcol, grow, hq, g_f, g_b, hv):
    B, T, _ = q.shape
    L = CHUNK
    nc = T // L

    def fwd(width):
        return pl.BlockSpec((1, L, width), lambda b, c: (b, c, 0))

    def bwd(width):
        return pl.BlockSpec((1, L, width), lambda b, c: (b, nc - 1 - c, 0))

    grow_f = pl.BlockSpec((1, N_GATES, L), lambda b, c: (b, 0, c))
    grow_b = pl.BlockSpec((1, N_GATES, L), lambda b, c: (b, 0, nc - 1 - c))
    in_specs = ([fwd(WIDTH)] * 3 + [fwd(LANES), grow_f] + [fwd(WIDTH)] * 3
                + [bwd(WIDTH)] * 3 + [bwd(LANES), grow_b] + [bwd(WIDTH)] * 3)
    out = jax.ShapeDtypeStruct((B, T, WIDTH), F32)
    n_state = 2 * N_HEADS
    return pl.pallas_call(
        _mixer_kernel,
        grid=(B, nc),
        in_specs=in_specs,
        out_specs=[fwd(WIDTH), fwd(WIDTH), bwd(WIDTH), bwd(WIDTH)],
        out_shape=[out, out, out, out],
        scratch_shapes=[
            pltpu.VMEM((n_state, D_HEAD, 2 * D_HEAD), F32),
            pltpu.VMEM((n_state, 1, LANES), F32),
            pltpu.VMEM((n_state, D_HEAD, D_HEAD), F32),
        ],
        compiler_params=pltpu.CompilerParams(
            dimension_semantics=("parallel", "arbitrary"), vmem_limit_bytes=VMEM_LIMIT),
        name="mixer",
    )(q, k, v, gcol, grow, hq, g_f, hv, q, k, v, gcol, grow, hq, g_b, hv)


def _head_norm(hsum, gain):
    parts = []
    for hd in range(N_HEADS):
        hh = hsum[:, hd * D_HEAD:(hd + 1) * D_HEAD]
        parts.append(hh * lax.rsqrt(jnp.mean(hh * hh, axis=-1, keepdims=True) + NORM_EPS))
    return jnp.concatenate(parts, axis=1) * gain


def _merge_kernel(hf_ref, hb_ref, of_ref, ob_ref, mo_ref, hgg_ref, x_ref, mn_ref, hn_ref, wo_ref,
                  n2_ref, wrh_ref, wrl_ref, br_ref, x1_ref, h2_ref, route_ref):
    m_out = _head_norm(hf_ref[0] + hb_ref[0], mn_ref[...]) * mo_ref[0]
    hg_out = _head_norm(of_ref[0] + ob_ref[0], hn_ref[...]) * hgg_ref[0]
    mixed = jnp.concatenate([m_out, hg_out], axis=1).astype(BF16)
    x1 = x_ref[0] + _dot(mixed, wo_ref[...])
    x1_ref[0] = x1
    h2 = _rms(x1, n2_ref[...])
    h_hi = h2.astype(BF16)
    h_hi32 = h_hi.astype(F32)
    bits = lax.bitcast_convert_type(h_hi32, jnp.uint32)
    half = D_MODEL // 2
    h2_ref[0] = (bits[:, half:] & jnp.uint32(0xFFFF0000)) | (bits[:, :half] >> 16)

    h_lo = (h2 - h_hi32).astype(BF16)
    logits = _dot(h_hi, wrh_ref[...]) + _dot(h_lo, wrh_ref[...]) + _dot(h_hi, wrl_ref[...]) + br_ref[...]
    lane = lax.broadcasted_iota(jnp.int32, logits.shape, 1)
    big = jnp.int32(LANES)
    neg = -jnp.inf
    g_log = jnp.where(lane < N_GROUPS, logits, neg)
    g_max = jnp.max(g_log, axis=-1, keepdims=True)
    g_idx = jnp.min(jnp.where(g_log == g_max, lane, big), axis=-1, keepdims=True)
    g_val = 1.0 / jnp.sum(jnp.exp(g_log - g_max), axis=-1, keepdims=True)
    e_lo = N_GROUPS + g_idx * EXPERTS_PER_GROUP
    e_log = jnp.where((lane >= e_lo) & (lane < e_lo + EXPERTS_PER_GROUP), logits, neg)
    m1 = jnp.max(e_log, axis=-1, keepdims=True)
    i1 = jnp.min(jnp.where(e_log == m1, lane, big), axis=-1, keepdims=True)
    e_log2 = jnp.where(lane == i1, neg, e_log)
    m2 = jnp.max(e_log2, axis=-1, keepdims=True)
    i2 = jnp.min(jnp.where(e_log2 == m2, lane, big), axis=-1, keepdims=True)
    r2 = jnp.exp(m2 - m1)
    w1 = g_val / (1.0 + r2)
    w2 = g_val * r2 / (1.0 + r2)
    route = jnp.where(lane == 0, (i1 - N_GROUPS).astype(F32),
                      jnp.where(lane == 1, (i2 - N_GROUPS).astype(F32),
                                jnp.where(lane == 2, w1, jnp.where(lane == 3, w2, 0.0))))
    route_ref[0] = route


def _merge(h_f, h_b, o_f, o_b, mo, hgg, x, m_norm, hg_norm, w_out, norm2, w_rg, b_rg, w_re, b_re):
    B, T, D = x.shape
    rows = PROJ_ROWS
    n_log = N_GROUPS + N_EXPERTS
    wr = jnp.pad(jnp.concatenate([w_rg, w_re], axis=1), ((0, 0), (0, LANES - n_log)))
    br = jnp.pad(jnp.concatenate([b_rg, b_re]), (0, LANES - n_log))[None, :]
    wr_hi = wr.astype(BF16)
    wr_lo = (wr - wr_hi.astype(F32)).astype(BF16)
    consts = [m_norm[None, :], hg_norm[None, :], w_out.astype(BF16), norm2[None, :], wr_hi, wr_lo, br]

    def full(arr):
        nd = arr.ndim
        return pl.BlockSpec(arr.shape, lambda b, t: (0,) * nd)

    def tok(width):
        return pl.BlockSpec((1, rows, width), lambda b, t: (b, t, 0))

    return pl.pallas_call(
        _merge_kernel,
        grid=(B, T // rows),
        in_specs=[tok(WIDTH)] * 6 + [tok(D)] + [full(c) for c in consts],
        out_specs=[tok(D), tok(D // 2), tok(LANES)],
        out_shape=[jax.ShapeDtypeStruct((B, T, D), F32), jax.ShapeDtypeStruct((B, T, D // 2), jnp.uint32),
                   jax.ShapeDtypeStruct((B, T, LANES), F32)],
        compiler_params=pltpu.CompilerParams(
            dimension_semantics=("parallel", "parallel"), vmem_limit_bytes=VMEM_LIMIT),
        name="merge",
    )(h_f, h_b, o_f, o_b, mo, hgg, x, *consts)


def _sc_gather_rows(src, idx):
    n_out = idx.shape[0]
    D = src.shape[1]
    n_sub = SC_CORES * SC_SUBCORES
    per = n_out // n_sub
    window = SC_WINDOW_BYTES // (D * src.dtype.itemsize)
    assert per * n_sub == n_out and per % window == 0, (n_out, per, window)
    mesh = plsc.VectorSubcoreMesh(core_axis_name="c", subcore_axis_name="s",
                                  num_cores=SC_CORES, num_subcores=SC_SUBCORES)

    def body(src_hbm, idx_hbm, out_hbm, idx_v, buf):
        base = (lax.axis_index("c") * SC_SUBCORES + lax.axis_index("s")) * per
        pltpu.sync_copy(idx_hbm.at[pl.ds(base, per)], idx_v)

        @pl.loop(0, per // window)
        def _(j):
            pltpu.sync_copy(src_hbm.at[idx_v.at[pl.ds(j * window, window)]], buf)
            pltpu.sync_copy(buf, out_hbm.at[pl.ds(base + j * window, window)])

    return pl.kernel(
        body,
        out_type=jax.ShapeDtypeStruct((n_out, D), src.dtype),
        mesh=mesh,
        scratch_types=[pltpu.VMEM((per,), jnp.int32), pltpu.VMEM((window, D), src.dtype)],
        name="sc_gather_rows",
    )(src, idx)


def _expert_kernel(be_ref, nu_ref, x_ref, w1_ref, w3_ref, w2_ref, o_ref, w1_bf, w3_bf, w2_bf):
    i = pl.program_id(0)
    active = i < nu_ref[0]
    new_expert = (i == 0) | (be_ref[i] != be_ref[jnp.maximum(i - 1, 0)])

    @pl.when(active & new_expert)
    def _():
        w1_bf[...] = w1_ref[0].astype(BF16)
        w3_bf[...] = w3_ref[0].astype(BF16)
        w2_bf[...] = w2_ref[0].astype(BF16)

    @pl.when(active)
    def _():
        half = D_MODEL // 2
        words = x_ref[...]
        x_lo = lax.bitcast_convert_type(words << 16, F32).astype(BF16)
        x_hi = lax.bitcast_convert_type(words & jnp.uint32(0xFFFF0000), F32).astype(BF16)
        a = _dot(x_lo, w1_bf[0:half, :]) + _dot(x_hi, w1_bf[half:, :])
        b = _dot(x_lo, w3_bf[0:half, :]) + _dot(x_hi, w3_bf[half:, :])
        o_ref[...] = _dot((_silu(a) * b).astype(BF16), w2_bf[...])


def _experts(xs, block_e, n_used, w1, w3, w2):
    rows = EXPERT_ROWS
    n_blocks = xs.shape[0] // rows
    D = D_MODEL

    def blk(i, be, nu):
        return jnp.minimum(i, nu[0] - 1)

    grid_spec = pltpu.PrefetchScalarGridSpec(
        num_scalar_prefetch=2,
        grid=(n_blocks,),
        in_specs=[
            pl.BlockSpec((rows, D // 2), lambda i, be, nu: (blk(i, be, nu), 0)),
            pl.BlockSpec((1, D, EXPERT_FF), lambda i, be, nu: (be[blk(i, be, nu)], 0, 0)),
            pl.BlockSpec((1, D, EXPERT_FF), lambda i, be, nu: (be[blk(i, be, nu)], 0, 0)),
            pl.BlockSpec((1, EXPERT_FF, D), lambda i, be, nu: (be[blk(i, be, nu)], 0, 0)),
        ],
        out_specs=pl.BlockSpec((rows, D), lambda i, be, nu: (blk(i, be, nu), 0)),
        scratch_shapes=[pltpu.VMEM((D, EXPERT_FF), BF16), pltpu.VMEM((D, EXPERT_FF), BF16),
                        pltpu.VMEM((EXPERT_FF, D), BF16)],
    )
    return pl.pallas_call(
        _expert_kernel,
        grid_spec=grid_spec,
        out_shape=jax.ShapeDtypeStruct((xs.shape[0], D), F32),
        compiler_params=pltpu.CompilerParams(
            dimension_semantics=("arbitrary",), vmem_limit_bytes=VMEM_LIMIT),
        name="experts",
    )(block_e, n_used, xs, w1, w3, w2)


def _combine_kernel(y0_ref, y1_ref, x1a_ref, x1b_ref, ra_ref, rb_ref, nf_ref, ya_ref, yb_ref, *, n_a):
    def finish(x1_ref, route_ref, out_ref):
        route = route_ref[...]
        y = x1_ref[...] + route[:, 2:3] * y0_ref[...] + route[:, 3:4] * y1_ref[...]
        out_ref[...] = _rms(y, nf_ref[...])

    @pl.when(pl.program_id(0) < n_a)
    def _():
        finish(x1a_ref, ra_ref, ya_ref)

    @pl.when(pl.program_id(0) >= n_a)
    def _():
        finish(x1b_ref, rb_ref, yb_ref)


def _combine(x1_a, x1_b, route_a, route_b, y_rows, norm_f):
    D = x1_a.shape[1]
    rows = PROJ_ROWS
    n_a = x1_a.shape[0] // rows
    n_b = x1_b.shape[0] // rows
    nt = n_a + n_b

    def side_a(width):
        return pl.BlockSpec((rows, width), lambda i: (jnp.minimum(i, n_a - 1), 0))

    def side_b(width):
        return pl.BlockSpec((rows, width), lambda i: (jnp.maximum(i - n_a, 0), 0))

    return pl.pallas_call(
        functools.partial(_combine_kernel, n_a=n_a),
        grid=(nt,),
        in_specs=[
            pl.BlockSpec((rows, D), lambda i: (i, 0)),
            pl.BlockSpec((rows, D), lambda i: (i + nt, 0)),
            side_a(D), side_b(D), side_a(LANES), side_b(LANES),
            pl.BlockSpec((1, D), lambda i: (0, 0)),
        ],
        out_specs=[side_a(D), side_b(D)],
        out_shape=[jax.ShapeDtypeStruct(x1_a.shape, F32), jax.ShapeDtypeStruct(x1_b.shape, F32)],
        compiler_params=pltpu.CompilerParams(
            dimension_semantics=("arbitrary",), vmem_limit_bytes=VMEM_LIMIT),
        name="combine",
    )(y_rows, y_rows, x1_a, x1_b, route_a, route_b, norm_f[None, :])


def _dispatch_plan(expert_ids):
    N = expert_ids.shape[0]
    A = N * TOP_K
    blk = EXPERT_ROWS
    flat_e = expert_ids.T.reshape(A)
    iota_a = jnp.arange(A, dtype=jnp.int32)
    sorted_e, order = lax.sort((flat_e, iota_a), num_keys=1)
    experts = jnp.arange(N_EXPERTS, dtype=jnp.int32)
    counts = jnp.sum((flat_e[None, :] == experts[:, None]).astype(jnp.int32), axis=1)
    seg_end = jnp.cumsum(counts)
    seg_start = seg_end - counts
    padded = ((counts + blk - 1) // blk) * blk
    pad_end = jnp.cumsum(padded)
    pad_start = pad_end - padded
    n_blocks = (A + N_EXPERTS * (blk - 1) + blk - 1) // blk
    block_start = jnp.arange(n_blocks, dtype=jnp.int32) * blk
    block_e = jnp.sum((pad_end[None, :] <= block_start[:, None]).astype(jnp.int32), axis=1)
    block_e = jnp.minimum(block_e, N_EXPERTS - 1)
    within =jnp.arange(blk, dtype=jnp.int32)[None, :]
    j = block_start[:, None] + within - pad_start[block_e][:, None]
    valid = j < counts[block_e][:, None]
    src = jnp.clip(seg_start[block_e][:, None] + j, 0, A - 1)
    row_tok = jnp.where(valid, order[src] % N, 0).reshape(n_blocks * blk)
    n_used = (pad_end[-1] // blk).astype(jnp.int32).reshape(1)
    shift = pad_start - seg_start
    pos = iota_a + jnp.sum(jnp.where(sorted_e[:, None] == experts[None, :], shift[None, :], 0), axis=1)
    _, dest = lax.sort((order, pos.astype(jnp.int32)), num_keys=1)
    return row_tok, dest, block_e, n_used


def _pre_moe(x, norm1, w_in, b_in, conv_w, conv_b, m_fgate_bias, m_norm, hg_lb_logits, hg_norm, w_out, norm2,
             w_rg, b_rg, w_re, b_re):
    B, T, D = x.shape
    N = B * T
    q, k, v, mo, gcol, grow, hq, g_f, g_b, hv, hgg = _in_proj(
        x, norm1, w_in, b_in, conv_w, conv_b, m_fgate_bias, hg_lb_logits)
    h_f, o_f, h_b, o_b = _mixer(q, k, v, gcol, grow, hq, g_f, g_b, hv)
    x1, h2, route = _merge(h_f, h_b, o_f, o_b, mo, hgg, x, m_norm, hg_norm, w_out, norm2, w_rg, b_rg, w_re, b_re)
    return x1.reshape(N, D), h2.reshape(N, D // 2), route.reshape(N, LANES)


def kernel(x_prompt, x_sample, norm1, w_in, b_in, conv_w, conv_b, m_fgate_bias, m_norm, hg_lb_logits, hg_norm,
           w_out, norm2, w_router_group, b_router_group, w_router_expert, b_router_expert, w1, w3, w2, norm_f):
    layer = 0
    args = (norm1[layer], w_in[layer], b_in[layer], conv_w[layer], conv_b[layer], m_fgate_bias[layer],
            m_norm[layer], hg_lb_logits, hg_norm[layer], w_out[layer], norm2[layer],
            w_router_group[layer], b_router_group[layer], w_router_expert[layer], b_router_expert[layer])
    x1_p, h2_p, route_p = _pre_moe(x_prompt, *args)
    x1_s, h2_s, route_s = _pre_moe(x_sample, *args)
    h2 = jnp.concatenate([h2_p, h2_s], axis=0)
    expert_ids = jnp.concatenate([route_p[:, 0:TOP_K], route_s[:, 0:TOP_K]], axis=0).astype(jnp.int32)
    row_tok, dest, block_e, n_used = _dispatch_plan(expert_ids)
    xs = _sc_gather_rows(h2, row_tok)
    out_rows = _experts(xs, block_e, n_used, w1[layer], w3[layer], w2[layer])
    y_rows = _sc_gather_rows(out_rows, dest)
    y_p, y_s = _combine(x1_p, x1_s, route_p, route_s, y_rows, norm_f)
    return (y_p.reshape(x_prompt.shape), y_s.reshape(x_sample.shape))
```

```python
import functools

import jax
import jax.numpy as jnp
from jax import lax
from jax.experimental import pallas as pl
from jax.experimental.pallas import tpu as pltpu
from jax.experimental.pallas import tpu_sc as plsc

F32 = jnp.float32
BF16 = jnp.bfloat16

D_MODEL = 1024
N_HEADS = 4
D_HEAD = 128
WIDTH = N_HEADS * D_HEAD
CONV_K = 5
CONV_PAD = CONV_K // 2
N_GROUPS = 4
EXPERTS_PER_GROUP = 8
N_EXPERTS = N_GROUPS * EXPERTS_PER_GROUP
TOP_K = 2
EXPERT_FF = D_MODEL // 2
NORM_EPS = 1e-6

LANES = 128
SUBLANES = 8
CHUNK = 128
PROJ_ROWS = 256
HALO = SUBLANES
EXPERT_ROWS = 256
N_GATES = 4 * N_HEADS
SC_CORES = 2
SC_SUBCORES = 16
SC_WINDOW_BYTES = 128 * 1024
VMEM_LIMIT = 56 * 1024 * 1024


def _dot(a, b):
    return jnp.dot(a, b, preferred_element_type=F32)


def _dot_nt(a, b):
    return lax.dot_general(a, b, (((1,), (1,)), ((), ())), preferred_element_type=F32)


def _dot_tn(a, b):
    return lax.dot_general(a, b, (((0,), (0,)), ((), ())), preferred_element_type=F32)


def _split3(x):
    hi = x.astype(BF16)
    r1 = x - hi.astype(F32)
    mid = r1.astype(BF16)
    lo = (r1 - mid.astype(F32)).astype(BF16)
    return hi, mid, lo


def _silu(x):
    return x * jax.nn.sigmoid(x)


def _log_sigmoid(x):
    return -(jnp.maximum(-x, 0.0) + jnp.log1p(jnp.exp(-jnp.abs(x))))


def _rms(x, gain):
    return x * lax.rsqrt(jnp.mean(x * x, axis=-1, keepdims=True) + NORM_EPS) * gain


def _in_proj_kernel(x_ref, xp_ref, xn_ref, n1_ref, wa_ref, ba_ref, wg_ref, bg_ref, wgt_ref, bgt_ref,
                    fbrow_ref, fbcol_ref, wh_ref, bh_ref, cw_ref, cb_ref, lbl_ref,
                    q_ref, k_ref, v_ref, mo_ref, gcol_ref, grow_ref, hq_ref, gf_ref, gb_ref, hv_ref, hgg_ref,
                    ext_ref):
    t = pl.program_id(1)
    nt = pl.num_programs(1)
    rows = x_ref.shape[1]
    gain = n1_ref[...]

    h = _rms(x_ref[0], gain).astype(BF16)
    hp = _rms(xp_ref[0], gain).astype(BF16)
    hn = _rms(xn_ref[0], gain).astype(BF16)

    pa = _dot(h, wa_ref[...]) + ba_ref[...]
    wqk = wa_ref[:, 0:2 * WIDTH]
    bqk = ba_ref[:, 0:2 * WIDTH]
    has_prev = (t > 0).astype(F32)
    has_next = (t < nt - 1).astype(F32)
    ext_ref[0:HALO, :] = (_dot(hp, wqk) + bqk) * has_prev
    ext_ref[HALO:HALO + rows, :] = pa[:, 0:2 * WIDTH]
    ext_ref[HALO + rows:2 * HALO + rows, :] = (_dot(hn, wqk) + bqk) * has_next
    acc = cb_ref[...] + ext_ref[pl.ds(HALO - CONV_PAD, rows), :] * cw_ref[0:1, :]
    for j in range(1, CONV_K):
        acc = acc + ext_ref[pl.ds(HALO - CONV_PAD + j, rows), :] * cw_ref[j:j + 1, :]
    qk = _silu(acc)
    q_ref[0] = qk[:, 0:WIDTH] * (D_HEAD ** -0.5)
    k_ref[0] = qk[:, WIDTH:2 * WIDTH]
    v_ref[0] = pa[:, 2 * WIDTH:3 * WIDTH]
    mo_ref[0] = jax.nn.sigmoid(pa[:, 3 * WIDTH:4 * WIDTH])

    gc = _dot(h, wg_ref[...]) + bg_ref[...]
    lane = lax.broadcasted_iota(jnp.int32, gc.shape, 1)
    is_f = (lane >= 2 * N_HEADS) & (lane < N_GATES)
    gcol_ref[0] = jnp.where(is_f, _log_sigmoid(gc + fbrow_ref[...]), gc)
    gr = _dot_nt(wgt_ref[...], h) + bgt_ref[...]
    sub = lax.broadcasted_iota(jnp.int32, gr.shape, 0)
    grow_ref[0] = jnp.where(sub >= 2 * N_HEADS, _log_sigmoid(gr + fbcol_ref[...]), gr)

    ph = _dot(h, wh_ref[...]) + bh_ref[...]
    lbl = lbl_ref[...]
    lmax = jnp.max(lbl, axis=0, keepdims=True)
    le = jnp.exp(lbl - lmax)
    lb = le[0:1, :] / jnp.sum(le, axis=0, keepdims=True)
    hq_ref[0] = _silu(ph[:, 0:WIDTH])
    gf_ref[0] = lb + (1.0 - lb) * jax.nn.sigmoid(ph[:, WIDTH:2 * WIDTH])
    gb_ref[0] = lb + (1.0 - lb) * jax.nn.sigmoid(ph[:, 2 * WIDTH:3 * WIDTH])
    hv_ref[0] = ph[:, 3 * WIDTH:4 * WIDTH]
    hgg_ref[0] = _silu(ph[:, 4 * WIDTH:5 * WIDTH])


def _in_proj(x, norm1, w_in, b_in, conv_w, conv_b, fgate_bias, lb_logits):
    B, T, D = x.shape
    rows = PROJ_ROWS
    nt = T // rows
    a_w = 4 * WIDTH
    wa = w_in[:, 0:a_w].astype(BF16)
    ba = b_in[None, 0:a_w]
    wg32 = jnp.pad(w_in[:, a_w:a_w + N_GATES], ((0, 0), (0, LANES - N_GATES)))
    bg = jnp.pad(b_in[a_w:a_w + N_GATES], (0, LANES - N_GATES))[None, :]
    wg = wg32.astype(BF16)
    wgt = w_in[:, a_w:a_w + N_GATES].T.astype(BF16)
    bgt = b_in[a_w:a_w + N_GATES][:, None]
    fb = fgate_bias.reshape(2 * N_HEADS)
    fbrow = jnp.zeros((1, LANES), F32).at[0, 2 * N_HEADS:N_GATES].set(fb)
    fbcol = jnp.zeros((N_GATES, 1), F32).at[2 * N_HEADS:N_GATES, 0].set(fb)
    wh = w_in[:, a_w + N_GATES:].astype(BF16)
    bh = b_in[None, a_w + N_GATES:]

    tiles_per_halo = rows // HALO
    n_halo = T // HALO

    def full(arr):
        nd = arr.ndim
        return pl.BlockSpec(arr.shape, lambda b, t: (0,) * nd)

    def tok(width):
        return pl.BlockSpec((1, rows, width), lambda b, t: (b, t, 0))

    in_specs = [
        tok(D),
        pl.BlockSpec((1, HALO, D), lambda b, t: (b, jnp.maximum(t * tiles_per_halo - 1, 0), 0)),
        pl.BlockSpec((1, HALO, D), lambda b, t: (b, jnp.minimum((t + 1) * tiles_per_halo, n_halo - 1), 0)),
    ]
    consts = [norm1[None, :], wa, ba, wg, bg, wgt, bgt, fbrow, fbcol, wh, bh, conv_w, conv_b[None, :], lb_logits]
    in_specs += [full(c) for c in consts]
    tok_out = jax.ShapeDtypeStruct((B, T, WIDTH), F32)
    out_shape = [tok_out, tok_out, tok_out, tok_out,
                 jax.ShapeDtypeStruct((B, T, LANES), F32),
                 jax.ShapeDtypeStruct((B, N_GATES, T), F32),
                 tok_out, tok_out, tok_out, tok_out, tok_out]
    out_specs = [tok(WIDTH)] * 4 + [tok(LANES), pl.BlockSpec((1, N_GATES, rows), lambda b, t: (b, 0, t))] + [tok(WIDTH)] * 5
    return pl.pallas_call(
        _in_proj_kernel,
        grid=(B, nt),
        in_specs=in_specs,
        out_specs=out_specs,
        out_shape=out_shape,
        scratch_shapes=[pltpu.VMEM((rows + 2 * HALO, 2 * WIDTH), F32)],
        compiler_params=pltpu.CompilerParams(
            dimension_semantics=("parallel", "parallel"), vmem_limit_bytes=VMEM_LIMIT),
        name="in_proj",
    )(x, x, x, *consts)


def _cumsum_rows(tri_bf, x):
    hi, mid, lo = _split3(x)
    return _dot(tri_bf, hi) + _dot(tri_bf, mid) + _dot(tri_bf, lo)


def _cumsum_lanes(x, tri_bf):
    hi, mid, lo = _split3(x)
    return _dot(hi, tri_bf) + _dot(mid, tri_bf) + _dot(lo, tri_bf)


def _mlstm_chunk(q, k, vext, i_col, b_col, i_row, b_row, seen, last, c_ref, m_ref, idx):
    m_prev = m_ref[idx][:, 0:1]
    c_prev = c_ref[idx]
    q_bf = q.astype(BF16)
    log_d = jnp.where(seen, b_col - b_row + i_row, -jnp.inf)
    m_inter = b_col + m_prev
    m_t = jnp.maximum(m_inter, jnp.max(log_d, axis=-1, keepdims=True))
    scores = (_dot_nt(q_bf, k.astype(BF16)) * jnp.exp(log_d - m_t)).astype(BF16)
    inter_scale = jnp.exp(m_inter - m_t)
    numden = _dot(scores, vext) + inter_scale * _dot(q_bf, c_prev.astype(BF16))
    num = numden[:, 0:D_HEAD]
    den = numden[:, D_HEAD:2 * D_HEAD]
    h = num / jnp.maximum(jnp.abs(den), jnp.exp(-m_t))

    b_last = b_col[last:last + 1, :]
    log_w = b_last - b_col + i_col
    m_new = jnp.maximum(b_last + m_prev, jnp.max(log_w, axis=0, keepdims=True))
    w = jnp.exp(log_w - m_new)
    decay = jnp.exp(b_last + m_prev - m_new)
    c_ref[idx] = decay * c_prev + _dot_tn((k * w).astype(BF16), vext)
    m_ref[idx] = jnp.broadcast_to(m_new, (1, LANES))
    return h


def _hgrn2_level_small(q3, k3, pre3, suf3, half, rev, sub_iota):
    upper = (sub_iota & half) != 0
    second = jnp.logical_not(upper) if rev else upper
    end = 0 if rev else half - 1
    y = jnp.where((sub_iota & (half - 1)) == end, pre3, 0.0)
    step = 1 if rev else -1
    span = 1
    while span < half:
        y = y + pltpu.roll(y, (step * span) % SUBLANES, 1)
        span *= 2
    if 2 * half == SUBLANES:
        other = pltpu.roll(y, half, 1)
    else:
        other = jnp.where(upper, pltpu.roll(y, half, 1), pltpu.roll(y, SUBLANES - half, 1))
    z = jnp.where(second, q3 * pre3, k3 * suf3)
    return z, pre3 * jnp.where(second, other, 1.0), suf3 * jnp.where(second, 1.0, other)


def _hgrn2_level_big(q, k, pre, suf, half, rev):
    L, width = q.shape
    shape = (L // (2 * half), 2, half, width)
    q4, k4, pre4, suf4 = (a.reshape(shape) for a in (q, k, pre, suf))
    first = 1 if rev else 0
    second = 1 - first
    end = 0 if rev else half - 1
    total_first = pre4[:, first, end:end + 1, :]
    total_second = pre4[:, second, end:end + 1, :]

    def join(at_first, at_second):
        parts = (at_second, at_first) if rev else (at_first, at_second)
        return jnp.stack(parts, axis=1).reshape(L, width)

    z = join(k4[:, first] * suf4[:, first], q4[:, second] * pre4[:, second])
    pre_new = join(pre4[:, first], pre4[:, second] * total_first)
    suf_new = join(suf4[:, first] * total_second, suf4[:, second])
    return z, pre_new, suf_new


def _hgrn2_chunk(q, g, v_bf, rev, level, diag, sub_iota, st_ref, idx0):
    L, width = q.shape
    heads = [slice(h * D_HEAD, (h + 1) * D_HEAD) for h in range(width // D_HEAD)]
    k = 1.0 - g
    q_bf = q.astype(BF16)
    k_bf = k.astype(BF16)
    att = [jnp.where(diag, _dot_nt(q_bf[:, s], k_bf[:, s]), 0.0) for s in heads]
    small = (L // SUBLANES, SUBLANES, width)
    q3, k3, pre, suf = q.reshape(small), k.reshape(small), g.reshape(small), jnp.ones(small, F32)
    half = 1
    bit = 0
    while half < L:
        if half == SUBLANES:
            pre, suf = pre.reshape(L, width), suf.reshape(L, width)
        if half < SUBLANES:
            z, pre, suf = _hgrn2_level_small(q3, k3, pre, suf, half, rev, sub_iota)
            z = z.reshape(L, width)
        else:
            z, pre, suf = _hgrn2_level_big(q, k, pre, suf, half, rev)
        z = z.astype(BF16)
        att = [jnp.where(level == bit, _dot_nt(z[:, s], z[:, s]), a) for a, s in zip(att, heads)]
        half *= 2
        bit += 1
    last = 0 if rev else L - 1
    q_dec = (q * pre).astype(BF16)
    k_dec = (k * suf).astype(BF16)
    outs = []
    for h, s in enumerate(heads):
        st_prev = st_ref[idx0 + h]
        outs.append(_dot_nt(q_dec[:, s], st_prev.astype(BF16)) + _dot(att[h].astype(BF16), v_bf[:, s]))
        st_ref[idx0 + h] = st_prev * pre[last:last + 1, s] + _dot_tn(v_bf[:, s], k_dec[:, s])
    return jnp.concatenate(outs, axis=1)


def _mixer_kernel(qf_ref, kf_ref, vf_ref, gcf_ref, grf_ref, hqf_ref, hgf_ref, hvf_ref,
                  qb_ref, kb_ref, vb_ref, gcb_ref, grb_ref, hqb_ref, hgb_ref, hvb_ref,
                  hf_ref, of_ref, hb_ref, ob_ref, c_ref, m_ref, st_ref):
    L = CHUNK

    @pl.when(pl.program_id(1) == 0)
    def _():
        c_ref[...] = jnp.zeros_like(c_ref)
        m_ref[...] = jnp.zeros_like(m_ref)
        st_ref[...] = jnp.zeros_like(st_ref)

    row = lax.broadcasted_iota(jnp.int32, (L, L), 0)
    col = lax.broadcasted_iota(jnp.int32, (L, L), 1)
    sub_iota = lax.broadcasted_iota(jnp.int32, (L // SUBLANES, SUBLANES, LANES), 1)
    diag = row == col
    diff = row ^ col
    high_bit = jnp.zeros((L, L), jnp.int32)
    half = 2
    while half < L:
        high_bit = high_bit + (diff >= half).astype(jnp.int32)
        half *= 2
    ones = jnp.ones((L, D_HEAD), BF16)

    dirs = (
        (0, qf_ref, kf_ref, vf_ref, gcf_ref, grf_ref, hqf_ref, hgf_ref, hvf_ref, hf_ref, of_ref),
        (1, qb_ref, kb_ref, vb_ref, gcb_ref, grb_ref, hqb_ref, hgb_ref, hvb_ref, hb_ref, ob_ref),
    )
    for d, q_ref, k_ref, v_ref, gc_ref, gr_ref, hq_ref, hg_ref, hv_ref, h_out, o_out in dirs:
        rev = d == 1
        seen = (col >= row) if rev else (col <= row)
        before = (col > row) if rev else (col < row)
        level = jnp.where(before, high_bit, -1)
        tri = seen.astype(BF16)
        tri_t = (row >= col if rev else row <= col).astype(BF16)
        last = 0 if rev else L - 1
        gc = gc_ref[0]
        gr = gr_ref[0]
        gc_cum = _cumsum_rows(tri, gc)
        gr_cum = _cumsum_lanes(gr, tri_t)
        for hd in range(N_HEADS):
            sl = slice(hd * D_HEAD, (hd + 1) * D_HEAD)
            gi = d * N_HEADS + hd
            gf = 2 * N_HEADS + gi
            idx = d * N_HEADS + hd
            vext = jnp.concatenate([v_ref[0, :, sl].astype(BF16), ones], axis=1)
            h_out[0, :, sl] = _mlstm_chunk(
                q_ref[0, :, sl], k_ref[0, :, sl], vext,
                gc[:, gi:gi + 1], gc_cum[:, gf:gf + 1], gr[gi:gi + 1, :], gr_cum[gf:gf + 1, :],
                seen, last, c_ref, m_ref, idx)
            o_out[0, :, sl] = _hgrn2_chunk(
                hq_ref[0, :, sl], hg_ref[0, :, sl], hv_ref[0, :, sl].astype(BF16),
                rev, level, diag, sub_iota, st_ref, idx)


def _mixer(q, k, v, gcol, grow, hq, g_f, g_b, hv):
    B, T, _ = q.shape
    L = CHUNK
    nc = T // L

    def fwd(width):
        return pl.BlockSpec((1, L, width), lambda b, c: (b, c, 0))

    def bwd(width):
        return pl.BlockSpec((1, L, width), lambda b, c: (b, nc - 1 - c, 0))

    grow_f = pl.BlockSpec((1, N_GATES, L), lambda b, c: (b, 0, c))
    grow_b = pl.BlockSpec((1, N_GATES, L), lambda b, c: (b, 0, nc - 1 - c))
    in_specs = ([fwd(WIDTH)] * 3 + [fwd(LANES), grow_f] + [fwd(WIDTH)] * 3
                + [bwd(WIDTH)] * 3 + [bwd(LANES), grow_b] + [bwd(WIDTH)] * 3)
    out = jax.ShapeDtypeStruct((B, T, WIDTH), F32)
    n_state = 2 * N_HEADS
    return pl.pallas_call(
        _mixer_kernel,
        grid=(B, nc),
        in_specs=in_specs,
        out_specs=[fwd(WIDTH), fwd(WIDTH), bwd(WIDTH), bwd(WIDTH)],
        out_shape=[out, out, out, out],
        scratch_shapes=[
            pltpu.VMEM((n_state, D_HEAD, 2 * D_HEAD), F32),
            pltpu.VMEM((n_state, 1, LANES), F32),
            pltpu.VMEM((n_state, D_HEAD, D_HEAD), F32),
        ],
        compiler_params=pltpu.CompilerParams(
            dimension_semantics=("parallel", "arbitrary"), vmem_limit_bytes=VMEM_LIMIT),
        name="mixer",
    )(q, k, v, gcol, grow, hq, g_f, hv, q, k, v, gcol, grow, hq, g_b, hv)


def _head_norm(hsum, gain):
    parts = []
    for hd in range(N_HEADS):
        hh = hsum[:, hd * D_HEAD:(hd + 1) * D_HEAD]
        parts.append(hh * lax.rsqrt(jnp.mean(hh * hh, axis=-1, keepdims=True) + NORM_EPS))
    return jnp.concatenate(parts, axis=1) * gain


def _merge_kernel(hf_ref, hb_ref, of_ref, ob_ref, mo_ref, hgg_ref, x_ref, mn_ref, hn_ref, wo_ref,
                  n2_ref, wrh_ref, wrl_ref, br_ref, x1_ref, h2_ref, route_ref):
    m_out = _head_norm(hf_ref[0] + hb_ref[0], mn_ref[...]) * mo_ref[0]
    hg_out = _head_norm(of_ref[0] + ob_ref[0], hn_ref[...]) * hgg_ref[0]
    mixed = jnp.concatenate([m_out, hg_out], axis=1).astype(BF16)
    x1 = x_ref[0] + _dot(mixed, wo_ref[...])
    x1_ref[0] = x1
    h2 = _rms(x1, n2_ref[...])
    h_hi = h2.astype(BF16)
    h_hi32 = h_hi.astype(F32)
    bits = lax.bitcast_convert_type(h_hi32, jnp.uint32)
    half = D_MODEL // 2
    h2_ref[0] = (bits[:, half:] & jnp.uint32(0xFFFF0000)) | (bits[:, :half] >> 16)

    h_lo = (h2 - h_hi32).astype(BF16)
    logits = _dot(h_hi, wrh_ref[...]) + _dot(h_lo, wrh_ref[...]) + _dot(h_hi, wrl_ref[...]) + br_ref[...]
    lane = lax.broadcasted_iota(jnp.int32, logits.shape, 1)
    big = jnp.int32(LANES)
    neg = -jnp.inf
    g_log = jnp.where(lane < N_GROUPS, logits, neg)
    g_max = jnp.max(g_log, axis=-1, keepdims=True)
    g_idx = jnp.min(jnp.where(g_log == g_max, lane, big), axis=-1, keepdims=True)
    g_val = 1.0 / jnp.sum(jnp.exp(g_log - g_max), axis=-1, keepdims=True)
    e_lo = N_GROUPS + g_idx * EXPERTS_PER_GROUP
    e_log = jnp.where((lane >= e_lo) & (lane < e_lo + EXPERTS_PER_GROUP), logits, neg)
    m1 = jnp.max(e_log, axis=-1, keepdims=True)
    i1 = jnp.min(jnp.where(e_log == m1, lane, big), axis=-1, keepdims=True)
    e_log2 = jnp.where(lane == i1, neg, e_log)
    m2 = jnp.max(e_log2, axis=-1, keepdims=True)
    i2 = jnp.min(jnp.where(e_log2 == m2, lane, big), axis=-1, keepdims=True)
    r2 = jnp.exp(m2 - m1)
    w1 = g_val / (1.0 + r2)
    w2 = g_val * r2 / (1.0 + r2)
    route = jnp.where(lane == 0, (i1 - N_GROUPS).astype(F32),
                      jnp.where(lane == 1, (i2 - N_GROUPS).astype(F32),
                                jnp.where(lane == 2, w1, jnp.where(lane == 3, w2, 0.0))))
    route_ref[0] = route


def _merge(h_f, h_b, o_f, o_b, mo, hgg, x, m_norm, hg_norm, w_out, norm2, w_rg, b_rg, w_re, b_re):
    B, T, D = x.shape
    rows = PROJ_ROWS
    n_log = N_GROUPS + N_EXPERTS
    wr = jnp.pad(jnp.concatenate([w_rg, w_re], axis=1), ((0, 0), (0, LANES - n_log)))
    br = jnp.pad(jnp.concatenate([b_rg, b_re]), (0, LANES - n_log))[None, :]
    wr_hi = wr.astype(BF16)
    wr_lo = (wr - wr_hi.astype(F32)).astype(BF16)
    consts = [m_norm[None, :], hg_norm[None, :], w_out.astype(BF16), norm2[None, :], wr_hi, wr_lo, br]

    def full(arr):
        nd = arr.ndim
        return pl.BlockSpec(arr.shape, lambda b, t: (0,) * nd)

    def tok(width):
        return pl.BlockSpec((1, rows, width), lambda b, t: (b, t, 0))

    return pl.pallas_call(
        _merge_kernel,
        grid=(B, T // rows),
        in_specs=[tok(WIDTH)] * 6 + [tok(D)] + [full(c) for c in consts],
        out_specs=[tok(D), tok(D // 2), tok(LANES)],
        out_shape=[jax.ShapeDtypeStruct((B, T, D), F32), jax.ShapeDtypeStruct((B, T, D // 2), jnp.uint32),
                   jax.ShapeDtypeStruct((B, T, LANES), F32)],
        compiler_params=pltpu.CompilerParams(
            dimension_semantics=("parallel", "parallel"), vmem_limit_bytes=VMEM_LIMIT),
        name="merge",
    )(h_f, h_b, o_f, o_b, mo, hgg, x, *consts)


def _sc_gather_rows(src, idx):
    n_out = idx.shape[0]
    D = src.shape[1]
    n_sub = SC_CORES * SC_SUBCORES
    per = n_out // n_sub
    window = SC_WINDOW_BYTES // (D * src.dtype.itemsize)
    assert per * n_sub == n_out and per % window == 0, (n_out, per, window)
    mesh = plsc.VectorSubcoreMesh(core_axis_name="c", subcore_axis_name="s",
                                  num_cores=SC_CORES, num_subcores=SC_SUBCORES)

    def body(src_hbm, idx_hbm, out_hbm, idx_v, buf):
        base = (lax.axis_index("c") * SC_SUBCORES + lax.axis_index("s")) * per
        pltpu.sync_copy(idx_hbm.at[pl.ds(base, per)], idx_v)

        @pl.loop(0, per // window)
        def _(j):
            pltpu.sync_copy(src_hbm.at[idx_v.at[pl.ds(j * window, window)]], buf)
            pltpu.sync_copy(buf, out_hbm.at[pl.ds(base + j * window, window)])

    return pl.kernel(
        body,
        out_type=jax.ShapeDtypeStruct((n_out, D), src.dtype),
        mesh=mesh,
        scratch_types=[pltpu.VMEM((per,), jnp.int32), pltpu.VMEM((window, D), src.dtype)],
        name="sc_gather_rows",
    )(src, idx)


def _sc_scatter_rows(src, idx, n_out):
    n_in, D = src.shape
    n_sub = SC_CORES * SC_SUBCORES
    per = n_in // n_sub
    window = SC_WINDOW_BYTES // (D * src.dtype.itemsize)
    assert per * n_sub == n_in and per % window == 0, (n_in, per, window)
    mesh = plsc.VectorSubcoreMesh(core_axis_name="c", subcore_axis_name="s",
                                  num_cores=SC_CORES, num_subcores=SC_SUBCORES)

    def body(src_hbm, idx_hbm, out_hbm, idx_v, buf):
        base = (lax.axis_index("c") * SC_SUBCORES + lax.axis_index("s")) * per
        pltpu.sync_copy(idx_hbm.at[pl.ds(base, per)], idx_v)

        @pl.loop(0, per // window)
        def _(j):
            pltpu.sync_copy(src_hbm.at[pl.ds(base + j * window, window)], buf)
            pltpu.sync_copy(buf, out_hbm.at[idx_v.at[pl.ds(j * window, window)]])

    return pl.kernel(
        body,
        out_type=jax.ShapeDtypeStruct((n_out, D), src.dtype),
        mesh=mesh,
        scratch_types=[pltpu.VMEM((per,), jnp.int32), pltpu.VMEM((window, D), src.dtype)],
        name="sc_scatter_rows",
    )(src, idx)


def _expert_kernel(be_ref, nu_ref, x_ref, w1_ref, w3_ref, w2_ref, o_ref, w1_bf, w3_bf, w2_bf):
    i = pl.program_id(0)
    active = i < nu_ref[0]
    new_expert = (i == 0) | (be_ref[i] != be_ref[jnp.maximum(i - 1, 0)])

    @pl.when(active & new_expert)
    def _():
        w1_bf[...] = w1_ref[0].astype(BF16)
        w3_bf[...] = w3_ref[0].astype(BF16)
        w2_bf[...] = w2_ref[0].astype(BF16)

    @pl.when(active)
    def _():
        half = D_MODEL // 2
        words = x_ref[...]
        x_lo = lax.bitcast_convert_type(words << 16, F32).astype(BF16)
        x_hi = lax.bitcast_convert_type(words & jnp.uint32(0xFFFF0000), F32).astype(BF16)
        a = _dot(x_lo, w1_bf[0:half, :]) + _dot(x_hi, w1_bf[half:, :])
        b = _dot(x_lo, w3_bf[0:half, :]) + _dot(x_hi, w3_bf[half:, :])
        o_ref[...] = _dot((_silu(a) * b).astype(BF16), w2_bf[...])


def _experts(xs, block_e, n_used, w1, w3, w2):
    rows = EXPERT_ROWS
    n_blocks = xs.shape[0] // rows
    D = D_MODEL

    def blk(i, be, nu):
        return jnp.minimum(i, nu[0] - 1)

    grid_spec = pltpu.PrefetchScalarGridSpec(
        num_scalar_prefetch=2,
        grid=(n_blocks,),
        in_specs=[
            pl.BlockSpec((rows, D // 2), lambda i, be, nu: (blk(i, be, nu), 0)),
            pl.BlockSpec((1, D, EXPERT_FF), lambda i, be, nu: (be[blk(i, be, nu)], 0, 0)),
            pl.BlockSpec((1, D, EXPERT_FF), lambda i, be, nu: (be[blk(i, be, nu)], 0, 0)),
            pl.BlockSpec((1, EXPERT_FF, D), lambda i, be, nu: (be[blk(i, be, nu)], 0, 0)),
        ],
        out_specs=pl.BlockSpec((rows, D), lambda i, be, nu: (blk(i, be, nu), 0)),
        scratch_shapes=[pltpu.VMEM((D, EXPERT_FF), BF16), pltpu.VMEM((D, EXPERT_FF), BF16),
                        pltpu.VMEM((EXPERT_FF, D), BF16)],
    )
    return pl.pallas_call(
        _expert_kernel,
        grid_spec=grid_spec,
        out_shape=jax.ShapeDtypeStruct((xs.shape[0], D), F32),
        compiler_params=pltpu.CompilerParams(
            dimension_semantics=("arbitrary",), vmem_limit_bytes=VMEM_LIMIT),
        name="experts",
    )(block_e, n_used, xs, w1, w3, w2)


def _combine_kernel(y0_ref, y1_ref, x1a_ref, x1b_ref, ra_ref, rb_ref, nf_ref, ya_ref, yb_ref, *, n_a):
    def finish(x1_ref, route_ref, out_ref):
        route = route_ref[...]
        y = x1_ref[...] + route[:, 2:3] * y0_ref[...] + route[:, 3:4] * y1_ref[...]
        out_ref[...] = _rms(y, nf_ref[...])

    @pl.when(pl.program_id(0) < n_a)
    def _():
        finish(x1a_ref, ra_ref, ya_ref)

    @pl.when(pl.program_id(0) >= n_a)
    def _():
        finish(x1b_ref, rb_ref, yb_ref)


def _combine(x1_a, x1_b, route_a, route_b, y_rows, norm_f):
    D = x1_a.shape[1]
    rows = PROJ_ROWS
    n_a = x1_a.shape[0] // rows
    n_b = x1_b.shape[0] // rows
    nt = n_a + n_b

    def side_a(width):
        return pl.BlockSpec((rows, width), lambda i: (jnp.minimum(i, n_a - 1), 0))

    def side_b(width):
        return pl.BlockSpec((rows, width), lambda i: (jnp.maximum(i - n_a, 0), 0))

    return pl.pallas_call(
        functools.partial(_combine_kernel, n_a=n_a),
        grid=(nt,),
        in_specs=[
            pl.BlockSpec((rows, D), lambda i: (i, 0)),
            pl.BlockSpec((rows, D), lambda i: (i + nt, 0)),
            side_a(D), side_b(D), side_a(LANES), side_b(LANES),
            pl.BlockSpec((1, D), lambda i: (0, 0)),
        ],
        out_specs=[side_a(D), side_b(D)],
        out_shape=[jax.ShapeDtypeStruct(x1_a.shape, F32), jax.ShapeDtypeStruct(x1_b.shape, F32)],
        compiler_params=pltpu.CompilerParams(
            dimension_semantics=("arbitrary",), vmem_limit_bytes=VMEM_LIMIT),
        name="combine",
    )(y_rows, y_rows, x1_a, x1_b, route_a, route_b, norm_f[None, :])


def _dispatch_plan(expert_ids):
    N = expert_ids.shape[0]
    A = N * TOP_K
    blk = EXPERT_ROWS
    flat_e = expert_ids.T.reshape(A)
    _, order = lax.sort((flat_e, jnp.arange(A, dtype=jnp.int32)), num_keys=1)
    experts = jnp.arange(N_EXPERTS, dtype=jnp.int32)
    counts = jnp.sum((flat_e[None, :] == experts[:, None]).astype(jnp.int32), axis=1)
    seg_end = jnp.cumsum(counts)
    seg_start = seg_end - counts
    padded = ((counts + blk - 1) // blk) * blk
    pad_end = jnp.cumsum(padded)
    pad_start = pad_end - padded
    n_blocks = (A + N_EXPERTS * (blk - 1) + blk - 1) // blk
    block_start = jnp.arange(n_blocks, dtype=jnp.int32) * blk
    block_e = jnp.sum((pad_end[None, :] <= block_start[:, None]).astype(jnp.int32), axis=1)
    block_e = jnp.minimum(block_e, N_EXPERTS - 1)
    row = block_start[:, None] + jnp.arange(blk, dtype=jnp.int32)[None, :]
    j = row - pad_start[block_e][:, None]
    valid = j < counts[block_e][:, None]
    assign = order[jnp.clip(seg_start[block_e][:, None] + j, 0, A - 1)]
    row_tok = jnp.where(valid, assign % N, row % N).reshape(n_blocks * blk)
    row_dst = jnp.where(valid, assign, A + row).reshape(n_blocks * blk)
    n_used = (pad_end[-1] // blk).astype(jnp.int32).reshape(1)
    return row_tok, row_dst, block_e, n_used


def _pre_moe(x, norm1, w_in, b_in, conv_w, conv_b, m_fgate_bias, m_norm, hg_lb_logits, hg_norm, w_out, norm2,
             w_rg, b_rg, w_re, b_re):
    B, T, D = x.shape
    N = B * T
    q, k, v, mo, gcol, grow, hq, g_f, g_b, hv, hgg = _in_proj(
        x, norm1, w_in, b_in, conv_w, conv_b, m_fgate_bias, hg_lb_logits)
    h_f, o_f, h_b, o_b = _mixer(q, k, v, gcol, grow, hq, g_f, g_b, hv)
    x1, h2, route = _merge(h_f, h_b, o_f, o_b, mo, hgg, x, m_norm, hg_norm, w_out, norm2, w_rg, b_rg, w_re, b_re)
    return x1.reshape(N, D), h2.reshape(N, D // 2), route.reshape(N, LANES)


def kernel(x_prompt, x_sample, norm1, w_in, b_in, conv_w, conv_b, m_fgate_bias, m_norm, hg_lb_logits, hg_norm,
           w_out, norm2, w_router_group, b_router_group, w_router_expert, b_router_expert, w1, w3, w2, norm_f):
    layer = 0
    args = (norm1[layer], w_in[layer], b_in[layer], conv_w[layer], conv_b[layer], m_fgate_bias[layer],
            m_norm[layer], hg_lb_logits, hg_norm[layer], w_out[layer], norm2[layer],
            w_router_group[layer], b_router_group[layer], w_router_expert[layer], b_router_expert[layer])
    x1_p, h2_p, route_p = _pre_moe(x_prompt, *args)
    x1_s, h2_s, route_s = _pre_moe(x_sample, *args)
    h2 = jnp.concatenate([h2_p, h2_s], axis=0)
    expert_ids = jnp.concatenate([route_p[:, 0:TOP_K], route_s[:, 0:TOP_K]], axis=0).astype(jnp.int32)
    row_tok, row_dst, block_e, n_used = _dispatch_plan(expert_ids)
    xs = _sc_gather_rows(h2, row_tok)
    out_rows = _experts(xs, block_e, n_used, w1[layer], w3[layer], w2[layer])
    y_rows = _sc_scatter_rows(out_rows, row_dst, TOP_K * h2.shape[0] + out_rows.shape[0])
    y_p, y_s = _combine(x1_p, x1_s, route_p, route_s, y_rows, norm_f)
    return (y_p.reshape(x_prompt.shape), y_s.reshape(x_sample.shape))
```

```python
import functools

import jax
import jax.numpy as jnp
from jax import lax
from jax.experimental import pallas as pl
from jax.experimental.pallas import tpu as pltpu
from jax.experimental.pallas import tpu_sc as plsc

F32 = jnp.float32
BF16 = jnp.bfloat16

D_MODEL = 1024
N_HEADS = 4
D_HEAD = 128
WIDTH = N_HEADS * D_HEAD
CONV_K = 5
CONV_PAD = CONV_K // 2
N_GROUPS = 4
EXPERTS_PER_GROUP = 8
N_EXPERTS = N_GROUPS * EXPERTS_PER_GROUP
TOP_K = 2
EXPERT_FF = D_MODEL // 2
NORM_EPS = 1e-6

LANES = 128
SUBLANES = 8
CHUNK = 128
PROJ_ROWS = 256
HALO = SUBLANES
EXPERT_ROWS = 256
N_GATES = 4 * N_HEADS
SC_CORES = 2
SC_SUBCORES = 16
SC_WINDOW_BYTES = 128 * 1024
VMEM_LIMIT = 56 * 1024 * 1024


def _dot(a, b):
    return jnp.dot(a, b, preferred_element_type=F32)


def _dot_nt(a, b):
    return lax.dot_general(a, b, (((1,), (1,)), ((), ())), preferred_element_type=F32)


def _dot_tn(a, b):
    return lax.dot_general(a, b, (((0,), (0,)), ((), ())), preferred_element_type=F32)


def _split3(x):
    hi = x.astype(BF16)
    r1 = x - hi.astype(F32)
    mid = r1.astype(BF16)
    lo = (r1 - mid.astype(F32)).astype(BF16)
    return hi, mid, lo


def _pack_bf16_pairs(x):
    half = x.shape[1] // 2
    bits = lax.bitcast_convert_type(x.astype(BF16).astype(F32), jnp.uint32)
    return (bits[:, half:] & jnp.uint32(0xFFFF0000)) | (bits[:, :half] >> 16)


def _unpack_bf16_pairs(words):
    lo = lax.bitcast_convert_type(words << 16, F32)
    hi = lax.bitcast_convert_type(words & jnp.uint32(0xFFFF0000), F32)
    return lo, hi


def _silu(x):
    return x * jax.nn.sigmoid(x)


def _log_sigmoid(x):
    return -(jnp.maximum(-x, 0.0) + jnp.log1p(jnp.exp(-jnp.abs(x))))


def _rms(x, gain):
    return x * lax.rsqrt(jnp.mean(x * x, axis=-1, keepdims=True) + NORM_EPS) * gain


def _in_proj_kernel(x_ref, xp_ref, xn_ref, n1_ref, wa_ref, ba_ref, wg_ref, bg_ref, wgt_ref, bgt_ref,
                    fbrow_ref, fbcol_ref, wh_ref, bh_ref, cw_ref, cb_ref, lbl_ref,
                    q_ref, k_ref, v_ref, mo_ref, gcol_ref, grow_ref, hq_ref, gf_ref, gb_ref, hv_ref, hgg_ref,
                    ext_ref):
    t = pl.program_id(1)
    nt = pl.num_programs(1)
    rows = x_ref.shape[1]
    gain = n1_ref[...]

    h = _rms(x_ref[0], gain).astype(BF16)
    hp = _rms(xp_ref[0], gain).astype(BF16)
    hn = _rms(xn_ref[0], gain).astype(BF16)

    def proj(w_ref, b_ref, lo, hi):
        return _dot(h, w_ref[:, lo:hi]) + b_ref[:, lo:hi]

    lbl = lbl_ref[...]
    lmax = jnp.max(lbl, axis=0, keepdims=True)
    le = jnp.exp(lbl - lmax)
    lb = le[0:1, :] / jnp.sum(le, axis=0, keepdims=True)

    wqk = wa_ref[:, 0:2 * WIDTH]
    bqk = ba_ref[:, 0:2 * WIDTH]
    has_prev = (t > 0).astype(F32)
    has_next = (t < nt - 1).astype(F32)
    ext_ref[0:HALO, :] = (_dot(hp, wqk) + bqk) * has_prev
    ext_ref[HALO:HALO + rows, :] = proj(wa_ref, ba_ref, 0, 2 * WIDTH)
    ext_ref[HALO + rows:2 * HALO + rows, :] = (_dot(hn, wqk) + bqk) * has_next
    hq_pre = proj(wh_ref, bh_ref, 0, WIDTH)
    acc = cb_ref[...] + ext_ref[pl.ds(HALO - CONV_PAD, rows), :] * cw_ref[0:1, :]
    for j in range(1, CONV_K):
        acc = acc + ext_ref[pl.ds(HALO - CONV_PAD + j, rows), :] * cw_ref[j:j + 1, :]
    qk = _silu(acc)
    q_ref[0] = qk[:, 0:WIDTH] * (D_HEAD ** -0.5)
    k_ref[0] = qk[:, WIDTH:2 * WIDTH]
    v_ref[0] = proj(wa_ref, ba_ref, 2 * WIDTH, 3 * WIDTH)
    hq_ref[0] = _silu(hq_pre)
    mo_ref[0] = jax.nn.sigmoid(proj(wa_ref, ba_ref, 3 * WIDTH, 4 * WIDTH))

    gf_ref[0] = lb + (1.0 - lb) * jax.nn.sigmoid(proj(wh_ref, bh_ref, WIDTH, 2 * WIDTH))
    gb_ref[0] = lb + (1.0 - lb) * jax.nn.sigmoid(proj(wh_ref, bh_ref, 2 * WIDTH, 3 * WIDTH))
    hv_ref[0] = proj(wh_ref, bh_ref, 3 * WIDTH, 4 * WIDTH)
    hgg_ref[0] = _silu(proj(wh_ref, bh_ref, 4 * WIDTH, 5 * WIDTH))

    gc = _dot(h, wg_ref[...]) + bg_ref[...]
    lane = lax.broadcasted_iota(jnp.int32, gc.shape, 1)
    is_f = (lane >= 2 * N_HEADS) & (lane < N_GATES)
    gcol_ref[0] = jnp.where(is_f, _log_sigmoid(gc + fbrow_ref[...]), gc)
    gr = _dot_nt(wgt_ref[...], h) + bgt_ref[...]
    sub = lax.broadcasted_iota(jnp.int32, gr.shape, 0)
    grow_ref[0] = jnp.where(sub >= 2 * N_HEADS, _log_sigmoid(gr + fbcol_ref[...]), gr)


def _in_proj(x, norm1, w_in, b_in, conv_w, conv_b, fgate_bias, lb_logits):
    B, T, D = x.shape
    rows = PROJ_ROWS
    nt = T // rows
    a_w = 4 * WIDTH
    wa = w_in[:, 0:a_w].astype(BF16)
    ba = b_in[None, 0:a_w]
    wg32 = jnp.pad(w_in[:, a_w:a_w + N_GATES], ((0, 0), (0, LANES - N_GATES)))
    bg = jnp.pad(b_in[a_w:a_w + N_GATES], (0, LANES - N_GATES))[None, :]
    wg = wg32.astype(BF16)
    wgt = w_in[:, a_w:a_w + N_GATES].T.astype(BF16)
    bgt = b_in[a_w:a_w + N_GATES][:, None]
    fb = fgate_bias.reshape(2 * N_HEADS)
    fbrow = jnp.zeros((1, LANES), F32).at[0, 2 * N_HEADS:N_GATES].set(fb)
    fbcol = jnp.zeros((N_GATES, 1), F32).at[2 * N_HEADS:N_GATES, 0].set(fb)
    wh = w_in[:, a_w + N_GATES:].astype(BF16)
    bh = b_in[None, a_w + N_GATES:]

    tiles_per_halo = rows // HALO
    n_halo = T // HALO

    def full(arr):
        nd = arr.ndim
        return pl.BlockSpec(arr.shape, lambda b, t: (0,) * nd)

    def tok(width):
        return pl.BlockSpec((1, rows, width), lambda b, t: (b, t, 0))

    in_specs = [
        tok(D),
        pl.BlockSpec((1, HALO, D), lambda b, t: (b, jnp.maximum(t * tiles_per_halo - 1, 0), 0)),
        pl.BlockSpec((1, HALO, D), lambda b, t: (b, jnp.minimum((t + 1) * tiles_per_halo, n_halo - 1), 0)),
    ]
    consts = [norm1[None, :], wa, ba, wg, bg, wgt, bgt, fbrow, fbcol, wh, bh, conv_w, conv_b[None, :], lb_logits]
    in_specs += [full(c) for c in consts]
    tok_out = jax.ShapeDtypeStruct((B, T, WIDTH), F32)
    out_shape = [tok_out, tok_out, tok_out, tok_out,
                 jax.ShapeDtypeStruct((B, T, LANES), F32),
                 jax.ShapeDtypeStruct((B, N_GATES, T), F32),
                 tok_out, tok_out, tok_out, tok_out, tok_out]
    out_specs = [tok(WIDTH)] * 4 + [tok(LANES), pl.BlockSpec((1, N_GATES, rows), lambda b, t: (b, 0, t))] + [tok(WIDTH)] * 5
    return pl.pallas_call(
        _in_proj_kernel,
        grid=(B, nt),
        in_specs=in_specs,
        out_specs=out_specs,
        out_shape=out_shape,
        scratch_shapes=[pltpu.VMEM((rows + 2 * HALO, 2 * WIDTH), F32)],
        compiler_params=pltpu.CompilerParams(
            dimension_semantics=("parallel", "parallel"), vmem_limit_bytes=VMEM_LIMIT),
        name="in_proj",
    )(x, x, x, *consts)


def _cumsum_rows(tri_bf, x):
    hi, mid, lo = _split3(x)
    return _dot(tri_bf, hi) + _dot(tri_bf, mid) + _dot(tri_bf, lo)


def _cumsum_lanes(x, tri_bf):
    hi, mid, lo = _split3(x)
    return _dot(hi, tri_bf) + _dot(mid, tri_bf) + _dot(lo, tri_bf)


def _mlstm_chunk(q, k, vext, i_col, b_col, i_row, b_row, seen, last, c_ref, m_ref, out_ref, sl):
    m_prev = m_ref[:, 0:1]
    c_prev = c_ref[...]
    q_bf = q.astype(BF16)
    log_d = jnp.where(seen, b_col - b_row + i_row, -jnp.inf)
    m_inter = b_col + m_prev
    m_t = jnp.maximum(m_inter, jnp.max(log_d, axis=-1, keepdims=True))
    qk = _dot_nt(q_bf, k.astype(BF16))
    yield
    scores = (qk * jnp.exp(log_d - m_t)).astype(BF16)
    inter_scale = jnp.exp(m_inter - m_t)
    b_last = b_col[last:last + 1, :]
    log_w = b_last - b_col + i_col
    m_new = jnp.maximum(b_last + m_prev, jnp.max(log_w, axis=0, keepdims=True))
    w = jnp.exp(log_w - m_new)
    decay = jnp.exp(b_last + m_prev - m_new)
    kw = (k * w).astype(BF16)
    yield
    numden = _dot(scores, vext) + inter_scale * _dot(q_bf, c_prev.astype(BF16))
    update = _dot_tn(kw, vext)
    yield
    num = numden[:, 0:D_HEAD]
    den = numden[:, D_HEAD:2 * D_HEAD]
    out_ref[0, :, sl] = num / jnp.maximum(jnp.abs(den), jnp.exp(-m_t))
    c_ref[...] = decay * c_prev + update
    m_ref[...] = jnp.broadcast_to(m_new, (1, LANES))


def _hgrn2_level_small(q3, k3, pre3, suf3, half, rev, sub_iota):
    upper = (sub_iota & half) != 0
    second = jnp.logical_not(upper) if rev else upper
    end = 0 if rev else half - 1
    y = jnp.where((sub_iota & (half - 1)) == end, pre3, 0.0)
    step = 1 if rev else -1
    span = 1
    while span < half:
        y = y + pltpu.roll(y, (step * span) % SUBLANES, 1)
        span *= 2
    if 2 * half == SUBLANES:
        other = pltpu.roll(y, half, 1)
    else:
        other = jnp.where(upper, pltpu.roll(y, half, 1), pltpu.roll(y, SUBLANES - half, 1))
    z = jnp.where(second, q3 * pre3, k3 * suf3)
    return z, pre3 * jnp.where(second, other, 1.0), suf3 * jnp.where(second, 1.0, other)


def _hgrn2_level_big(q, k, pre, suf, half, rev):
    L, width = q.shape
    shape = (L // (2 * half), 2, half, width)
    q4, k4, pre4, suf4 = (a.reshape(shape) for a in (q, k, pre, suf))
    first = 1 if rev else 0
    second = 1 - first
    end = 0 if rev else half - 1
    total_first = pre4[:, first, end:end + 1, :]
    total_second = pre4[:, second, end:end + 1, :]

    def join(at_first, at_second):
        parts = (at_second, at_first) if rev else (at_first, at_second)
        return jnp.stack(parts, axis=1).reshape(L, width)

    z = join(k4[:, first] * suf4[:, first], q4[:, second] * pre4[:, second])
    pre_new = join(pre4[:, first], pre4[:, second] * total_first)
    suf_new = join(suf4[:, first] * total_second, suf4[:, second])
    return z, pre_new, suf_new


def _hgrn2_chunk(q, g, v_bf, rev, level, diag, sub_iota, st_refs, out_ref, sl):
    L, width = q.shape
    heads = [slice(h * D_HEAD, (h + 1) * D_HEAD) for h in range(width // D_HEAD)]
    k = 1.0 - g
    q_bf = q.astype(BF16)
    k_bf = k.astype(BF16)
    att = [jnp.where(diag, _dot_nt(q_bf[:, s], k_bf[:, s]), 0.0) for s in heads]
    small = (L // SUBLANES, SUBLANES, width)
    q3, k3, pre, suf = q.reshape(small), k.reshape(small), g.reshape(small), jnp.ones(small, F32)
    half = 1
    bit = 0
    while half < L:
        if half == SUBLANES:
            pre, suf = pre.reshape(L, width), suf.reshape(L, width)
        if half < SUBLANES:
            z, pre, suf = _hgrn2_level_small(q3, k3, pre, suf, half, rev, sub_iota)
            z = z.reshape(L, width)
        else:
            z, pre, suf = _hgrn2_level_big(q, k, pre, suf, half, rev)
        z = z.astype(BF16)
        att = [jnp.where(level == bit, _dot_nt(z[:, s], z[:, s]), a) for a, s in zip(att, heads)]
        half *= 2
        bit += 1
        yield
    last = 0 if rev else L - 1
    q_dec = (q * pre).astype(BF16)
    k_dec = (k * suf).astype(BF16)
    outs = []
    for h, s in enumerate(heads):
        st_prev = st_refs[h][...]
        outs.append(_dot_nt(q_dec[:, s], st_prev.astype(BF16)) + _dot(att[h].astype(BF16), v_bf[:, s]))
        st_refs[h][...] = st_prev * pre[last:last + 1, s] + _dot_tn(v_bf[:, s], k_dec[:, s])
    out_ref[0, :, sl] = jnp.concatenate(outs, axis=1)


def _mixer_kernel(qf_ref, kf_ref, vf_ref, gcf_ref, grf_ref, hqf_ref, hgf_ref, hvf_ref,
                  qb_ref, kb_ref, vb_ref, gcb_ref, grb_ref, hqb_ref, hgb_ref, hvb_ref,
                  hf_ref, of_ref, hb_ref, ob_ref, *state_refs):
    L = CHUNK
    n_state = 2 * N_HEADS
    c_refs, m_refs, st_refs = (state_refs[i * n_state:(i + 1) * n_state] for i in range(3))

    @pl.when(pl.program_id(1) == 0)
    def _():
        for ref in state_refs:
            ref[...] = jnp.zeros_like(ref)

    row = lax.broadcasted_iota(jnp.int32, (L, L), 0)
    col = lax.broadcasted_iota(jnp.int32, (L, L), 1)
    sub_iota = lax.broadcasted_iota(jnp.int32, (L // SUBLANES, SUBLANES, LANES), 1)
    diag = row == col
    diff = row ^ col
    high_bit = jnp.zeros((L, L), jnp.int32)
    half = 2
    while half < L:
        high_bit = high_bit + (diff >= half).astype(jnp.int32)
        half *= 2
    ones = jnp.ones((L, D_HEAD), BF16)

    dirs = (
        (0, qf_ref, kf_ref, vf_ref, gcf_ref, grf_ref, hqf_ref, hgf_ref, hvf_ref, hf_ref, of_ref),
        (1, qb_ref, kb_ref, vb_ref, gcb_ref, grb_ref, hqb_ref, hgb_ref, hvb_ref, hb_ref, ob_ref),
    )
    stages = []
    for d, q_ref, k_ref, v_ref, gc_ref, gr_ref, hq_ref, hg_ref, hv_ref, h_out, o_out in dirs:
        rev = d == 1
        seen = (col >= row) if rev else (col <= row)
        before = (col > row) if rev else (col < row)
        level = jnp.where(before, high_bit, -1)
        tri = seen.astype(BF16)
        tri_t = (row >= col if rev else row <= col).astype(BF16)
        last = 0 if rev else L - 1
        gc = gc_ref[0]
        gr = gr_ref[0]
        gc_cum = _cumsum_rows(tri, gc)
        gr_cum = _cumsum_lanes(gr, tri_t)
        for hd in range(N_HEADS):
            sl = slice(hd * D_HEAD, (hd + 1) * D_HEAD)
            gi = d * N_HEADS + hd
            gf = 2 * N_HEADS + gi
            idx = d * N_HEADS + hd
            vext = jnp.concatenate([v_ref[0, :, sl].astype(BF16), ones], axis=1)
            stages.append(_mlstm_chunk(
                q_ref[0, :, sl], k_ref[0, :, sl], vext,
                gc[:, gi:gi + 1], gc_cum[:, gf:gf + 1], gr[gi:gi + 1, :], gr_cum[gf:gf + 1, :],
                seen, last, c_refs[idx], m_refs[idx], h_out, sl))
            stages.append(_hgrn2_chunk(
                hq_ref[0, :, sl], hg_ref[0, :, sl], hv_ref[0, :, sl].astype(BF16),
                rev, level, diag, sub_iota, [st_refs[idx]], o_out, sl))
    group_size = 2 * N_HEADS
    for start in range(0, len(stages), group_size):
        group = stages[start:start + group_size]
        while group:
            for gen in list(group):
                try:
                    next(gen)
                except StopIteration:
                    group.remove(gen)


def _mixer(q, k, v, gcol, grow, hq, g_f, g_b, hv):
    B, T, _ = q.shape
    L = CHUNK
    nc = T // L

    def fwd(width):
        return pl.BlockSpec((1, L, width), lambda b, c: (b, c, 0))

    def bwd(width):
        return pl.BlockSpec((1, L, width), lambda b, c: (b, nc - 1 - c, 0))

    grow_f = pl.BlockSpec((1, N_GATES, L), lambda b, c: (b, 0, c))
    grow_b = pl.BlockSpec((1, N_GATES, L), lambda b, c: (b, 0, nc - 1 - c))
    in_specs = ([fwd(WIDTH)] * 3 + [fwd(LANES), grow_f] + [fwd(WIDTH)] * 3
                + [bwd(WIDTH)] * 3 + [bwd(LANES), grow_b] + [bwd(WIDTH)] * 3)
    out = jax.ShapeDtypeStruct((B, T, WIDTH), F32)
    n_state = 2 * N_HEADS
    return pl.pallas_call(
        _mixer_kernel,
        grid=(B, nc),
        in_specs=in_specs,
        out_specs=[fwd(WIDTH), fwd(WIDTH), bwd(WIDTH), bwd(WIDTH)],
        out_shape=[out, out, out, out],
        scratch_shapes=([pltpu.VMEM((D_HEAD, 2 * D_HEAD), F32)] * n_state
                        + [pltpu.VMEM((1, LANES), F32)] * n_state
                        + [pltpu.VMEM((D_HEAD, D_HEAD), F32)] * n_state),
        compiler_params=pltpu.CompilerParams(
            dimension_semantics=("parallel", "arbitrary"), vmem_limit_bytes=VMEM_LIMIT),
        name="mixer",
    )(q, k, v, gcol, grow, hq, g_f, hv, q, k, v, gcol, grow, hq, g_b, hv)


def _head_norm(hsum, gain):
    parts = []
    for hd in range(N_HEADS):
        hh = hsum[:, hd * D_HEAD:(hd + 1) * D_HEAD]
        parts.append(hh * lax.rsqrt(jnp.mean(hh * hh, axis=-1, keepdims=True) + NORM_EPS))
    return jnp.concatenate(parts, axis=1) * gain


def _merge_kernel(hf_ref, hb_ref, of_ref, ob_ref, mo_ref, hgg_ref, x_ref, mn_ref, hn_ref, wo_ref,
                  n2_ref, wrh_ref, wrl_ref, br_ref, x1_ref, h2_ref, route_ref):
    m_out = _head_norm(hf_ref[0] + hb_ref[0], mn_ref[...]) * mo_ref[0]
    hg_out = _head_norm(of_ref[0] + ob_ref[0], hn_ref[...]) * hgg_ref[0]
    mixed = jnp.concatenate([m_out, hg_out], axis=1).astype(BF16)
    x1 = x_ref[0] + _dot(mixed, wo_ref[...])
    x1_ref[0] = x1
    h2 = _rms(x1, n2_ref[...])
    h2_ref[0] = _pack_bf16_pairs(h2)
    h_hi = h2.astype(BF16)
    h_hi32 = h_hi.astype(F32)

    h_lo = (h2 - h_hi32).astype(BF16)
    logits = _dot(h_hi, wrh_ref[...]) + _dot(h_lo, wrh_ref[...]) + _dot(h_hi, wrl_ref[...]) + br_ref[...]
    lane = lax.broadcasted_iota(jnp.int32, logits.shape, 1)
    big = jnp.int32(LANES)
    neg = -jnp.inf
    g_log = jnp.where(lane < N_GROUPS, logits, neg)
    g_max = jnp.max(g_log, axis=-1, keepdims=True)
    g_idx = jnp.min(jnp.where(g_log == g_max, lane, big), axis=-1, keepdims=True)
    g_val = 1.0 / jnp.sum(jnp.exp(g_log - g_max), axis=-1, keepdims=True)
    e_lo = N_GROUPS + g_idx * EXPERTS_PER_GROUP
    e_log = jnp.where((lane >= e_lo) & (lane < e_lo + EXPERTS_PER_GROUP), logits, neg)
    m1 = jnp.max(e_log, axis=-1, keepdims=True)
    i1 = jnp.min(jnp.where(e_log == m1, lane, big), axis=-1, keepdims=True)
    e_log2 = jnp.where(lane == i1, neg, e_log)
    m2 = jnp.max(e_log2, axis=-1, keepdims=True)
    i2 = jnp.min(jnp.where(e_log2 == m2, lane, big), axis=-1, keepdims=True)
    r2 = jnp.exp(m2 - m1)
    w1 = g_val / (1.0 + r2)
    w2 = g_val * r2 / (1.0 + r2)
    route = jnp.where(lane == 0, (i1 - N_GROUPS).astype(F32),
                      jnp.where(lane == 1, (i2 - N_GROUPS).astype(F32),
                                jnp.where(lane == 2, w1, jnp.where(lane == 3, w2, 0.0))))
    route_ref[0] = route


def _merge(h_f, h_b, o_f, o_b, mo, hgg, x, m_norm, hg_norm, w_out, norm2, w_rg, b_rg, w_re, b_re):
    B, T, D = x.shape
    rows = PROJ_ROWS
    n_log = N_GROUPS + N_EXPERTS
    wr = jnp.pad(jnp.concatenate([w_rg, w_re], axis=1), ((0, 0), (0, LANES - n_log)))
    br = jnp.pad(jnp.concatenate([b_rg, b_re]), (0, LANES - n_log))[None, :]
    wr_hi = wr.astype(BF16)
    wr_lo = (wr - wr_hi.astype(F32)).astype(BF16)
    consts = [m_norm[None, :], hg_norm[None, :], w_out.astype(BF16), norm2[None, :], wr_hi, wr_lo, br]

    def full(arr):
        nd = arr.ndim
        return pl.BlockSpec(arr.shape, lambda b, t: (0,) * nd)

    def tok(width):
        return pl.BlockSpec((1, rows, width), lambda b, t: (b, t, 0))

    return pl.pallas_call(
        _merge_kernel,
        grid=(B, T // rows),
        in_specs=[tok(WIDTH)] * 6 + [tok(D)] + [full(c) for c in consts],
        out_specs=[tok(D), tok(D // 2), tok(LANES)],
        out_shape=[jax.ShapeDtypeStruct((B, T, D), F32), jax.ShapeDtypeStruct((B, T, D // 2), jnp.uint32),
                   jax.ShapeDtypeStruct((B, T, LANES), F32)],
        compiler_params=pltpu.CompilerParams(
            dimension_semantics=("parallel", "parallel"), vmem_limit_bytes=VMEM_LIMIT),
        name="merge",
    )(h_f, h_b, o_f, o_b, mo, hgg, x, *consts)


def _sc_gather_rows(src, idx):
    n_out = idx.shape[0]
    D = src.shape[1]
    n_sub = SC_CORES * SC_SUBCORES
    per = n_out // n_sub
    window = SC_WINDOW_BYTES // (D * src.dtype.itemsize)
    assert per * n_sub == n_out and per % window == 0, (n_out, per, window)
    mesh = plsc.VectorSubcoreMesh(core_axis_name="c", subcore_axis_name="s",
                                  num_cores=SC_CORES, num_subcores=SC_SUBCORES)

    def body(src_hbm, idx_hbm, out_hbm, idx_v, buf):
        base = (lax.axis_index("c") * SC_SUBCORES + lax.axis_index("s")) * per
        pltpu.sync_copy(idx_hbm.at[pl.ds(base, per)], idx_v)

        @pl.loop(0, per // window)
        def _(j):
            pltpu.sync_copy(src_hbm.at[idx_v.at[pl.ds(j * window, window)]], buf)
            pltpu.sync_copy(buf, out_hbm.at[pl.ds(base + j * window, window)])

    return pl.kernel(
        body,
        out_type=jax.ShapeDtypeStruct((n_out, D), src.dtype),
        mesh=mesh,
        scratch_types=[pltpu.VMEM((per,), jnp.int32), pltpu.VMEM((window, D), src.dtype)],
        name="sc_gather_rows",
    )(src, idx)


def _sc_scatter_rows(src, idx, n_out):
    n_in, D = src.shape
    n_sub = SC_CORES * SC_SUBCORES
    per = n_in // n_sub
    window = SC_WINDOW_BYTES // (D * src.dtype.itemsize)
    assert per * n_sub == n_in and per % window == 0, (n_in, per, window)
    mesh = plsc.VectorSubcoreMesh(core_axis_name="c", subcore_axis_name="s",
                                  num_cores=SC_CORES, num_subcores=SC_SUBCORES)

    def body(src_hbm, idx_hbm, out_hbm, idx_v, buf):
        base = (lax.axis_index("c") * SC_SUBCORES + lax.axis_index("s")) * per
        pltpu.sync_copy(idx_hbm.at[pl.ds(base, per)], idx_v)

        @pl.loop(0, per // window)
        def _(j):
            pltpu.sync_copy(src_hbm.at[pl.ds(base + j * window, window)], buf)
            pltpu.sync_copy(buf, out_hbm.at[idx_v.at[pl.ds(j * window, window)]])

    return pl.kernel(
        body,
        out_type=jax.ShapeDtypeStruct((n_out, D), src.dtype),
        mesh=mesh,
        scratch_types=[pltpu.VMEM((per,), jnp.int32), pltpu.VMEM((window, D), src.dtype)],
        name="sc_scatter_rows",
    )(src, idx)


def _expert_kernel(be_ref, nu_ref, x_ref, w1_ref, w3_ref, w2_ref, o_ref, w1_bf, w3_bf, w2_bf):
    i = pl.program_id(0)
    active = i < nu_ref[0]
    new_expert = (i == 0) | (be_ref[i] != be_ref[jnp.maximum(i - 1, 0)])

    @pl.when(active & new_expert)
    def _():
        w1_bf[...] = w1_ref[0].astype(BF16)
        w3_bf[...] = w3_ref[0].astype(BF16)
        w2_bf[...] = w2_ref[0].astype(BF16)

    @pl.when(active)
    def _():
        half = D_MODEL // 2
        x_lo, x_hi = (part.astype(BF16) for part in _unpack_bf16_pairs(x_ref[...]))
        a = _dot(x_lo, w1_bf[0:half, :]) + _dot(x_hi, w1_bf[half:, :])
        b = _dot(x_lo, w3_bf[0:half, :]) + _dot(x_hi, w3_bf[half:, :])
        o_ref[...] = _pack_bf16_pairs(_dot((_silu(a) * b).astype(BF16), w2_bf[...]))


def _experts(xs, block_e, n_used, w1, w3, w2):
    rows = EXPERT_ROWS
    n_blocks = xs.shape[0] // rows
    D = D_MODEL

    def blk(i, be, nu):
        return jnp.minimum(i, nu[0] - 1)

    grid_spec = pltpu.PrefetchScalarGridSpec(
        num_scalar_prefetch=2,
        grid=(n_blocks,),
        in_specs=[
            pl.BlockSpec((rows, D // 2), lambda i, be, nu: (blk(i, be, nu), 0)),
            pl.BlockSpec((1, D, EXPERT_FF), lambda i, be, nu: (be[blk(i, be, nu)], 0, 0)),
            pl.BlockSpec((1, D, EXPERT_FF), lambda i, be, nu: (be[blk(i, be, nu)], 0, 0)),
            pl.BlockSpec((1, EXPERT_FF, D), lambda i, be, nu: (be[blk(i, be, nu)], 0, 0)),
        ],
        out_specs=pl.BlockSpec((rows, D // 2), lambda i, be, nu: (blk(i, be, nu), 0)),
        scratch_shapes=[pltpu.VMEM((D, EXPERT_FF), BF16), pltpu.VMEM((D, EXPERT_FF), BF16),
                        pltpu.VMEM((EXPERT_FF, D), BF16)],
    )
    return pl.pallas_call(
        _expert_kernel,
        grid_spec=grid_spec,
        out_shape=jax.ShapeDtypeStruct((xs.shape[0], D // 2), jnp.uint32),
        compiler_params=pltpu.CompilerParams(
            dimension_semantics=("arbitrary",), vmem_limit_bytes=VMEM_LIMIT),
        name="experts",
    )(block_e, n_used, xs, w1, w3, w2)


def _combine_kernel(y0_ref, y1_ref, x1a_ref, x1b_ref, ra_ref, rb_ref, nf_ref, ya_ref, yb_ref, *, n_a):
    def finish(x1_ref, route_ref, out_ref):
        route = route_ref[...]
        r0 = jnp.concatenate(_unpack_bf16_pairs(y0_ref[...]), axis=1)
        r1 = jnp.concatenate(_unpack_bf16_pairs(y1_ref[...]), axis=1)
        y = x1_ref[...] + route[:, 2:3] * r0 + route[:, 3:4] * r1
        out_ref[...] = _rms(y, nf_ref[...])

    @pl.when(pl.program_id(0) < n_a)
    def _():
        finish(x1a_ref, ra_ref, ya_ref)

    @pl.when(pl.program_id(0) >= n_a)
    def _():
        finish(x1b_ref, rb_ref, yb_ref)


def _combine(x1_a, x1_b, route_a, route_b, y_rows, norm_f):
    D = x1_a.shape[1]
    rows = PROJ_ROWS
    n_a = x1_a.shape[0] // rows
    n_b = x1_b.shape[0] // rows
    nt = n_a + n_b

    def side_a(width):
        return pl.BlockSpec((rows, width), lambda i: (jnp.minimum(i, n_a - 1), 0))

    def side_b(width):
        return pl.BlockSpec((rows, width), lambda i: (jnp.maximum(i - n_a, 0), 0))

    return pl.pallas_call(
        functools.partial(_combine_kernel, n_a=n_a),
        grid=(nt,),
        in_specs=[
            pl.BlockSpec((rows, D // 2), lambda i: (i, 0)),
            pl.BlockSpec((rows, D // 2), lambda i: (i + nt, 0)),
            side_a(D), side_b(D), side_a(LANES), side_b(LANES),
            pl.BlockSpec((1, D), lambda i: (0, 0)),
        ],
        out_specs=[side_a(D), side_b(D)],
        out_shape=[jax.ShapeDtypeStruct(x1_a.shape, F32), jax.ShapeDtypeStruct(x1_b.shape, F32)],
        compiler_params=pltpu.CompilerParams(
            dimension_semantics=("arbitrary",), vmem_limit_bytes=VMEM_LIMIT),
        name="combine",
    )(y_rows, y_rows, x1_a, x1_b, route_a, route_b, norm_f[None, :])


def _dispatch_plan(expert_ids):
    N = expert_ids.shape[0]
    A = N * TOP_K
    blk = EXPERT_ROWS
    flat_e = expert_ids.T.reshape(A)
    _, order = lax.sort((flat_e, jnp.arange(A, dtype=jnp.int32)), num_keys=1)
    experts = jnp.arange(N_EXPERTS, dtype=jnp.int32)
    counts = jnp.sum((flat_e[None, :] == experts[:, None]).astype(jnp.int32), axis=1)
    seg_end = jnp.cumsum(counts)
    seg_start = seg_end - counts
    padded = ((counts + blk - 1) // blk) * blk
    pad_end = jnp.cumsum(padded)
    pad_start = pad_end - padded
    n_blocks = (A + N_EXPERTS * (blk - 1) + blk - 1) // blk
    block_start = jnp.arange(n_blocks, dtype=jnp.int32) * blk
    block_e = jnp.sum((pad_end[None, :] <= block_start[:, None]).astype(jnp.int32), axis=1)
    block_e = jnp.minimum(block_e, N_EXPERTS - 1)
    row = block_start[:, None] + jnp.arange(blk, dtype=jnp.int32)[None, :]
    j = row - pad_start[block_e][:, None]
    valid = j < counts[block_e][:, None]
    assign = order[jnp.clip(seg_start[block_e][:, None] + j, 0, A - 1)]
    row_tok = jnp.where(valid, assign % N, row % N).reshape(n_blocks * blk)
    row_dst = jnp.where(valid, assign, A + row).reshape(n_blocks * blk)
    n_used = (pad_end[-1] // blk).astype(jnp.int32).reshape(1)
    return row_tok, row_dst, block_e, n_used


def _pre_moe(x, norm1, w_in, b_in, conv_w, conv_b, m_fgate_bias, m_norm, hg_lb_logits, hg_norm, w_out, norm2,
             w_rg, b_rg, w_re, b_re):
    B, T, D = x.shape
    N = B * T
    q, k, v, mo, gcol, grow, hq, g_f, g_b, hv, hgg = _in_proj(
        x, norm1, w_in, b_in, conv_w, conv_b, m_fgate_bias, hg_lb_logits)
    h_f, o_f, h_b, o_b = _mixer(q, k, v, gcol, grow, hq, g_f, g_b, hv)
    x1, h2, route = _merge(h_f, h_b, o_f, o_b, mo, hgg, x, m_norm, hg_norm, w_out, norm2, w_rg, b_rg, w_re, b_re)
    return x1.reshape(N, D), h2.reshape(N, D // 2), route.reshape(N, LANES)


def kernel(x_prompt, x_sample, norm1, w_in, b_in, conv_w, conv_b, m_fgate_bias, m_norm, hg_lb_logits, hg_norm,
           w_out, norm2, w_router_group, b_router_group, w_router_expert, b_router_expert, w1, w3, w2, norm_f):
    layer = 0
    args = (norm1[layer], w_in[layer], b_in[layer], conv_w[layer], conv_b[layer], m_fgate_bias[layer],
            m_norm[layer], hg_lb_logits, hg_norm[layer], w_out[layer], norm2[layer],
            w_router_group[layer], b_router_group[layer], w_router_expert[layer], b_router_expert[layer])
    x1_p, h2_p, route_p = _pre_moe(x_prompt, *args)
    x1_s, h2_s, route_s = _pre_moe(x_sample, *args)
    h2 = jnp.concatenate([h2_p, h2_s], axis=0)
    expert_ids = jnp.concatenate([route_p[:, 0:TOP_K], route_s[:, 0:TOP_K]], axis=0).astype(jnp.int32)
    row_tok, row_dst, block_e, n_used = _dispatch_plan(expert_ids)
    xs = _sc_gather_rows(h2, row_tok)
    out_rows = _experts(xs, block_e, n_used, w1[layer], w3[layer], w2[layer])
    y_rows = _sc_scatter_rows(out_rows, row_dst, TOP_K * h2.shape[0] + out_rows.shape[0])
    y_p, y_s = _combine(x1_p, x1_s, route_p, route_s, y_rows, norm_f)
    return (y_p.reshape(x_prompt.shape), y_s.reshape(x_sample.shape))
```

```python
import functools

import jax
import jax.numpy as jnp
from jax import lax
from jax.experimental import pallas as pl
from jax.experimental.pallas import tpu as pltpu
from jax.experimental.pallas import tpu_sc as plsc

F32 = jnp.float32
BF16 = jnp.bfloat16

D_MODEL = 1024
N_HEADS = 4
D_HEAD = 128
WIDTH = N_HEADS * D_HEAD
CONV_K = 5
CONV_PAD = CONV_K // 2
N_GROUPS = 4
EXPERTS_PER_GROUP = 8
N_EXPERTS = N_GROUPS * EXPERTS_PER_GROUP
TOP_K = 2
EXPERT_FF = D_MODEL // 2
NORM_EPS = 1e-6

LANES = 128
SUBLANES = 8
CHUNK = 128
PROJ_ROWS = 256
IN_PROJ_ROWS = 512
IN_PROJ_STREAM = 256
HALO = SUBLANES
EXPERT_ROWS = 512
N_GATES = 4 * N_HEADS
SC_CORES = 2
SC_SUBCORES = 16
SC_WINDOW_BYTES = 128 * 1024
VMEM_LIMIT = 56 * 1024 * 1024


def _dot(a, b):
    return jnp.dot(a, b, preferred_element_type=F32)


def _dot_nt(a, b):
    return lax.dot_general(a, b, (((1,), (1,)), ((), ())), preferred_element_type=F32)


def _dot_tn(a, b):
    return lax.dot_general(a, b, (((0,), (0,)), ((), ())), preferred_element_type=F32)


def _split3(x):
    hi = x.astype(BF16)
    r1 = x - hi.astype(F32)
    mid = r1.astype(BF16)
    lo = (r1 - mid.astype(F32)).astype(BF16)
    return hi, mid, lo


def _pack_bf16_pairs(x):
    half = x.shape[1] // 2
    bits = lax.bitcast_convert_type(x.astype(BF16).astype(F32), jnp.uint32)
    return (bits[:, half:] & jnp.uint32(0xFFFF0000)) | (bits[:, :half] >> 16)


def _unpack_bf16_pairs(words):
    lo = lax.bitcast_convert_type(words << 16, F32)
    hi = lax.bitcast_convert_type(words & jnp.uint32(0xFFFF0000), F32)
    return lo, hi


def _silu(x):
    return x * jax.nn.sigmoid(x)


def _log_sigmoid(x):
    return -(jnp.maximum(-x, 0.0) + jnp.log1p(jnp.exp(-jnp.abs(x))))


def _rms(x, gain):
    return x * lax.rsqrt(jnp.mean(x * x, axis=-1, keepdims=True) + NORM_EPS) * gain


def _run_round_robin(generators):
    live = list(generators)
    while live:
        for gen in list(live):
            try:
                next(gen)
            except StopIteration:
                live.remove(gen)


def _in_proj_kernel(x_ref, xp_ref, xn_ref, n1_ref, wa_ref, ba_ref, wg_ref, bg_ref, wgt_ref, bgt_ref,
                    fbrow_ref, fbcol_ref, wh_ref, bh_ref, cw_ref, cb_ref, lbl_ref,
                    q_ref, k_ref, v_ref, mo_ref, gcol_ref, grow_ref, hq_ref, gf_ref, gb_ref, hv_ref, hgg_ref,
                    ext_ref):
    t = pl.program_id(1)
    nt = pl.num_programs(1)
    rows = x_ref.shape[1]
    gain = n1_ref[...]

    lbl = lbl_ref[...]
    lmax = jnp.max(lbl, axis=0, keepdims=True)
    le = jnp.exp(lbl - lmax)
    lb = le[0:1, :] / jnp.sum(le, axis=0, keepdims=True)

    wqk = wa_ref[:, 0:2 * WIDTH]
    bqk = ba_ref[:, 0:2 * WIDTH]
    hp = _rms(xp_ref[0], gain).astype(BF16)
    hn = _rms(xn_ref[0], gain).astype(BF16)
    ext_ref[0:HALO, :] = (_dot(hp, wqk) + bqk) * (t > 0).astype(F32)
    ext_ref[HALO + rows:2 * HALO + rows, :] = (_dot(hn, wqk) + bqk) * (t < nt - 1).astype(F32)

    def stream(r0, n):
        rs = slice(r0, r0 + n)
        h = _rms(x_ref[0, rs, :], gain).astype(BF16)

        def proj(w_ref, b_ref, lo, hi):
            return _dot(h, w_ref[:, lo:hi]) + b_ref[:, lo:hi]

        ext_ref[HALO + r0:HALO + r0 + n, :] = proj(wa_ref, ba_ref, 0, 2 * WIDTH)
        hq_pre = proj(wh_ref, bh_ref, 0, WIDTH)
        yield
        acc = cb_ref[...] + ext_ref[pl.ds(HALO - CONV_PAD + r0, n), :] * cw_ref[0:1, :]
        for j in range(1, CONV_K):
            acc = acc + ext_ref[pl.ds(HALO - CONV_PAD + j + r0, n), :] * cw_ref[j:j + 1, :]
        qk = _silu(acc)
        q_ref[0, rs, :] = qk[:, 0:WIDTH] * (D_HEAD ** -0.5)
        k_ref[0, rs, :] = qk[:, WIDTH:2 * WIDTH]
        v_ref[0, rs, :] = proj(wa_ref, ba_ref, 2 * WIDTH, 3 * WIDTH)
        yield
        hq_ref[0, rs, :] = _silu(hq_pre)
        mo_ref[0, rs, :] = jax.nn.sigmoid(proj(wa_ref, ba_ref, 3 * WIDTH, 4 * WIDTH))
        yield
        gf_ref[0, rs, :] = lb + (1.0 - lb) * jax.nn.sigmoid(proj(wh_ref, bh_ref, WIDTH, 2 * WIDTH))
        yield
        gb_ref[0, rs, :] = lb + (1.0 - lb) * jax.nn.sigmoid(proj(wh_ref, bh_ref, 2 * WIDTH, 3 * WIDTH))
        hv_ref[0, rs, :] = proj(wh_ref, bh_ref, 3 * WIDTH, 4 * WIDTH)
        yield
        hgg_ref[0, rs, :] = _silu(proj(wh_ref, bh_ref, 4 * WIDTH, 5 * WIDTH))
        gc = _dot(h, wg_ref[...]) + bg_ref[...]
        lane = lax.broadcasted_iota(jnp.int32, gc.shape, 1)
        is_f = (lane >= 2 * N_HEADS) & (lane < N_GATES)
        gcol_ref[0, rs, :] = jnp.where(is_f, _log_sigmoid(gc + fbrow_ref[...]), gc)
        gr = _dot_nt(wgt_ref[...], h) + bgt_ref[...]
        sub = lax.broadcasted_iota(jnp.int32, gr.shape, 0)
        grow_ref[0, :, rs] = jnp.where(sub >= 2 * N_HEADS, _log_sigmoid(gr + fbcol_ref[...]), gr)

    _run_round_robin([stream(r0, IN_PROJ_STREAM) for r0 in range(0, rows, IN_PROJ_STREAM)])


def _in_proj(x, norm1, w_in, b_in, conv_w, conv_b, fgate_bias, lb_logits):
    B, T, D = x.shape
    rows = IN_PROJ_ROWS
    assert T % rows == 0, (T, rows)
    nt = T // rows
    a_w = 4 * WIDTH
    wa = w_in[:, 0:a_w].astype(BF16)
    ba = b_in[None, 0:a_w]
    wg32 = jnp.pad(w_in[:, a_w:a_w + N_GATES], ((0, 0), (0, LANES - N_GATES)))
    bg = jnp.pad(b_in[a_w:a_w + N_GATES], (0, LANES - N_GATES))[None, :]
    wg = wg32.astype(BF16)
    wgt = w_in[:, a_w:a_w + N_GATES].T.astype(BF16)
    bgt = b_in[a_w:a_w + N_GATES][:, None]
    fb = fgate_bias.reshape(2 * N_HEADS)
    fbrow = jnp.zeros((1, LANES), F32).at[0, 2 * N_HEADS:N_GATES].set(fb)
    fbcol = jnp.zeros((N_GATES, 1), F32).at[2 * N_HEADS:N_GATES, 0].set(fb)
    wh = w_in[:, a_w + N_GATES:].astype(BF16)
    bh = b_in[None, a_w + N_GATES:]

    tiles_per_halo = rows // HALO
    n_halo = T // HALO

    def full(arr):
        nd = arr.ndim
        return pl.BlockSpec(arr.shape, lambda b, t: (0,) * nd)

    def tok(width):
        return pl.BlockSpec((1, rows, width), lambda b, t: (b, t, 0))

    in_specs = [
        tok(D),
        pl.BlockSpec((1, HALO, D), lambda b, t: (b, jnp.maximum(t * tiles_per_halo - 1, 0), 0)),
        pl.BlockSpec((1, HALO, D), lambda b, t: (b, jnp.minimum((t + 1) * tiles_per_halo, n_halo - 1), 0)),
    ]
    consts = [norm1[None, :], wa, ba, wg, bg, wgt, bgt, fbrow, fbcol, wh, bh, conv_w, conv_b[None, :], lb_logits]
    in_specs += [full(c) for c in consts]
    tok_out = jax.ShapeDtypeStruct((B, T, WIDTH), F32)
    out_shape = [tok_out, tok_out, tok_out, tok_out,
                 jax.ShapeDtypeStruct((B, T, LANES), F32),
                 jax.ShapeDtypeStruct((B, N_GATES, T), F32),
                 tok_out, tok_out, tok_out, tok_out, tok_out]
    out_specs = [tok(WIDTH)] * 4 + [tok(LANES), pl.BlockSpec((1, N_GATES, rows), lambda b, t: (b, 0, t))] + [tok(WIDTH)] * 5
    return pl.pallas_call(
        _in_proj_kernel,
        grid=(B, nt),
        in_specs=in_specs,
        out_specs=out_specs,
        out_shape=out_shape,
        scratch_shapes=[pltpu.VMEM((rows + 2 * HALO, 2 * WIDTH), F32)],
        compiler_params=pltpu.CompilerParams(
            dimension_semantics=("parallel", "parallel"), vmem_limit_bytes=VMEM_LIMIT),
        name="in_proj",
    )(x, x, x, *consts)


def _cumsum_rows(tri_bf, x):
    hi, mid, lo = _split3(x)
    return _dot(tri_bf, hi) + _dot(tri_bf, mid) + _dot(tri_bf, lo)


def _cumsum_lanes(x, tri_bf):
    hi, mid, lo = _split3(x)
    return _dot(hi, tri_bf) + _dot(mid, tri_bf) + _dot(lo, tri_bf)


def _mlstm_chunk(q, k, vext, i_col, b_col, i_row, b_row, seen, last, c_ref, m_ref, out_ref, sl):
    m_prev = m_ref[:, 0:1]
    c_prev = c_ref[...]
    q_bf = q.astype(BF16)
    log_d = jnp.where(seen, b_col - b_row + i_row, -jnp.inf)
    m_inter = b_col + m_prev
    m_t = jnp.maximum(m_inter, jnp.max(log_d, axis=-1, keepdims=True))
    qk = _dot_nt(q_bf, k.astype(BF16))
    yield
    scores = (qk * jnp.exp(log_d - m_t)).astype(BF16)
    inter_scale = jnp.exp(m_inter - m_t)
    b_last = b_col[last:last + 1, :]
    log_w = b_last - b_col + i_col
    m_new = jnp.maximum(b_last + m_prev, jnp.max(log_w, axis=0, keepdims=True))
    w = jnp.exp(log_w - m_new)
    decay = jnp.exp(b_last + m_prev - m_new)
    kw = (k * w).astype(BF16)
    yield
    numden = _dot(scores, vext) + inter_scale * _dot(q_bf, c_prev.astype(BF16))
    update = _dot_tn(kw, vext)
    yield
    num = numden[:, 0:D_HEAD]
    den = numden[:, D_HEAD:2 * D_HEAD]
    out_ref[0, :, sl] = num / jnp.maximum(jnp.abs(den), jnp.exp(-m_t))
    c_ref[...] = decay * c_prev + update
    m_ref[...] = jnp.broadcast_to(m_new, (1, LANES))


def _hgrn2_level_small(q3, k3, pre3, suf3, half, rev, sub_iota):
    upper = (sub_iota & half) != 0
    second = jnp.logical_not(upper) if rev else upper
    end = 0 if rev else half - 1
    y = jnp.where((sub_iota & (half - 1)) == end, pre3, 0.0)
    step = 1 if rev else -1
    span = 1
    while span < half:
        y = y + pltpu.roll(y, (step * span) % SUBLANES, 1)
        span *= 2
    if 2 * half == SUBLANES:
        other = pltpu.roll(y, half, 1)
    else:
        other = jnp.where(upper, pltpu.roll(y, half, 1), pltpu.roll(y, SUBLANES - half, 1))
    z = jnp.where(second, q3 * pre3, k3 * suf3)
    return z, pre3 * jnp.where(second, other, 1.0), suf3 * jnp.where(second, 1.0, other)


def _hgrn2_level_big(q, k, pre, suf, half, rev):
    L, width = q.shape
    shape = (L // (2 * half), 2, half, width)
    q4, k4, pre4, suf4 = (a.reshape(shape) for a in (q, k, pre, suf))
    first = 1 if rev else 0
    second = 1 - first
    end = 0 if rev else half - 1
    total_first = pre4[:, first, end:end + 1, :]
    total_second = pre4[:, second, end:end + 1, :]

    def join(at_first, at_second):
        parts = (at_second, at_first) if rev else (at_first, at_second)
        return jnp.stack(parts, axis=1).reshape(L, width)

    z = join(k4[:, first] * suf4[:, first], q4[:, second] * pre4[:, second])
    pre_new = join(pre4[:, first], pre4[:, second] * total_first)
    suf_new = join(suf4[:, first] * total_second, suf4[:, second])
    return z, pre_new, suf_new


def _hgrn2_chunk(q, g, v_bf, rev, level, diag, sub_iota, st_refs, out_ref, sl):
    L, width = q.shape
    heads = [slice(h * D_HEAD, (h + 1) * D_HEAD) for h in range(width // D_HEAD)]
    k = 1.0 - g
    q_bf = q.astype(BF16)
    k_bf = k.astype(BF16)
    att = [jnp.where(diag, _dot_nt(q_bf[:, s], k_bf[:, s]), 0.0) for s in heads]
    small = (L // SUBLANES, SUBLANES, width)
    q3, k3, pre, suf = q.reshape(small), k.reshape(small), g.reshape(small), jnp.ones(small, F32)
    half = 1
    bit = 0
    while half < L:
        if half == SUBLANES:
            pre, suf = pre.reshape(L, width), suf.reshape(L, width)
        if half < SUBLANES:
            z, pre, suf = _hgrn2_level_small(q3, k3, pre, suf, half, rev, sub_iota)
            z = z.reshape(L, width)
        else:
            z, pre, suf = _hgrn2_level_big(q, k, pre, suf, half, rev)
        z = z.astype(BF16)
        att = [jnp.where(level == bit, _dot_nt(z[:, s], z[:, s]), a) for a, s in zip(att, heads)]
        half *= 2
        bit += 1
        yield
    last = 0 if rev else L - 1
    q_dec = (q * pre).astype(BF16)
    k_dec = (k * suf).astype(BF16)
    outs = []
    for h, s in enumerate(heads):
        st_prev = st_refs[h][...]
        outs.append(_dot_nt(q_dec[:, s], st_prev.astype(BF16)) + _dot(att[h].astype(BF16), v_bf[:, s]))
        st_refs[h][...] = st_prev * pre[last:last + 1, s] + _dot_tn(v_bf[:, s], k_dec[:, s])
    out_ref[0, :, sl] = jnp.concatenate(outs, axis=1)


def _mixer_kernel(qf_ref, kf_ref, vf_ref, gcf_ref, grf_ref, hqf_ref, hgf_ref, hvf_ref,
                  qb_ref, kb_ref, vb_ref, gcb_ref, grb_ref, hqb_ref, hgb_ref, hvb_ref,
                  hf_ref, of_ref, hb_ref, ob_ref, *state_refs):
    L = CHUNK
    n_state = 2 * N_HEADS
    c_refs, m_refs, st_refs = (state_refs[i * n_state:(i + 1) * n_state] for i in range(3))

    @pl.when(pl.program_id(1) == 0)
    def _():
        for ref in state_refs:
            ref[...] = jnp.zeros_like(ref)

    row = lax.broadcasted_iota(jnp.int32, (L, L), 0)
    col = lax.broadcasted_iota(jnp.int32, (L, L), 1)
    sub_iota = lax.broadcasted_iota(jnp.int32, (L // SUBLANES, SUBLANES, LANES), 1)
    diag = row == col
    diff = row ^ col
    high_bit = jnp.zeros((L, L), jnp.int32)
    half = 2
    while half < L:
        high_bit = high_bit + (diff >= half).astype(jnp.int32)
        half *= 2
    ones = jnp.ones((L, D_HEAD), BF16)

    dirs = (
        (0, qf_ref, kf_ref, vf_ref, gcf_ref, grf_ref, hqf_ref, hgf_ref, hvf_ref, hf_ref, of_ref),
        (1, qb_ref, kb_ref, vb_ref, gcb_ref, grb_ref, hqb_ref, hgb_ref, hvb_ref, hb_ref, ob_ref),
    )
    stages = []
    for d, q_ref, k_ref, v_ref, gc_ref, gr_ref, hq_ref, hg_ref, hv_ref, h_out, o_out in dirs:
        rev = d == 1
        seen = (col >= row) if rev else (col <= row)
        before = (col > row) if rev else (col < row)
        level = jnp.where(before, high_bit, -1)
        tri = seen.astype(BF16)
        tri_t = (row >= col if rev else row <= col).astype(BF16)
        last = 0 if rev else L - 1
        gc = gc_ref[0]
        gr = gr_ref[0]
        gc_cum = _cumsum_rows(tri, gc)
        gr_cum = _cumsum_lanes(gr, tri_t)
        for hd in range(N_HEADS):
            sl = slice(hd * D_HEAD, (hd + 1) * D_HEAD)
            gi = d * N_HEADS + hd
            gf = 2 * N_HEADS + gi
            idx = d * N_HEADS + hd
            vext = jnp.concatenate([v_ref[0, :, sl].astype(BF16), ones], axis=1)
            stages.append(_mlstm_chunk(
                q_ref[0, :, sl], k_ref[0, :, sl], vext,
                gc[:, gi:gi + 1], gc_cum[:, gf:gf + 1], gr[gi:gi + 1, :], gr_cum[gf:gf + 1, :],
                seen, last, c_refs[idx], m_refs[idx], h_out, sl))
            stages.append(_hgrn2_chunk(
                hq_ref[0, :, sl], hg_ref[0, :, sl], hv_ref[0, :, sl].astype(BF16),
                rev, level, diag, sub_iota, [st_refs[idx]], o_out, sl))
    group = 2 * N_HEADS
    for start in range(0, len(stages), group):
        _run_round_robin(stages[start:start + group])


def _mixer(q, k, v, gcol, grow, hq, g_f, g_b, hv):
    B, T, _ = q.shape
    L = CHUNK
    nc = T // L

    def fwd(width):
        return pl.BlockSpec((1, L, width), lambda b, c: (b, c, 0))

    def bwd(width):
        return pl.BlockSpec((1, L, width), lambda b, c: (b, nc - 1 - c, 0))

    grow_f = pl.BlockSpec((1, N_GATES, L), lambda b, c: (b, 0, c))
    grow_b = pl.BlockSpec((1, N_GATES, L), lambda b, c: (b, 0, nc - 1 - c))
    in_specs = ([fwd(WIDTH)] * 3 + [fwd(LANES), grow_f] + [fwd(WIDTH)] * 3
                + [bwd(WIDTH)] * 3 + [bwd(LANES), grow_b] + [bwd(WIDTH)] * 3)
    out = jax.ShapeDtypeStruct((B, T, WIDTH), F32)
    n_state = 2 * N_HEADS
    return pl.pallas_call(
        _mixer_kernel,
        grid=(B, nc),
        in_specs=in_specs,
        out_specs=[fwd(WIDTH), fwd(WIDTH), bwd(WIDTH), bwd(WIDTH)],
        out_shape=[out, out, out, out],
        scratch_shapes=([pltpu.VMEM((D_HEAD, 2 * D_HEAD), F32)] * n_state
                        + [pltpu.VMEM((1, LANES), F32)] * n_state
                        + [pltpu.VMEM((D_HEAD, D_HEAD), F32)] * n_state),
        compiler_params=pltpu.CompilerParams(
            dimension_semantics=("parallel", "arbitrary"), vmem_limit_bytes=VMEM_LIMIT),
        name="mixer",
    )(q, k, v, gcol, grow, hq, g_f, hv, q, k, v, gcol, grow, hq, g_b, hv)


def _head_norm(hsum, gain):
    parts = []
    for hd in range(N_HEADS):
        hh = hsum[:, hd * D_HEAD:(hd + 1) * D_HEAD]
        parts.append(hh * lax.rsqrt(jnp.mean(hh * hh, axis=-1, keepdims=True) + NORM_EPS))
    return jnp.concatenate(parts, axis=1) * gain


def _merge_tile(hf_ref, hb_ref, of_ref, ob_ref, mo_ref, hgg_ref, x_ref, mn_ref, hn_ref, wo_ref,
                n2_ref, wrh_ref, wrl_ref, br_ref, x1_ref, h2_ref, route_ref):
    m_out = _head_norm(hf_ref[...] + hb_ref[...], mn_ref[...]) * mo_ref[...]
    hg_out = _head_norm(of_ref[...] + ob_ref[...], hn_ref[...]) * hgg_ref[...]
    mixed = jnp.concatenate([m_out, hg_out], axis=1).astype(BF16)
    x1 = x_ref[...] + _dot(mixed, wo_ref[...])
    x1_ref[...] = x1
    h2 = _rms(x1, n2_ref[...])
    h2_ref[...] = _pack_bf16_pairs(h2)
    h_hi = h2.astype(BF16)
    h_hi32 = h_hi.astype(F32)

    h_lo = (h2 - h_hi32).astype(BF16)
    logits = _dot(h_hi, wrh_ref[...]) + _dot(h_lo, wrh_ref[...]) + _dot(h_hi, wrl_ref[...]) + br_ref[...]
    lane = lax.broadcasted_iota(jnp.int32, logits.shape, 1)
    big = jnp.int32(LANES)
    neg = -jnp.inf
    g_log = jnp.where(lane < N_GROUPS, logits, neg)
    g_max = jnp.max(g_log, axis=-1, keepdims=True)
    g_idx = jnp.min(jnp.where(g_log == g_max, lane, big), axis=-1, keepdims=True)
    g_val = 1.0 / jnp.sum(jnp.exp(g_log - g_max), axis=-1, keepdims=True)
    e_lo = N_GROUPS + g_idx * EXPERTS_PER_GROUP
    e_log = jnp.where((lane >= e_lo) & (lane < e_lo + EXPERTS_PER_GROUP), logits, neg)
    m1 = jnp.max(e_log, axis=-1, keepdims=True)
    i1 = jnp.min(jnp.where(e_log == m1, lane, big), axis=-1, keepdims=True)
    e_log2 = jnp.where(lane == i1, neg, e_log)
    m2 = jnp.max(e_log2, axis=-1, keepdims=True)
    i2 = jnp.min(jnp.where(e_log2 == m2, lane, big), axis=-1, keepdims=True)
    r2 = jnp.exp(m2 - m1)
    w1 = g_val / (1.0 + r2)
    w2 = g_val * r2 / (1.0 + r2)
    route = jnp.where(lane == 0, (i1 - N_GROUPS).astype(F32),
                      jnp.where(lane == 1, (i2 - N_GROUPS).astype(F32),
                                jnp.where(lane == 2, w1, jnp.where(lane == 3, w2, 0.0))))
    route_ref[...] = route


N_MERGE_STREAMS = 7


def _merge_kernel(*refs, n_a):
    side_a = refs[0:N_MERGE_STREAMS]
    side_b = refs[N_MERGE_STREAMS:2 * N_MERGE_STREAMS]
    rest = refs[2 * N_MERGE_STREAMS:]

    @pl.when(pl.program_id(0) < n_a)
    def _():
        _merge_tile(*side_a, *rest)

    @pl.when(pl.program_id(0) >= n_a)
    def _():
        _merge_tile(*side_b, *rest)


def _merge(streams_a, streams_b, m_norm, hg_norm, w_out, norm2, w_rg, b_rg, w_re, b_re):
    D = D_MODEL
    rows = PROJ_ROWS
    n_a = streams_a[0].shape[0] // rows
    n_b = streams_b[0].shape[0] // rows
    n_all = (n_a + n_b) * rows
    n_log = N_GROUPS + N_EXPERTS
    wr = jnp.pad(jnp.concatenate([w_rg, w_re], axis=1), ((0, 0), (0, LANES - n_log)))
    br = jnp.pad(jnp.concatenate([b_rg, b_re]), (0, LANES - n_log))[None, :]
    wr_hi = wr.astype(BF16)
    wr_lo = (wr - wr_hi.astype(F32)).astype(BF16)
    consts = [m_norm[None, :], hg_norm[None, :], w_out.astype(BF16), norm2[None, :], wr_hi, wr_lo, br]

    def full(arr):
        nd = arr.ndim
        return pl.BlockSpec(arr.shape, lambda i: (0,) * nd)

    def side_a(arr):
        return pl.BlockSpec((rows, arr.shape[1]), lambda i: (jnp.minimum(i, n_a - 1), 0))

    def side_b(arr):
        return pl.BlockSpec((rows, arr.shape[1]), lambda i: (jnp.maximum(i - n_a, 0), 0))

    def out(width):
        return pl.BlockSpec((rows, width), lambda i: (i, 0))

    return pl.pallas_call(
        functools.partial(_merge_kernel, n_a=n_a),
        grid=(n_a + n_b,),
        in_specs=[side_a(s) for s in streams_a] + [side_b(s) for s in streams_b] + [full(c) for c in consts],
        out_specs=[out(D), out(D // 2), out(LANES)],
        out_shape=[jax.ShapeDtypeStruct((n_all, D), F32), jax.ShapeDtypeStruct((n_all, D // 2), jnp.uint32),
                   jax.ShapeDtypeStruct((n_all, LANES), F32)],
        compiler_params=pltpu.CompilerParams(
            dimension_semantics=("arbitrary",), vmem_limit_bytes=VMEM_LIMIT),
        name="merge",
    )(*streams_a, *streams_b, *consts)


def _sc_row_mover(src, idx, n_out, scatter, name):
    n_moved = idx.shape[0]
    D = src.shape[1]
    n_sub = SC_CORES * SC_SUBCORES
    per = n_moved // n_sub
    window = SC_WINDOW_BYTES // (D * src.dtype.itemsize)
    assert per * n_sub == n_moved and per % window == 0, (n_moved, per, window)
    mesh = plsc.VectorSubcoreMesh(core_axis_name="c", subcore_axis_name="s",
                                  num_cores=SC_CORES, num_subcores=SC_SUBCORES)

    def body(src_hbm, idx_hbm, out_hbm, idx_v, buf):
        base = (lax.axis_index("c") * SC_SUBCORES + lax.axis_index("s")) * per
        pltpu.sync_copy(idx_hbm.at[pl.ds(base, per)], idx_v)

        @pl.loop(0, per // window)
        def _(j):
            linear = pl.ds(base + j * window, window)
            indexed = idx_v.at[pl.ds(j * window, window)]
            if scatter:
                pltpu.sync_copy(src_hbm.at[linear], buf)
                pltpu.sync_copy(buf, out_hbm.at[indexed])
            else:
                pltpu.sync_copy(src_hbm.at[indexed], buf)
                pltpu.sync_copy(buf, out_hbm.at[linear])

    return pl.kernel(
        body,
        out_type=jax.ShapeDtypeStruct((n_out, D), src.dtype),
        mesh=mesh,
        scratch_types=[pltpu.VMEM((per,), jnp.int32), pltpu.VMEM((window, D), src.dtype)],
        name=name,
    )(src, idx)


def _sc_gather_rows(src, idx):
    return _sc_row_mover(src, idx, idx.shape[0], False, "sc_gather_rows")


def _sc_scatter_rows(src, idx, n_out):
    return _sc_row_mover(src, idx, n_out, True, "sc_scatter_rows")


def _expert_kernel(be_ref, nu_ref, x_ref, w1_ref, w3_ref, w2_ref, o_ref, w1_bf, w3_bf, w2_bf):
    i = pl.program_id(0)
    active = i < nu_ref[0]
    new_expert = (i == 0) | (be_ref[i] != be_ref[jnp.maximum(i - 1, 0)])

    @pl.when(active & new_expert)
    def _():
        w1_bf[...] = w1_ref[0].astype(BF16)
        w3_bf[...] = w3_ref[0].astype(BF16)
        w2_bf[...] = w2_ref[0].astype(BF16)

    @pl.when(active)
    def _():
        half = D_MODEL // 2
        x_lo, x_hi = (part.astype(BF16) for part in _unpack_bf16_pairs(x_ref[...]))
        a = _dot(x_lo, w1_bf[0:half, :]) + _dot(x_hi, w1_bf[half:, :])
        b = _dot(x_lo, w3_bf[0:half, :]) + _dot(x_hi, w3_bf[half:, :])
        o_ref[...] = _pack_bf16_pairs(_dot((_silu(a) * b).astype(BF16), w2_bf[...]))


def _experts(xs, block_e, n_used, w1, w3, w2):
    rows = EXPERT_ROWS
    n_blocks = xs.shape[0] // rows
    D = D_MODEL

    def blk(i, be, nu):
        return jnp.minimum(i, nu[0] - 1)

    grid_spec = pltpu.PrefetchScalarGridSpec(
        num_scalar_prefetch=2,
        grid=(n_blocks,),
        in_specs=[
            pl.BlockSpec((rows, D // 2), lambda i, be, nu: (blk(i, be, nu), 0)),
            pl.BlockSpec((1, D, EXPERT_FF), lambda i, be, nu: (be[blk(i, be, nu)], 0, 0)),
            pl.BlockSpec((1, D, EXPERT_FF), lambda i, be, nu: (be[blk(i, be, nu)], 0, 0)),
            pl.BlockSpec((1, EXPERT_FF, D), lambda i, be, nu: (be[blk(i, be, nu)], 0, 0)),
        ],
        out_specs=pl.BlockSpec((rows, D // 2), lambda i, be, nu: (blk(i, be, nu), 0)),
        scratch_shapes=[pltpu.VMEM((D, EXPERT_FF), BF16), pltpu.VMEM((D, EXPERT_FF), BF16),
                        pltpu.VMEM((EXPERT_FF, D), BF16)],
    )
    return pl.pallas_call(
        _expert_kernel,
        grid_spec=grid_spec,
        out_shape=jax.ShapeDtypeStruct((xs.shape[0], D // 2), jnp.uint32),
        compiler_params=pltpu.CompilerParams(
            dimension_semantics=("arbitrary",), vmem_limit_bytes=VMEM_LIMIT),
        name="experts",
    )(block_e, n_used, xs, w1, w3, w2)


def _combine_kernel(y0_ref, y1_ref, x1_ref, route_ref, nf_ref, ya_ref, yb_ref, *, n_a):
    route = route_ref[...]
    r0 = jnp.concatenate(_unpack_bf16_pairs(y0_ref[...]), axis=1)
    r1 = jnp.concatenate(_unpack_bf16_pairs(y1_ref[...]), axis=1)
    y = _rms(x1_ref[...] + route[:, 2:3] * r0 + route[:, 3:4] * r1, nf_ref[...])

    @pl.when(pl.program_id(0) < n_a)
    def _():
        ya_ref[...] = y

    @pl.when(pl.program_id(0) >= n_a)
    def _():
        yb_ref[...] = y


def _combine(x1, route, y_rows, norm_f, n_tok_a):
    N, D = x1.shape
    rows = PROJ_ROWS
    nt = N // rows
    n_a = n_tok_a // rows

    def tok(width, offset=0):
        return pl.BlockSpec((rows, width), lambda i: (i + offset, 0))

    return pl.pallas_call(
        functools.partial(_combine_kernel, n_a=n_a),
        grid=(nt,),
        in_specs=[tok(D // 2), tok(D // 2, nt), tok(D), tok(LANES), pl.BlockSpec((1, D), lambda i: (0, 0))],
        out_specs=[pl.BlockSpec((rows, D), lambda i: (jnp.minimum(i, n_a - 1), 0)),
                   pl.BlockSpec((rows, D), lambda i: (jnp.maximum(i - n_a, 0), 0))],
        out_shape=[jax.ShapeDtypeStruct((n_tok_a, D), F32), jax.ShapeDtypeStruct((N - n_tok_a, D), F32)],
        compiler_params=pltpu.CompilerParams(
            dimension_semantics=("arbitrary",), vmem_limit_bytes=VMEM_LIMIT),
        name="combine",
    )(y_rows, y_rows, x1, route, norm_f[None, :])


def _dispatch_plan(expert_ids):
    N = expert_ids.shape[0]
    A = N * TOP_K
    blk = EXPERT_ROWS
    flat_e = expert_ids.T.reshape(A)
    _, order = lax.sort((flat_e, jnp.arange(A, dtype=jnp.int32)), num_keys=1)
    experts = jnp.arange(N_EXPERTS, dtype=jnp.int32)
    counts = jnp.sum((flat_e[None, :] == experts[:, None]).astype(jnp.int32), axis=1)
    seg_end = jnp.cumsum(counts)
    seg_start = seg_end - counts
    padded = ((counts + blk - 1) // blk) * blk
    pad_end = jnp.cumsum(padded)
    pad_start = pad_end - padded
    n_blocks = (A + N_EXPERTS * (blk - 1) + blk - 1) // blk
    block_start = jnp.arange(n_blocks, dtype=jnp.int32) * blk
    block_e = jnp.sum((pad_end[None, :] <= block_start[:, None]).astype(jnp.int32), axis=1)
    block_e = jnp.minimum(block_e, N_EXPERTS - 1)
    row = block_start[:, None] + jnp.arange(blk, dtype=jnp.int32)[None, :]
    j = row - pad_start[block_e][:, None]
    valid = j < counts[block_e][:, None]
    assign = order[jnp.clip(seg_start[block_e][:, None] + j, 0, A - 1)]
    row_tok = jnp.where(valid, assign % N, row % N).reshape(n_blocks * blk)
    row_dst = jnp.where(valid, assign, A + row).reshape(n_blocks * blk)
    n_used = (pad_end[-1] // blk).astype(jnp.int32).reshape(1)
    return row_tok, row_dst, block_e, n_used


def _token_mixer(x, norm1, w_in, b_in, conv_w, conv_b, m_fgate_bias, hg_lb_logits):
    B, T, D = x.shape
    q, k, v, mo, gcol, grow, hq, g_f, g_b, hv, hgg = _in_proj(
        x, norm1, w_in, b_in, conv_w, conv_b, m_fgate_bias, hg_lb_logits)
    h_f, o_f, h_b, o_b = _mixer(q, k, v, gcol, grow, hq, g_f, g_b, hv)
    return [a.reshape(B * T, a.shape[-1]) for a in (h_f, h_b, o_f, o_b, mo, hgg, x)]


def kernel(x_prompt, x_sample, norm1, w_in, b_in, conv_w, conv_b, m_fgate_bias, m_norm, hg_lb_logits, hg_norm,
           w_out, norm2, w_router_group, b_router_group, w_router_expert, b_router_expert, w1, w3, w2, norm_f):
    layer = 0
    mixer_args = (norm1[layer], w_in[layer], b_in[layer], conv_w[layer], conv_b[layer], m_fgate_bias[layer],
                  hg_lb_logits)
    streams_p = _token_mixer(x_prompt, *mixer_args)
    streams_s = _token_mixer(x_sample, *mixer_args)
    x1, h2, route = _merge(streams_p, streams_s, m_norm[layer], hg_norm[layer], w_out[layer], norm2[layer],
                           w_router_group[layer], b_router_group[layer], w_router_expert[layer],
                           b_router_expert[layer])
    expert_ids = route[:, 0:TOP_K].astype(jnp.int32)
    row_tok, row_dst, block_e, n_used = _dispatch_plan(expert_ids)
    xs = _sc_gather_rows(h2, row_tok)
    out_rows = _experts(xs, block_e, n_used, w1[layer], w3[layer], w2[layer])
    y_rows = _sc_scatter_rows(out_rows, row_dst, TOP_K * h2.shape[0] + out_rows.shape[0])
    y_p, y_s = _combine(x1, route, y_rows, norm_f, streams_p[0].shape[0])
    return (y_p.reshape(x_prompt.shape), y_s.reshape(x_sample.shape))
```

```python
import functools

import jax
import jax.numpy as jnp
from jax import lax
from jax.experimental import pallas as pl
from jax.experimental.pallas import tpu as pltpu
from jax.experimental.pallas import tpu_sc as plsc

F32 = jnp.float32
BF16 = jnp.bfloat16

D_MODEL = 1024
N_HEADS = 4
D_HEAD = 128
WIDTH = N_HEADS * D_HEAD
CONV_K = 5
CONV_PAD = CONV_K // 2
N_GROUPS = 4
EXPERTS_PER_GROUP = 8
N_EXPERTS = N_GROUPS * EXPERTS_PER_GROUP
TOP_K = 2
EXPERT_FF = D_MODEL // 2
NORM_EPS = 1e-6

LANES = 128
SUBLANES = 8
CHUNK = 128
PROJ_ROWS = 256
IN_PROJ_ROWS = 512
IN_PROJ_STREAM = 256
HALO = SUBLANES
EXPERT_ROWS = 512
N_GATES = 4 * N_HEADS
SC_CORES = 2
SC_SUBCORES = 16
SC_WINDOW_BYTES = 128 * 1024
VMEM_LIMIT = 56 * 1024 * 1024


def _dot(a, b):
    return jnp.dot(a, b, preferred_element_type=F32)


def _dot_nt(a, b):
    return lax.dot_general(a, b, (((1,), (1,)), ((), ())), preferred_element_type=F32)


def _dot_tn(a, b):
    return lax.dot_general(a, b, (((0,), (0,)), ((), ())), preferred_element_type=F32)


def _split3(x):
    hi = x.astype(BF16)
    r1 = x - hi.astype(F32)
    mid = r1.astype(BF16)
    lo = (r1 - mid.astype(F32)).astype(BF16)
    return hi, mid, lo


def _pack_bf16_pairs(x):
    half = x.shape[1] // 2
    bits = lax.bitcast_convert_type(x.astype(BF16).astype(F32), jnp.uint32)
    return (bits[:, half:] & jnp.uint32(0xFFFF0000)) | (bits[:, :half] >> 16)


def _unpack_bf16_pairs(words):
    lo = lax.bitcast_convert_type(words << 16, F32)
    hi = lax.bitcast_convert_type(words & jnp.uint32(0xFFFF0000), F32)
    return lo, hi


def _silu(x):
    return x * jax.nn.sigmoid(x)


def _log_sigmoid(x):
    return -(jnp.maximum(-x, 0.0) + jnp.log1p(jnp.exp(-jnp.abs(x))))


def _rms(x, gain):
    return x * lax.rsqrt(jnp.mean(x * x, axis=-1, keepdims=True) + NORM_EPS) * gain


def _run_round_robin(generators):
    live = list(generators)
    while live:
        for gen in list(live):
            try:
                next(gen)
            except StopIteration:
                live.remove(gen)


def _in_proj_kernel(x_ref, xp_ref, xn_ref, n1_ref, wa_ref, ba_ref, wg_ref, bg_ref, wgt_ref, bgt_ref,
                    fbrow_ref, fbcol_ref, wh_ref, bh_ref, cw_ref, cb_ref, lbl_ref,
                    q_ref, k_ref, v_ref, mo_ref, gcol_ref, grow_ref, hq_ref, gf_ref, gb_ref, hv_ref, hgg_ref,
                    ext_ref):
    t = pl.program_id(1)
    nt = pl.num_programs(1)
    rows = x_ref.shape[1]
    gain = n1_ref[...]

    lbl = lbl_ref[...]
    lmax = jnp.max(lbl, axis=0, keepdims=True)
    le = jnp.exp(lbl - lmax)
    lb = le[0:1, :] / jnp.sum(le, axis=0, keepdims=True)

    wqk = wa_ref[:, 0:2 * WIDTH]
    bqk = ba_ref[:, 0:2 * WIDTH]
    hp = _rms(xp_ref[0], gain).astype(BF16)
    hn = _rms(xn_ref[0], gain).astype(BF16)
    ext_ref[0:HALO, :] = (_dot(hp, wqk) + bqk) * (t > 0).astype(F32)
    ext_ref[HALO + rows:2 * HALO + rows, :] = (_dot(hn, wqk) + bqk) * (t < nt - 1).astype(F32)

    def stream(r0, n):
        rs = slice(r0, r0 + n)
        h = _rms(x_ref[0, rs, :], gain).astype(BF16)

        def proj(w_ref, b_ref, lo, hi):
            return _dot(h, w_ref[:, lo:hi]) + b_ref[:, lo:hi]

        ext_ref[HALO + r0:HALO + r0 + n, :] = proj(wa_ref, ba_ref, 0, 2 * WIDTH)
        hq_pre = proj(wh_ref, bh_ref, 0, WIDTH)
        yield
        acc = cb_ref[...] + ext_ref[pl.ds(HALO - CONV_PAD + r0, n), :] * cw_ref[0:1, :]
        for j in range(1, CONV_K):
            acc = acc + ext_ref[pl.ds(HALO - CONV_PAD + j + r0, n), :] * cw_ref[j:j + 1, :]
        qk = _silu(acc)
        q_ref[0, rs, :] = qk[:, 0:WIDTH] * (D_HEAD ** -0.5)
        k_ref[0, rs, :] = qk[:, WIDTH:2 * WIDTH]
        v_ref[0, rs, :] = proj(wa_ref, ba_ref, 2 * WIDTH, 3 * WIDTH)
        yield
        hq_ref[0, rs, :] = _silu(hq_pre)
        mo_ref[0, rs, :] = jax.nn.sigmoid(proj(wa_ref, ba_ref, 3 * WIDTH, 4 * WIDTH))
        yield
        gf_ref[0, rs, :] = lb + (1.0 - lb) * jax.nn.sigmoid(proj(wh_ref, bh_ref, WIDTH, 2 * WIDTH))
        yield
        gb_ref[0, rs, :] = lb + (1.0 - lb) * jax.nn.sigmoid(proj(wh_ref, bh_ref, 2 * WIDTH, 3 * WIDTH))
        hv_ref[0, rs, :] = proj(wh_ref, bh_ref, 3 * WIDTH, 4 * WIDTH)
        yield
        hgg_ref[0, rs, :] = _silu(proj(wh_ref, bh_ref, 4 * WIDTH, 5 * WIDTH))
        gc = _dot(h, wg_ref[...]) + bg_ref[...]
        lane = lax.broadcasted_iota(jnp.int32, gc.shape, 1)
        is_f = (lane >= 2 * N_HEADS) & (lane < N_GATES)
        gcol_ref[0, rs, :] = jnp.where(is_f, _log_sigmoid(gc + fbrow_ref[...]), gc)
        gr = _dot_nt(wgt_ref[...], h) + bgt_ref[...]
        sub = lax.broadcasted_iota(jnp.int32, gr.shape, 0)
        grow_ref[0, :, rs] = jnp.where(sub >= 2 * N_HEADS, _log_sigmoid(gr + fbcol_ref[...]), gr)

    _run_round_robin([stream(r0, IN_PROJ_STREAM) for r0 in range(0, rows, IN_PROJ_STREAM)])


def _in_proj(x, norm1, w_in, b_in, conv_w, conv_b, fgate_bias, lb_logits):
    B, T, D = x.shape
    rows = IN_PROJ_ROWS
    assert T % rows == 0, (T, rows)
    nt = T // rows
    a_w = 4 * WIDTH
    wa = w_in[:, 0:a_w].astype(BF16)
    ba = b_in[None, 0:a_w]
    wg32 = jnp.pad(w_in[:, a_w:a_w + N_GATES], ((0, 0), (0, LANES - N_GATES)))
    bg = jnp.pad(b_in[a_w:a_w + N_GATES], (0, LANES - N_GATES))[None, :]
    wg = wg32.astype(BF16)
    wgt = w_in[:, a_w:a_w + N_GATES].T.astype(BF16)
    bgt = b_in[a_w:a_w + N_GATES][:, None]
    fb = fgate_bias.reshape(2 * N_HEADS)
    fbrow = jnp.zeros((1, LANES), F32).at[0, 2 * N_HEADS:N_GATES].set(fb)
    fbcol = jnp.zeros((N_GATES, 1), F32).at[2 * N_HEADS:N_GATES, 0].set(fb)
    wh = w_in[:, a_w + N_GATES:].astype(BF16)
    bh = b_in[None, a_w + N_GATES:]

    tiles_per_halo = rows // HALO
    n_halo = T // HALO

    def full(arr):
        nd = arr.ndim
        return pl.BlockSpec(arr.shape, lambda b, t: (0,) * nd)

    def tok(width):
        return pl.BlockSpec((1, rows, width), lambda b, t: (b, t, 0))

    in_specs = [
        tok(D),
        pl.BlockSpec((1, HALO, D), lambda b, t: (b, jnp.maximum(t * tiles_per_halo - 1, 0), 0)),
        pl.BlockSpec((1, HALO, D), lambda b, t: (b, jnp.minimum((t + 1) * tiles_per_halo, n_halo - 1), 0)),
    ]
    consts = [norm1[None, :], wa, ba, wg, bg, wgt, bgt, fbrow, fbcol, wh, bh, conv_w, conv_b[None, :], lb_logits]
    in_specs += [full(c) for c in consts]
    tok_out = jax.ShapeDtypeStruct((B, T, WIDTH), F32)
    out_shape = [tok_out, tok_out, tok_out, tok_out,
                 jax.ShapeDtypeStruct((B, T, LANES), F32),
                 jax.ShapeDtypeStruct((B, N_GATES, T), F32),
                 tok_out, tok_out, tok_out, tok_out, tok_out]
    out_specs = [tok(WIDTH)] * 4 + [tok(LANES), pl.BlockSpec((1, N_GATES, rows), lambda b, t: (b, 0, t))] + [tok(WIDTH)] * 5
    return pl.pallas_call(
        _in_proj_kernel,
        grid=(B, nt),
        in_specs=in_specs,
        out_specs=out_specs,
        out_shape=out_shape,
        scratch_shapes=[pltpu.VMEM((rows + 2 * HALO, 2 * WIDTH), F32)],
        compiler_params=pltpu.CompilerParams(
            dimension_semantics=("parallel", "parallel"), vmem_limit_bytes=VMEM_LIMIT),
        name="in_proj",
    )(x, x, x, *consts)


def _cumsum_rows(tri_bf, x):
    hi, mid, lo = _split3(x)
    return _dot(tri_bf, hi) + _dot(tri_bf, mid) + _dot(tri_bf, lo)


def _cumsum_lanes(x, tri_bf):
    hi, mid, lo = _split3(x)
    return _dot(hi, tri_bf) + _dot(mid, tri_bf) + _dot(lo, tri_bf)


def _mlstm_chunk(q, k, vext, i_col, b_col, i_row, b_row, seen, last, c_ref, m_ref, out_ref, sl):
    m_prev = m_ref[:, 0:1]
    c_prev = c_ref[...]
    q_bf = q.astype(BF16)
    log_d = jnp.where(seen, b_col - b_row + i_row, -jnp.inf)
    m_inter = b_col + m_prev
    m_t = jnp.maximum(m_inter, jnp.max(log_d, axis=-1, keepdims=True))
    qk = _dot_nt(q_bf, k.astype(BF16))
    yield
    scores = (qk * jnp.exp(log_d - m_t)).astype(BF16)
    inter_scale = jnp.exp(m_inter - m_t)
    b_last = b_col[last:last + 1, :]
    log_w = b_last - b_col + i_col
    m_new = jnp.maximum(b_last + m_prev, jnp.max(log_w, axis=0, keepdims=True))
    w = jnp.exp(log_w - m_new)
    decay = jnp.exp(b_last + m_prev - m_new)
    kw = (k * w).astype(BF16)
    yield
    numden = _dot(scores, vext) + inter_scale * _dot(q_bf, c_prev.astype(BF16))
    update = _dot_tn(kw, vext)
    yield
    num = numden[:, 0:D_HEAD]
    den = numden[:, D_HEAD:2 * D_HEAD]
    out_ref[0, :, sl] = num / jnp.maximum(jnp.abs(den), jnp.exp(-m_t))
    c_ref[...] = decay * c_prev + update
    m_ref[...] = jnp.broadcast_to(m_new, (1, LANES))


def _hgrn2_level_small(q3, k3, pre3, suf3, half, rev, sub_iota):
    upper = (sub_iota & half) != 0
    second = jnp.logical_not(upper) if rev else upper
    end = 0 if rev else half - 1
    y = jnp.where((sub_iota & (half - 1)) == end, pre3, 0.0)
    step = 1 if rev else -1
    span = 1
    while span < half:
        y = y + pltpu.roll(y, (step * span) % SUBLANES, 1)
        span *= 2
    if 2 * half == SUBLANES:
        other = pltpu.roll(y, half, 1)
    else:
        other = jnp.where(upper, pltpu.roll(y, half, 1), pltpu.roll(y, SUBLANES - half, 1))
    z = jnp.where(second, q3 * pre3, k3 * suf3)
    return z, pre3 * jnp.where(second, other, 1.0), suf3 * jnp.where(second, 1.0, other)


def _hgrn2_level_big(q, k, pre, suf, half, rev):
    L, width = q.shape
    shape = (L // (2 * half), 2, half, width)
    q4, k4, pre4, suf4 = (a.reshape(shape) for a in (q, k, pre, suf))
    first = 1 if rev else 0
    second = 1 - first
    end = 0 if rev else half - 1
    total_first = pre4[:, first, end:end + 1, :]
    total_second = pre4[:, second, end:end + 1, :]

    def join(at_first, at_second):
        parts = (at_second, at_first) if rev else (at_first, at_second)
        return jnp.stack(parts, axis=1).reshape(L, width)

    z = join(k4[:, first] * suf4[:, first], q4[:, second] * pre4[:, second])
    pre_new = join(pre4[:, first], pre4[:, second] * total_first)
    suf_new = join(suf4[:, first] * total_second, suf4[:, second])
    return z, pre_new, suf_new


def _hgrn2_chunk(q, g, v_bf, rev, level, diag, sub_iota, st_refs, out_ref, sl):
    L, width = q.shape
    heads = [slice(h * D_HEAD, (h + 1) * D_HEAD) for h in range(width // D_HEAD)]
    k = 1.0 - g
    q_bf = q.astype(BF16)
    k_bf = k.astype(BF16)
    att = [jnp.where(diag, _dot_nt(q_bf[:, s], k_bf[:, s]), 0.0) for s in heads]
    small = (L // SUBLANES, SUBLANES, width)
    q3, k3, pre, suf = q.reshape(small), k.reshape(small), g.reshape(small), jnp.ones(small, F32)
    half = 1
    bit = 0
    while half < L:
        if half == SUBLANES:
            pre, suf = pre.reshape(L, width), suf.reshape(L, width)
        if half < SUBLANES:
            z, pre, suf = _hgrn2_level_small(q3, k3, pre, suf, half, rev, sub_iota)
            z = z.reshape(L, width)
        else:
            z, pre, suf = _hgrn2_level_big(q, k, pre, suf, half, rev)
        z = z.astype(BF16)
        att = [jnp.where(level == bit, _dot_nt(z[:, s], z[:, s]), a) for a, s in zip(att, heads)]
        half *= 2
        bit += 1
        yield
    last = 0 if rev else L - 1
    q_dec = (q * pre).astype(BF16)
    k_dec = (k * suf).astype(BF16)
    outs = []
    for h, s in enumerate(heads):
        st_prev = st_refs[h][...]
        outs.append(_dot_nt(q_dec[:, s], st_prev.astype(BF16)) + _dot(att[h].astype(BF16), v_bf[:, s]))
        st_refs[h][...] = st_prev * pre[last:last + 1, s] + _dot_tn(v_bf[:, s], k_dec[:, s])
    out_ref[0, :, sl] = jnp.concatenate(outs, axis=1)


def _mixer_kernel(qf_ref, kf_ref, vf_ref, gcf_ref, grf_ref, hqf_ref, hgf_ref, hvf_ref,
                  qb_ref, kb_ref, vb_ref, gcb_ref, grb_ref, hqb_ref, hgb_ref, hvb_ref,
                  hf_ref, of_ref, hb_ref, ob_ref, *state_refs):
    L = CHUNK
    n_state = 2 * N_HEADS
    c_refs, m_refs, st_refs = (state_refs[i * n_state:(i + 1) * n_state] for i in range(3))

    @pl.when(pl.program_id(1) == 0)
    def _():
        for ref in state_refs:
            ref[...] = jnp.zeros_like(ref)

    row = lax.broadcasted_iota(jnp.int32, (L, L), 0)
    col = lax.broadcasted_iota(jnp.int32, (L, L), 1)
    sub_iota = lax.broadcasted_iota(jnp.int32, (L // SUBLANES, SUBLANES, LANES), 1)
    diag = row == col
    diff = row ^ col
    high_bit = jnp.zeros((L, L), jnp.int32)
    half = 2
    while half < L:
        high_bit = high_bit + (diff >= half).astype(jnp.int32)
        half *= 2
    ones = jnp.ones((L, D_HEAD), BF16)

    dirs = (
        (0, qf_ref, kf_ref, vf_ref, gcf_ref, grf_ref, hqf_ref, hgf_ref, hvf_ref, hf_ref, of_ref),
        (1, qb_ref, kb_ref, vb_ref, gcb_ref, grb_ref, hqb_ref, hgb_ref, hvb_ref, hb_ref, ob_ref),
    )
    stages = []
    for d, q_ref, k_ref, v_ref, gc_ref, gr_ref, hq_ref, hg_ref, hv_ref, h_out, o_out in dirs:
        rev = d == 1
        seen = (col >= row) if rev else (col <= row)
        before = (col > row) if rev else (col < row)
        level = jnp.where(before, high_bit, -1)
        tri = seen.astype(BF16)
        tri_t = (row >= col if rev else row <= col).astype(BF16)
        last = 0 if rev else L - 1
        gc = gc_ref[0]
        gr = gr_ref[0]
        gc_cum = _cumsum_rows(tri, gc)
        gr_cum = _cumsum_lanes(gr, tri_t)
        for hd in range(N_HEADS):
            sl = slice(hd * D_HEAD, (hd + 1) * D_HEAD)
            gi = d * N_HEADS + hd
            gf = 2 * N_HEADS + gi
            idx = d * N_HEADS + hd
            vext = jnp.concatenate([v_ref[0, :, sl].astype(BF16), ones], axis=1)
            stages.append(_mlstm_chunk(
                q_ref[0, :, sl], k_ref[0, :, sl], vext,
                gc[:, gi:gi + 1], gc_cum[:, gf:gf + 1], gr[gi:gi + 1, :], gr_cum[gf:gf + 1, :],
                seen, last, c_refs[idx], m_refs[idx], h_out, sl))
            stages.append(_hgrn2_chunk(
                hq_ref[0, :, sl], hg_ref[0, :, sl], hv_ref[0, :, sl].astype(BF16),
                rev, level, diag, sub_iota, [st_refs[idx]], o_out, sl))
    group = 2 * N_HEADS
    for start in range(0, len(stages), group):
        _run_round_robin(stages[start:start + group])


def _mixer(q, k, v, gcol, grow, hq, g_f, g_b, hv):
    B, T, _ = q.shape
    L = CHUNK
    nc = T // L

    def fwd(width):
        return pl.BlockSpec((1, L, width), lambda b, c: (b, c, 0))

    def bwd(width):
        return pl.BlockSpec((1, L, width), lambda b, c: (b, nc - 1 - c, 0))

    grow_f = pl.BlockSpec((1, N_GATES, L), lambda b, c: (b, 0, c))
    grow_b = pl.BlockSpec((1, N_GATES, L), lambda b, c: (b, 0, nc - 1 - c))
    in_specs = ([fwd(WIDTH)] * 3 + [fwd(LANES), grow_f] + [fwd(WIDTH)] * 3
                + [bwd(WIDTH)] * 3 + [bwd(LANES), grow_b] + [bwd(WIDTH)] * 3)
    out = jax.ShapeDtypeStruct((B, T, WIDTH), F32)
    n_state = 2 * N_HEADS
    return pl.pallas_call(
        _mixer_kernel,
        grid=(B, nc),
        in_specs=in_specs,
        out_specs=[fwd(WIDTH), fwd(WIDTH), bwd(WIDTH), bwd(WIDTH)],
        out_shape=[out, out, out, out],
        scratch_shapes=([pltpu.VMEM((D_HEAD, 2 * D_HEAD), F32)] * n_state
                        + [pltpu.VMEM((1, LANES), F32)] * n_state
                        + [pltpu.VMEM((D_HEAD, D_HEAD), F32)] * n_state),
        compiler_params=pltpu.CompilerParams(
            dimension_semantics=("parallel", "arbitrary"), vmem_limit_bytes=VMEM_LIMIT),
        name="mixer",
    )(q, k, v, gcol, grow, hq, g_f, hv, q, k, v, gcol, grow, hq, g_b, hv)


def _head_norm(hsum, gain):
    parts = []
    for hd in range(N_HEADS):
        hh = hsum[:, hd * D_HEAD:(hd + 1) * D_HEAD]
        parts.append(hh * lax.rsqrt(jnp.mean(hh * hh, axis=-1, keepdims=True) + NORM_EPS))
    return jnp.concatenate(parts, axis=1) * gain


def _merge_tile(hf_ref, hb_ref, of_ref, ob_ref, mo_ref, hgg_ref, x_ref, mn_ref, hn_ref, wo_ref,
                n2_ref, wrh_ref, wrl_ref, br_ref, x1_ref, h2_ref, route_ref, hist_ref):
    m_out = _head_norm(hf_ref[...] + hb_ref[...], mn_ref[...]) * mo_ref[...]
    hg_out = _head_norm(of_ref[...] + ob_ref[...], hn_ref[...]) * hgg_ref[...]
    mixed = jnp.concatenate([m_out, hg_out], axis=1).astype(BF16)
    x1 = x_ref[...] + _dot(mixed, wo_ref[...])
    x1_ref[...] = x1
    h2 = _rms(x1, n2_ref[...])
    h2_ref[...] = _pack_bf16_pairs(h2)
    h_hi = h2.astype(BF16)
    h_hi32 = h_hi.astype(F32)

    h_lo = (h2 - h_hi32).astype(BF16)
    logits = _dot(h_hi, wrh_ref[...]) + _dot(h_lo, wrh_ref[...]) + _dot(h_hi, wrl_ref[...]) + br_ref[...]
    lane = lax.broadcasted_iota(jnp.int32, logits.shape, 1)
    big = jnp.int32(LANES)
    neg = -jnp.inf
    g_log = jnp.where(lane < N_GROUPS, logits, neg)
    g_max = jnp.max(g_log, axis=-1, keepdims=True)
    g_idx = jnp.min(jnp.where(g_log == g_max, lane, big), axis=-1, keepdims=True)
    g_val = 1.0 / jnp.sum(jnp.exp(g_log - g_max), axis=-1, keepdims=True)
    e_lo = N_GROUPS + g_idx * EXPERTS_PER_GROUP
    e_log = jnp.where((lane >= e_lo) & (lane < e_lo + EXPERTS_PER_GROUP), logits, neg)
    m1 = jnp.max(e_log, axis=-1, keepdims=True)
    i1 = jnp.min(jnp.where(e_log == m1, lane, big), axis=-1, keepdims=True)
    e_log2 = jnp.where(lane == i1, neg, e_log)
    m2 = jnp.max(e_log2, axis=-1, keepdims=True)
    i2 = jnp.min(jnp.where(e_log2 == m2, lane, big), axis=-1, keepdims=True)
    r2 = jnp.exp(m2 - m1)
    w1 = g_val / (1.0 + r2)
    w2 = g_val * r2 / (1.0 + r2)
    rows = logits.shape[0]
    pick0 = lane == i1 - N_GROUPS
    pick1 = lane == i2 - N_GROUPS
    earlier = (lax.broadcasted_iota(jnp.int32, (rows, rows), 1)
               < lax.broadcasted_iota(jnp.int32, (rows, rows), 0)).astype(BF16)
    cnt0 = jnp.sum(pick0.astype(F32), axis=0, keepdims=True)
    cnt1 = jnp.sum(pick1.astype(F32), axis=0, keepdims=True)
    rank0 = jnp.sum(jnp.where(pick0, _dot(earlier, pick0.astype(BF16)), 0.0), axis=-1, keepdims=True)
    rank1 = jnp.sum(jnp.where(pick1, _dot(earlier, pick1.astype(BF16)) + cnt0, 0.0), axis=-1, keepdims=True)
    columns = ((i1 - N_GROUPS).astype(F32), (i2 - N_GROUPS).astype(F32), w1, w2, rank0, rank1)
    route = jnp.zeros_like(logits)
    for c, value in enumerate(columns):
        route = jnp.where(lane == c, value, route)
    route_ref[...] = route
    sub = lax.broadcasted_iota(jnp.int32, hist_ref.shape, 0)
    hist_ref[...] = jnp.where(sub == 0, cnt0 + cnt1, 0.0)


N_MERGE_STREAMS = 7


def _merge_kernel(*refs, n_a):
    side_a = refs[0:N_MERGE_STREAMS]
    side_b = refs[N_MERGE_STREAMS:2 * N_MERGE_STREAMS]
    rest = refs[2 * N_MERGE_STREAMS:]

    @pl.when(pl.program_id(0) < n_a)
    def _():
        _merge_tile(*side_a, *rest)

    @pl.when(pl.program_id(0) >= n_a)
    def _():
        _merge_tile(*side_b, *rest)


def _merge(streams_a, streams_b, m_norm, hg_norm, w_out, norm2, w_rg, b_rg, w_re, b_re):
    D = D_MODEL
    rows = PROJ_ROWS
    n_a = streams_a[0].shape[0] // rows
    n_b = streams_b[0].shape[0] // rows
    n_all = (n_a + n_b) * rows
    n_log = N_GROUPS + N_EXPERTS
    wr = jnp.pad(jnp.concatenate([w_rg, w_re], axis=1), ((0, 0), (0, LANES - n_log)))
    br = jnp.pad(jnp.concatenate([b_rg, b_re]), (0, LANES - n_log))[None, :]
    wr_hi = wr.astype(BF16)
    wr_lo = (wr - wr_hi.astype(F32)).astype(BF16)
    consts = [m_norm[None, :], hg_norm[None, :], w_out.astype(BF16), norm2[None, :], wr_hi, wr_lo, br]

    def full(arr):
        nd = arr.ndim
        return pl.BlockSpec(arr.shape, lambda i: (0,) * nd)

    def side_a(arr):
        return pl.BlockSpec((rows, arr.shape[1]), lambda i: (jnp.minimum(i, n_a - 1), 0))

    def side_b(arr):
        return pl.BlockSpec((rows, arr.shape[1]), lambda i: (jnp.maximum(i - n_a, 0), 0))

    def out(width):
        return pl.BlockSpec((rows, width), lambda i: (i, 0))

    return pl.pallas_call(
        functools.partial(_merge_kernel, n_a=n_a),
        grid=(n_a + n_b,),
        in_specs=[side_a(s) for s in streams_a] + [side_b(s) for s in streams_b] + [full(c) for c in consts],
        out_specs=[out(D), out(D // 2), out(LANES), pl.BlockSpec((SUBLANES, LANES), lambda i: (i, 0))],
        out_shape=[jax.ShapeDtypeStruct((n_all, D), F32), jax.ShapeDtypeStruct((n_all, D // 2), jnp.uint32),
                   jax.ShapeDtypeStruct((n_all, LANES), F32),
                   jax.ShapeDtypeStruct(((n_a + n_b) * SUBLANES, LANES), F32)],
        compiler_params=pltpu.CompilerParams(
            dimension_semantics=("arbitrary",), vmem_limit_bytes=VMEM_LIMIT),
        name="merge",
    )(*streams_a, *streams_b, *consts)


def _sc_row_mover(src, idx, n_out, scatter, name):
    n_moved = idx.shape[0]
    D = src.shape[1]
    n_sub = SC_CORES * SC_SUBCORES
    per = n_moved // n_sub
    window = SC_WINDOW_BYTES // (D * src.dtype.itemsize)
    assert per * n_sub == n_moved and per % window == 0, (n_moved, per, window)
    assert not scatter or src.shape[0] % per == 0, (src.shape, per)
    mesh = plsc.VectorSubcoreMesh(core_axis_name="c", subcore_axis_name="s",
                                  num_cores=SC_CORES, num_subcores=SC_SUBCORES)

    def body(src_hbm, idx_hbm, out_hbm, idx_v, buf):
        base = (lax.axis_index("c") * SC_SUBCORES + lax.axis_index("s")) * per
        pltpu.sync_copy(idx_hbm.at[pl.ds(base, per)], idx_v)

        @pl.loop(0, per // window)
        def _(j):
            linear = pl.ds(base + j * window, window)
            indexed = idx_v.at[pl.ds(j * window, window)]
            if scatter:
                pltpu.sync_copy(src_hbm.at[pl.ds(lax.rem(base, src.shape[0]) + j * window, window)], buf)
                pltpu.sync_copy(buf, out_hbm.at[indexed])
            else:
                pltpu.sync_copy(src_hbm.at[indexed], buf)
                pltpu.sync_copy(buf, out_hbm.at[linear])

    return pl.kernel(
        body,
        out_type=jax.ShapeDtypeStruct((n_out, D), src.dtype),
        mesh=mesh,
        scratch_types=[pltpu.VMEM((per,), jnp.int32), pltpu.VMEM((window, D), src.dtype)],
        name=name,
    )(src, idx)


def _sc_gather_rows(src, idx):
    return _sc_row_mover(src, idx, idx.shape[0], False, "sc_gather_rows")


def _sc_scatter_rows(src, idx, n_out):
    return _sc_row_mover(src, idx, n_out, True, "sc_scatter_rows")


def _expert_kernel(be_ref, nu_ref, x_ref, w1_ref, w3_ref, w2_ref, o_ref, w1_bf, w3_bf, w2_bf):
    i = pl.program_id(0)
    active = i < nu_ref[0]
    new_expert = (i == 0) | (be_ref[i] != be_ref[jnp.maximum(i - 1, 0)])

    @pl.when(active & new_expert)
    def _():
        w1_bf[...] = w1_ref[0].astype(BF16)
        w3_bf[...] = w3_ref[0].astype(BF16)
        w2_bf[...] = w2_ref[0].astype(BF16)

    @pl.when(active)
    def _():
        half = D_MODEL // 2
        x_lo, x_hi = (part.astype(BF16) for part in _unpack_bf16_pairs(x_ref[...]))
        a = _dot(x_lo, w1_bf[0:half, :]) + _dot(x_hi, w1_bf[half:, :])
        b = _dot(x_lo, w3_bf[0:half, :]) + _dot(x_hi, w3_bf[half:, :])
        o_ref[...] = _pack_bf16_pairs(_dot((_silu(a) * b).astype(BF16), w2_bf[...]))


def _experts(xs, block_e, n_used, w1, w3, w2):
    rows = EXPERT_ROWS
    n_blocks = xs.shape[0] // rows
    D = D_MODEL

    def blk(i, be, nu):
        return jnp.minimum(i, nu[0] - 1)

    grid_spec = pltpu.PrefetchScalarGridSpec(
        num_scalar_prefetch=2,
        grid=(n_blocks,),
        in_specs=[
            pl.BlockSpec((rows, D // 2), lambda i, be, nu: (blk(i, be, nu), 0)),
            pl.BlockSpec((1, D, EXPERT_FF), lambda i, be, nu: (be[blk(i, be, nu)], 0, 0)),
            pl.BlockSpec((1, D, EXPERT_FF), lambda i, be, nu: (be[blk(i, be, nu)], 0, 0)),
            pl.BlockSpec((1, EXPERT_FF, D), lambda i, be, nu: (be[blk(i, be, nu)], 0, 0)),
        ],
        out_specs=pl.BlockSpec((rows, D // 2), lambda i, be, nu: (blk(i, be, nu), 0)),
        scratch_shapes=[pltpu.VMEM((D, EXPERT_FF), BF16), pltpu.VMEM((D, EXPERT_FF), BF16),
                        pltpu.VMEM((EXPERT_FF, D), BF16)],
    )
    return pl.pallas_call(
        _expert_kernel,
        grid_spec=grid_spec,
        out_shape=jax.ShapeDtypeStruct((xs.shape[0], D // 2), jnp.uint32),
        compiler_params=pltpu.CompilerParams(
            dimension_semantics=("arbitrary",), vmem_limit_bytes=VMEM_LIMIT),
        name="experts",
    )(block_e, n_used, xs, w1, w3, w2)


def _combine_kernel(y0_ref, y1_ref, x1_ref, route_ref, nf_ref, ya_ref, yb_ref, *, n_a):
    route = route_ref[...]
    r0 = jnp.concatenate(_unpack_bf16_pairs(y0_ref[...]), axis=1)
    r1 = jnp.concatenate(_unpack_bf16_pairs(y1_ref[...]), axis=1)
    y = _rms(x1_ref[...] + route[:, 2:3] * r0 + route[:, 3:4] * r1, nf_ref[...])

    @pl.when(pl.program_id(0) < n_a)
    def _():
        ya_ref[...] = y

    @pl.when(pl.program_id(0) >= n_a)
    def _():
        yb_ref[...] = y


def _combine(x1, route, y_rows, norm_f, n_tok_a):
    N, D = x1.shape
    rows = PROJ_ROWS
    nt = N // rows
    n_a = n_tok_a // rows

    def tok(width, offset=0):
        return pl.BlockSpec((rows, width), lambda i: (i + offset, 0))

    return pl.pallas_call(
        functools.partial(_combine_kernel, n_a=n_a),
        grid=(nt,),
        in_specs=[tok(D // 2), tok(D // 2, nt), tok(D), tok(LANES), pl.BlockSpec((1, D), lambda i: (0, 0))],
        out_specs=[pl.BlockSpec((rows, D), lambda i: (jnp.minimum(i, n_a - 1), 0)),
                   pl.BlockSpec((rows, D), lambda i: (jnp.maximum(i - n_a, 0), 0))],
        out_shape=[jax.ShapeDtypeStruct((n_tok_a, D), F32), jax.ShapeDtypeStruct((N - n_tok_a, D), F32)],
        compiler_params=pltpu.CompilerParams(
            dimension_semantics=("arbitrary",), vmem_limit_bytes=VMEM_LIMIT),
        name="combine",
    )(y_rows, y_rows, x1, route, norm_f[None, :])


def _plan_kernel(route_ref, table_ref, dest_ref):
    route = route_ref[...]
    lane = lax.broadcasted_iota(jnp.int32, route.shape, 1)
    lane_f = lane.astype(F32)
    first = table_ref[0:1, :]
    d0 = jnp.sum(jnp.where(lane_f == route[:, 0:1], first, 0.0), axis=-1, keepdims=True) + route[:, 4:5]
    d1 = jnp.sum(jnp.where(lane_f == route[:, 1:2], first, 0.0), axis=-1, keepdims=True) + route[:, 5:6]
    dest_ref[...] = jnp.where(lane == 0, d0, jnp.where(lane == 1, d1, 0.0)).astype(jnp.int32)


def _dispatch_plan(route, hist):
    N = route.shape[0]
    rows = PROJ_ROWS
    n_tiles = N // rows
    blk = EXPERT_ROWS
    tile_counts = hist.reshape(n_tiles, SUBLANES, LANES)[:, 0, 0:N_EXPERTS].astype(jnp.int32)
    tile_first = jnp.cumsum(tile_counts, axis=0) - tile_counts
    counts = jnp.sum(tile_counts, axis=0)
    padded = ((counts + blk - 1) // blk) * blk
    pad_end = jnp.cumsum(padded)
    pad_start = pad_end - padded
    table = jnp.zeros((n_tiles, SUBLANES, LANES), F32).at[:, 0, 0:N_EXPERTS].set(
        (pad_start[None, :] + tile_first).astype(F32)).reshape(n_tiles * SUBLANES, LANES)
    dest_cols = pl.pallas_call(
        _plan_kernel,
        grid=(n_tiles,),
        in_specs=[pl.BlockSpec((rows, LANES), lambda i: (i, 0)), pl.BlockSpec((SUBLANES, LANES), lambda i: (i, 0))],
        out_specs=pl.BlockSpec((rows, LANES), lambda i: (i, 0)),
        out_shape=jax.ShapeDtypeStruct((N, LANES), jnp.int32),
        compiler_params=pltpu.CompilerParams(dimension_semantics=("parallel",)),
        name="plan",
    )(route, table)
    dest = dest_cols[:, 0:TOP_K].T.reshape(TOP_K * N)
    n_blocks = (TOP_K * N + N_EXPERTS * (blk - 1) + blk - 1) // blk
    block_start = jnp.arange(n_blocks, dtype=jnp.int32) * blk
    block_e = jnp.sum((pad_end[None, :] <= block_start[:, None]).astype(jnp.int32), axis=1)
    block_e = jnp.minimum(block_e, N_EXPERTS - 1)
    n_used = (pad_end[-1] // blk).astype(jnp.int32).reshape(1)
    return dest, block_e, n_used, n_blocks * blk


def _token_mixer(x, norm1, w_in, b_in, conv_w, conv_b, m_fgate_bias, hg_lb_logits):
    B, T, D = x.shape
    q, k, v, mo, gcol, grow, hq, g_f, g_b, hv, hgg = _in_proj(
        x, norm1, w_in, b_in, conv_w, conv_b, m_fgate_bias, hg_lb_logits)
    h_f, o_f, h_b, o_b = _mixer(q, k, v, gcol, grow, hq, g_f, g_b, hv)
    return [a.reshape(B * T, a.shape[-1]) for a in (h_f, h_b, o_f, o_b, mo, hgg, x)]


def kernel(x_prompt, x_sample, norm1, w_in, b_in, conv_w, conv_b, m_fgate_bias, m_norm, hg_lb_logits, hg_norm,
           w_out, norm2, w_router_group, b_router_group, w_router_expert, b_router_expert, w1, w3, w2, norm_f):
    layer = 0
    mixer_args = (norm1[layer], w_in[layer], b_in[layer], conv_w[layer], conv_b[layer], m_fgate_bias[layer],
                  hg_lb_logits)
    streams_p = _token_mixer(x_prompt, *mixer_args)
    streams_s = _token_mixer(x_sample, *mixer_args)
    x1, h2, route, hist = _merge(streams_p, streams_s, m_norm[layer], hg_norm[layer], w_out[layer], norm2[layer],
                                 w_router_group[layer], b_router_group[layer], w_router_expert[layer],
                                 b_router_expert[layer])
    dest, block_e, n_used, n_rows = _dispatch_plan(route, hist)
    xs = _sc_scatter_rows(h2, dest, n_rows)
    out_rows = _experts(xs, block_e, n_used, w1[layer], w3[layer], w2[layer])
    y_rows = _sc_gather_rows(out_rows, dest)
    y_p, y_s = _combine(x1, route, y_rows, norm_f, streams_p[0].shape[0])
    return (y_p.reshape(x_prompt.shape), y_s.reshape(x_sample.shape))
```

```python
import functools

import jax
import jax.numpy as jnp
from jax import lax
from jax.experimental import pallas as pl
from jax.experimental.pallas import tpu as pltpu
from jax.experimental.pallas import tpu_sc as plsc

F32 = jnp.float32
BF16 = jnp.bfloat16

D_MODEL = 1024
N_HEADS = 4
D_HEAD = 128
WIDTH = N_HEADS * D_HEAD
CONV_K = 5
CONV_PAD = CONV_K // 2
N_GROUPS = 4
EXPERTS_PER_GROUP = 8
N_EXPERTS = N_GROUPS * EXPERTS_PER_GROUP
TOP_K = 2
EXPERT_FF = D_MODEL // 2
NORM_EPS = 1e-6

LANES = 128
SUBLANES = 8
CHUNK = 128
PROJ_ROWS = 256
MERGE_ROWS = 512
PLAN_TILES = 8
IN_PROJ_ROWS = 512
IN_PROJ_STREAM = 256
HALO = SUBLANES
EXPERT_ROWS = 512
N_GATES = 4 * N_HEADS
SC_CORES = 2
SC_SUBCORES = 16
SC_WINDOW_BYTES = 128 * 1024
VMEM_LIMIT = 56 * 1024 * 1024


def _dot(a, b):
    return jnp.dot(a, b, preferred_element_type=F32)


def _dot_nt(a, b):
    return lax.dot_general(a, b, (((1,), (1,)), ((), ())), preferred_element_type=F32)


def _dot_tn(a, b):
    return lax.dot_general(a, b, (((0,), (0,)), ((), ())), preferred_element_type=F32)


def _split3(x):
    hi = x.astype(BF16)
    r1 = x - hi.astype(F32)
    mid = r1.astype(BF16)
    lo = (r1 - mid.astype(F32)).astype(BF16)
    return hi, mid, lo


def _pack_bf16_pairs(x):
    half = x.shape[1] // 2
    bits = lax.bitcast_convert_type(x.astype(BF16).astype(F32), jnp.uint32)
    return (bits[:, half:] & jnp.uint32(0xFFFF0000)) | (bits[:, :half] >> 16)


def _unpack_bf16_pairs(words):
    lo = lax.bitcast_convert_type(words << 16, F32)
    hi = lax.bitcast_convert_type(words & jnp.uint32(0xFFFF0000), F32)
    return lo, hi


def _silu(x):
    return x * jax.nn.sigmoid(x)


def _log_sigmoid(x):
    return -(jnp.maximum(-x, 0.0) + jnp.log1p(jnp.exp(-jnp.abs(x))))


def _rms(x, gain):
    return x * lax.rsqrt(jnp.mean(x * x, axis=-1, keepdims=True) + NORM_EPS) * gain


def _run_round_robin(generators):
    live = list(generators)
    while live:
        for gen in list(live):
            try:
                next(gen)
            except StopIteration:
                live.remove(gen)


def _in_proj_kernel(x_ref, xp_ref, xn_ref, n1_ref, wa_ref, ba_ref, wg_ref, bg_ref, wgt_ref, bgt_ref,
                    fbrow_ref, fbcol_ref, wh_ref, bh_ref, cw_ref, cb_ref, lbl_ref,
                    q_ref, k_ref, v_ref, mo_ref, gcol_ref, grow_ref, hq_ref, gf_ref, gb_ref, hv_ref, hgg_ref,
                    ext_ref):
    t = pl.program_id(1)
    nt = pl.num_programs(1)
    rows = x_ref.shape[1]
    gain = n1_ref[...]

    lbl = lbl_ref[...]
    lmax = jnp.max(lbl, axis=0, keepdims=True)
    le = jnp.exp(lbl - lmax)
    lb = le[0:1, :] / jnp.sum(le, axis=0, keepdims=True)

    wqk = wa_ref[:, 0:2 * WIDTH]
    bqk = ba_ref[:, 0:2 * WIDTH]
    hp = _rms(xp_ref[0], gain).astype(BF16)
    hn = _rms(xn_ref[0], gain).astype(BF16)
    ext_ref[0:HALO, :] = (_dot(hp, wqk) + bqk) * (t > 0).astype(F32)
    ext_ref[HALO + rows:2 * HALO + rows, :] = (_dot(hn, wqk) + bqk) * (t < nt - 1).astype(F32)

    def stream(r0, n):
        rs = slice(r0, r0 + n)
        h = _rms(x_ref[0, rs, :], gain).astype(BF16)

        def proj(w_ref, b_ref, lo, hi):
            return _dot(h, w_ref[:, lo:hi]) + b_ref[:, lo:hi]

        ext_ref[HALO + r0:HALO + r0 + n, :] = proj(wa_ref, ba_ref, 0, 2 * WIDTH)
        hq_pre = proj(wh_ref, bh_ref, 0, WIDTH)
        yield
        acc = cb_ref[...] + ext_ref[pl.ds(HALO - CONV_PAD + r0, n), :] * cw_ref[0:1, :]
        for j in range(1, CONV_K):
            acc = acc + ext_ref[pl.ds(HALO - CONV_PAD + j + r0, n), :] * cw_ref[j:j + 1, :]
        qk = _silu(acc)
        q_ref[0, rs, :] = qk[:, 0:WIDTH] * (D_HEAD ** -0.5)
        k_ref[0, rs, :] = qk[:, WIDTH:2 * WIDTH]
        v_ref[0, rs, :] = proj(wa_ref, ba_ref, 2 * WIDTH, 3 * WIDTH)
        yield
        hq_ref[0, rs, :] = _silu(hq_pre)
        mo_ref[0, rs, :] = jax.nn.sigmoid(proj(wa_ref, ba_ref, 3 * WIDTH, 4 * WIDTH))
        yield
        gf_ref[0, rs, :] = lb + (1.0 - lb) * jax.nn.sigmoid(proj(wh_ref, bh_ref, WIDTH, 2 * WIDTH))
        yield
        gb_ref[0, rs, :] = lb + (1.0 - lb) * jax.nn.sigmoid(proj(wh_ref, bh_ref, 2 * WIDTH, 3 * WIDTH))
        hv_ref[0, rs, :] = proj(wh_ref, bh_ref, 3 * WIDTH, 4 * WIDTH)
        yield
        hgg_ref[0, rs, :] = _silu(proj(wh_ref, bh_ref, 4 * WIDTH, 5 * WIDTH))
        gc = _dot(h, wg_ref[...]) + bg_ref[...]
        lane = lax.broadcasted_iota(jnp.int32, gc.shape, 1)
        is_f = (lane >= 2 * N_HEADS) & (lane < N_GATES)
        gcol_ref[0, rs, :] = jnp.where(is_f, _log_sigmoid(gc + fbrow_ref[...]), gc)
        gr = _dot_nt(wgt_ref[...], h) + bgt_ref[...]
        sub = lax.broadcasted_iota(jnp.int32, gr.shape, 0)
        grow_ref[0, :, rs] = jnp.where(sub >= 2 * N_HEADS, _log_sigmoid(gr + fbcol_ref[...]), gr)

    _run_round_robin([stream(r0, IN_PROJ_STREAM) for r0 in range(0, rows, IN_PROJ_STREAM)])


def _in_proj(x, norm1, w_in, b_in, conv_w, conv_b, fgate_bias, lb_logits):
    B, T, D = x.shape
    rows = IN_PROJ_ROWS
    assert T % rows == 0, (T, rows)
    nt = T // rows
    a_w = 4 * WIDTH
    wa = w_in[:, 0:a_w].astype(BF16)
    ba = b_in[None, 0:a_w]
    wg32 = jnp.pad(w_in[:, a_w:a_w + N_GATES], ((0, 0), (0, LANES - N_GATES)))
    bg = jnp.pad(b_in[a_w:a_w + N_GATES], (0, LANES - N_GATES))[None, :]
    wg = wg32.astype(BF16)
    wgt = w_in[:, a_w:a_w + N_GATES].T.astype(BF16)
    bgt = b_in[a_w:a_w + N_GATES][:, None]
    fb = fgate_bias.reshape(2 * N_HEADS)
    fbrow = jnp.zeros((1, LANES), F32).at[0, 2 * N_HEADS:N_GATES].set(fb)
    fbcol = jnp.zeros((N_GATES, 1), F32).at[2 * N_HEADS:N_GATES, 0].set(fb)
    wh = w_in[:, a_w + N_GATES:].astype(BF16)
    bh = b_in[None, a_w + N_GATES:]

    tiles_per_halo = rows // HALO
    n_halo = T // HALO

    def full(arr):
        nd = arr.ndim
        return pl.BlockSpec(arr.shape, lambda b, t: (0,) * nd)

    def tok(width):
        return pl.BlockSpec((1, rows, width), lambda b, t: (b, t, 0))

    in_specs = [
        tok(D),
        pl.BlockSpec((1, HALO, D), lambda b, t: (b, jnp.maximum(t * tiles_per_halo - 1, 0), 0)),
        pl.BlockSpec((1, HALO, D), lambda b, t: (b, jnp.minimum((t + 1) * tiles_per_halo, n_halo - 1), 0)),
    ]
    consts = [norm1[None, :], wa, ba, wg, bg, wgt, bgt, fbrow, fbcol, wh, bh, conv_w, conv_b[None, :], lb_logits]
    in_specs += [full(c) for c in consts]
    tok_out = jax.ShapeDtypeStruct((B, T, WIDTH), F32)
    out_shape = [tok_out, tok_out, tok_out, tok_out,
                 jax.ShapeDtypeStruct((B, T, LANES), F32),
                 jax.ShapeDtypeStruct((B, N_GATES, T), F32),
                 tok_out, tok_out, tok_out, tok_out, tok_out]
    out_specs = [tok(WIDTH)] * 4 + [tok(LANES), pl.BlockSpec((1, N_GATES, rows), lambda b, t: (b, 0, t))] + [tok(WIDTH)] * 5
    return pl.pallas_call(
        _in_proj_kernel,
        grid=(B, nt),
        in_specs=in_specs,
        out_specs=out_specs,
        out_shape=out_shape,
        scratch_shapes=[pltpu.VMEM((rows + 2 * HALO, 2 * WIDTH), F32)],
        compiler_params=pltpu.CompilerParams(
            dimension_semantics=("parallel", "parallel"), vmem_limit_bytes=VMEM_LIMIT),
        name="in_proj",
    )(x, x, x, *consts)


def _cumsum_rows(tri_bf, x):
    hi, mid, lo = _split3(x)
    return _dot(tri_bf, hi) + _dot(tri_bf, mid) + _dot(tri_bf, lo)


def _cumsum_lanes(x, tri_bf):
    hi, mid, lo = _split3(x)
    return _dot(hi, tri_bf) + _dot(mid, tri_bf) + _dot(lo, tri_bf)


def _mlstm_chunk(q, k, vext, i_col, b_col, i_row, b_row, seen, last, c_ref, m_ref, out_ref, sl):
    m_prev = m_ref[:, 0:1]
    c_prev = c_ref[...]
    q_bf = q.astype(BF16)
    log_d = jnp.where(seen, b_col - b_row + i_row, -jnp.inf)
    m_inter = b_col + m_prev
    m_t = jnp.maximum(m_inter, jnp.max(log_d, axis=-1, keepdims=True))
    qk = _dot_nt(q_bf, k.astype(BF16))
    yield
    scores = (qk * jnp.exp(log_d - m_t)).astype(BF16)
    inter_scale = jnp.exp(m_inter - m_t)
    b_last = b_col[last:last + 1, :]
    log_w = b_last - b_col + i_col
    m_new = jnp.maximum(b_last + m_prev, jnp.max(log_w, axis=0, keepdims=True))
    w = jnp.exp(log_w - m_new)
    decay = jnp.exp(b_last + m_prev - m_new)
    kw = (k * w).astype(BF16)
    yield
    numden = _dot(scores, vext) + inter_scale * _dot(q_bf, c_prev.astype(BF16))
    update = _dot_tn(kw, vext)
    yield
    num = numden[:, 0:D_HEAD]
    den = numden[:, D_HEAD:2 * D_HEAD]
    out_ref[0, :, sl] = num / jnp.maximum(jnp.abs(den), jnp.exp(-m_t))
    c_ref[...] = decay * c_prev + update
    m_ref[...] = jnp.broadcast_to(m_new, (1, LANES))


def _hgrn2_level_small(q3, k3, pre3, suf3, half, rev, sub_iota):
    upper = (sub_iota & half) != 0
    second = jnp.logical_not(upper) if rev else upper
    end = 0 if rev else half - 1
    y = jnp.where((sub_iota & (half - 1)) == end, pre3, 0.0)
    step = 1 if rev else -1
    span = 1
    while span < half:
        y = y + pltpu.roll(y, (step * span) % SUBLANES, 1)
        span *= 2
    if 2 * half == SUBLANES:
        other = pltpu.roll(y, half, 1)
    else:
        other = jnp.where(upper, pltpu.roll(y, half, 1), pltpu.roll(y, SUBLANES - half, 1))
    z = jnp.where(second, q3 * pre3, k3 * suf3)
    return z, pre3 * jnp.where(second, other, 1.0), suf3 * jnp.where(second, 1.0, other)


def _hgrn2_level_big(q, k, pre, suf, half, rev):
    L, width = q.shape
    shape = (L // (2 * half), 2, half, width)
    q4, k4, pre4, suf4 = (a.reshape(shape) for a in (q, k, pre, suf))
    first = 1 if rev else 0
    second = 1 - first
    end = 0 if rev else half - 1
    total_first = pre4[:, first, end:end + 1, :]
    total_second = pre4[:, second, end:end + 1, :]

    def join(at_first, at_second):
        parts = (at_second, at_first) if rev else (at_first, at_second)
        return jnp.stack(parts, axis=1).reshape(L, width)

    z = join(k4[:, first] * suf4[:, first], q4[:, second] * pre4[:, second])
    pre_new = join(pre4[:, first], pre4[:, second] * total_first)
    suf_new = join(suf4[:, first] * total_second, suf4[:, second])
    return z, pre_new, suf_new


def _hgrn2_chunk(q, g, v_bf, rev, level, diag, sub_iota, st_refs, out_ref, sl):
    L, width = q.shape
    heads = [slice(h * D_HEAD, (h + 1) * D_HEAD) for h in range(width // D_HEAD)]
    k = 1.0 - g
    q_bf = q.astype(BF16)
    k_bf = k.astype(BF16)
    att = [jnp.where(diag, _dot_nt(q_bf[:, s], k_bf[:, s]), 0.0) for s in heads]
    small = (L // SUBLANES, SUBLANES, width)
    q3, k3, pre, suf = q.reshape(small), k.reshape(small), g.reshape(small), jnp.ones(small, F32)
    half = 1
    bit = 0
    while half < L:
        if half == SUBLANES:
            pre, suf = pre.reshape(L, width), suf.reshape(L, width)
        if half < SUBLANES:
            z, pre, suf = _hgrn2_level_small(q3, k3, pre, suf, half, rev, sub_iota)
            z = z.reshape(L, width)
        else:
            z, pre, suf = _hgrn2_level_big(q, k, pre, suf, half, rev)
        z = z.astype(BF16)
        att = [jnp.where(level == bit, _dot_nt(z[:, s], z[:, s]), a) for a, s in zip(att, heads)]
        half *= 2
        bit += 1
        yield
    last = 0 if rev else L - 1
    q_dec = (q * pre).astype(BF16)
    k_dec = (k * suf).astype(BF16)
    outs = []
    for h, s in enumerate(heads):
        st_prev = st_refs[h][...]
        outs.append(_dot_nt(q_dec[:, s], st_prev.astype(BF16)) + _dot(att[h].astype(BF16), v_bf[:, s]))
        st_refs[h][...] = st_prev * pre[last:last + 1, s] + _dot_tn(v_bf[:, s], k_dec[:, s])
    out_ref[0, :, sl] = jnp.concatenate(outs, axis=1)


def _mixer_kernel(qf_ref, kf_ref, vf_ref, gcf_ref, grf_ref, hqf_ref, hgf_ref, hvf_ref,
                  qb_ref, kb_ref, vb_ref, gcb_ref, grb_ref, hqb_ref, hgb_ref, hvb_ref,
                  hf_ref, of_ref, hb_ref, ob_ref, *state_refs):
    L = CHUNK
    n_state = 2 * N_HEADS
    c_refs, m_refs, st_refs = (state_refs[i * n_state:(i + 1) * n_state] for i in range(3))

    @pl.when(pl.program_id(1) == 0)
    def _():
        for ref in state_refs:
            ref[...] = jnp.zeros_like(ref)

    row = lax.broadcasted_iota(jnp.int32, (L, L), 0)
    col = lax.broadcasted_iota(jnp.int32, (L, L), 1)
    sub_iota = lax.broadcasted_iota(jnp.int32, (L // SUBLANES, SUBLANES, LANES), 1)
    diag = row == col
    diff = row ^ col
    high_bit = jnp.zeros((L, L), jnp.int32)
    half = 2
    while half < L:
        high_bit = high_bit + (diff >= half).astype(jnp.int32)
        half *= 2
    ones = jnp.ones((L, D_HEAD), BF16)

    dirs = (
        (0, qf_ref, kf_ref, vf_ref, gcf_ref, grf_ref, hqf_ref, hgf_ref, hvf_ref, hf_ref, of_ref),
        (1, qb_ref, kb_ref, vb_ref, gcb_ref, grb_ref, hqb_ref, hgb_ref, hvb_ref, hb_ref, ob_ref),
    )
    stages = []
    for d, q_ref, k_ref, v_ref, gc_ref, gr_ref, hq_ref, hg_ref, hv_ref, h_out, o_out in dirs:
        rev = d == 1
        seen = (col >= row) if rev else (col <= row)
        before = (col > row) if rev else (col < row)
        level = jnp.where(before, high_bit, -1)
        tri = seen.astype(BF16)
        tri_t = (row >= col if rev else row <= col).astype(BF16)
        last = 0 if rev else L - 1
        gc = gc_ref[0]
        gr = gr_ref[0]
        gc_cum = _cumsum_rows(tri, gc)
        gr_cum = _cumsum_lanes(gr, tri_t)
        for hd in range(N_HEADS):
            sl = slice(hd * D_HEAD, (hd + 1) * D_HEAD)
            gi = d * N_HEADS + hd
            gf = 2 * N_HEADS + gi
            idx = d * N_HEADS + hd
            vext = jnp.concatenate([v_ref[0, :, sl].astype(BF16), ones], axis=1)
            stages.append(_mlstm_chunk(
                q_ref[0, :, sl], k_ref[0, :, sl], vext,
                gc[:, gi:gi + 1], gc_cum[:, gf:gf + 1], gr[gi:gi + 1, :], gr_cum[gf:gf + 1, :],
                seen, last, c_refs[idx], m_refs[idx], h_out, sl))
            stages.append(_hgrn2_chunk(
                hq_ref[0, :, sl], hg_ref[0, :, sl], hv_ref[0, :, sl].astype(BF16),
                rev, level, diag, sub_iota, [st_refs[idx]], o_out, sl))
    group = 2 * N_HEADS
    for start in range(0, len(stages), group):
        _run_round_robin(stages[start:start + group])


def _mixer(q, k, v, gcol, grow, hq, g_f, g_b, hv):
    B, T, _ = q.shape
    L = CHUNK
    nc = T // L

    def fwd(width):
        return pl.BlockSpec((1, L, width), lambda b, c: (b, c, 0))

    def bwd(width):
        return pl.BlockSpec((1, L, width), lambda b, c: (b, nc - 1 - c, 0))

    grow_f = pl.BlockSpec((1, N_GATES, L), lambda b, c: (b, 0, c))
    grow_b = pl.BlockSpec((1, N_GATES, L), lambda b, c: (b, 0, nc - 1 - c))
    in_specs = ([fwd(WIDTH)] * 3 + [fwd(LANES), grow_f] + [fwd(WIDTH)] * 3
                + [bwd(WIDTH)] * 3 + [bwd(LANES), grow_b] + [bwd(WIDTH)] * 3)
    out = jax.ShapeDtypeStruct((B, T, WIDTH), F32)
    n_state = 2 * N_HEADS
    return pl.pallas_call(
        _mixer_kernel,
        grid=(B, nc),
        in_specs=in_specs,
        out_specs=[fwd(WIDTH), fwd(WIDTH), bwd(WIDTH), bwd(WIDTH)],
        out_shape=[out, out, out, out],
        scratch_shapes=([pltpu.VMEM((D_HEAD, 2 * D_HEAD), F32)] * n_state
                        + [pltpu.VMEM((1, LANES), F32)] * n_state
                        + [pltpu.VMEM((D_HEAD, D_HEAD), F32)] * n_state),
        compiler_params=pltpu.CompilerParams(
            dimension_semantics=("parallel", "arbitrary"), vmem_limit_bytes=VMEM_LIMIT),
        name="mixer",
    )(q, k, v, gcol, grow, hq, g_f, hv, q, k, v, gcol, grow, hq, g_b, hv)


def _head_norm(hsum, gain):
    parts = []
    for hd in range(N_HEADS):
        hh = hsum[:, hd * D_HEAD:(hd + 1) * D_HEAD]
        parts.append(hh * lax.rsqrt(jnp.mean(hh * hh, axis=-1, keepdims=True) + NORM_EPS))
    return jnp.concatenate(parts, axis=1) * gain


def _merge_tile(r0, hf_ref, hb_ref, of_ref, ob_ref, mo_ref, hgg_ref, x_ref, mn_ref, hn_ref, wo_ref,
                n2_ref, wrh_ref, wrl_ref, br_ref, x1_ref, h2_ref, route_ref, hist_ref):
    rs = slice(r0, r0 + PROJ_ROWS)
    m_out = _head_norm(hf_ref[rs, :] + hb_ref[rs, :], mn_ref[...]) * mo_ref[rs, :]
    hg_out = _head_norm(of_ref[rs, :] + ob_ref[rs, :], hn_ref[...]) * hgg_ref[rs, :]
    mixed = jnp.concatenate([m_out, hg_out], axis=1).astype(BF16)
    yield
    x1 = x_ref[rs, :] + _dot(mixed, wo_ref[...])
    x1_ref[rs, :] = x1
    h2 = _rms(x1, n2_ref[...])
    h2_ref[rs, :] = _pack_bf16_pairs(h2)
    h_hi = h2.astype(BF16)
    h_hi32 = h_hi.astype(F32)
    yield

    h_lo = (h2 - h_hi32).astype(BF16)
    logits = _dot(h_hi, wrh_ref[...]) + _dot(h_lo, wrh_ref[...]) + _dot(h_hi, wrl_ref[...]) + br_ref[...]
    lane = lax.broadcasted_iota(jnp.int32, logits.shape, 1)
    big = jnp.int32(LANES)
    neg = -jnp.inf
    yield
    g_log = jnp.where(lane < N_GROUPS, logits, neg)
    g_max = jnp.max(g_log, axis=-1, keepdims=True)
    g_idx = jnp.min(jnp.where(g_log == g_max, lane, big), axis=-1, keepdims=True)
    g_val = 1.0 / jnp.sum(jnp.exp(g_log - g_max), axis=-1, keepdims=True)
    yield
    e_lo = N_GROUPS + g_idx * EXPERTS_PER_GROUP
    e_log = jnp.where((lane >= e_lo) & (lane < e_lo + EXPERTS_PER_GROUP), logits, neg)
    m1 = jnp.max(e_log, axis=-1, keepdims=True)
    i1 = jnp.min(jnp.where(e_log == m1, lane, big), axis=-1, keepdims=True)
    yield
    e_log2 = jnp.where(lane == i1, neg, e_log)
    m2 = jnp.max(e_log2, axis=-1, keepdims=True)
    i2 = jnp.min(jnp.where(e_log2 == m2, lane, big), axis=-1, keepdims=True)
    r2 = jnp.exp(m2 - m1)
    w1 = g_val / (1.0 + r2)
    w2 = g_val * r2 / (1.0 + r2)
    yield
    rows = logits.shape[0]
    pick0 = lane == i1 - N_GROUPS
    pick1 = lane == i2 - N_GROUPS
    earlier = (lax.broadcasted_iota(jnp.int32, (rows, rows), 1)
               < lax.broadcasted_iota(jnp.int32, (rows, rows), 0)).astype(BF16)
    cnt0 = jnp.sum(pick0.astype(F32), axis=0, keepdims=True)
    cnt1 = jnp.sum(pick1.astype(F32), axis=0, keepdims=True)
    rank0 = jnp.sum(jnp.where(pick0, _dot(earlier, pick0.astype(BF16)), 0.0), axis=-1, keepdims=True)
    rank1 = jnp.sum(jnp.where(pick1, _dot(earlier, pick1.astype(BF16)) + cnt0, 0.0), axis=-1, keepdims=True)
    yield
    columns = ((i1 - N_GROUPS).astype(F32), (i2 - N_GROUPS).astype(F32), w1, w2, rank0, rank1)
    route = jnp.zeros_like(logits)
    for c, value in enumerate(columns):
        route = jnp.where(lane == c, value, route)
    route_ref[rs, :] = route
    hs = slice(r0 // PROJ_ROWS * SUBLANES, (r0 // PROJ_ROWS + 1) * SUBLANES)
    sub = lax.broadcasted_iota(jnp.int32, (SUBLANES, LANES), 0)
    hist_ref[hs, :] = jnp.where(sub == 0, cnt0 + cnt1, 0.0)


N_MERGE_STREAMS = 7


def _merge_kernel(*refs, n_a):
    side_a = refs[0:N_MERGE_STREAMS]
    side_b = refs[N_MERGE_STREAMS:2 * N_MERGE_STREAMS]
    rest = refs[2 * N_MERGE_STREAMS:]

    def block(side):
        _run_round_robin([_merge_tile(r0, *side, *rest) for r0 in range(0, MERGE_ROWS, PROJ_ROWS)])

    @pl.when(pl.program_id(0) < n_a)
    def _():
        block(side_a)

    @pl.when(pl.program_id(0) >= n_a)
    def _():
        block(side_b)


def _merge(streams_a, streams_b, m_norm, hg_norm, w_out, norm2, w_rg, b_rg, w_re, b_re):
    D = D_MODEL
    rows = MERGE_ROWS
    n_a = streams_a[0].shape[0] // rows
    n_b = streams_b[0].shape[0] // rows
    n_all = (n_a + n_b) * rows
    n_log = N_GROUPS + N_EXPERTS
    wr = jnp.pad(jnp.concatenate([w_rg, w_re], axis=1), ((0, 0), (0, LANES - n_log)))
    br = jnp.pad(jnp.concatenate([b_rg, b_re]), (0, LANES - n_log))[None, :]
    wr_hi = wr.astype(BF16)
    wr_lo = (wr - wr_hi.astype(F32)).astype(BF16)
    consts = [m_norm[None, :], hg_norm[None, :], w_out.astype(BF16), norm2[None, :], wr_hi, wr_lo, br]

    def full(arr):
        nd = arr.ndim
        return pl.BlockSpec(arr.shape, lambda i: (0,) * nd)

    def side_a(arr):
        return pl.BlockSpec((rows, arr.shape[1]), lambda i: (jnp.minimum(i, n_a - 1), 0))

    def side_b(arr):
        return pl.BlockSpec((rows, arr.shape[1]), lambda i: (jnp.maximum(i - n_a, 0), 0))

    def out(width):
        return pl.BlockSpec((rows, width), lambda i: (i, 0))

    return pl.pallas_call(
        functools.partial(_merge_kernel, n_a=n_a),
        grid=(n_a + n_b,),
        in_specs=[side_a(s) for s in streams_a] + [side_b(s) for s in streams_b] + [full(c) for c in consts],
        out_specs=[out(D), out(D // 2), out(LANES), pl.BlockSpec((rows // PROJ_ROWS * SUBLANES, LANES), lambda i: (i, 0))],
        out_shape=[jax.ShapeDtypeStruct((n_all, D), F32), jax.ShapeDtypeStruct((n_all, D // 2), jnp.uint32),
                   jax.ShapeDtypeStruct((n_all, LANES), F32),
                   jax.ShapeDtypeStruct((n_all // PROJ_ROWS * SUBLANES, LANES), F32)],
        compiler_params=pltpu.CompilerParams(
            dimension_semantics=("arbitrary",), vmem_limit_bytes=VMEM_LIMIT),
        name="merge",
    )(*streams_a, *streams_b, *consts)


def _sc_row_mover(src, idx, n_out, scatter, name):
    n_moved = idx.shape[0]
    D = src.shape[1]
    n_sub = SC_CORES * SC_SUBCORES
    per = n_moved // n_sub
    window = SC_WINDOW_BYTES // (D * src.dtype.itemsize)
    assert per * n_sub == n_moved and per % window == 0, (n_moved, per, window)
    assert not scatter or src.shape[0] % per == 0, (src.shape, per)
    mesh = plsc.VectorSubcoreMesh(core_axis_name="c", subcore_axis_name="s",
                                  num_cores=SC_CORES, num_subcores=SC_SUBCORES)

    def body(src_hbm, idx_hbm, out_hbm, idx_v, buf):
        base = (lax.axis_index("c") * SC_SUBCORES + lax.axis_index("s")) * per
        pltpu.sync_copy(idx_hbm.at[pl.ds(base, per)], idx_v)

        @pl.loop(0, per // window)
        def _(j):
            linear = pl.ds(base + j * window, window)
            indexed = idx_v.at[pl.ds(j * window, window)]
            if scatter:
                pltpu.sync_copy(src_hbm.at[pl.ds(lax.rem(base, src.shape[0]) + j * window, window)], buf)
                pltpu.sync_copy(buf, out_hbm.at[indexed])
            else:
                pltpu.sync_copy(src_hbm.at[indexed], buf)
                pltpu.sync_copy(buf, out_hbm.at[linear])

    return pl.kernel(
        body,
        out_type=jax.ShapeDtypeStruct((n_out, D), src.dtype),
        mesh=mesh,
        scratch_types=[pltpu.VMEM((per,), jnp.int32), pltpu.VMEM((window, D), src.dtype)],
        name=name,
    )(src, idx)


def _sc_gather_rows(src, idx):
    return _sc_row_mover(src, idx, idx.shape[0], False, "sc_gather_rows")


def _sc_scatter_rows(src, idx, n_out):
    return _sc_row_mover(src, idx, n_out, True, "sc_scatter_rows")


def _expert_kernel(be_ref, nu_ref, x_ref, w1_ref, w3_ref, w2_ref, o_ref, w1_bf, w3_bf, w2_bf):
    i = pl.program_id(0)
    active = i < nu_ref[0]
    new_expert = (i == 0) | (be_ref[i] != be_ref[jnp.maximum(i - 1, 0)])

    @pl.when(active & new_expert)
    def _():
        w1_bf[...] = w1_ref[0].astype(BF16)
        w3_bf[...] = w3_ref[0].astype(BF16)
        w2_bf[...] = w2_ref[0].astype(BF16)

    @pl.when(active)
    def _():
        half = D_MODEL // 2
        x_lo, x_hi = (part.astype(BF16) for part in _unpack_bf16_pairs(x_ref[...]))
        a = _dot(x_lo, w1_bf[0:half, :]) + _dot(x_hi, w1_bf[half:, :])
        b = _dot(x_lo, w3_bf[0:half, :]) + _dot(x_hi, w3_bf[half:, :])
        o_ref[...] = _pack_bf16_pairs(_dot((_silu(a) * b).astype(BF16), w2_bf[...]))


def _experts(xs, block_e, n_used, w1, w3, w2):
    rows = EXPERT_ROWS
    n_blocks = xs.shape[0] // rows
    D = D_MODEL

    def blk(i, be, nu):
        return jnp.minimum(i, nu[0] - 1)

    grid_spec = pltpu.PrefetchScalarGridSpec(
        num_scalar_prefetch=2,
        grid=(n_blocks,),
        in_specs=[
            pl.BlockSpec((rows, D // 2), lambda i, be, nu: (blk(i, be, nu), 0)),
            pl.BlockSpec((1, D, EXPERT_FF), lambda i, be, nu: (be[blk(i, be, nu)], 0, 0)),
            pl.BlockSpec((1, D, EXPERT_FF), lambda i, be, nu: (be[blk(i, be, nu)], 0, 0)),
            pl.BlockSpec((1, EXPERT_FF, D), lambda i, be, nu: (be[blk(i, be, nu)], 0, 0)),
        ],
        out_specs=pl.BlockSpec((rows, D // 2), lambda i, be, nu: (blk(i, be, nu), 0)),
        scratch_shapes=[pltpu.VMEM((D, EXPERT_FF), BF16), pltpu.VMEM((D, EXPERT_FF), BF16),
                        pltpu.VMEM((EXPERT_FF, D), BF16)],
    )
    return pl.pallas_call(
        _expert_kernel,
        grid_spec=grid_spec,
        out_shape=jax.ShapeDtypeStruct((xs.shape[0], D // 2), jnp.uint32),
        compiler_params=pltpu.CompilerParams(
            dimension_semantics=("arbitrary",), vmem_limit_bytes=VMEM_LIMIT),
        name="experts",
    )(block_e, n_used, xs, w1, w3, w2)


def _combine_kernel(y0_ref, y1_ref, x1_ref, route_ref, nf_ref, ya_ref, yb_ref, *, n_a):
    route = route_ref[...]
    r0 = jnp.concatenate(_unpack_bf16_pairs(y0_ref[...]), axis=1)
    r1 = jnp.concatenate(_unpack_bf16_pairs(y1_ref[...]), axis=1)
    y = _rms(x1_ref[...] + route[:, 2:3] * r0 + route[:, 3:4] * r1, nf_ref[...])

    @pl.when(pl.program_id(0) < n_a)
    def _():
        ya_ref[...] = y

    @pl.when(pl.program_id(0) >= n_a)
    def _():
        yb_ref[...] = y


def _combine(x1, route, y_rows, norm_f, n_tok_a):
    N, D = x1.shape
    rows = PROJ_ROWS
    nt = N // rows
    n_a = n_tok_a // rows

    def tok(width, offset=0):
        return pl.BlockSpec((rows, width), lambda i: (i + offset, 0))

    return pl.pallas_call(
        functools.partial(_combine_kernel, n_a=n_a),
        grid=(nt,),
        in_specs=[tok(D // 2), tok(D // 2, nt), tok(D), tok(LANES), pl.BlockSpec((1, D), lambda i: (0, 0))],
        out_specs=[pl.BlockSpec((rows, D), lambda i: (jnp.minimum(i, n_a - 1), 0)),
                   pl.BlockSpec((rows, D), lambda i: (jnp.maximum(i - n_a, 0), 0))],
        out_shape=[jax.ShapeDtypeStruct((n_tok_a, D), F32), jax.ShapeDtypeStruct((N - n_tok_a, D), F32)],
        compiler_params=pltpu.CompilerParams(
            dimension_semantics=("arbitrary",), vmem_limit_bytes=VMEM_LIMIT),
        name="combine",
    )(y_rows, y_rows, x1, route, norm_f[None, :])


def _plan_kernel(route_ref, table_ref, dest_ref):
    lane = lax.broadcasted_iota(jnp.int32, (PROJ_ROWS, LANES), 1)
    lane_f = lane.astype(F32)
    for tile in range(route_ref.shape[0] // PROJ_ROWS):
        rs = slice(tile * PROJ_ROWS, (tile + 1) * PROJ_ROWS)
        route = route_ref[rs, :]
        first = table_ref[tile * SUBLANES:tile * SUBLANES + 1, :]
        d0 = jnp.sum(jnp.where(lane_f == route[:, 0:1], first, 0.0), axis=-1, keepdims=True) + route[:, 4:5]
        d1 = jnp.sum(jnp.where(lane_f == route[:, 1:2], first, 0.0), axis=-1, keepdims=True) + route[:, 5:6]
        dest_ref[rs, :] = jnp.where(lane == 0, d0, jnp.where(lane == 1, d1, 0.0)).astype(jnp.int32)


def _dispatch_plan(route, hist):
    N = route.shape[0]
    rows = PROJ_ROWS
    n_tiles = N // rows
    blk = EXPERT_ROWS
    tile_counts = hist.reshape(n_tiles, SUBLANES, LANES)[:, 0, 0:N_EXPERTS].astype(jnp.int32)
    tile_first = jnp.cumsum(tile_counts, axis=0) - tile_counts
    counts = jnp.sum(tile_counts, axis=0)
    padded = ((counts + blk - 1) // blk) * blk
    pad_end = jnp.cumsum(padded)
    pad_start = pad_end - padded
    table = jnp.zeros((n_tiles, SUBLANES, LANES), F32).at[:, 0, 0:N_EXPERTS].set(
        (pad_start[None, :] + tile_first).astype(F32)).reshape(n_tiles * SUBLANES, LANES)
    per_step = PLAN_TILES
    assert n_tiles % per_step == 0, (n_tiles, per_step)
    dest_cols = pl.pallas_call(
        _plan_kernel,
        grid=(n_tiles // per_step,),
        in_specs=[pl.BlockSpec((per_step * rows, LANES), lambda i: (i, 0)),
                  pl.BlockSpec((per_step * SUBLANES, LANES), lambda i: (i, 0))],
        out_specs=pl.BlockSpec((per_step * rows, LANES), lambda i: (i, 0)),
        out_shape=jax.ShapeDtypeStruct((N, LANES), jnp.int32),
        compiler_params=pltpu.CompilerParams(dimension_semantics=("parallel",)),
        name="plan",
    )(route, table)
    dest = dest_cols[:, 0:TOP_K].T.reshape(TOP_K * N)
    n_blocks = (TOP_K * N + N_EXPERTS * (blk - 1) + blk - 1) // blk
    block_start = jnp.arange(n_blocks, dtype=jnp.int32) * blk
    block_e = jnp.sum((pad_end[None, :] <= block_start[:, None]).astype(jnp.int32), axis=1)
    block_e = jnp.minimum(block_e, N_EXPERTS - 1)
    n_used = (pad_end[-1] // blk).astype(jnp.int32).reshape(1)
    return dest, block_e, n_used, n_blocks * blk


def _token_mixer(x, norm1, w_in, b_in, conv_w, conv_b, m_fgate_bias, hg_lb_logits):
    B, T, D = x.shape
    q, k, v, mo, gcol, grow, hq, g_f, g_b, hv, hgg = _in_proj(
        x, norm1, w_in, b_in, conv_w, conv_b, m_fgate_bias, hg_lb_logits)
    h_f, o_f, h_b, o_b = _mixer(q, k, v, gcol, grow, hq, g_f, g_b, hv)
    return [a.reshape(B * T, a.shape[-1]) for a in (h_f, h_b, o_f, o_b, mo, hgg, x)]


def kernel(x_prompt, x_sample, norm1, w_in, b_in, conv_w, conv_b, m_fgate_bias, m_norm, hg_lb_logits, hg_norm,
           w_out, norm2, w_router_group, b_router_group, w_router_expert, b_router_expert, w1, w3, w2, norm_f):
    layer = 0
    mixer_args = (norm1[layer], w_in[layer], b_in[layer], conv_w[layer], conv_b[layer], m_fgate_bias[layer],
                  hg_lb_logits)
    streams_p = _token_mixer(x_prompt, *mixer_args)
    streams_s = _token_mixer(x_sample, *mixer_args)
    x1, h2, route, hist = _merge(streams_p, streams_s, m_norm[layer], hg_norm[layer], w_out[layer], norm2[layer],
                                 w_router_group[layer], b_router_group[layer], w_router_expert[layer],
                                 b_router_expert[layer])
    dest, block_e, n_used, n_rows = _dispatch_plan(route, hist)
    xs = _sc_scatter_rows(h2, dest, n_rows)
    out_rows = _experts(xs, block_e, n_used, w1[layer], w3[layer], w2[layer])
    y_rows = _sc_gather_rows(out_rows, dest)
    y_p, y_s = _combine(x1, route, y_rows, norm_f, streams_p[0].shape[0])
    return (y_p.reshape(x_prompt.shape), y_s.reshape(x_sample.shape))
```

```python
import functools

import jax
import jax.numpy as jnp
from jax import lax
from jax.experimental import pallas as pl
from jax.experimental.pallas import tpu as pltpu
from jax.experimental.pallas import tpu_sc as plsc

F32 = jnp.float32
BF16 = jnp.bfloat16

D_MODEL = 1024
N_HEADS = 4
D_HEAD = 128
WIDTH = N_HEADS * D_HEAD
CONV_K = 5
CONV_PAD = CONV_K // 2
N_GROUPS = 4
EXPERTS_PER_GROUP = 8
N_EXPERTS = N_GROUPS * EXPERTS_PER_GROUP
TOP_K = 2
EXPERT_FF = D_MODEL // 2
NORM_EPS = 1e-6

LANES = 128
SUBLANES = 8
CHUNK = 128
MIXER_CHUNKS = 2
PROJ_ROWS = 256
COMBINE_ROWS = 512
MERGE_ROWS = 512
PLAN_TILES = 8
IN_PROJ_ROWS = 512
IN_PROJ_STREAM = 256
HALO = SUBLANES
EXPERT_ROWS = 512
N_GATES = 4 * N_HEADS
SC_CORES = 2
SC_SUBCORES = 16
SC_WINDOW_BYTES = 128 * 1024
VMEM_LIMIT = 56 * 1024 * 1024


def _dot(a, b):
    return jnp.dot(a, b, preferred_element_type=F32)


def _dot_nt(a, b):
    return lax.dot_general(a, b, (((1,), (1,)), ((), ())), preferred_element_type=F32)


def _dot_tn(a, b):
    return lax.dot_general(a, b, (((0,), (0,)), ((), ())), preferred_element_type=F32)


def _split3(x):
    hi = x.astype(BF16)
    r1 = x - hi.astype(F32)
    mid = r1.astype(BF16)
    lo = (r1 - mid.astype(F32)).astype(BF16)
    return hi, mid, lo


def _pack_bf16_pairs(x):
    half = x.shape[1] // 2
    bits = lax.bitcast_convert_type(x.astype(BF16).astype(F32), jnp.uint32)
    return (bits[:, half:] & jnp.uint32(0xFFFF0000)) | (bits[:, :half] >> 16)


def _unpack_bf16_pairs(words):
    lo = lax.bitcast_convert_type(words << 16, F32)
    hi = lax.bitcast_convert_type(words & jnp.uint32(0xFFFF0000), F32)
    return lo, hi


def _silu(x):
    return x * jax.nn.sigmoid(x)


def _log_sigmoid(x):
    return -(jnp.maximum(-x, 0.0) + jnp.log1p(jnp.exp(-jnp.abs(x))))


def _rms(x, gain):
    return x * lax.rsqrt(jnp.mean(x * x, axis=-1, keepdims=True) + NORM_EPS) * gain


def _run_round_robin(generators):
    live = list(generators)
    while live:
        for gen in list(live):
            try:
                next(gen)
            except StopIteration:
                live.remove(gen)


def _in_proj_kernel(x_ref, xp_ref, xn_ref, n1_ref, wa_ref, ba_ref, wg_ref, bg_ref, wgt_ref, bgt_ref,
                    fbrow_ref, fbcol_ref, wh_ref, bh_ref, cw_ref, cb_ref, lbl_ref,
                    q_ref, k_ref, v_ref, mo_ref, gcol_ref, grow_ref, hq_ref, gf_ref, gb_ref, hv_ref, hgg_ref,
                    ext_ref):
    t = pl.program_id(1)
    nt = pl.num_programs(1)
    rows = x_ref.shape[1]
    gain = n1_ref[...]

    lbl = lbl_ref[...]
    lmax = jnp.max(lbl, axis=0, keepdims=True)
    le = jnp.exp(lbl - lmax)
    lb = le[0:1, :] / jnp.sum(le, axis=0, keepdims=True)

    wqk = wa_ref[:, 0:2 * WIDTH]
    bqk = ba_ref[:, 0:2 * WIDTH]
    hp = _rms(xp_ref[0], gain).astype(BF16)
    hn = _rms(xn_ref[0], gain).astype(BF16)
    ext_ref[0:HALO, :] = (_dot(hp, wqk) + bqk) * (t > 0).astype(F32)
    ext_ref[HALO + rows:2 * HALO + rows, :] = (_dot(hn, wqk) + bqk) * (t < nt - 1).astype(F32)

    def stream(r0, n):
        rs = slice(r0, r0 + n)
        h = _rms(x_ref[0, rs, :], gain).astype(BF16)

        def proj(w_ref, b_ref, lo, hi):
            return _dot(h, w_ref[:, lo:hi]) + b_ref[:, lo:hi]

        ext_ref[HALO + r0:HALO + r0 + n, :] = proj(wa_ref, ba_ref, 0, 2 * WIDTH)
        hq_pre = proj(wh_ref, bh_ref, 0, WIDTH)
        yield
        acc = cb_ref[...] + ext_ref[pl.ds(HALO - CONV_PAD + r0, n), :] * cw_ref[0:1, :]
        for j in range(1, CONV_K):
            acc = acc + ext_ref[pl.ds(HALO - CONV_PAD + j + r0, n), :] * cw_ref[j:j + 1, :]
        qk = _silu(acc)
        q_ref[0, rs, :] = qk[:, 0:WIDTH] * (D_HEAD ** -0.5)
        k_ref[0, rs, :] = qk[:, WIDTH:2 * WIDTH]
        v_ref[0, rs, :] = proj(wa_ref, ba_ref, 2 * WIDTH, 3 * WIDTH)
        yield
        hq_ref[0, rs, :] = _silu(hq_pre)
        mo_ref[0, rs, :] = jax.nn.sigmoid(proj(wa_ref, ba_ref, 3 * WIDTH, 4 * WIDTH))
        yield
        gf_ref[0, rs, :] = lb + (1.0 - lb) * jax.nn.sigmoid(proj(wh_ref, bh_ref, WIDTH, 2 * WIDTH))
        yield
        gb_ref[0, rs, :] = lb + (1.0 - lb) * jax.nn.sigmoid(proj(wh_ref, bh_ref, 2 * WIDTH, 3 * WIDTH))
        hv_ref[0, rs, :] = proj(wh_ref, bh_ref, 3 * WIDTH, 4 * WIDTH)
        yield
        hgg_ref[0, rs, :] = _silu(proj(wh_ref, bh_ref, 4 * WIDTH, 5 * WIDTH))
        gc = _dot(h, wg_ref[...]) + bg_ref[...]
        lane = lax.broadcasted_iota(jnp.int32, gc.shape, 1)
        is_f = (lane >= 2 * N_HEADS) & (lane < N_GATES)
        gcol_ref[0, rs, :] = jnp.where(is_f, _log_sigmoid(gc + fbrow_ref[...]), gc)
        gr = _dot_nt(wgt_ref[...], h) + bgt_ref[...]
        sub = lax.broadcasted_iota(jnp.int32, gr.shape, 0)
        grow_ref[0, :, rs] = jnp.where(sub >= 2 * N_HEADS, _log_sigmoid(gr + fbcol_ref[...]), gr)

    _run_round_robin([stream(r0, IN_PROJ_STREAM) for r0 in range(0, rows, IN_PROJ_STREAM)])


def _in_proj(x, norm1, w_in, b_in, conv_w, conv_b, fgate_bias, lb_logits):
    B, T, D = x.shape
    rows = IN_PROJ_ROWS
    assert T % rows == 0, (T, rows)
    nt = T // rows
    a_w = 4 * WIDTH
    wa = w_in[:, 0:a_w].astype(BF16)
    ba = b_in[None, 0:a_w]
    wg32 = jnp.pad(w_in[:, a_w:a_w + N_GATES], ((0, 0), (0, LANES - N_GATES)))
    bg = jnp.pad(b_in[a_w:a_w + N_GATES], (0, LANES - N_GATES))[None, :]
    wg = wg32.astype(BF16)
    wgt = w_in[:, a_w:a_w + N_GATES].T.astype(BF16)
    bgt = b_in[a_w:a_w + N_GATES][:, None]
    fb = fgate_bias.reshape(2 * N_HEADS)
    fbrow = jnp.zeros((1, LANES), F32).at[0, 2 * N_HEADS:N_GATES].set(fb)
    fbcol = jnp.zeros((N_GATES, 1), F32).at[2 * N_HEADS:N_GATES, 0].set(fb)
    wh = w_in[:, a_w + N_GATES:].astype(BF16)
    bh = b_in[None, a_w + N_GATES:]

    tiles_per_halo = rows // HALO
    n_halo = T // HALO

    def full(arr):
        nd = arr.ndim
        return pl.BlockSpec(arr.shape, lambda b, t: (0,) * nd)

    def tok(width):
        return pl.BlockSpec((1, rows, width), lambda b, t: (b, t, 0))

    in_specs = [
        tok(D),
        pl.BlockSpec((1, HALO, D), lambda b, t: (b, jnp.maximum(t * tiles_per_halo - 1, 0), 0)),
        pl.BlockSpec((1, HALO, D), lambda b, t: (b, jnp.minimum((t + 1) * tiles_per_halo, n_halo - 1), 0)),
    ]
    consts = [norm1[None, :], wa, ba, wg, bg, wgt, bgt, fbrow, fbcol, wh, bh, conv_w, conv_b[None, :], lb_logits]
    in_specs += [full(c) for c in consts]
    tok_out = jax.ShapeDtypeStruct((B, T, WIDTH), F32)
    out_shape = [tok_out, tok_out, tok_out, tok_out,
                 jax.ShapeDtypeStruct((B, T, LANES), F32),
                 jax.ShapeDtypeStruct((B, N_GATES, T), F32),
                 tok_out, tok_out, tok_out, tok_out, tok_out]
    out_specs = [tok(WIDTH)] * 4 + [tok(LANES), pl.BlockSpec((1, N_GATES, rows), lambda b, t: (b, 0, t))] + [tok(WIDTH)] * 5
    return pl.pallas_call(
        _in_proj_kernel,
        grid=(B, nt),
        in_specs=in_specs,
        out_specs=out_specs,
        out_shape=out_shape,
        scratch_shapes=[pltpu.VMEM((rows + 2 * HALO, 2 * WIDTH), F32)],
        compiler_params=pltpu.CompilerParams(
            dimension_semantics=("parallel", "parallel"), vmem_limit_bytes=VMEM_LIMIT),
        name="in_proj",
    )(x, x, x, *consts)


def _cumsum_rows(tri_bf, x):
    hi, mid, lo = _split3(x)
    return _dot(tri_bf, hi) + _dot(tri_bf, mid) + _dot(tri_bf, lo)


def _cumsum_lanes(x, tri_bf):
    hi, mid, lo = _split3(x)
    return _dot(hi, tri_bf) + _dot(mid, tri_bf) + _dot(lo, tri_bf)


def _mlstm_chunk(q, k, vext, i_col, b_col, i_row, b_row, seen, last, c_ref, m_ref, out_ref, rs, sl):
    m_prev = m_ref[:, 0:1]
    c_prev = c_ref[...]
    q_bf = q.astype(BF16)
    log_d = jnp.where(seen, b_col - b_row + i_row, -jnp.inf)
    m_inter = b_col + m_prev
    m_t = jnp.maximum(m_inter, jnp.max(log_d, axis=-1, keepdims=True))
    qk = _dot_nt(q_bf, k.astype(BF16))
    yield
    scores = (qk * jnp.exp(log_d - m_t)).astype(BF16)
    inter_scale = jnp.exp(m_inter - m_t)
    b_last = b_col[last:last + 1, :]
    log_w = b_last - b_col + i_col
    m_new = jnp.maximum(b_last + m_prev, jnp.max(log_w, axis=0, keepdims=True))
    w = jnp.exp(log_w - m_new)
    decay = jnp.exp(b_last + m_prev - m_new)
    kw = (k * w).astype(BF16)
    yield
    numden = _dot(scores, vext) + inter_scale * _dot(q_bf, c_prev.astype(BF16))
    update = _dot_tn(kw, vext)
    yield
    num = numden[:, 0:D_HEAD]
    den = numden[:, D_HEAD:2 * D_HEAD]
    out_ref[0, rs, sl] = num / jnp.maximum(jnp.abs(den), jnp.exp(-m_t))
    c_ref[...] = decay * c_prev + update
    m_ref[...] = jnp.broadcast_to(m_new, (1, LANES))


def _hgrn2_level_small(q3, k3, pre3, suf3, half, rev, sub_iota):
    upper = (sub_iota & half) != 0
    second = jnp.logical_not(upper) if rev else upper
    end = 0 if rev else half - 1
    y = jnp.where((sub_iota & (half - 1)) == end, pre3, 0.0)
    step = 1 if rev else -1
    span = 1
    while span < half:
        y = y + pltpu.roll(y, (step * span) % SUBLANES, 1)
        span *= 2
    if 2 * half == SUBLANES:
        other = pltpu.roll(y, half, 1)
    else:
        other = jnp.where(upper, pltpu.roll(y, half, 1), pltpu.roll(y, SUBLANES - half, 1))
    z = jnp.where(second, q3 * pre3, k3 * suf3)
    return z, pre3 * jnp.where(second, other, 1.0), suf3 * jnp.where(second, 1.0, other)


def _hgrn2_level_big(q, k, pre, suf, half, rev):
    L, width = q.shape
    shape = (L // (2 * half), 2, half, width)
    q4, k4, pre4, suf4 = (a.reshape(shape) for a in (q, k, pre, suf))
    first = 1 if rev else 0
    second = 1 - first
    end = 0 if rev else half - 1
    total_first = pre4[:, first, end:end + 1, :]
    total_second = pre4[:, second, end:end + 1, :]

    def join(at_first, at_second):
        parts = (at_second, at_first) if rev else (at_first, at_second)
        return jnp.stack(parts, axis=1).reshape(L, width)

    z = join(k4[:, first] * suf4[:, first], q4[:, second] * pre4[:, second])
    pre_new = join(pre4[:, first], pre4[:, second] * total_first)
    suf_new = join(suf4[:, first] * total_second, suf4[:, second])
    return z, pre_new, suf_new


def _hgrn2_chunk(q, g, v_bf, rev, level, diag, sub_iota, st_refs, out_ref, rs, sl):
    L, width = q.shape
    heads = [slice(h * D_HEAD, (h + 1) * D_HEAD) for h in range(width // D_HEAD)]
    k = 1.0 - g
    q_bf = q.astype(BF16)
    k_bf = k.astype(BF16)
    att = [jnp.where(diag, _dot_nt(q_bf[:, s], k_bf[:, s]), 0.0) for s in heads]
    small = (L // SUBLANES, SUBLANES, width)
    q3, k3, pre, suf = q.reshape(small), k.reshape(small), g.reshape(small), jnp.ones(small, F32)
    half = 1
    bit = 0
    while half < L:
        if half == SUBLANES:
            pre, suf = pre.reshape(L, width), suf.reshape(L, width)
        if half < SUBLANES:
            z, pre, suf = _hgrn2_level_small(q3, k3, pre, suf, half, rev, sub_iota)
            z = z.reshape(L, width)
        else:
            z, pre, suf = _hgrn2_level_big(q, k, pre, suf, half, rev)
        z = z.astype(BF16)
        att = [jnp.where(level == bit, _dot_nt(z[:, s], z[:, s]), a) for a, s in zip(att, heads)]
        half *= 2
        bit += 1
        yield
    last = 0 if rev else L - 1
    q_dec = (q * pre).astype(BF16)
    k_dec = (k * suf).astype(BF16)
    outs = []
    for h, s in enumerate(heads):
        st_prev = st_refs[h][...]
        outs.append(_dot_nt(q_dec[:, s], st_prev.astype(BF16)) + _dot(att[h].astype(BF16), v_bf[:, s]))
        st_refs[h][...] = st_prev * pre[last:last + 1, s] + _dot_tn(v_bf[:, s], k_dec[:, s])
    out_ref[0, rs, sl] = jnp.concatenate(outs, axis=1)


def _mixer_kernel(qf_ref, kf_ref, vf_ref, gcf_ref, grf_ref, hqf_ref, hgf_ref, hvf_ref,
                  qb_ref, kb_ref, vb_ref, gcb_ref, grb_ref, hqb_ref, hgb_ref, hvb_ref,
                  hf_ref, of_ref, hb_ref, ob_ref, *state_refs):
    L = CHUNK
    n_state = 2 * N_HEADS
    c_refs, m_refs, st_refs = (state_refs[i * n_state:(i + 1) * n_state] for i in range(3))

    @pl.when(pl.program_id(1) == 0)
    def _():
        for ref in state_refs:
            ref[...] = jnp.zeros_like(ref)

    row = lax.broadcasted_iota(jnp.int32, (L, L), 0)
    col = lax.broadcasted_iota(jnp.int32, (L, L), 1)
    sub_iota = lax.broadcasted_iota(jnp.int32, (L // SUBLANES, SUBLANES, LANES), 1)
    diag = row == col
    diff = row ^ col
    high_bit = jnp.zeros((L, L), jnp.int32)
    half = 2
    while half < L:
        high_bit = high_bit + (diff >= half).astype(jnp.int32)
        half *= 2
    ones = jnp.ones((L, D_HEAD), BF16)

    dirs = (
        (0, qf_ref, kf_ref, vf_ref, gcf_ref, grf_ref, hqf_ref, hgf_ref, hvf_ref, hf_ref, of_ref),
        (1, qb_ref, kb_ref, vb_ref, gcb_ref, grb_ref, hqb_ref, hgb_ref, hvb_ref, hb_ref, ob_ref),
    )
    masks = []
    for rev in (False, True):
        seen = (col >= row) if rev else (col <= row)
        before = (col > row) if rev else (col < row)
        level = jnp.where(before, high_bit, -1)
        tri = seen.astype(BF16)
        tri_t = (row >= col if rev else row <= col).astype(BF16)
        masks.append((seen, level, tri, tri_t))
    n_sub = qf_ref.shape[1] // L
    for step in range(n_sub):
        stages = []
        for d, q_ref, k_ref, v_ref, gc_ref, gr_ref, hq_ref, hg_ref, hv_ref, h_out, o_out in dirs:
            rev = d == 1
            seen, level, tri, tri_t = masks[d]
            last = 0 if rev else L - 1
            sub_chunk = n_sub - 1 - step if rev else step
            rs = slice(sub_chunk * L, (sub_chunk + 1) * L)
            gc = gc_ref[0, rs, :]
            gr = gr_ref[0, :, rs]
            gc_cum = _cumsum_rows(tri, gc)
            gr_cum = _cumsum_lanes(gr, tri_t)
            for hd in range(N_HEADS):
                sl = slice(hd * D_HEAD, (hd + 1) * D_HEAD)
                gi = d * N_HEADS + hd
                gf = 2 * N_HEADS + gi
                idx = d * N_HEADS + hd
                vext = jnp.concatenate([v_ref[0, rs, sl].astype(BF16), ones], axis=1)
                stages.append(_mlstm_chunk(
                    q_ref[0, rs, sl], k_ref[0, rs, sl], vext,
                    gc[:, gi:gi + 1], gc_cum[:, gf:gf + 1], gr[gi:gi + 1, :], gr_cum[gf:gf + 1, :],
                    seen, last, c_refs[idx], m_refs[idx], h_out, rs, sl))
                stages.append(_hgrn2_chunk(
                    hq_ref[0, rs, sl], hg_ref[0, rs, sl], hv_ref[0, rs, sl].astype(BF16),
                    rev, level, diag, sub_iota, [st_refs[idx]], o_out, rs, sl))
        group = 2 * N_HEADS
        for start in range(0, len(stages), group):
            _run_round_robin(stages[start:start + group])


def _mixer(q, k, v, gcol, grow, hq, g_f, g_b, hv):
    B, T, _ = q.shape
    L = CHUNK * MIXER_CHUNKS
    nc = T // L

    def fwd(width):
        return pl.BlockSpec((1, L, width), lambda b, c: (b, c, 0))

    def bwd(width):
        return pl.BlockSpec((1, L, width), lambda b, c: (b, nc - 1 - c, 0))

    grow_f = pl.BlockSpec((1, N_GATES, L), lambda b, c: (b, 0, c))
    grow_b = pl.BlockSpec((1, N_GATES, L), lambda b, c: (b, 0, nc - 1 - c))
    in_specs = ([fwd(WIDTH)] * 3 + [fwd(LANES), grow_f] + [fwd(WIDTH)] * 3
                + [bwd(WIDTH)] * 3 + [bwd(LANES), grow_b] + [bwd(WIDTH)] * 3)
    out = jax.ShapeDtypeStruct((B, T, WIDTH), F32)
    n_state = 2 * N_HEADS
    return pl.pallas_call(
        _mixer_kernel,
        grid=(B, nc),
        in_specs=in_specs,
        out_specs=[fwd(WIDTH), fwd(WIDTH), bwd(WIDTH), bwd(WIDTH)],
        out_shape=[out, out, out, out],
        scratch_shapes=([pltpu.VMEM((D_HEAD, 2 * D_HEAD), F32)] * n_state
                        + [pltpu.VMEM((1, LANES), F32)] * n_state
                        + [pltpu.VMEM((D_HEAD, D_HEAD), F32)] * n_state),
        compiler_params=pltpu.CompilerParams(
            dimension_semantics=("parallel", "arbitrary"), vmem_limit_bytes=VMEM_LIMIT),
        name="mixer",
    )(q, k, v, gcol, grow, hq, g_f, hv, q, k, v, gcol, grow, hq, g_b, hv)


def _head_norm(hsum, gain):
    parts = []
    for hd in range(N_HEADS):
        hh = hsum[:, hd * D_HEAD:(hd + 1) * D_HEAD]
        parts.append(hh * lax.rsqrt(jnp.mean(hh * hh, axis=-1, keepdims=True) + NORM_EPS))
    return jnp.concatenate(parts, axis=1) * gain


def _merge_tile(r0, hf_ref, hb_ref, of_ref, ob_ref, mo_ref, hgg_ref, x_ref, mn_ref, hn_ref, wo_ref,
                n2_ref, wrh_ref, wrl_ref, br_ref, x1_ref, h2_ref, route_ref, hist_ref):
    rs = slice(r0, r0 + PROJ_ROWS)
    m_out = _head_norm(hf_ref[rs, :] + hb_ref[rs, :], mn_ref[...]) * mo_ref[rs, :]
    hg_out = _head_norm(of_ref[rs, :] + ob_ref[rs, :], hn_ref[...]) * hgg_ref[rs, :]
    mixed = jnp.concatenate([m_out, hg_out], axis=1).astype(BF16)
    yield
    x1 = x_ref[rs, :] + _dot(mixed, wo_ref[...])
    x1_ref[rs, :] = x1
    h2 = _rms(x1, n2_ref[...])
    h2_ref[rs, :] = _pack_bf16_pairs(h2)
    h_hi = h2.astype(BF16)
    h_hi32 = h_hi.astype(F32)
    yield

    h_lo = (h2 - h_hi32).astype(BF16)
    logits = _dot(h_hi, wrh_ref[...]) + _dot(h_lo, wrh_ref[...]) + _dot(h_hi, wrl_ref[...]) + br_ref[...]
    lane = lax.broadcasted_iota(jnp.int32, logits.shape, 1)
    big = jnp.int32(LANES)
    neg = -jnp.inf
    yield
    g_log = jnp.where(lane < N_GROUPS, logits, neg)
    g_max = jnp.max(g_log, axis=-1, keepdims=True)
    g_idx = jnp.min(jnp.where(g_log == g_max, lane, big), axis=-1, keepdims=True)
    g_val = 1.0 / jnp.sum(jnp.exp(g_log - g_max), axis=-1, keepdims=True)
    yield
    e_lo = N_GROUPS + g_idx * EXPERTS_PER_GROUP
    e_log = jnp.where((lane >= e_lo) & (lane < e_lo + EXPERTS_PER_GROUP), logits, neg)
    m1 = jnp.max(e_log, axis=-1, keepdims=True)
    i1 = jnp.min(jnp.where(e_log == m1, lane, big), axis=-1, keepdims=True)
    yield
    e_log2 = jnp.where(lane == i1, neg, e_log)
    m2 = jnp.max(e_log2, axis=-1, keepdims=True)
    i2 = jnp.min(jnp.where(e_log2 == m2, lane, big), axis=-1, keepdims=True)
    r2 = jnp.exp(m2 - m1)
    w1 = g_val / (1.0 + r2)
    w2 = g_val * r2 / (1.0 + r2)
    yield
    rows = logits.shape[0]
    pick0 = lane == i1 - N_GROUPS
    pick1 = lane == i2 - N_GROUPS
    earlier = (lax.broadcasted_iota(jnp.int32, (rows, rows), 1)
               < lax.broadcasted_iota(jnp.int32, (rows, rows), 0)).astype(BF16)
    cnt0 = jnp.sum(pick0.astype(F32), axis=0, keepdims=True)
    cnt1 = jnp.sum(pick1.astype(F32), axis=0, keepdims=True)
    rank0 = jnp.sum(jnp.where(pick0, _dot(earlier, pick0.astype(BF16)), 0.0), axis=-1, keepdims=True)
    rank1 = jnp.sum(jnp.where(pick1, _dot(earlier, pick1.astype(BF16)) + cnt0, 0.0), axis=-1, keepdims=True)
    yield
    columns = ((i1 - N_GROUPS).astype(F32), (i2 - N_GROUPS).astype(F32), w1, w2, rank0, rank1)
    route = jnp.zeros_like(logits)
    for c, value in enumerate(columns):
        route = jnp.where(lane == c, value, route)
    route_ref[rs, :] = route
    hs = slice(r0 // PROJ_ROWS * SUBLANES, (r0 // PROJ_ROWS + 1) * SUBLANES)
    sub = lax.broadcasted_iota(jnp.int32, (SUBLANES, LANES), 0)
    hist_ref[hs, :] = jnp.where(sub == 0, cnt0 + cnt1, 0.0)


N_MERGE_STREAMS = 7


def _merge_kernel(*refs, n_a):
    side_a = refs[0:N_MERGE_STREAMS]
    side_b = refs[N_MERGE_STREAMS:2 * N_MERGE_STREAMS]
    rest = refs[2 * N_MERGE_STREAMS:]

    def block(side):
        _run_round_robin([_merge_tile(r0, *side, *rest) for r0 in range(0, MERGE_ROWS, PROJ_ROWS)])

    @pl.when(pl.program_id(0) < n_a)
    def _():
        block(side_a)

    @pl.when(pl.program_id(0) >= n_a)
    def _():
        block(side_b)


def _merge(streams_a, streams_b, m_norm, hg_norm, w_out, norm2, w_rg, b_rg, w_re, b_re):
    D = D_MODEL
    rows = MERGE_ROWS
    n_a = streams_a[0].shape[0] // rows
    n_b = streams_b[0].shape[0] // rows
    n_all = (n_a + n_b) * rows
    n_log = N_GROUPS + N_EXPERTS
    wr = jnp.pad(jnp.concatenate([w_rg, w_re], axis=1), ((0, 0), (0, LANES - n_log)))
    br = jnp.pad(jnp.concatenate([b_rg, b_re]), (0, LANES - n_log))[None, :]
    wr_hi = wr.astype(BF16)
    wr_lo = (wr - wr_hi.astype(F32)).astype(BF16)
    consts = [m_norm[None, :], hg_norm[None, :], w_out.astype(BF16), norm2[None, :], wr_hi, wr_lo, br]

    def full(arr):
        nd = arr.ndim
        return pl.BlockSpec(arr.shape, lambda i: (0,) * nd)

    def side_a(arr):
        return pl.BlockSpec((rows, arr.shape[1]), lambda i: (jnp.minimum(i, n_a - 1), 0))

    def side_b(arr):
        return pl.BlockSpec((rows, arr.shape[1]), lambda i: (jnp.maximum(i - n_a, 0), 0))

    def out(width):
        return pl.BlockSpec((rows, width), lambda i: (i, 0))

    return pl.pallas_call(
        functools.partial(_merge_kernel, n_a=n_a),
        grid=(n_a + n_b,),
        in_specs=[side_a(s) for s in streams_a] + [side_b(s) for s in streams_b] + [full(c) for c in consts],
        out_specs=[out(D), out(D // 2), out(LANES), pl.BlockSpec((rows // PROJ_ROWS * SUBLANES, LANES), lambda i: (i, 0))],
        out_shape=[jax.ShapeDtypeStruct((n_all, D), F32), jax.ShapeDtypeStruct((n_all, D // 2), jnp.uint32),
                   jax.ShapeDtypeStruct((n_all, LANES), F32),
                   jax.ShapeDtypeStruct((n_all // PROJ_ROWS * SUBLANES, LANES), F32)],
        compiler_params=pltpu.CompilerParams(
            dimension_semantics=("arbitrary",), vmem_limit_bytes=VMEM_LIMIT),
        name="merge",
    )(*streams_a, *streams_b, *consts)


def _sc_row_mover(src, idx, n_out, scatter, name):
    n_moved = idx.shape[0]
    D = src.shape[1]
    n_sub = SC_CORES * SC_SUBCORES
    per = n_moved // n_sub
    window = SC_WINDOW_BYTES // (D * src.dtype.itemsize)
    assert per * n_sub == n_moved and per % window == 0, (n_moved, per, window)
    assert not scatter or src.shape[0] % per == 0, (src.shape, per)
    mesh = plsc.VectorSubcoreMesh(core_axis_name="c", subcore_axis_name="s",
                                  num_cores=SC_CORES, num_subcores=SC_SUBCORES)

    def body(src_hbm, idx_hbm, out_hbm, idx_v, buf):
        base = (lax.axis_index("c") * SC_SUBCORES + lax.axis_index("s")) * per
        pltpu.sync_copy(idx_hbm.at[pl.ds(base, per)], idx_v)

        @pl.loop(0, per // window)
        def _(j):
            linear = pl.ds(base + j * window, window)
            indexed = idx_v.at[pl.ds(j * window, window)]
            if scatter:
                pltpu.sync_copy(src_hbm.at[pl.ds(lax.rem(base, src.shape[0]) + j * window, window)], buf)
                pltpu.sync_copy(buf, out_hbm.at[indexed])
            else:
                pltpu.sync_copy(src_hbm.at[indexed], buf)
                pltpu.sync_copy(buf, out_hbm.at[linear])

    return pl.kernel(
        body,
        out_type=jax.ShapeDtypeStruct((n_out, D), src.dtype),
        mesh=mesh,
        scratch_types=[pltpu.VMEM((per,), jnp.int32), pltpu.VMEM((window, D), src.dtype)],
        name=name,
    )(src, idx)


def _sc_gather_rows(src, idx):
    return _sc_row_mover(src, idx, idx.shape[0], False, "sc_gather_rows")


def _sc_scatter_rows(src, idx, n_out):
    return _sc_row_mover(src, idx, n_out, True, "sc_scatter_rows")


def _expert_kernel(be_ref, nu_ref, x_ref, w1_ref, w3_ref, w2_ref, o_ref, w1_bf, w3_bf, w2_bf):
    i = pl.program_id(0)
    active = i < nu_ref[0]
    new_expert = (i == 0) | (be_ref[i] != be_ref[jnp.maximum(i - 1, 0)])

    @pl.when(active & new_expert)
    def _():
        w1_bf[...] = w1_ref[0].astype(BF16)
        w3_bf[...] = w3_ref[0].astype(BF16)
        w2_bf[...] = w2_ref[0].astype(BF16)

    @pl.when(active)
    def _():
        half = D_MODEL // 2
        x_lo, x_hi = (part.astype(BF16) for part in _unpack_bf16_pairs(x_ref[...]))
        a = _dot(x_lo, w1_bf[0:half, :]) + _dot(x_hi, w1_bf[half:, :])
        b = _dot(x_lo, w3_bf[0:half, :]) + _dot(x_hi, w3_bf[half:, :])
        o_ref[...] = _pack_bf16_pairs(_dot((_silu(a) * b).astype(BF16), w2_bf[...]))


def _experts(xs, block_e, n_used, w1, w3, w2):
    rows = EXPERT_ROWS
    n_blocks = xs.shape[0] // rows
    D = D_MODEL

    def blk(i, be, nu):
        return jnp.minimum(i, nu[0] - 1)

    grid_spec = pltpu.PrefetchScalarGridSpec(
        num_scalar_prefetch=2,
        grid=(n_blocks,),
        in_specs=[
            pl.BlockSpec((rows, D // 2), lambda i, be, nu: (blk(i, be, nu), 0)),
            pl.BlockSpec((1, D, EXPERT_FF), lambda i, be, nu: (be[blk(i, be, nu)], 0, 0)),
            pl.BlockSpec((1, D, EXPERT_FF), lambda i, be, nu: (be[blk(i, be, nu)], 0, 0)),
            pl.BlockSpec((1, EXPERT_FF, D), lambda i, be, nu: (be[blk(i, be, nu)], 0, 0)),
        ],
        out_specs=pl.BlockSpec((rows, D // 2), lambda i, be, nu: (blk(i, be, nu), 0)),
        scratch_shapes=[pltpu.VMEM((D, EXPERT_FF), BF16), pltpu.VMEM((D, EXPERT_FF), BF16),
                        pltpu.VMEM((EXPERT_FF, D), BF16)],
    )
    return pl.pallas_call(
        _expert_kernel,
        grid_spec=grid_spec,
        out_shape=jax.ShapeDtypeStruct((xs.shape[0], D // 2), jnp.uint32),
        compiler_params=pltpu.CompilerParams(
            dimension_semantics=("arbitrary",), vmem_limit_bytes=VMEM_LIMIT),
        name="experts",
    )(block_e, n_used, xs, w1, w3, w2)


def _combine_kernel(y0_ref, y1_ref, x1_ref, route_ref, nf_ref, ya_ref, yb_ref, *, n_a):
    route = route_ref[...]
    r0 = jnp.concatenate(_unpack_bf16_pairs(y0_ref[...]), axis=1)
    r1 = jnp.concatenate(_unpack_bf16_pairs(y1_ref[...]), axis=1)
    y = _rms(x1_ref[...] + route[:, 2:3] * r0 + route[:, 3:4] * r1, nf_ref[...])

    @pl.when(pl.program_id(0) < n_a)
    def _():
        ya_ref[...] = y

    @pl.when(pl.program_id(0) >= n_a)
    def _():
        yb_ref[...] = y


def _combine(x1, route, y_rows, norm_f, n_tok_a):
    N, D = x1.shape
    rows = COMBINE_ROWS
    assert N % rows == 0 and n_tok_a % rows == 0, (N, n_tok_a, rows)
    nt = N // rows
    n_a = n_tok_a // rows

    def tok(width, offset=0):
        return pl.BlockSpec((rows, width), lambda i: (i + offset, 0))

    return pl.pallas_call(
        functools.partial(_combine_kernel, n_a=n_a),
        grid=(nt,),
        in_specs=[tok(D // 2), tok(D // 2, nt), tok(D), tok(LANES), pl.BlockSpec((1, D), lambda i: (0, 0))],
        out_specs=[pl.BlockSpec((rows, D), lambda i: (jnp.minimum(i, n_a - 1), 0)),
                   pl.BlockSpec((rows, D), lambda i: (jnp.maximum(i - n_a, 0), 0))],
        out_shape=[jax.ShapeDtypeStruct((n_tok_a, D), F32), jax.ShapeDtypeStruct((N - n_tok_a, D), F32)],
        compiler_params=pltpu.CompilerParams(
            dimension_semantics=("arbitrary",), vmem_limit_bytes=VMEM_LIMIT),
        name="combine",
    )(y_rows, y_rows, x1, route, norm_f[None, :])


def _plan_kernel(route_ref, table_ref, dest_ref):
    lane = lax.broadcasted_iota(jnp.int32, (PROJ_ROWS, LANES), 1)
    lane_f = lane.astype(F32)
    for tile in range(route_ref.shape[0] // PROJ_ROWS):
        rs = slice(tile * PROJ_ROWS, (tile + 1) * PROJ_ROWS)
        route = route_ref[rs, :]
        first = table_ref[tile * SUBLANES:tile * SUBLANES + 1, :]
        d0 = jnp.sum(jnp.where(lane_f == route[:, 0:1], first, 0.0), axis=-1, keepdims=True) + route[:, 4:5]
        d1 = jnp.sum(jnp.where(lane_f == route[:, 1:2], first, 0.0), axis=-1, keepdims=True) + route[:, 5:6]
        dest_ref[rs, :] = jnp.where(lane == 0, d0, jnp.where(lane == 1, d1, 0.0)).astype(jnp.int32)


def _dispatch_plan(route, hist):
    N = route.shape[0]
    rows = PROJ_ROWS
    n_tiles = N // rows
    blk = EXPERT_ROWS
    tile_counts = hist.reshape(n_tiles, SUBLANES, LANES)[:, 0, 0:N_EXPERTS].astype(jnp.int32)
    tile_first = jnp.cumsum(tile_counts, axis=0) - tile_counts
    counts = jnp.sum(tile_counts, axis=0)
    padded = ((counts + blk - 1) // blk) * blk
    pad_end = jnp.cumsum(padded)
    pad_start = pad_end - padded
    table = jnp.zeros((n_tiles, SUBLANES, LANES), F32).at[:, 0, 0:N_EXPERTS].set(
        (pad_start[None, :] + tile_first).astype(F32)).reshape(n_tiles * SUBLANES, LANES)
    per_step = PLAN_TILES
    assert n_tiles % per_step == 0, (n_tiles, per_step)
    dest_cols = pl.pallas_call(
        _plan_kernel,
        grid=(n_tiles // per_step,),
        in_specs=[pl.BlockSpec((per_step * rows, LANES), lambda i: (i, 0)),
                  pl.BlockSpec((per_step * SUBLANES, LANES), lambda i: (i, 0))],
        out_specs=pl.BlockSpec((per_step * rows, LANES), lambda i: (i, 0)),
        out_shape=jax.ShapeDtypeStruct((N, LANES), jnp.int32),
        compiler_params=pltpu.CompilerParams(dimension_semantics=("parallel",)),
        name="plan",
    )(route, table)
    dest = dest_cols[:, 0:TOP_K].T.reshape(TOP_K * N)
    n_blocks = (TOP_K * N + N_EXPERTS * (blk - 1) + blk - 1) // blk
    block_start = jnp.arange(n_blocks, dtype=jnp.int32) * blk
    block_e = jnp.sum((pad_end[None, :] <= block_start[:, None]).astype(jnp.int32), axis=1)
    block_e = jnp.minimum(block_e, N_EXPERTS - 1)
    n_used = (pad_end[-1] // blk).astype(jnp.int32).reshape(1)
    return dest, block_e, n_used, n_blocks * blk


def _token_mixer(x, norm1, w_in, b_in, conv_w, conv_b, m_fgate_bias, hg_lb_logits):
    B, T, D = x.shape
    q, k, v, mo, gcol, grow, hq, g_f, g_b, hv, hgg = _in_proj(
        x, norm1, w_in, b_in, conv_w, conv_b, m_fgate_bias, hg_lb_logits)
    h_f, o_f, h_b, o_b = _mixer(q, k, v, gcol, grow, hq, g_f, g_b, hv)
    return [a.reshape(B * T, a.shape[-1]) for a in (h_f, h_b, o_f, o_b, mo, hgg, x)]


def kernel(x_prompt, x_sample, norm1, w_in, b_in, conv_w, conv_b, m_fgate_bias, m_norm, hg_lb_logits, hg_norm,
           w_out, norm2, w_router_group, b_router_group, w_router_expert, b_router_expert, w1, w3, w2, norm_f):
    layer = 0
    mixer_args = (norm1[layer], w_in[layer], b_in[layer], conv_w[layer], conv_b[layer], m_fgate_bias[layer],
                  hg_lb_logits)
    streams_p = _token_mixer(x_prompt, *mixer_args)
    streams_s = _token_mixer(x_sample, *mixer_args)
    x1, h2, route, hist = _merge(streams_p, streams_s, m_norm[layer], hg_norm[layer], w_out[layer], norm2[layer],
                                 w_router_group[layer], b_router_group[layer], w_router_expert[layer],
                                 b_router_expert[layer])
    dest, block_e, n_used, n_rows = _dispatch_plan(route, hist)
    xs = _sc_scatter_rows(h2, dest, n_rows)
    out_rows = _experts(xs, block_e, n_used, w1[layer], w3[layer], w2[layer])
    y_rows = _sc_gather_rows(out_rows, dest)
    y_p, y_s = _combine(x1, route, y_rows, norm_f, streams_p[0].shape[0])
    return (y_p.reshape(x_prompt.shape), y_s.reshape(x_sample.shape))
```

```python
import functools

import jax
import jax.numpy as jnp
from jax import lax
from jax.experimental import pallas as pl
from jax.experimental.pallas import tpu as pltpu
from jax.experimental.pallas import tpu_sc as plsc

F32 = jnp.float32
BF16 = jnp.bfloat16

D_MODEL = 1024
N_HEADS = 4
D_HEAD = 128
WIDTH = N_HEADS * D_HEAD
CONV_K = 5
CONV_PAD = CONV_K // 2
N_GROUPS = 4
EXPERTS_PER_GROUP = 8
N_EXPERTS = N_GROUPS * EXPERTS_PER_GROUP
TOP_K = 2
EXPERT_FF = D_MODEL // 2
NORM_EPS = 1e-6

LANES = 128
SUBLANES = 8
CHUNK = 128
MIXER_CHUNKS = 4
PROJ_ROWS = 256
COMBINE_ROWS = 512
MERGE_ROWS = 512
PLAN_TILES = 8
IN_PROJ_ROWS = 512
IN_PROJ_STREAM = 256
HALO = SUBLANES
EXPERT_ROWS = 512
N_GATES = 4 * N_HEADS
SC_CORES = 2
SC_SUBCORES = 16
SC_WINDOW_BYTES = 128 * 1024
VMEM_LIMIT = 56 * 1024 * 1024


def _dot(a, b):
    return jnp.dot(a, b, preferred_element_type=F32)


def _dot_nt(a, b):
    return lax.dot_general(a, b, (((1,), (1,)), ((), ())), preferred_element_type=F32)


def _dot_tn(a, b):
    return lax.dot_general(a, b, (((0,), (0,)), ((), ())), preferred_element_type=F32)


def _split3(x):
    hi = x.astype(BF16)
    r1 = x - hi.astype(F32)
    mid = r1.astype(BF16)
    lo = (r1 - mid.astype(F32)).astype(BF16)
    return hi, mid, lo


def _pack_bf16_pairs(x):
    half = x.shape[1] // 2
    bits = lax.bitcast_convert_type(x.astype(BF16).astype(F32), jnp.uint32)
    return (bits[:, half:] & jnp.uint32(0xFFFF0000)) | (bits[:, :half] >> 16)


def _unpack_bf16_pairs(words):
    lo = lax.bitcast_convert_type(words << 16, F32)
    hi = lax.bitcast_convert_type(words & jnp.uint32(0xFFFF0000), F32)
    return lo, hi


def _silu(x):
    return x * jax.nn.sigmoid(x)


def _log_sigmoid(x):
    return -(jnp.maximum(-x, 0.0) + jnp.log1p(jnp.exp(-jnp.abs(x))))


def _rms(x, gain):
    return x * lax.rsqrt(jnp.mean(x * x, axis=-1, keepdims=True) + NORM_EPS) * gain


def _run_round_robin(generators):
    live = list(generators)
    while live:
        for gen in list(live):
            try:
                next(gen)
            except StopIteration:
                live.remove(gen)


def _in_proj_kernel(x_ref, xp_ref, xn_ref, n1_ref, wa_ref, ba_ref, wg_ref, bg_ref, wgt_ref, bgt_ref,
                    fbrow_ref, fbcol_ref, wh_ref, bh_ref, cw_ref, cb_ref, lbl_ref,
                    q_ref, k_ref, v_ref, mo_ref, gcol_ref, grow_ref, hq_ref, gf_ref, gb_ref, hv_ref, hgg_ref,
                    ext_ref):
    t = pl.program_id(1)
    nt = pl.num_programs(1)
    rows = x_ref.shape[1]
    gain = n1_ref[...]

    lbl = lbl_ref[...]
    lmax = jnp.max(lbl, axis=0, keepdims=True)
    le = jnp.exp(lbl - lmax)
    lb = le[0:1, :] / jnp.sum(le, axis=0, keepdims=True)

    wqk = wa_ref[:, 0:2 * WIDTH]
    bqk = ba_ref[:, 0:2 * WIDTH]
    hp = _rms(xp_ref[0], gain).astype(BF16)
    hn = _rms(xn_ref[0], gain).astype(BF16)
    ext_ref[0:HALO, :] = (_dot(hp, wqk) + bqk) * (t > 0).astype(F32)
    ext_ref[HALO + rows:2 * HALO + rows, :] = (_dot(hn, wqk) + bqk) * (t < nt - 1).astype(F32)

    def stream(r0, n):
        rs = slice(r0, r0 + n)
        h = _rms(x_ref[0, rs, :], gain).astype(BF16)

        def proj(w_ref, b_ref, lo, hi):
            return _dot(h, w_ref[:, lo:hi]) + b_ref[:, lo:hi]

        ext_ref[HALO + r0:HALO + r0 + n, :] = proj(wa_ref, ba_ref, 0, 2 * WIDTH)
        hq_pre = proj(wh_ref, bh_ref, 0, WIDTH)
        yield
        acc = cb_ref[...] + ext_ref[pl.ds(HALO - CONV_PAD + r0, n), :] * cw_ref[0:1, :]
        for j in range(1, CONV_K):
            acc = acc + ext_ref[pl.ds(HALO - CONV_PAD + j + r0, n), :] * cw_ref[j:j + 1, :]
        qk = _silu(acc)
        q_ref[0, rs, :] = qk[:, 0:WIDTH] * (D_HEAD ** -0.5)
        k_ref[0, rs, :] = qk[:, WIDTH:2 * WIDTH]
        v_ref[0, rs, :] = proj(wa_ref, ba_ref, 2 * WIDTH, 3 * WIDTH)
        yield
        hq_ref[0, rs, :] = _silu(hq_pre)
        mo_ref[0, rs, :] = jax.nn.sigmoid(proj(wa_ref, ba_ref, 3 * WIDTH, 4 * WIDTH))
        yield
        gf_ref[0, rs, :] = lb + (1.0 - lb) * jax.nn.sigmoid(proj(wh_ref, bh_ref, WIDTH, 2 * WIDTH))
        yield
        gb_ref[0, rs, :] = lb + (1.0 - lb) * jax.nn.sigmoid(proj(wh_ref, bh_ref, 2 * WIDTH, 3 * WIDTH))
        hv_ref[0, rs, :] = proj(wh_ref, bh_ref, 3 * WIDTH, 4 * WIDTH)
        yield
        hgg_ref[0, rs, :] = _silu(proj(wh_ref, bh_ref, 4 * WIDTH, 5 * WIDTH))
        gc = _dot(h, wg_ref[...]) + bg_ref[...]
        lane = lax.broadcasted_iota(jnp.int32, gc.shape, 1)
        is_f = (lane >= 2 * N_HEADS) & (lane < N_GATES)
        gcol_ref[0, rs, :] = jnp.where(is_f, _log_sigmoid(gc + fbrow_ref[...]), gc)
        gr = _dot_nt(wgt_ref[...], h) + bgt_ref[...]
        sub = lax.broadcasted_iota(jnp.int32, gr.shape, 0)
        grow_ref[0, :, rs] = jnp.where(sub >= 2 * N_HEADS, _log_sigmoid(gr + fbcol_ref[...]), gr)

    _run_round_robin([stream(r0, IN_PROJ_STREAM) for r0 in range(0, rows, IN_PROJ_STREAM)])


def _in_proj(x, norm1, w_in, b_in, conv_w, conv_b, fgate_bias, lb_logits):
    B, T, D = x.shape
    rows = IN_PROJ_ROWS
    assert T % rows == 0, (T, rows)
    nt = T // rows
    a_w = 4 * WIDTH
    wa = w_in[:, 0:a_w].astype(BF16)
    ba = b_in[None, 0:a_w]
    wg32 = jnp.pad(w_in[:, a_w:a_w + N_GATES], ((0, 0), (0, LANES - N_GATES)))
    bg = jnp.pad(b_in[a_w:a_w + N_GATES], (0, LANES - N_GATES))[None, :]
    wg = wg32.astype(BF16)
    wgt = w_in[:, a_w:a_w + N_GATES].T.astype(BF16)
    bgt = b_in[a_w:a_w + N_GATES][:, None]
    fb = fgate_bias.reshape(2 * N_HEADS)
    fbrow = jnp.zeros((1, LANES), F32).at[0, 2 * N_HEADS:N_GATES].set(fb)
    fbcol = jnp.zeros((N_GATES, 1), F32).at[2 * N_HEADS:N_GATES, 0].set(fb)
    wh = w_in[:, a_w + N_GATES:].astype(BF16)
    bh = b_in[None, a_w + N_GATES:]

    tiles_per_halo = rows // HALO
    n_halo = T // HALO

    def full(arr):
        nd = arr.ndim
        return pl.BlockSpec(arr.shape, lambda b, t: (0,) * nd)

    def tok(width):
        return pl.BlockSpec((1, rows, width), lambda b, t: (b, t, 0))

    in_specs = [
        tok(D),
        pl.BlockSpec((1, HALO, D), lambda b, t: (b, jnp.maximum(t * tiles_per_halo - 1, 0), 0)),
        pl.BlockSpec((1, HALO, D), lambda b, t: (b, jnp.minimum((t + 1) * tiles_per_halo, n_halo - 1), 0)),
    ]
    consts = [norm1[None, :], wa, ba, wg, bg, wgt, bgt, fbrow, fbcol, wh, bh, conv_w, conv_b[None, :], lb_logits]
    in_specs += [full(c) for c in consts]
    tok_out = jax.ShapeDtypeStruct((B, T, WIDTH), F32)
    out_shape = [tok_out, tok_out, tok_out, tok_out,
                 jax.ShapeDtypeStruct((B, T, LANES), F32),
                 jax.ShapeDtypeStruct((B, N_GATES, T), F32),
                 tok_out, tok_out, tok_out, tok_out, tok_out]
    out_specs = [tok(WIDTH)] * 4 + [tok(LANES), pl.BlockSpec((1, N_GATES, rows), lambda b, t: (b, 0, t))] + [tok(WIDTH)] * 5
    return pl.pallas_call(
        _in_proj_kernel,
        grid=(B, nt),
        in_specs=in_specs,
        out_specs=out_specs,
        out_shape=out_shape,
        scratch_shapes=[pltpu.VMEM((rows + 2 * HALO, 2 * WIDTH), F32)],
        compiler_params=pltpu.CompilerParams(
            dimension_semantics=("parallel", "parallel"), vmem_limit_bytes=VMEM_LIMIT),
        name="in_proj",
    )(x, x, x, *consts)


def _cumsum_rows(tri_bf, x):
    hi, mid, lo = _split3(x)
    return _dot(tri_bf, hi) + _dot(tri_bf, mid) + _dot(tri_bf, lo)


def _cumsum_lanes(x, tri_bf):
    hi, mid, lo = _split3(x)
    return _dot(hi, tri_bf) + _dot(mid, tri_bf) + _dot(lo, tri_bf)


def _mlstm_chunk(q, k, vext, i_col, b_col, i_row, b_row, seen, last, c_ref, m_ref, out_ref, rs, sl):
    m_prev = m_ref[:, 0:1]
    c_prev = c_ref[...]
    q_bf = q.astype(BF16)
    log_d = jnp.where(seen, b_col - b_row + i_row, -jnp.inf)
    m_inter = b_col + m_prev
    m_t = jnp.maximum(m_inter, jnp.max(log_d, axis=-1, keepdims=True))
    qk = _dot_nt(q_bf, k.astype(BF16))
    yield
    scores = (qk * jnp.exp(log_d - m_t)).astype(BF16)
    inter_scale = jnp.exp(m_inter - m_t)
    b_last = b_col[last:last + 1, :]
    log_w = b_last - b_col + i_col
    m_new = jnp.maximum(b_last + m_prev, jnp.max(log_w, axis=0, keepdims=True))
    w = jnp.exp(log_w - m_new)
    decay = jnp.exp(b_last + m_prev - m_new)
    kw = (k * w).astype(BF16)
    yield
    numden = _dot(scores, vext) + inter_scale * _dot(q_bf, c_prev.astype(BF16))
    update = _dot_tn(kw, vext)
    yield
    num = numden[:, 0:D_HEAD]
    den = numden[:, D_HEAD:2 * D_HEAD]
    out_ref[0, rs, sl] = num / jnp.maximum(jnp.abs(den), jnp.exp(-m_t))
    c_ref[...] = decay * c_prev + update
    m_ref[...] = jnp.broadcast_to(m_new, (1, LANES))


def _hgrn2_level_small(q3, k3, pre3, suf3, half, rev, sub_iota):
    upper = (sub_iota & half) != 0
    second = jnp.logical_not(upper) if rev else upper
    end = 0 if rev else half - 1
    y = jnp.where((sub_iota & (half - 1)) == end, pre3, 0.0)
    step = 1 if rev else -1
    span = 1
    while span < half:
        y = y + pltpu.roll(y, (step * span) % SUBLANES, 1)
        span *= 2
    if 2 * half == SUBLANES:
        other = pltpu.roll(y, half, 1)
    else:
        other = jnp.where(upper, pltpu.roll(y, half, 1), pltpu.roll(y, SUBLANES - half, 1))
    z = jnp.where(second, q3 * pre3, k3 * suf3)
    return z, pre3 * jnp.where(second, other, 1.0), suf3 * jnp.where(second, 1.0, other)


def _hgrn2_level_big(q, k, pre, suf, half, rev):
    L, width = q.shape
    shape = (L // (2 * half), 2, half, width)
    q4, k4, pre4, suf4 = (a.reshape(shape) for a in (q, k, pre, suf))
    first = 1 if rev else 0
    second = 1 - first
    end = 0 if rev else half - 1
    total_first = pre4[:, first, end:end + 1, :]
    total_second = pre4[:, second, end:end + 1, :]

    def join(at_first, at_second):
        parts = (at_second, at_first) if rev else (at_first, at_second)
        return jnp.stack(parts, axis=1).reshape(L, width)

    z = join(k4[:, first] * suf4[:, first], q4[:, second] * pre4[:, second])
    pre_new = join(pre4[:, first], pre4[:, second] * total_first)
    suf_new = join(suf4[:, first] * total_second, suf4[:, second])
    return z, pre_new, suf_new


def _hgrn2_chunk(q, g, v_bf, rev, level, diag, sub_iota, st_refs, out_ref, rs, sl):
    L, width = q.shape
    heads = [slice(h * D_HEAD, (h + 1) * D_HEAD) for h in range(width // D_HEAD)]
    k = 1.0 - g
    q_bf = q.astype(BF16)
    k_bf = k.astype(BF16)
    att = [jnp.where(diag, _dot_nt(q_bf[:, s], k_bf[:, s]), 0.0) for s in heads]
    small = (L // SUBLANES, SUBLANES, width)
    q3, k3, pre, suf = q.reshape(small), k.reshape(small), g.reshape(small), jnp.ones(small, F32)
    half = 1
    bit = 0
    while half < L:
        if half == SUBLANES:
            pre, suf = pre.reshape(L, width), suf.reshape(L, width)
        if half < SUBLANES:
            z, pre, suf = _hgrn2_level_small(q3, k3, pre, suf, half, rev, sub_iota)
            z = z.reshape(L, width)
        else:
            z, pre, suf = _hgrn2_level_big(q, k, pre, suf, half, rev)
        z = z.astype(BF16)
        att = [jnp.where(level == bit, _dot_nt(z[:, s], z[:, s]), a) for a, s in zip(att, heads)]
        half *= 2
        bit += 1
        yield
    last = 0 if rev else L - 1
    q_dec = (q * pre).astype(BF16)
    k_dec = (k * suf).astype(BF16)
    outs = []
    for h, s in enumerate(heads):
        st_prev = st_refs[h][...]
        outs.append(_dot_nt(q_dec[:, s], st_prev.astype(BF16)) + _dot(att[h].astype(BF16), v_bf[:, s]))
        st_refs[h][...] = st_prev * pre[last:last + 1, s] + _dot_tn(v_bf[:, s], k_dec[:, s])
    out_ref[0, rs, sl] = jnp.concatenate(outs, axis=1)


def _mixer_kernel(qf_ref, kf_ref, vf_ref, gcf_ref, grf_ref, hqf_ref, hgf_ref, hvf_ref,
                  qb_ref, kb_ref, vb_ref, gcb_ref, grb_ref, hqb_ref, hgb_ref, hvb_ref,
                  hf_ref, of_ref, hb_ref, ob_ref, *state_refs):
    L = CHUNK
    n_state = 2 * N_HEADS
    c_refs, m_refs, st_refs = (state_refs[i * n_state:(i + 1) * n_state] for i in range(3))

    @pl.when(pl.program_id(1) == 0)
    def _():
        for ref in state_refs:
            ref[...] = jnp.zeros_like(ref)

    row = lax.broadcasted_iota(jnp.int32, (L, L), 0)
    col = lax.broadcasted_iota(jnp.int32, (L, L), 1)
    sub_iota = lax.broadcasted_iota(jnp.int32, (L // SUBLANES, SUBLANES, LANES), 1)
    diag = row == col
    diff = row ^ col
    high_bit = jnp.zeros((L, L), jnp.int32)
    half = 2
    while half < L:
        high_bit = high_bit + (diff >= half).astype(jnp.int32)
        half *= 2
    ones = jnp.ones((L, D_HEAD), BF16)

    dirs = (
        (0, qf_ref, kf_ref, vf_ref, gcf_ref, grf_ref, hqf_ref, hgf_ref, hvf_ref, hf_ref, of_ref),
        (1, qb_ref, kb_ref, vb_ref, gcb_ref, grb_ref, hqb_ref, hgb_ref, hvb_ref, hb_ref, ob_ref),
    )
    masks = []
    for rev in (False, True):
        seen = (col >= row) if rev else (col <= row)
        before = (col > row) if rev else (col < row)
        level = jnp.where(before, high_bit, -1)
        tri = seen.astype(BF16)
        tri_t = (row >= col if rev else row <= col).astype(BF16)
        masks.append((seen, level, tri, tri_t))
    n_sub = qf_ref.shape[1] // L
    for step in range(n_sub):
        stages = []
        for d, q_ref, k_ref, v_ref, gc_ref, gr_ref, hq_ref, hg_ref, hv_ref, h_out, o_out in dirs:
            rev = d == 1
            seen, level, tri, tri_t = masks[d]
            last = 0 if rev else L - 1
            sub_chunk = n_sub - 1 - step if rev else step
            rs = slice(sub_chunk * L, (sub_chunk + 1) * L)
            gc = gc_ref[0, rs, :]
            gr = gr_ref[0, :, rs]
            gc_cum = _cumsum_rows(tri, gc)
            gr_cum = _cumsum_lanes(gr, tri_t)
            for hd in range(N_HEADS):
                sl = slice(hd * D_HEAD, (hd + 1) * D_HEAD)
                gi = d * N_HEADS + hd
                gf = 2 * N_HEADS + gi
                idx = d * N_HEADS + hd
                vext = jnp.concatenate([v_ref[0, rs, sl].astype(BF16), ones], axis=1)
                stages.append(_mlstm_chunk(
                    q_ref[0, rs, sl], k_ref[0, rs, sl], vext,
                    gc[:, gi:gi + 1], gc_cum[:, gf:gf + 1], gr[gi:gi + 1, :], gr_cum[gf:gf + 1, :],
                    seen, last, c_refs[idx], m_refs[idx], h_out, rs, sl))
                stages.append(_hgrn2_chunk(
                    hq_ref[0, rs, sl], hg_ref[0, rs, sl], hv_ref[0, rs, sl].astype(BF16),
                    rev, level, diag, sub_iota, [st_refs[idx]], o_out, rs, sl))
        group = 2 * N_HEADS
        for start in range(0, len(stages), group):
            _run_round_robin(stages[start:start + group])


def _mixer(q, k, v, gcol, grow, hq, g_f, g_b, hv):
    B, T, _ = q.shape
    L = CHUNK * MIXER_CHUNKS
    nc = T // L

    def fwd(width):
        return pl.BlockSpec((1, L, width), lambda b, c: (b, c, 0))

    def bwd(width):
        return pl.BlockSpec((1, L, width), lambda b, c: (b, nc - 1 - c, 0))

    grow_f = pl.BlockSpec((1, N_GATES, L), lambda b, c: (b, 0, c))
    grow_b = pl.BlockSpec((1, N_GATES, L), lambda b, c: (b, 0, nc - 1 - c))
    in_specs = ([fwd(WIDTH)] * 3 + [fwd(LANES), grow_f] + [fwd(WIDTH)] * 3
                + [bwd(WIDTH)] * 3 + [bwd(LANES), grow_b] + [bwd(WIDTH)] * 3)
    out = jax.ShapeDtypeStruct((B, T, WIDTH), F32)
    n_state = 2 * N_HEADS
    return pl.pallas_call(
        _mixer_kernel,
        grid=(B, nc),
        in_specs=in_specs,
        out_specs=[fwd(WIDTH), fwd(WIDTH), bwd(WIDTH), bwd(WIDTH)],
        out_shape=[out, out, out, out],
        scratch_shapes=([pltpu.VMEM((D_HEAD, 2 * D_HEAD), F32)] * n_state
                        + [pltpu.VMEM((1, LANES), F32)] * n_state
                        + [pltpu.VMEM((D_HEAD, D_HEAD), F32)] * n_state),
        compiler_params=pltpu.CompilerParams(
            dimension_semantics=("parallel", "arbitrary"), vmem_limit_bytes=VMEM_LIMIT),
        name="mixer",
    )(q, k, v, gcol, grow, hq, g_f, hv, q, k, v, gcol, grow, hq, g_b, hv)


def _head_norm(hsum, gain):
    parts = []
    for hd in range(N_HEADS):
        hh = hsum[:, hd * D_HEAD:(hd + 1) * D_HEAD]
        parts.append(hh * lax.rsqrt(jnp.mean(hh * hh, axis=-1, keepdims=True) + NORM_EPS))
    return jnp.concatenate(parts, axis=1) * gain


def _merge_tile(r0, hf_ref, hb_ref, of_ref, ob_ref, mo_ref, hgg_ref, x_ref, mn_ref, hn_ref, wo_ref,
                n2_ref, wrh_ref, wrl_ref, br_ref, x1_ref, h2_ref, route_ref, hist_ref):
    rs = slice(r0, r0 + PROJ_ROWS)
    m_out = _head_norm(hf_ref[rs, :] + hb_ref[rs, :], mn_ref[...]) * mo_ref[rs, :]
    hg_out = _head_norm(of_ref[rs, :] + ob_ref[rs, :], hn_ref[...]) * hgg_ref[rs, :]
    mixed = jnp.concatenate([m_out, hg_out], axis=1).astype(BF16)
    yield
    x1 = x_ref[rs, :] + _dot(mixed, wo_ref[...])
    x1_ref[rs, :] = x1
    h2 = _rms(x1, n2_ref[...])
    h2_ref[rs, :] = _pack_bf16_pairs(h2)
    h_hi = h2.astype(BF16)
    h_hi32 = h_hi.astype(F32)
    yield

    h_lo = (h2 - h_hi32).astype(BF16)
    logits = _dot(h_hi, wrh_ref[...]) + _dot(h_lo, wrh_ref[...]) + _dot(h_hi, wrl_ref[...]) + br_ref[...]
    lane = lax.broadcasted_iota(jnp.int32, logits.shape, 1)
    big = jnp.int32(LANES)
    neg = -jnp.inf
    yield
    g_log = jnp.where(lane < N_GROUPS, logits, neg)
    g_max = jnp.max(g_log, axis=-1, keepdims=True)
    g_idx = jnp.min(jnp.where(g_log == g_max, lane, big), axis=-1, keepdims=True)
    g_val = 1.0 / jnp.sum(jnp.exp(g_log - g_max), axis=-1, keepdims=True)
    yield
    e_lo = N_GROUPS + g_idx * EXPERTS_PER_GROUP
    e_log = jnp.where((lane >= e_lo) & (lane < e_lo + EXPERTS_PER_GROUP), logits, neg)
    m1 = jnp.max(e_log, axis=-1, keepdims=True)
    i1 = jnp.min(jnp.where(e_log == m1, lane, big), axis=-1, keepdims=True)
    yield
    e_log2 = jnp.where(lane == i1, neg, e_log)
    m2 = jnp.max(e_log2, axis=-1, keepdims=True)
    i2 = jnp.min(jnp.where(e_log2 == m2, lane, big), axis=-1, keepdims=True)
    r2 = jnp.exp(m2 - m1)
    w1 = g_val / (1.0 + r2)
    w2 = g_val * r2 / (1.0 + r2)
    yield
    rows = logits.shape[0]
    pick0 = lane == i1 - N_GROUPS
    pick1 = lane == i2 - N_GROUPS
    earlier = (lax.broadcasted_iota(jnp.int32, (rows, rows), 1)
               < lax.broadcasted_iota(jnp.int32, (rows, rows), 0)).astype(BF16)
    cnt0 = jnp.sum(pick0.astype(F32), axis=0, keepdims=True)
    cnt1 = jnp.sum(pick1.astype(F32), axis=0, keepdims=True)
    rank0 = jnp.sum(jnp.where(pick0, _dot(earlier, pick0.astype(BF16)), 0.0), axis=-1, keepdims=True)
    rank1 = jnp.sum(jnp.where(pick1, _dot(earlier, pick1.astype(BF16)) + cnt0, 0.0), axis=-1, keepdims=True)
    yield
    columns = ((i1 - N_GROUPS).astype(F32), (i2 - N_GROUPS).astype(F32), w1, w2, rank0, rank1)
    route = jnp.zeros_like(logits)
    for c, value in enumerate(columns):
        route = jnp.where(lane == c, value, route)
    route_ref[rs, :] = route
    hs = slice(r0 // PROJ_ROWS * SUBLANES, (r0 // PROJ_ROWS + 1) * SUBLANES)
    sub = lax.broadcasted_iota(jnp.int32, (SUBLANES, LANES), 0)
    hist_ref[hs, :] = jnp.where(sub == 0, cnt0 + cnt1, 0.0)


N_MERGE_STREAMS = 7


def _merge_kernel(*refs, n_a):
    side_a = refs[0:N_MERGE_STREAMS]
    side_b = refs[N_MERGE_STREAMS:2 * N_MERGE_STREAMS]
    rest = refs[2 * N_MERGE_STREAMS:]

    def block(side):
        _run_round_robin([_merge_tile(r0, *side, *rest) for r0 in range(0, MERGE_ROWS, PROJ_ROWS)])

    @pl.when(pl.program_id(0) < n_a)
    def _():
        block(side_a)

    @pl.when(pl.program_id(0) >= n_a)
    def _():
        block(side_b)


def _merge(streams_a, streams_b, m_norm, hg_norm, w_out, norm2, w_rg, b_rg, w_re, b_re):
    D = D_MODEL
    rows = MERGE_ROWS
    n_a = streams_a[0].shape[0] // rows
    n_b = streams_b[0].shape[0] // rows
    n_all = (n_a + n_b) * rows
    n_log = N_GROUPS + N_EXPERTS
    wr = jnp.pad(jnp.concatenate([w_rg, w_re], axis=1), ((0, 0), (0, LANES - n_log)))
    br = jnp.pad(jnp.concatenate([b_rg, b_re]), (0, LANES - n_log))[None, :]
    wr_hi = wr.astype(BF16)
    wr_lo = (wr - wr_hi.astype(F32)).astype(BF16)
    consts = [m_norm[None, :], hg_norm[None, :], w_out.astype(BF16), norm2[None, :], wr_hi, wr_lo, br]

    def full(arr):
        nd = arr.ndim
        return pl.BlockSpec(arr.shape, lambda i: (0,) * nd)

    def side_a(arr):
        return pl.BlockSpec((rows, arr.shape[1]), lambda i: (jnp.minimum(i, n_a - 1), 0))

    def side_b(arr):
        return pl.BlockSpec((rows, arr.shape[1]), lambda i: (jnp.maximum(i - n_a, 0), 0))

    def out(width):
        return pl.BlockSpec((rows, width), lambda i: (i, 0))

    return pl.pallas_call(
        functools.partial(_merge_kernel, n_a=n_a),
        grid=(n_a + n_b,),
        in_specs=[side_a(s) for s in streams_a] + [side_b(s) for s in streams_b] + [full(c) for c in consts],
        out_specs=[out(D), out(D // 2), out(LANES), pl.BlockSpec((rows // PROJ_ROWS * SUBLANES, LANES), lambda i: (i, 0))],
        out_shape=[jax.ShapeDtypeStruct((n_all, D), F32), jax.ShapeDtypeStruct((n_all, D // 2), jnp.uint32),
                   jax.ShapeDtypeStruct((n_all, LANES), F32),
                   jax.ShapeDtypeStruct((n_all // PROJ_ROWS * SUBLANES, LANES), F32)],
        compiler_params=pltpu.CompilerParams(
            dimension_semantics=("arbitrary",), vmem_limit_bytes=VMEM_LIMIT),
        name="merge",
    )(*streams_a, *streams_b, *consts)


def _sc_row_mover(src, idx, n_out, scatter, name):
    n_moved = idx.shape[0]
    D = src.shape[1]
    n_sub = SC_CORES * SC_SUBCORES
    per = n_moved // n_sub
    window = SC_WINDOW_BYTES // (D * src.dtype.itemsize)
    assert per * n_sub == n_moved and per % window == 0, (n_moved, per, window)
    assert not scatter or src.shape[0] % per == 0, (src.shape, per)
    mesh = plsc.VectorSubcoreMesh(core_axis_name="c", subcore_axis_name="s",
                                  num_cores=SC_CORES, num_subcores=SC_SUBCORES)

    def body(src_hbm, idx_hbm, out_hbm, idx_v, buf):
        base = (lax.axis_index("c") * SC_SUBCORES + lax.axis_index("s")) * per
        pltpu.sync_copy(idx_hbm.at[pl.ds(base, per)], idx_v)

        @pl.loop(0, per // window)
        def _(j):
            linear = pl.ds(base + j * window, window)
            indexed = idx_v.at[pl.ds(j * window, window)]
            if scatter:
                pltpu.sync_copy(src_hbm.at[pl.ds(lax.rem(base, src.shape[0]) + j * window, window)], buf)
                pltpu.sync_copy(buf, out_hbm.at[indexed])
            else:
                pltpu.sync_copy(src_hbm.at[indexed], buf)
                pltpu.sync_copy(buf, out_hbm.at[linear])

    return pl.kernel(
        body,
        out_type=jax.ShapeDtypeStruct((n_out, D), src.dtype),
        mesh=mesh,
        scratch_types=[pltpu.VMEM((per,), jnp.int32), pltpu.VMEM((window, D), src.dtype)],
        name=name,
    )(src, idx)


def _sc_gather_rows(src, idx):
    return _sc_row_mover(src, idx, idx.shape[0], False, "sc_gather_rows")


def _sc_scatter_rows(src, idx, n_out):
    return _sc_row_mover(src, idx, n_out, True, "sc_scatter_rows")


def _expert_kernel(be_ref, nu_ref, x_ref, w1_ref, w3_ref, w2_ref, o_ref, w1_bf, w3_bf, w2_bf):
    i = pl.program_id(0)
    active = i < nu_ref[0]
    new_expert = (i == 0) | (be_ref[i] != be_ref[jnp.maximum(i - 1, 0)])

    @pl.when(active & new_expert)
    def _():
        w1_bf[...] = w1_ref[0].astype(BF16)
        w3_bf[...] = w3_ref[0].astype(BF16)
        w2_bf[...] = w2_ref[0].astype(BF16)

    @pl.when(active)
    def _():
        half = D_MODEL // 2
        x_lo, x_hi = (part.astype(BF16) for part in _unpack_bf16_pairs(x_ref[...]))
        a = _dot(x_lo, w1_bf[0:half, :]) + _dot(x_hi, w1_bf[half:, :])
        b = _dot(x_lo, w3_bf[0:half, :]) + _dot(x_hi, w3_bf[half:, :])
        o_ref[...] = _pack_bf16_pairs(_dot((_silu(a) * b).astype(BF16), w2_bf[...]))


def _experts(xs, block_e, n_used, w1, w3, w2):
    rows = EXPERT_ROWS
    n_blocks = xs.shape[0] // rows
    D = D_MODEL

    def blk(i, be, nu):
        return jnp.minimum(i, nu[0] - 1)

    grid_spec = pltpu.PrefetchScalarGridSpec(
        num_scalar_prefetch=2,
        grid=(n_blocks,),
        in_specs=[
            pl.BlockSpec((rows, D // 2), lambda i, be, nu: (blk(i, be, nu), 0)),
            pl.BlockSpec((1, D, EXPERT_FF), lambda i, be, nu: (be[blk(i, be, nu)], 0, 0)),
            pl.BlockSpec((1, D, EXPERT_FF), lambda i, be, nu: (be[blk(i, be, nu)], 0, 0)),
            pl.BlockSpec((1, EXPERT_FF, D), lambda i, be, nu: (be[blk(i, be, nu)], 0, 0)),
        ],
        out_specs=pl.BlockSpec((rows, D // 2), lambda i, be, nu: (blk(i, be, nu), 0)),
        scratch_shapes=[pltpu.VMEM((D, EXPERT_FF), BF16), pltpu.VMEM((D, EXPERT_FF), BF16),
                        pltpu.VMEM((EXPERT_FF, D), BF16)],
    )
    return pl.pallas_call(
        _expert_kernel,
        grid_spec=grid_spec,
        out_shape=jax.ShapeDtypeStruct((xs.shape[0], D // 2), jnp.uint32),
        compiler_params=pltpu.CompilerParams(
            dimension_semantics=("arbitrary",), vmem_limit_bytes=VMEM_LIMIT),
        name="experts",
    )(block_e, n_used, xs, w1, w3, w2)


def _combine_kernel(y0_ref, y1_ref, x1_ref, route_ref, nf_ref, ya_ref, yb_ref, *, n_a):
    route = route_ref[...]
    r0 = jnp.concatenate(_unpack_bf16_pairs(y0_ref[...]), axis=1)
    r1 = jnp.concatenate(_unpack_bf16_pairs(y1_ref[...]), axis=1)
    y = _rms(x1_ref[...] + route[:, 2:3] * r0 + route[:, 3:4] * r1, nf_ref[...])

    @pl.when(pl.program_id(0) < n_a)
    def _():
        ya_ref[...] = y

    @pl.when(pl.program_id(0) >= n_a)
    def _():
        yb_ref[...] = y


def _combine(x1, route, y_rows, norm_f, n_tok_a):
    N, D = x1.shape
    rows = COMBINE_ROWS
    assert N % rows == 0 and n_tok_a % rows == 0, (N, n_tok_a, rows)
    nt = N // rows
    n_a = n_tok_a // rows

    def tok(width, offset=0):
        return pl.BlockSpec((rows, width), lambda i: (i + offset, 0))

    return pl.pallas_call(
        functools.partial(_combine_kernel, n_a=n_a),
        grid=(nt,),
        in_specs=[tok(D // 2), tok(D // 2, nt), tok(D), tok(LANES), pl.BlockSpec((1, D), lambda i: (0, 0))],
        out_specs=[pl.BlockSpec((rows, D), lambda i: (jnp.minimum(i, n_a - 1), 0)),
                   pl.BlockSpec((rows, D), lambda i: (jnp.maximum(i - n_a, 0), 0))],
        out_shape=[jax.ShapeDtypeStruct((n_tok_a, D), F32), jax.ShapeDtypeStruct((N - n_tok_a, D), F32)],
        compiler_params=pltpu.CompilerParams(
            dimension_semantics=("arbitrary",), vmem_limit_bytes=VMEM_LIMIT),
        name="combine",
    )(y_rows, y_rows, x1, route, norm_f[None, :])


def _plan_kernel(route_ref, table_ref, dest_ref):
    lane = lax.broadcasted_iota(jnp.int32, (PROJ_ROWS, LANES), 1)
    lane_f = lane.astype(F32)
    for tile in range(route_ref.shape[0] // PROJ_ROWS):
        rs = slice(tile * PROJ_ROWS, (tile + 1) * PROJ_ROWS)
        route = route_ref[rs, :]
        first = table_ref[tile * SUBLANES:tile * SUBLANES + 1, :]
        d0 = jnp.sum(jnp.where(lane_f == route[:, 0:1], first, 0.0), axis=-1, keepdims=True) + route[:, 4:5]
        d1 = jnp.sum(jnp.where(lane_f == route[:, 1:2], first, 0.0), axis=-1, keepdims=True) + route[:, 5:6]
        dest_ref[rs, :] = jnp.where(lane == 0, d0, jnp.where(lane == 1, d1, 0.0)).astype(jnp.int32)


def _dispatch_plan(route, hist):
    N = route.shape[0]
    rows = PROJ_ROWS
    n_tiles = N // rows
    blk = EXPERT_ROWS
    tile_counts = hist.reshape(n_tiles, SUBLANES, LANES)[:, 0, 0:N_EXPERTS].astype(jnp.int32)
    tile_first = jnp.cumsum(tile_counts, axis=0) - tile_counts
    counts = jnp.sum(tile_counts, axis=0)
    padded = ((counts + blk - 1) // blk) * blk
    pad_end = jnp.cumsum(padded)
    pad_start = pad_end - padded
    table = jnp.zeros((n_tiles, SUBLANES, LANES), F32).at[:, 0, 0:N_EXPERTS].set(
        (pad_start[None, :] + tile_first).astype(F32)).reshape(n_tiles * SUBLANES, LANES)
    per_step = PLAN_TILES
    assert n_tiles % per_step == 0, (n_tiles, per_step)
    dest_cols = pl.pallas_call(
        _plan_kernel,
        grid=(n_tiles // per_step,),
        in_specs=[pl.BlockSpec((per_step * rows, LANES), lambda i: (i, 0)),
                  pl.BlockSpec((per_step * SUBLANES, LANES), lambda i: (i, 0))],
        out_specs=pl.BlockSpec((per_step * rows, LANES), lambda i: (i, 0)),
        out_shape=jax.ShapeDtypeStruct((N, LANES), jnp.int32),
        compiler_params=pltpu.CompilerParams(dimension_semantics=("parallel",)),
        name="plan",
    )(route, table)
    dest = dest_cols[:, 0:TOP_K].T.reshape(TOP_K * N)
    n_blocks = (TOP_K * N + N_EXPERTS * (blk - 1) + blk - 1) // blk
    block_start = jnp.arange(n_blocks, dtype=jnp.int32) * blk
    block_e = jnp.sum((pad_end[None, :] <= block_start[:, None]).astype(jnp.int32), axis=1)
    block_e = jnp.minimum(block_e, N_EXPERTS - 1)
    n_used = (pad_end[-1] // blk).astype(jnp.int32).reshape(1)
    return dest, block_e, n_used, n_blocks * blk


def _token_mixer(x, norm1, w_in, b_in, conv_w, conv_b, m_fgate_bias, hg_lb_logits):
    B, T, D = x.shape
    q, k, v, mo, gcol, grow, hq, g_f, g_b, hv, hgg = _in_proj(
        x, norm1, w_in, b_in, conv_w, conv_b, m_fgate_bias, hg_lb_logits)
    h_f, o_f, h_b, o_b = _mixer(q, k, v, gcol, grow, hq, g_f, g_b, hv)
    return [a.reshape(B * T, a.shape[-1]) for a in (h_f, h_b, o_f, o_b, mo, hgg, x)]


def kernel(x_prompt, x_sample, norm1, w_in, b_in, conv_w, conv_b, m_fgate_bias, m_norm, hg_lb_logits, hg_norm,
           w_out, norm2, w_router_group, b_router_group, w_router_expert, b_router_expert, w1, w3, w2, norm_f):
    layer = 0
    mixer_args = (norm1[layer], w_in[layer], b_in[layer], conv_w[layer], conv_b[layer], m_fgate_bias[layer],
                  hg_lb_logits)
    streams_p = _token_mixer(x_prompt, *mixer_args)
    streams_s = _token_mixer(x_sample, *mixer_args)
    x1, h2, route, hist = _merge(streams_p, streams_s, m_norm[layer], hg_norm[layer], w_out[layer], norm2[layer],
                                 w_router_group[layer], b_router_group[layer], w_router_expert[layer],
                                 b_router_expert[layer])
    dest, block_e, n_used, n_rows = _dispatch_plan(route, hist)
    xs = _sc_scatter_rows(h2, dest, n_rows)
    out_rows = _experts(xs, block_e, n_used, w1[layer], w3[layer], w2[layer])
    y_rows = _sc_gather_rows(out_rows, dest)
    y_p, y_s = _combine(x1, route, y_rows, norm_f, streams_p[0].shape[0])
    return (y_p.reshape(x_prompt.shape), y_s.reshape(x_sample.shape))
```

```python
import functools

import jax
import jax.numpy as jnp
from jax import lax
from jax.experimental import pallas as pl
from jax.experimental.pallas import tpu as pltpu
from jax.experimental.pallas import tpu_sc as plsc

F32 = jnp.float32
BF16 = jnp.bfloat16

D_MODEL = 1024
N_HEADS = 4
D_HEAD = 128
WIDTH = N_HEADS * D_HEAD
CONV_K = 5
CONV_PAD = CONV_K // 2
N_GROUPS = 4
EXPERTS_PER_GROUP = 8
N_EXPERTS = N_GROUPS * EXPERTS_PER_GROUP
TOP_K = 2
EXPERT_FF = D_MODEL // 2
NORM_EPS = 1e-6

LANES = 128
SUBLANES = 8
CHUNK = 128
MIXER_CHUNKS = 2
PROJ_ROWS = 256
COMBINE_ROWS = 512
MERGE_ROWS = 512
PLAN_TILES = 8
IN_PROJ_ROWS = 512
IN_PROJ_STREAM = 256
HALO = SUBLANES
EXPERT_ROWS = 512
N_GATES = 4 * N_HEADS
SC_CORES = 2
SC_SUBCORES = 16
SC_WINDOW_BYTES = 128 * 1024
VMEM_LIMIT = 56 * 1024 * 1024


def _dot(a, b):
    return jnp.dot(a, b, preferred_element_type=F32)


def _dot_nt(a, b):
    return lax.dot_general(a, b, (((1,), (1,)), ((), ())), preferred_element_type=F32)


def _dot_tn(a, b):
    return lax.dot_general(a, b, (((0,), (0,)), ((), ())), preferred_element_type=F32)


def _split3(x):
    hi = x.astype(BF16)
    r1 = x - hi.astype(F32)
    mid = r1.astype(BF16)
    lo = (r1 - mid.astype(F32)).astype(BF16)
    return hi, mid, lo


def _pack_bf16_pairs(x):
    half = x.shape[1] // 2
    bits = lax.bitcast_convert_type(x.astype(BF16).astype(F32), jnp.uint32)
    return (bits[:, half:] & jnp.uint32(0xFFFF0000)) | (bits[:, :half] >> 16)


def _unpack_bf16_pairs(words):
    lo = lax.bitcast_convert_type(words << 16, F32)
    hi = lax.bitcast_convert_type(words & jnp.uint32(0xFFFF0000), F32)
    return lo, hi


def _silu(x):
    return x * jax.nn.sigmoid(x)


def _log_sigmoid(x):
    return -(jnp.maximum(-x, 0.0) + jnp.log1p(jnp.exp(-jnp.abs(x))))


def _rms(x, gain):
    return x * lax.rsqrt(jnp.mean(x * x, axis=-1, keepdims=True) + NORM_EPS) * gain


def _run_round_robin(generators):
    live = list(generators)
    while live:
        for gen in list(live):
            try:
                next(gen)
            except StopIteration:
                live.remove(gen)


def _in_proj_kernel(x_ref, xp_ref, xn_ref, n1_ref, wa_ref, ba_ref, wg_ref, bg_ref, wgt_ref, bgt_ref,
                    fbrow_ref, fbcol_ref, wh_ref, bh_ref, cw_ref, cb_ref, lbl_ref,
                    q_ref, k_ref, v_ref, mo_ref, gcol_ref, grow_ref, hq_ref, gf_ref, gb_ref, hv_ref, hgg_ref,
                    ext_ref):
    t = pl.program_id(1)
    nt = pl.num_programs(1)
    rows = x_ref.shape[1]
    gain = n1_ref[...]

    lbl = lbl_ref[...]
    lmax = jnp.max(lbl, axis=0, keepdims=True)
    le = jnp.exp(lbl - lmax)
    lb = le[0:1, :] / jnp.sum(le, axis=0, keepdims=True)

    wqk = wa_ref[:, 0:2 * WIDTH]
    bqk = ba_ref[:, 0:2 * WIDTH]
    hp = _rms(xp_ref[0], gain).astype(BF16)
    hn = _rms(xn_ref[0], gain).astype(BF16)
    ext_ref[0:HALO, :] = (_dot(hp, wqk) + bqk) * (t > 0).astype(F32)
    ext_ref[HALO + rows:2 * HALO + rows, :] = (_dot(hn, wqk) + bqk) * (t < nt - 1).astype(F32)

    def stream(r0, n):
        rs = slice(r0, r0 + n)
        h = _rms(x_ref[0, rs, :], gain).astype(BF16)

        def proj(w_ref, b_ref, lo, hi):
            return _dot(h, w_ref[:, lo:hi]) + b_ref[:, lo:hi]

        ext_ref[HALO + r0:HALO + r0 + n, :] = proj(wa_ref, ba_ref, 0, 2 * WIDTH)
        hq_pre = proj(wh_ref, bh_ref, 0, WIDTH)
        yield
        acc = cb_ref[...] + ext_ref[pl.ds(HALO - CONV_PAD + r0, n), :] * cw_ref[0:1, :]
        for j in range(1, CONV_K):
            acc = acc + ext_ref[pl.ds(HALO - CONV_PAD + j + r0, n), :] * cw_ref[j:j + 1, :]
        qk = _silu(acc)
        q_ref[0, rs, :] = (qk[:, 0:WIDTH] * (D_HEAD ** -0.5)).astype(BF16)
        k_ref[0, rs, :] = qk[:, WIDTH:2 * WIDTH]
        v_ref[0, rs, :] = proj(wa_ref, ba_ref, 2 * WIDTH, 3 * WIDTH).astype(BF16)
        yield
        hq_ref[0, rs, :] = _silu(hq_pre)
        mo_ref[0, rs, :] = jax.nn.sigmoid(proj(wa_ref, ba_ref, 3 * WIDTH, 4 * WIDTH))
        yield
        gf_ref[0, rs, :] = lb + (1.0 - lb) * jax.nn.sigmoid(proj(wh_ref, bh_ref, WIDTH, 2 * WIDTH))
        yield
        gb_ref[0, rs, :] = lb + (1.0 - lb) * jax.nn.sigmoid(proj(wh_ref, bh_ref, 2 * WIDTH, 3 * WIDTH))
        hv_ref[0, rs, :] = proj(wh_ref, bh_ref, 3 * WIDTH, 4 * WIDTH).astype(BF16)
        yield
        hgg_ref[0, rs, :] = _silu(proj(wh_ref, bh_ref, 4 * WIDTH, 5 * WIDTH))
        gc = _dot(h, wg_ref[...]) + bg_ref[...]
        lane = lax.broadcasted_iota(jnp.int32, gc.shape, 1)
        is_f = (lane >= 2 * N_HEADS) & (lane < N_GATES)
        gcol_ref[0, rs, :] = jnp.where(is_f, _log_sigmoid(gc + fbrow_ref[...]), gc)
        gr = _dot_nt(wgt_ref[...], h) + bgt_ref[...]
        sub = lax.broadcasted_iota(jnp.int32, gr.shape, 0)
        grow_ref[0, :, rs] = jnp.where(sub >= 2 * N_HEADS, _log_sigmoid(gr + fbcol_ref[...]), gr)

    _run_round_robin([stream(r0, IN_PROJ_STREAM) for r0 in range(0, rows, IN_PROJ_STREAM)])


def _in_proj(x, norm1, w_in, b_in, conv_w, conv_b, fgate_bias, lb_logits):
    B, T, D = x.shape
    rows = IN_PROJ_ROWS
    assert T % rows == 0, (T, rows)
    nt = T // rows
    a_w = 4 * WIDTH
    wa = w_in[:, 0:a_w].astype(BF16)
    ba = b_in[None, 0:a_w]
    wg32 = jnp.pad(w_in[:, a_w:a_w + N_GATES], ((0, 0), (0, LANES - N_GATES)))
    bg = jnp.pad(b_in[a_w:a_w + N_GATES], (0, LANES - N_GATES))[None, :]
    wg = wg32.astype(BF16)
    wgt = w_in[:, a_w:a_w + N_GATES].T.astype(BF16)
    bgt = b_in[a_w:a_w + N_GATES][:, None]
    fb = fgate_bias.reshape(2 * N_HEADS)
    fbrow = jnp.zeros((1, LANES), F32).at[0, 2 * N_HEADS:N_GATES].set(fb)
    fbcol = jnp.zeros((N_GATES, 1), F32).at[2 * N_HEADS:N_GATES, 0].set(fb)
    wh = w_in[:, a_w + N_GATES:].astype(BF16)
    bh = b_in[None, a_w + N_GATES:]

    tiles_per_halo = rows // HALO
    n_halo = T // HALO

    def full(arr):
        nd = arr.ndim
        return pl.BlockSpec(arr.shape, lambda b, t: (0,) * nd)

    def tok(width):
        return pl.BlockSpec((1, rows, width), lambda b, t: (b, t, 0))

    in_specs = [
        tok(D),
        pl.BlockSpec((1, HALO, D), lambda b, t: (b, jnp.maximum(t * tiles_per_halo - 1, 0), 0)),
        pl.BlockSpec((1, HALO, D), lambda b, t: (b, jnp.minimum((t + 1) * tiles_per_halo, n_halo - 1), 0)),
    ]
    consts = [norm1[None, :], wa, ba, wg, bg, wgt, bgt, fbrow, fbcol, wh, bh, conv_w, conv_b[None, :], lb_logits]
    in_specs += [full(c) for c in consts]
    tok_out = jax.ShapeDtypeStruct((B, T, WIDTH), F32)
    tok_bf = jax.ShapeDtypeStruct((B, T, WIDTH), BF16)
    out_shape = [tok_bf, tok_out, tok_bf, tok_out,
                 jax.ShapeDtypeStruct((B, T, LANES), F32),
                 jax.ShapeDtypeStruct((B, N_GATES, T), F32),
                 tok_out, tok_out, tok_out, tok_bf, tok_out]
    out_specs = [tok(WIDTH)] * 4 + [tok(LANES), pl.BlockSpec((1, N_GATES, rows), lambda b, t: (b, 0, t))] + [tok(WIDTH)] * 5
    return pl.pallas_call(
        _in_proj_kernel,
        grid=(B, nt),
        in_specs=in_specs,
        out_specs=out_specs,
        out_shape=out_shape,
        scratch_shapes=[pltpu.VMEM((rows + 2 * HALO, 2 * WIDTH), F32)],
        compiler_params=pltpu.CompilerParams(
            dimension_semantics=("parallel", "parallel"), vmem_limit_bytes=VMEM_LIMIT),
        name="in_proj",
    )(x, x, x, *consts)


def _cumsum_rows(tri_bf, x):
    hi, mid, lo = _split3(x)
    return _dot(tri_bf, hi) + _dot(tri_bf, mid) + _dot(tri_bf, lo)


def _cumsum_lanes(x, tri_bf):
    hi, mid, lo = _split3(x)
    return _dot(hi, tri_bf) + _dot(mid, tri_bf) + _dot(lo, tri_bf)


def _mlstm_chunk(q, k, vext, i_col, b_col, i_row, b_row, seen, last, c_ref, m_ref, out_ref, rs, sl):
    m_prev = m_ref[:, 0:1]
    c_prev = c_ref[...]
    q_bf = q
    log_d =jnp.where(seen, b_col - b_row + i_row, -jnp.inf)
    m_inter = b_col + m_prev
    m_t = jnp.maximum(m_inter, jnp.max(log_d, axis=-1, keepdims=True))
    qk = _dot_nt(q_bf, k.astype(BF16))
    yield
    scores = (qk * jnp.exp(log_d - m_t)).astype(BF16)
    inter_scale = jnp.exp(m_inter - m_t)
    b_last = b_col[last:last + 1, :]
    log_w = b_last - b_col + i_col
    m_new = jnp.maximum(b_last + m_prev, jnp.max(log_w, axis=0, keepdims=True))
    w = jnp.exp(log_w - m_new)
    decay = jnp.exp(b_last + m_prev - m_new)
    kw = (k * w).astype(BF16)
    yield
    numden = _dot(scores, vext) + inter_scale * _dot(q_bf, c_prev.astype(BF16))
    update = _dot_tn(kw, vext)
    yield
    num = numden[:, 0:D_HEAD]
    den = numden[:, D_HEAD:2 * D_HEAD]
    out_ref[0, rs, sl] = num / jnp.maximum(jnp.abs(den), jnp.exp(-m_t))
    c_ref[...] = decay * c_prev + update
    m_ref[...] = jnp.broadcast_to(m_new, (1, LANES))


def _hgrn2_level_small(q3, k3, pre3, suf3, half, rev, sub_iota):
    upper = (sub_iota & half) != 0
    second = jnp.logical_not(upper) if rev else upper
    end = 0 if rev else half - 1
    y = jnp.where((sub_iota & (half - 1)) == end, pre3, 0.0)
    step = 1 if rev else -1
    span = 1
    while span < half:
        y = y + pltpu.roll(y, (step * span) % SUBLANES, 1)
        span *= 2
    if 2 * half == SUBLANES:
        other = pltpu.roll(y, half, 1)
    else:
        other = jnp.where(upper, pltpu.roll(y, half, 1), pltpu.roll(y, SUBLANES - half, 1))
    z = jnp.where(second, q3 * pre3, k3 * suf3)
    return z, pre3 * jnp.where(second, other, 1.0), suf3 * jnp.where(second, 1.0, other)


def _hgrn2_level_big(q, k, pre, suf, half, rev):
    L, width = q.shape
    shape = (L // (2 * half), 2, half, width)
    q4, k4, pre4, suf4 = (a.reshape(shape) for a in (q, k, pre, suf))
    first = 1 if rev else 0
    second = 1 - first
    end = 0 if rev else half - 1
    total_first = pre4[:, first, end:end + 1, :]
    total_second = pre4[:, second, end:end + 1, :]

    def join(at_first, at_second):
        parts = (at_second, at_first) if rev else (at_first, at_second)
        return jnp.stack(parts, axis=1).reshape(L, width)

    z = join(k4[:, first] * suf4[:, first], q4[:, second] * pre4[:, second])
    pre_new = join(pre4[:, first], pre4[:, second] * total_first)
    suf_new = join(suf4[:, first] * total_second, suf4[:, second])
    return z, pre_new, suf_new


def _hgrn2_chunk(q, g, v_bf, rev, level, diag, sub_iota, st_refs, out_ref, rs, sl):
    L, width = q.shape
    heads = [slice(h * D_HEAD, (h + 1) * D_HEAD) for h in range(width // D_HEAD)]
    k = 1.0 - g
    q_bf = q.astype(BF16)
    k_bf = k.astype(BF16)
    att = [jnp.where(diag, _dot_nt(q_bf[:, s], k_bf[:, s]).astype(BF16), jnp.zeros((), BF16)) for s in heads]
    small = (L // SUBLANES, SUBLANES, width)
    q3, k3, pre, suf = q.reshape(small), k.reshape(small), g.reshape(small), jnp.ones(small, F32)
    half = 1
    bit = 0
    while half < L:
        if half == SUBLANES:
            pre, suf = pre.reshape(L, width), suf.reshape(L, width)
        if half < SUBLANES:
            z, pre, suf = _hgrn2_level_small(q3, k3, pre, suf, half, rev, sub_iota)
            z = z.reshape(L, width)
        else:
            z, pre, suf = _hgrn2_level_big(q, k, pre, suf, half, rev)
        z = z.astype(BF16)
        att = [jnp.where(level == bit, _dot_nt(z[:, s], z[:, s]).astype(BF16), a) for a, s in zip(att, heads)]
        half *= 2
        bit += 1
        yield
    last = 0 if rev else L - 1
    q_dec = (q * pre).astype(BF16)
    k_dec = (k * suf).astype(BF16)
    outs = []
    for h, s in enumerate(heads):
        st_prev = st_refs[h][...]
        outs.append(_dot_nt(q_dec[:, s], st_prev.astype(BF16)) + _dot(att[h], v_bf[:, s]))
        st_refs[h][...] = st_prev * pre[last:last + 1, s] + _dot_tn(v_bf[:, s], k_dec[:, s])
    out_ref[0, rs, sl] = jnp.concatenate(outs, axis=1)


def _mixer_kernel(qf_ref, kf_ref, vf_ref, gcf_ref, grf_ref, hqf_ref, hgf_ref, hvf_ref,
                  qb_ref, kb_ref, vb_ref, gcb_ref, grb_ref, hqb_ref, hgb_ref, hvb_ref,
                  hf_ref, of_ref, hb_ref, ob_ref, *state_refs):
    L = CHUNK
    n_state = 2 * N_HEADS
    c_refs, m_refs, st_refs = (state_refs[i * n_state:(i + 1) * n_state] for i in range(3))

    @pl.when(pl.program_id(1) == 0)
    def _():
        for ref in state_refs:
            ref[...] = jnp.zeros_like(ref)

    row = lax.broadcasted_iota(jnp.int32, (L, L), 0)
    col = lax.broadcasted_iota(jnp.int32, (L, L), 1)
    sub_iota = lax.broadcasted_iota(jnp.int32, (L // SUBLANES, SUBLANES, LANES), 1)
    diag = row == col
    diff = row ^ col
    high_bit = jnp.zeros((L, L), jnp.int32)
    half = 2
    while half < L:
        high_bit = high_bit + (diff >= half).astype(jnp.int32)
        half *= 2
    ones = jnp.ones((L, D_HEAD), BF16)

    dirs = (
        (0, qf_ref, kf_ref, vf_ref, gcf_ref, grf_ref, hqf_ref, hgf_ref, hvf_ref, hf_ref, of_ref),
        (1, qb_ref, kb_ref, vb_ref, gcb_ref, grb_ref, hqb_ref, hgb_ref, hvb_ref, hb_ref, ob_ref),
    )
    masks = []
    for rev in (False, True):
        seen = (col >= row) if rev else (col <= row)
        before = (col > row) if rev else (col < row)
        level = jnp.where(before, high_bit, -1)
        tri = seen.astype(BF16)
        tri_t = (row >= col if rev else row <= col).astype(BF16)
        masks.append((seen, level, tri, tri_t))
    n_sub = qf_ref.shape[1] // L
    for step in range(n_sub):
        stages = []
        for d, q_ref, k_ref, v_ref, gc_ref, gr_ref, hq_ref, hg_ref, hv_ref, h_out, o_out in dirs:
            rev = d == 1
            seen, level, tri, tri_t = masks[d]
            last = 0 if rev else L - 1
            sub_chunk = n_sub - 1 - step if rev else step
            rs = slice(sub_chunk * L, (sub_chunk + 1) * L)
            gc = gc_ref[0, rs, :]
            gr = gr_ref[0, :, rs]
            gc_cum = _cumsum_rows(tri, gc)
            gr_cum = _cumsum_lanes(gr, tri_t)
            for hd in range(N_HEADS):
                sl = slice(hd * D_HEAD, (hd + 1) * D_HEAD)
                gi = d * N_HEADS + hd
                gf = 2 * N_HEADS + gi
                idx = d * N_HEADS + hd
                vext = jnp.concatenate([v_ref[0, rs, sl], ones], axis=1)
                stages.append(_mlstm_chunk(
                    q_ref[0, rs, sl], k_ref[0, rs, sl], vext,
                    gc[:, gi:gi + 1], gc_cum[:, gf:gf + 1], gr[gi:gi + 1, :], gr_cum[gf:gf + 1, :],
                    seen, last, c_refs[idx], m_refs[idx], h_out, rs, sl))
                stages.append(_hgrn2_chunk(
                    hq_ref[0, rs, sl], hg_ref[0, rs, sl], hv_ref[0, rs, sl],
                    rev, level, diag, sub_iota, [st_refs[idx]], o_out, rs, sl))
        group = 2 * N_HEADS
        for start in range(0, len(stages), group):
            _run_round_robin(stages[start:start + group])


def _mixer(q, k, v, gcol, grow, hq, g_f, g_b, hv):
    B, T, _ = q.shape
    L = CHUNK * MIXER_CHUNKS
    nc = T // L

    def fwd(width):
        return pl.BlockSpec((1, L, width), lambda b, c: (b, c, 0))

    def bwd(width):
        return pl.BlockSpec((1, L, width), lambda b, c: (b, nc - 1 - c, 0))

    grow_f = pl.BlockSpec((1, N_GATES, L), lambda b, c: (b, 0, c))
    grow_b = pl.BlockSpec((1, N_GATES, L), lambda b, c: (b, 0, nc - 1 - c))
    in_specs = ([fwd(WIDTH)] * 3 + [fwd(LANES), grow_f] + [fwd(WIDTH)] * 3
                + [bwd(WIDTH)] * 3 + [bwd(LANES), grow_b] + [bwd(WIDTH)] * 3)
    out = jax.ShapeDtypeStruct((B, T, WIDTH), F32)
    n_state = 2 * N_HEADS
    return pl.pallas_call(
        _mixer_kernel,
        grid=(B, nc),
        in_specs=in_specs,
        out_specs=[fwd(WIDTH), fwd(WIDTH), bwd(WIDTH), bwd(WIDTH)],
        out_shape=[out, out, out, out],
        scratch_shapes=([pltpu.VMEM((D_HEAD, 2 * D_HEAD), F32)] * n_state
                        + [pltpu.VMEM((1, LANES), F32)] * n_state
                        + [pltpu.VMEM((D_HEAD, D_HEAD), F32)] * n_state),
        compiler_params=pltpu.CompilerParams(
            dimension_semantics=("parallel", "arbitrary"), vmem_limit_bytes=VMEM_LIMIT),
        name="mixer",
    )(q, k, v, gcol, grow, hq, g_f, hv, q, k, v, gcol, grow, hq, g_b, hv)


def _head_norm(hsum, gain):
    parts = []
    for hd in range(N_HEADS):
        hh = hsum[:, hd * D_HEAD:(hd + 1) * D_HEAD]
        parts.append(hh * lax.rsqrt(jnp.mean(hh * hh, axis=-1, keepdims=True) + NORM_EPS))
    return jnp.concatenate(parts, axis=1) * gain


def _merge_tile(r0, hf_ref, hb_ref, of_ref, ob_ref, mo_ref, hgg_ref, x_ref, mn_ref, hn_ref, wo_ref,
                n2_ref, wrh_ref, wrl_ref, br_ref, x1_ref, h2_ref, route_ref, hist_ref):
    rs = slice(r0, r0 + PROJ_ROWS)
    m_out = _head_norm(hf_ref[rs, :] + hb_ref[rs, :], mn_ref[...]) * mo_ref[rs, :]
    hg_out = _head_norm(of_ref[rs, :] + ob_ref[rs, :], hn_ref[...]) * hgg_ref[rs, :]
    mixed = jnp.concatenate([m_out, hg_out], axis=1).astype(BF16)
    yield
    x1 = x_ref[rs, :] + _dot(mixed, wo_ref[...])
    x1_ref[rs, :] = x1
    h2 = _rms(x1, n2_ref[...])
    h2_ref[rs, :] = _pack_bf16_pairs(h2)
    h_hi = h2.astype(BF16)
    h_hi32 = h_hi.astype(F32)
    yield

    h_lo = (h2 - h_hi32).astype(BF16)
    logits = _dot(h_hi, wrh_ref[...]) + _dot(h_lo, wrh_ref[...]) + _dot(h_hi, wrl_ref[...]) + br_ref[...]
    lane = lax.broadcasted_iota(jnp.int32, logits.shape, 1)
    big = jnp.int32(LANES)
    neg = -jnp.inf
    yield
    g_log = jnp.where(lane < N_GROUPS, logits, neg)
    g_max = jnp.max(g_log, axis=-1, keepdims=True)
    g_idx = jnp.min(jnp.where(g_log == g_max, lane, big), axis=-1, keepdims=True)
    g_val = 1.0 / jnp.sum(jnp.exp(g_log - g_max), axis=-1, keepdims=True)
    yield
    e_lo = N_GROUPS + g_idx * EXPERTS_PER_GROUP
    e_log = jnp.where((lane >= e_lo) & (lane < e_lo + EXPERTS_PER_GROUP), logits, neg)
    m1 = jnp.max(e_log, axis=-1, keepdims=True)
    i1 = jnp.min(jnp.where(e_log == m1, lane, big), axis=-1, keepdims=True)
    yield
    e_log2 = jnp.where(lane == i1, neg, e_log)
    m2 = jnp.max(e_log2, axis=-1, keepdims=True)
    i2 = jnp.min(jnp.where(e_log2 == m2, lane, big), axis=-1, keepdims=True)
    r2 = jnp.exp(m2 - m1)
    w1 = g_val / (1.0 + r2)
    w2 = g_val * r2 / (1.0 + r2)
    yield
    rows = logits.shape[0]
    pick0 = lane == i1 - N_GROUPS
    pick1 = lane == i2 - N_GROUPS
    earlier = (lax.broadcasted_iota(jnp.int32, (rows, rows), 1)
               < lax.broadcasted_iota(jnp.int32, (rows, rows), 0)).astype(BF16)
    cnt0 = jnp.sum(pick0.astype(F32), axis=0, keepdims=True)
    cnt1 = jnp.sum(pick1.astype(F32), axis=0, keepdims=True)
    rank0 = jnp.sum(jnp.where(pick0, _dot(earlier, pick0.astype(BF16)), 0.0), axis=-1, keepdims=True)
    rank1 = jnp.sum(jnp.where(pick1, _dot(earlier, pick1.astype(BF16)) + cnt0, 0.0), axis=-1, keepdims=True)
    yield
    columns = ((i1 - N_GROUPS).astype(F32), (i2 - N_GROUPS).astype(F32), w1, w2, rank0, rank1)
    route = jnp.zeros_like(logits)
    for c, value in enumerate(columns):
        route = jnp.where(lane == c, value, route)
    route_ref[rs, :] = route
    hs = slice(r0 // PROJ_ROWS * SUBLANES, (r0 // PROJ_ROWS + 1) * SUBLANES)
    sub = lax.broadcasted_iota(jnp.int32, (SUBLANES, LANES), 0)
    hist_ref[hs, :] = jnp.where(sub == 0, cnt0 + cnt1, 0.0)


N_MERGE_STREAMS = 7


def _merge_kernel(*refs, n_a):
    side_a = refs[0:N_MERGE_STREAMS]
    side_b = refs[N_MERGE_STREAMS:2 * N_MERGE_STREAMS]
    rest = refs[2 * N_MERGE_STREAMS:]

    def block(side):
        _run_round_robin([_merge_tile(r0, *side, *rest) for r0 in range(0, MERGE_ROWS, PROJ_ROWS)])

    @pl.when(pl.program_id(0) < n_a)
    def _():
        block(side_a)

    @pl.when(pl.program_id(0) >= n_a)
    def _():
        block(side_b)


def _merge(streams_a, streams_b, m_norm, hg_norm, w_out, norm2, w_rg, b_rg, w_re, b_re):
    D = D_MODEL
    rows = MERGE_ROWS
    n_a = streams_a[0].shape[0] // rows
    n_b = streams_b[0].shape[0] // rows
    n_all = (n_a + n_b) * rows
    n_log = N_GROUPS + N_EXPERTS
    wr = jnp.pad(jnp.concatenate([w_rg, w_re], axis=1), ((0, 0), (0, LANES - n_log)))
    br = jnp.pad(jnp.concatenate([b_rg, b_re]), (0, LANES - n_log))[None, :]
    wr_hi = wr.astype(BF16)
    wr_lo = (wr - wr_hi.astype(F32)).astype(BF16)
    consts = [m_norm[None, :], hg_norm[None, :], w_out.astype(BF16), norm2[None, :], wr_hi, wr_lo, br]

    def full(arr):
        nd = arr.ndim
        return pl.BlockSpec(arr.shape, lambda i: (0,) * nd)

    def side_a(arr):
        return pl.BlockSpec((rows, arr.shape[1]), lambda i: (jnp.minimum(i, n_a - 1), 0))

    def side_b(arr):
        return pl.BlockSpec((rows, arr.shape[1]), lambda i: (jnp.maximum(i - n_a, 0), 0))

    def out(width):
        return pl.BlockSpec((rows, width), lambda i: (i, 0))

    return pl.pallas_call(
        functools.partial(_merge_kernel, n_a=n_a),
        grid=(n_a + n_b,),
        in_specs=[side_a(s) for s in streams_a] + [side_b(s) for s in streams_b] + [full(c) for c in consts],
        out_specs=[out(D), out(D // 2), out(LANES), pl.BlockSpec((rows // PROJ_ROWS * SUBLANES, LANES), lambda i: (i, 0))],
        out_shape=[jax.ShapeDtypeStruct((n_all, D), F32), jax.ShapeDtypeStruct((n_all, D // 2), jnp.uint32),
                   jax.ShapeDtypeStruct((n_all, LANES), F32),
                   jax.ShapeDtypeStruct((n_all // PROJ_ROWS * SUBLANES, LANES), F32)],
        compiler_params=pltpu.CompilerParams(
            dimension_semantics=("arbitrary",), vmem_limit_bytes=VMEM_LIMIT),
        name="merge",
    )(*streams_a, *streams_b, *consts)


def _sc_row_mover(src, idx, n_out, scatter, name):
    n_moved = idx.shape[0]
    D = src.shape[1]
    n_sub = SC_CORES * SC_SUBCORES
    per = n_moved // n_sub
    window = SC_WINDOW_BYTES // (D * src.dtype.itemsize)
    assert per * n_sub == n_moved and per % window == 0, (n_moved, per, window)
    assert not scatter or src.shape[0] % per == 0, (src.shape, per)
    mesh = plsc.VectorSubcoreMesh(core_axis_name="c", subcore_axis_name="s",
                                  num_cores=SC_CORES, num_subcores=SC_SUBCORES)

    def body(src_hbm, idx_hbm, out_hbm, idx_v, buf):
        base = (lax.axis_index("c") * SC_SUBCORES + lax.axis_index("s")) * per
        pltpu.sync_copy(idx_hbm.at[pl.ds(base, per)], idx_v)

        @pl.loop(0, per // window)
        def _(j):
            linear = pl.ds(base + j * window, window)
            indexed = idx_v.at[pl.ds(j * window, window)]
            if scatter:
                pltpu.sync_copy(src_hbm.at[pl.ds(lax.rem(base, src.shape[0]) + j * window, window)], buf)
                pltpu.sync_copy(buf, out_hbm.at[indexed])
            else:
                pltpu.sync_copy(src_hbm.at[indexed], buf)
                pltpu.sync_copy(buf, out_hbm.at[linear])

    return pl.kernel(
        body,
        out_type=jax.ShapeDtypeStruct((n_out, D), src.dtype),
        mesh=mesh,
        scratch_types=[pltpu.VMEM((per,), jnp.int32), pltpu.VMEM((window, D), src.dtype)],
        name=name,
    )(src, idx)


def _sc_gather_rows(src, idx):
    return _sc_row_mover(src, idx, idx.shape[0], False, "sc_gather_rows")


def _sc_scatter_rows(src, idx, n_out):
    return _sc_row_mover(src, idx, n_out, True, "sc_scatter_rows")


def _expert_kernel(be_ref, nu_ref, x_ref, w1_ref, w3_ref, w2_ref, o_ref, w1_bf, w3_bf, w2_bf):
    i = pl.program_id(0)
    active = i < nu_ref[0]
    new_expert = (i == 0) | (be_ref[i] != be_ref[jnp.maximum(i - 1, 0)])

    @pl.when(active & new_expert)
    def _():
        w1_bf[...] = w1_ref[0].astype(BF16)
        w3_bf[...] = w3_ref[0].astype(BF16)
        w2_bf[...] = w2_ref[0].astype(BF16)

    @pl.when(active)
    def _():
        half = D_MODEL // 2
        x_lo, x_hi = (part.astype(BF16) for part in _unpack_bf16_pairs(x_ref[...]))
        a = _dot(x_lo, w1_bf[0:half, :]) + _dot(x_hi, w1_bf[half:, :])
        b = _dot(x_lo, w3_bf[0:half, :]) + _dot(x_hi, w3_bf[half:, :])
        o_ref[...] = _pack_bf16_pairs(_dot((_silu(a) * b).astype(BF16), w2_bf[...]))


def _experts(xs, block_e, n_used, w1, w3, w2):
    rows = EXPERT_ROWS
    n_blocks = xs.shape[0] // rows
    D = D_MODEL

    def blk(i, be, nu):
        return jnp.minimum(i, nu[0] - 1)

    grid_spec = pltpu.PrefetchScalarGridSpec(
        num_scalar_prefetch=2,
        grid=(n_blocks,),
        in_specs=[
            pl.BlockSpec((rows, D // 2), lambda i, be, nu: (blk(i, be, nu), 0)),
            pl.BlockSpec((1, D, EXPERT_FF), lambda i, be, nu: (be[blk(i, be, nu)], 0, 0)),
            pl.BlockSpec((1, D, EXPERT_FF), lambda i, be, nu: (be[blk(i, be, nu)], 0, 0)),
            pl.BlockSpec((1, EXPERT_FF, D), lambda i, be, nu: (be[blk(i, be, nu)], 0, 0)),
        ],
        out_specs=pl.BlockSpec((rows, D // 2), lambda i, be, nu: (blk(i, be, nu), 0)),
        scratch_shapes=[pltpu.VMEM((D, EXPERT_FF), BF16), pltpu.VMEM((D, EXPERT_FF), BF16),
                        pltpu.VMEM((EXPERT_FF, D), BF16)],
    )
    return pl.pallas_call(
        _expert_kernel,
        grid_spec=grid_spec,
        out_shape=jax.ShapeDtypeStruct((xs.shape[0], D // 2), jnp.uint32),
        compiler_params=pltpu.CompilerParams(
            dimension_semantics=("arbitrary",), vmem_limit_bytes=VMEM_LIMIT),
        name="experts",
    )(block_e, n_used, xs, w1, w3, w2)


def _combine_kernel(y0_ref, y1_ref, x1_ref, route_ref, nf_ref, ya_ref, yb_ref, *, n_a):
    route = route_ref[...]
    r0 = jnp.concatenate(_unpack_bf16_pairs(y0_ref[...]), axis=1)
    r1 = jnp.concatenate(_unpack_bf16_pairs(y1_ref[...]), axis=1)
    y = _rms(x1_ref[...] + route[:, 2:3] * r0 + route[:, 3:4] * r1, nf_ref[...])

    @pl.when(pl.program_id(0) < n_a)
    def _():
        ya_ref[...] = y

    @pl.when(pl.program_id(0) >= n_a)
    def _():
        yb_ref[...] = y


def _combine(x1, route, y_rows, norm_f, n_tok_a):
    N, D = x1.shape
    rows = COMBINE_ROWS
    assert N % rows == 0 and n_tok_a % rows == 0, (N, n_tok_a, rows)
    nt = N // rows
    n_a = n_tok_a // rows

    def tok(width, offset=0):
        return pl.BlockSpec((rows, width), lambda i: (i + offset, 0))

    return pl.pallas_call(
        functools.partial(_combine_kernel, n_a=n_a),
        grid=(nt,),
        in_specs=[tok(D // 2), tok(D // 2, nt), tok(D), tok(LANES), pl.BlockSpec((1, D), lambda i: (0, 0))],
        out_specs=[pl.BlockSpec((rows, D), lambda i: (jnp.minimum(i, n_a - 1), 0)),
                   pl.BlockSpec((rows, D), lambda i: (jnp.maximum(i - n_a, 0), 0))],
        out_shape=[jax.ShapeDtypeStruct((n_tok_a, D), F32), jax.ShapeDtypeStruct((N - n_tok_a, D), F32)],
        compiler_params=pltpu.CompilerParams(
            dimension_semantics=("arbitrary",), vmem_limit_bytes=VMEM_LIMIT),
        name="combine",
    )(y_rows, y_rows, x1, route, norm_f[None, :])


def _plan_kernel(route_ref, table_ref, dest_ref):
    lane = lax.broadcasted_iota(jnp.int32, (PROJ_ROWS, LANES), 1)
    lane_f = lane.astype(F32)
    for tile in range(route_ref.shape[0] // PROJ_ROWS):
        rs = slice(tile * PROJ_ROWS, (tile + 1) * PROJ_ROWS)
        route = route_ref[rs, :]
        first = table_ref[tile * SUBLANES:tile * SUBLANES + 1, :]
        d0 = jnp.sum(jnp.where(lane_f == route[:, 0:1], first, 0.0), axis=-1, keepdims=True) + route[:, 4:5]
        d1 = jnp.sum(jnp.where(lane_f == route[:, 1:2], first, 0.0), axis=-1, keepdims=True) + route[:, 5:6]
        dest_ref[rs, :] = jnp.where(lane == 0, d0, jnp.where(lane == 1, d1, 0.0)).astype(jnp.int32)


def _dispatch_plan(route, hist):
    N = route.shape[0]
    rows = PROJ_ROWS
    n_tiles = N // rows
    blk = EXPERT_ROWS
    tile_counts = hist.reshape(n_tiles, SUBLANES, LANES)[:, 0, 0:N_EXPERTS].astype(jnp.int32)
    tile_first = jnp.cumsum(tile_counts, axis=0) - tile_counts
    counts = jnp.sum(tile_counts, axis=0)
    padded = ((counts + blk - 1) // blk) * blk
    pad_end = jnp.cumsum(padded)
    pad_start = pad_end - padded
    table = jnp.zeros((n_tiles, SUBLANES, LANES), F32).at[:, 0, 0:N_EXPERTS].set(
        (pad_start[None, :] + tile_first).astype(F32)).reshape(n_tiles * SUBLANES, LANES)
    per_step = PLAN_TILES
    assert n_tiles % per_step == 0, (n_tiles, per_step)
    dest_cols = pl.pallas_call(
        _plan_kernel,
        grid=(n_tiles // per_step,),
        in_specs=[pl.BlockSpec((per_step * rows, LANES), lambda i: (i, 0)),
                  pl.BlockSpec((per_step * SUBLANES, LANES), lambda i: (i, 0))],
        out_specs=pl.BlockSpec((per_step * rows, LANES), lambda i: (i, 0)),
        out_shape=jax.ShapeDtypeStruct((N, LANES), jnp.int32),
        compiler_params=pltpu.CompilerParams(dimension_semantics=("parallel",)),
        name="plan",
    )(route, table)
    dest = dest_cols[:, 0:TOP_K].T.reshape(TOP_K * N)
    n_blocks = (TOP_K * N + N_EXPERTS * (blk - 1) + blk - 1) // blk
    block_start = jnp.arange(n_blocks, dtype=jnp.int32) * blk
    block_e = jnp.sum((pad_end[None, :] <= block_start[:, None]).astype(jnp.int32), axis=1)
    block_e = jnp.minimum(block_e, N_EXPERTS - 1)
    n_used = (pad_end[-1] // blk).astype(jnp.int32).reshape(1)
    return dest, block_e, n_used, n_blocks * blk


def _token_mixer(x, norm1, w_in, b_in, conv_w, conv_b, m_fgate_bias, hg_lb_logits):
    B, T, D = x.shape
    q, k, v, mo, gcol, grow, hq, g_f, g_b, hv, hgg = _in_proj(
        x, norm1, w_in, b_in, conv_w, conv_b, m_fgate_bias, hg_lb_logits)
    h_f, o_f, h_b, o_b = _mixer(q, k, v, gcol, grow, hq, g_f, g_b, hv)
    return [a.reshape(B * T, a.shape[-1]) for a in (h_f, h_b, o_f, o_b, mo, hgg, x)]


def kernel(x_prompt, x_sample, norm1, w_in, b_in, conv_w, conv_b, m_fgate_bias, m_norm, hg_lb_logits, hg_norm,
           w_out, norm2, w_router_group, b_router_group, w_router_expert, b_router_expert, w1, w3, w2, norm_f):
    layer = 0
    mixer_args = (norm1[layer], w_in[layer], b_in[layer], conv_w[layer], conv_b[layer], m_fgate_bias[layer],
                  hg_lb_logits)
    streams_p = _token_mixer(x_prompt, *mixer_args)
    streams_s = _token_mixer(x_sample, *mixer_args)
    x1, h2, route, hist = _merge(streams_p, streams_s, m_norm[layer], hg_norm[layer], w_out[layer], norm2[layer],
                                 w_router_group[layer], b_router_group[layer], w_router_expert[layer],
                                 b_router_expert[layer])
    dest, block_e, n_used, n_rows = _dispatch_plan(route, hist)
    xs = _sc_scatter_rows(h2, dest, n_rows)
    out_rows = _experts(xs, block_e, n_used, w1[layer], w3[layer], w2[layer])
    y_rows = _sc_gather_rows(out_rows, dest)
    y_p, y_s = _combine(x1, route, y_rows, norm_f, streams_p[0].shape[0])
    return (y_p.reshape(x_prompt.shape), y_s.reshape(x_sample.shape))
```

```python
import functools

import jax
import jax.numpy as jnp
from jax import lax
from jax.experimental import pallas as pl
from jax.experimental.pallas import tpu as pltpu
from jax.experimental.pallas import tpu_sc as plsc

F32 = jnp.float32
BF16 = jnp.bfloat16

D_MODEL = 1024
N_HEADS = 4
D_HEAD = 128
WIDTH = N_HEADS * D_HEAD
CONV_K = 5
CONV_PAD = CONV_K // 2
N_GROUPS = 4
EXPERTS_PER_GROUP = 8
N_EXPERTS = N_GROUPS * EXPERTS_PER_GROUP
TOP_K = 2
EXPERT_FF = D_MODEL // 2
NORM_EPS = 1e-6

LANES = 128
SUBLANES = 8
CHUNK = 128
MIXER_CHUNKS = 2
PROJ_ROWS = 256
COMBINE_ROWS = 512
MERGE_ROWS = 512
PLAN_TILES = 8
IN_PROJ_ROWS = 512
IN_PROJ_STREAM = 256
HALO = SUBLANES
EXPERT_ROWS = 512
N_GATES = 4 * N_HEADS
SC_CORES = 2
SC_SUBCORES = 16
SC_WINDOW_BYTES = 128 * 1024
VMEM_LIMIT = 56 * 1024 * 1024


def _dot(a, b):
    return jnp.dot(a, b, preferred_element_type=F32)


def _dot_nt(a, b):
    return lax.dot_general(a, b, (((1,), (1,)), ((), ())), preferred_element_type=F32)


def _dot_tn(a, b):
    return lax.dot_general(a, b, (((0,), (0,)), ((), ())), preferred_element_type=F32)


def _split3(x):
    hi = x.astype(BF16)
    r1 = x - hi.astype(F32)
    mid = r1.astype(BF16)
    lo = (r1 - mid.astype(F32)).astype(BF16)
    return hi, mid, lo


def _pack_bf16_pairs(x):
    half = x.shape[1] // 2
    bits = lax.bitcast_convert_type(x.astype(BF16).astype(F32), jnp.uint32)
    return (bits[:, half:] & jnp.uint32(0xFFFF0000)) | (bits[:, :half] >> 16)


def _unpack_bf16_pairs(words):
    lo = lax.bitcast_convert_type(words << 16, F32)
    hi = lax.bitcast_convert_type(words & jnp.uint32(0xFFFF0000), F32)
    return lo, hi


def _silu(x):
    return x * jax.nn.sigmoid(x)


def _log_sigmoid(x):
    return -(jnp.maximum(-x, 0.0) + jnp.log1p(jnp.exp(-jnp.abs(x))))


def _rms(x, gain):
    return x * lax.rsqrt(jnp.mean(x * x, axis=-1, keepdims=True) + NORM_EPS) * gain


def _run_round_robin(generators):
    live = list(generators)
    while live:
        for gen in list(live):
            try:
                next(gen)
            except StopIteration:
                live.remove(gen)


def _in_proj_kernel(x_ref, xp_ref, xn_ref, n1_ref, wa_ref, ba_ref, wg_ref, bg_ref, wgt_ref, bgt_ref,
                    fbrow_ref, fbcol_ref, wh_ref, bh_ref, cw_ref, cb_ref, lbl_ref,
                    q_ref, k_ref, v_ref, mo_ref, gcol_ref, grow_ref, hq_ref, gf_ref, gb_ref, hv_ref, hgg_ref,
                    ext_ref):
    t = pl.program_id(1)
    nt = pl.num_programs(1)
    rows = x_ref.shape[1]
    gain = n1_ref[...]

    lbl = lbl_ref[...]
    lmax = jnp.max(lbl, axis=0, keepdims=True)
    le = jnp.exp(lbl - lmax)
    lb = le[0:1, :] / jnp.sum(le, axis=0, keepdims=True)

    wqk = wa_ref[:, 0:2 * WIDTH]
    bqk = ba_ref[:, 0:2 * WIDTH]
    hp = _rms(xp_ref[0], gain).astype(BF16)
    hn = _rms(xn_ref[0], gain).astype(BF16)
    ext_ref[0:HALO, :] = (_dot(hp, wqk) + bqk) * (t > 0).astype(F32)
    ext_ref[HALO + rows:2 * HALO + rows, :] = (_dot(hn, wqk) + bqk) * (t < nt - 1).astype(F32)

    def stream(r0, n):
        rs = slice(r0, r0 + n)
        h = _rms(x_ref[0, rs, :], gain).astype(BF16)

        def proj(w_ref, b_ref, lo, hi):
            return _dot(h, w_ref[:, lo:hi]) + b_ref[:, lo:hi]

        ext_ref[HALO + r0:HALO + r0 + n, :] = proj(wa_ref, ba_ref, 0, 2 * WIDTH)
        hq_pre = proj(wh_ref, bh_ref, 0, WIDTH)
        yield
        acc = cb_ref[...] + ext_ref[pl.ds(HALO - CONV_PAD + r0, n), :] * cw_ref[0:1, :]
        for j in range(1, CONV_K):
            acc = acc + ext_ref[pl.ds(HALO - CONV_PAD + j + r0, n), :] * cw_ref[j:j + 1, :]
        qk = _silu(acc)
        q_ref[0, rs, :] = (qk[:, 0:WIDTH] * (D_HEAD ** -0.5)).astype(BF16)
        k_ref[0, rs, :] = qk[:, WIDTH:2 * WIDTH]
        v_ref[0, rs, :] = proj(wa_ref, ba_ref, 2 * WIDTH, 3 * WIDTH).astype(BF16)
        yield
        hq_ref[0, rs, :] = _silu(hq_pre)
        mo_ref[0, rs, :] = jax.nn.sigmoid(proj(wa_ref, ba_ref, 3 * WIDTH, 4 * WIDTH))
        yield
        gf_ref[0, rs, :] = lb + (1.0 - lb) * jax.nn.sigmoid(proj(wh_ref, bh_ref, WIDTH, 2 * WIDTH))
        yield
        gb_ref[0, rs, :] = lb + (1.0 - lb) * jax.nn.sigmoid(proj(wh_ref, bh_ref, 2 * WIDTH, 3 * WIDTH))
        hv_ref[0, rs, :] = proj(wh_ref, bh_ref, 3 * WIDTH, 4 * WIDTH).astype(BF16)
        yield
        hgg_ref[0, rs, :] = _silu(proj(wh_ref, bh_ref, 4 * WIDTH, 5 * WIDTH))
        gc = _dot(h, wg_ref[...]) + bg_ref[...]
        lane = lax.broadcasted_iota(jnp.int32, gc.shape, 1)
        is_f = (lane >= 2 * N_HEADS) & (lane < N_GATES)
        gcol_ref[0, rs, :] = jnp.where(is_f, _log_sigmoid(gc + fbrow_ref[...]), gc)
        gr = _dot_nt(wgt_ref[...], h) + bgt_ref[...]
        sub = lax.broadcasted_iota(jnp.int32, gr.shape, 0)
        grow_ref[0, :, rs] = jnp.where(sub >= 2 * N_HEADS, _log_sigmoid(gr + fbcol_ref[...]), gr)

    _run_round_robin([stream(r0, IN_PROJ_STREAM) for r0 in range(0, rows, IN_PROJ_STREAM)])


def _in_proj(x, norm1, w_in, b_in, conv_w, conv_b, fgate_bias, lb_logits):
    B, T, D = x.shape
    rows = IN_PROJ_ROWS
    assert T % rows == 0, (T, rows)
    nt = T // rows
    a_w = 4 * WIDTH
    wa = w_in[:, 0:a_w].astype(BF16)
    ba = b_in[None, 0:a_w]
    wg32 = jnp.pad(w_in[:, a_w:a_w + N_GATES], ((0, 0), (0, LANES - N_GATES)))
    bg = jnp.pad(b_in[a_w:a_w + N_GATES], (0, LANES - N_GATES))[None, :]
    wg = wg32.astype(BF16)
    wgt = w_in[:, a_w:a_w + N_GATES].T.astype(BF16)
    bgt = b_in[a_w:a_w + N_GATES][:, None]
    fb = fgate_bias.reshape(2 * N_HEADS)
    fbrow = jnp.zeros((1, LANES), F32).at[0, 2 * N_HEADS:N_GATES].set(fb)
    fbcol = jnp.zeros((N_GATES, 1), F32).at[2 * N_HEADS:N_GATES, 0].set(fb)
    wh = w_in[:, a_w + N_GATES:].astype(BF16)
    bh = b_in[None, a_w + N_GATES:]

    tiles_per_halo = rows // HALO
    n_halo = T // HALO

    def full(arr):
        nd = arr.ndim
        return pl.BlockSpec(arr.shape, lambda b, t: (0,) * nd)

    def tok(width):
        return pl.BlockSpec((1, rows, width), lambda b, t: (b, t, 0))

    in_specs = [
        tok(D),
        pl.BlockSpec((1, HALO, D), lambda b, t: (b, jnp.maximum(t * tiles_per_halo - 1, 0), 0)),
        pl.BlockSpec((1, HALO, D), lambda b, t: (b, jnp.minimum((t + 1) * tiles_per_halo, n_halo - 1), 0)),
    ]
    consts = [norm1[None, :], wa, ba, wg, bg, wgt, bgt, fbrow, fbcol, wh, bh, conv_w, conv_b[None, :], lb_logits]
    in_specs += [full(c) for c in consts]
    tok_out = jax.ShapeDtypeStruct((B, T, WIDTH), F32)
    tok_bf = jax.ShapeDtypeStruct((B, T, WIDTH), BF16)
    out_shape = [tok_bf, tok_out, tok_bf, tok_out,
                 jax.ShapeDtypeStruct((B, T, LANES), F32),
                 jax.ShapeDtypeStruct((B, N_GATES, T), F32),
                 tok_out, tok_out, tok_out, tok_bf, tok_out]
    out_specs = [tok(WIDTH)] * 4 + [tok(LANES), pl.BlockSpec((1, N_GATES, rows), lambda b, t: (b, 0, t))] + [tok(WIDTH)] * 5
    return pl.pallas_call(
        _in_proj_kernel,
        grid=(B, nt),
        in_specs=in_specs,
        out_specs=out_specs,
        out_shape=out_shape,
        scratch_shapes=[pltpu.VMEM((rows + 2 * HALO, 2 * WIDTH), F32)],
        compiler_params=pltpu.CompilerParams(
            dimension_semantics=("parallel", "parallel"), vmem_limit_bytes=VMEM_LIMIT),
        name="in_proj",
    )(x, x, x, *consts)


def _cumsum_rows(tri_bf, x):
    hi, mid, lo = _split3(x)
    return _dot(tri_bf, hi) + _dot(tri_bf, mid) + _dot(tri_bf, lo)


def _cumsum_lanes(x, tri_bf):
    hi, mid, lo = _split3(x)
    return _dot(hi, tri_bf) + _dot(mid, tri_bf) + _dot(lo, tri_bf)


def _mlstm_chunk(q, k, vext, i_col, b_col, i_row, b_row, seen, last, c_ref, m_ref, out_ref, rs, sl):
    m_prev = m_ref[:, 0:1]
    c_prev = c_ref[...]
    q_bf = q
    log_d = jnp.where(seen, b_col - b_row + i_row, -jnp.inf)
    m_inter = b_col + m_prev
    m_t = jnp.maximum(m_inter, jnp.max(log_d, axis=-1, keepdims=True))
    qk = _dot_nt(q_bf, k.astype(BF16))
    yield
    scores = (qk * jnp.exp(log_d - m_t)).astype(BF16)
    inter_scale = jnp.exp(m_inter - m_t)
    b_last = b_col[last:last + 1, :]
    log_w = b_last - b_col + i_col
    m_new = jnp.maximum(b_last + m_prev, jnp.max(log_w, axis=0, keepdims=True))
    w = jnp.exp(log_w - m_new)
    decay = jnp.exp(b_last + m_prev - m_new)
    kw = (k * w).astype(BF16)
    yield
    numden = _dot(scores, vext) + inter_scale * _dot(q_bf, c_prev.astype(BF16))
    update = _dot_tn(kw, vext)
    yield
    num = numden[:, 0:D_HEAD]
    den = numden[:, D_HEAD:2 * D_HEAD]
    out_ref[0, rs, sl] = num / jnp.maximum(jnp.abs(den), jnp.exp(-m_t))
    c_ref[...] = decay * c_prev + update
    m_ref[...] = jnp.broadcast_to(m_new, (1, LANES))


def _hgrn2_level_small(q3, k3, pre3, suf3, half, rev, sub_iota):
    upper = (sub_iota & half) != 0
    second = jnp.logical_not(upper) if rev else upper
    end = 0 if rev else half - 1
    y = jnp.where((sub_iota & (half - 1)) == end, pre3, 0.0)
    step = 1 if rev else -1
    span = 1
    while span < half:
        y = y + pltpu.roll(y, (step * span) % SUBLANES, 1)
        span *= 2
    if 2 * half == SUBLANES:
        other = pltpu.roll(y, half, 1)
    else:
        other = jnp.where(upper, pltpu.roll(y, half, 1), pltpu.roll(y, SUBLANES - half, 1))
    z = jnp.where(second, q3 * pre3, k3 * suf3)
    return z, pre3 * jnp.where(second, other, 1.0), suf3 * jnp.where(second, 1.0, other)


def _hgrn2_level_big(q, k, pre, suf, half, rev):
    L, width = q.shape
    shape = (L // (2 * half), 2, half, width)
    q4, k4, pre4, suf4 = (a.reshape(shape) for a in (q, k, pre, suf))
    first = 1 if rev else 0
    second = 1 - first
    end = 0 if rev else half - 1
    total_first = pre4[:, first, end:end + 1, :]
    total_second = pre4[:, second, end:end + 1, :]

    def join(at_first, at_second):
        parts = (at_second, at_first) if rev else (at_first, at_second)
        return jnp.stack(parts, axis=1).reshape(L, width)

    z = join(k4[:, first] * suf4[:, first], q4[:, second] * pre4[:, second])
    pre_new = join(pre4[:, first], pre4[:, second] * total_first)
    suf_new = join(suf4[:, first] * total_second, suf4[:, second])
    return z, pre_new, suf_new


def _hgrn2_chunk(q, g, v_bf, rev, level, diag, sub_iota, st_refs, out_ref, rs, sl):
    L, width = q.shape
    heads = [slice(h * D_HEAD, (h + 1) * D_HEAD) for h in range(width // D_HEAD)]
    k = 1.0 - g
    q_bf = q.astype(BF16)
    k_bf = k.astype(BF16)
    att = [jnp.where(diag, _dot_nt(q_bf[:, s], k_bf[:, s]).astype(BF16), jnp.zeros((), BF16)) for s in heads]
    small = (L // SUBLANES, SUBLANES, width)
    q3, k3, pre, suf = q.reshape(small), k.reshape(small), g.reshape(small), jnp.ones(small, F32)
    half = 1
    bit = 0
    while half < L:
        if half == SUBLANES:
            pre, suf = pre.reshape(L, width), suf.reshape(L, width)
        if half < SUBLANES:
            z, pre, suf = _hgrn2_level_small(q3, k3, pre, suf, half, rev, sub_iota)
            z = z.reshape(L, width)
        else:
            z, pre, suf = _hgrn2_level_big(q, k, pre, suf, half, rev)
        z = z.astype(BF16)
        att = [jnp.where(level == bit, _dot_nt(z[:, s], z[:, s]).astype(BF16), a) for a, s in zip(att, heads)]
        half *= 2
        bit += 1
        yield
    last = 0 if rev else L - 1
    q_dec = (q * pre).astype(BF16)
    k_dec = (k * suf).astype(BF16)
    outs = []
    for h, s in enumerate(heads):
        st_prev = st_refs[h][...]
        outs.append(_dot_nt(q_dec[:, s], st_prev.astype(BF16)) + _dot(att[h], v_bf[:, s]))
        st_refs[h][...] = st_prev * pre[last:last + 1, s] + _dot_tn(v_bf[:, s], k_dec[:, s])
    out_ref[0, rs, sl] = jnp.concatenate(outs, axis=1)


def _mixer_kernel(qf_ref, kf_ref, vf_ref, gcf_ref, grf_ref, hqf_ref, hgf_ref, hvf_ref,
                  qb_ref, kb_ref, vb_ref, gcb_ref, grb_ref, hqb_ref, hgb_ref, hvb_ref,
                  hf_ref, of_ref, hb_ref, ob_ref, *state_refs):
    L = CHUNK
    n_state = 2 * N_HEADS
    c_refs, m_refs, st_refs = (state_refs[i * n_state:(i + 1) * n_state] for i in range(3))

    @pl.when(pl.program_id(1) == 0)
    def _():
        for ref in state_refs:
            ref[...] = jnp.zeros_like(ref)

    row = lax.broadcasted_iota(jnp.int32, (L, L), 0)
    col = lax.broadcasted_iota(jnp.int32, (L, L), 1)
    sub_iota = lax.broadcasted_iota(jnp.int32, (L // SUBLANES, SUBLANES, LANES), 1)
    diag = row == col
    diff = row ^ col
    high_bit = jnp.zeros((L, L), jnp.int32)
    half = 2
    while half < L:
        high_bit = high_bit + (diff >= half).astype(jnp.int32)
        half *= 2
    ones = jnp.ones((L, D_HEAD), BF16)

    dirs = (
        (0, qf_ref, kf_ref, vf_ref, gcf_ref, grf_ref, hqf_ref, hgf_ref, hvf_ref, hf_ref, of_ref),
        (1, qb_ref, kb_ref, vb_ref, gcb_ref, grb_ref, hqb_ref, hgb_ref, hvb_ref, hb_ref, ob_ref),
    )
    masks = []
    for rev in (False, True):
        seen = (col >= row) if rev else (col <= row)
        before = (col > row) if rev else (col < row)
        level = jnp.where(before, high_bit, -1)
        tri = seen.astype(BF16)
        tri_t = (row >= col if rev else row <= col).astype(BF16)
        masks.append((seen, level, tri, tri_t))
    n_sub = qf_ref.shape[1] // L
    for step in range(n_sub):
        stages = []
        for d, q_ref, k_ref, v_ref, gc_ref, gr_ref, hq_ref, hg_ref, hv_ref, h_out, o_out in dirs:
            rev = d == 1
            seen, level, tri, tri_t = masks[d]
            last = 0 if rev else L - 1
            sub_chunk = n_sub - 1 - step if rev else step
            rs = slice(sub_chunk * L, (sub_chunk + 1) * L)
            gc = gc_ref[0, rs, :]
            gr = gr_ref[0, :, rs]
            gc_cum = _cumsum_rows(tri, gc)
            gr_cum = _cumsum_lanes(gr, tri_t)
            for hd in range(N_HEADS):
                sl = slice(hd * D_HEAD, (hd + 1) * D_HEAD)
                gi = d * N_HEADS + hd
                gf = 2 * N_HEADS + gi
                idx = d * N_HEADS + hd
                vext = jnp.concatenate([v_ref[0, rs, sl], ones], axis=1)
                stages.append(_mlstm_chunk(
                    q_ref[0, rs, sl], k_ref[0, rs, sl], vext,
                    gc[:, gi:gi + 1], gc_cum[:, gf:gf + 1], gr[gi:gi + 1, :], gr_cum[gf:gf + 1, :],
                    seen, last, c_refs[idx], m_refs[idx], h_out, rs, sl))
                stages.append(_hgrn2_chunk(
                    hq_ref[0, rs, sl], hg_ref[0, rs, sl], hv_ref[0, rs, sl],
                    rev, level, diag, sub_iota, [st_refs[idx]], o_out, rs, sl))
        group = 2 * N_HEADS
        for start in range(0, len(stages), group):
            _run_round_robin(stages[start:start + group])


def _mixer(q, k, v, gcol, grow, hq, g_f, g_b, hv):
    B, T, _ = q.shape
    L = CHUNK * MIXER_CHUNKS
    nc = T // L

    def fwd(width):
        return pl.BlockSpec((1, L, width), lambda b, c: (b, c, 0))

    def bwd(width):
        return pl.BlockSpec((1, L, width), lambda b, c: (b, nc - 1 - c, 0))

    grow_f = pl.BlockSpec((1, N_GATES, L), lambda b, c: (b, 0, c))
    grow_b = pl.BlockSpec((1, N_GATES, L), lambda b, c: (b, 0, nc - 1 - c))
    in_specs = ([fwd(WIDTH)] * 3 + [fwd(LANES), grow_f] + [fwd(WIDTH)] * 3
                + [bwd(WIDTH)] * 3 + [bwd(LANES), grow_b] + [bwd(WIDTH)] * 3)
    out = jax.ShapeDtypeStruct((B, T, WIDTH), F32)
    n_state = 2 * N_HEADS
    return pl.pallas_call(
        _mixer_kernel,
        grid=(B, nc),
        in_specs=in_specs,
        out_specs=[fwd(WIDTH), fwd(WIDTH), bwd(WIDTH), bwd(WIDTH)],
        out_shape=[out, out, out, out],
        scratch_shapes=([pltpu.VMEM((D_HEAD, 2 * D_HEAD), F32)] * n_state
                        + [pltpu.VMEM((1, LANES), F32)] * n_state
                        + [pltpu.VMEM((D_HEAD, D_HEAD), F32)] * n_state),
        compiler_params=pltpu.CompilerParams(
            dimension_semantics=("parallel", "arbitrary"), vmem_limit_bytes=VMEM_LIMIT),
        name="mixer",
    )(q, k, v, gcol, grow, hq, g_f, hv, q, k, v, gcol, grow, hq, g_b, hv)


def _head_norm(hsum, gain):
    parts = []
    for hd in range(N_HEADS):
        hh = hsum[:, hd * D_HEAD:(hd + 1) * D_HEAD]
        parts.append(hh * lax.rsqrt(jnp.mean(hh * hh, axis=-1, keepdims=True) + NORM_EPS))
    return jnp.concatenate(parts, axis=1) * gain


def _merge_tile(r0, hf_ref, hb_ref, of_ref, ob_ref, mo_ref, hgg_ref, x_ref, mn_ref, hn_ref, wo_ref,
                n2_ref, wrh_ref, wrl_ref, br_ref, x1_ref, h2_ref, route_ref, hist_ref):
    rs = slice(r0, r0 + PROJ_ROWS)
    m_out = _head_norm(hf_ref[rs, :] + hb_ref[rs, :], mn_ref[...]) * mo_ref[rs, :]
    hg_out = _head_norm(of_ref[rs, :] + ob_ref[rs, :], hn_ref[...]) * hgg_ref[rs, :]
    mixed = jnp.concatenate([m_out, hg_out], axis=1).astype(BF16)
    yield
    x1 = x_ref[rs, :] + _dot(mixed, wo_ref[...])
    x1_ref[rs, :] = x1
    h2 = _rms(x1, n2_ref[...])
    h2_ref[rs, :] = _pack_bf16_pairs(h2)
    h_hi = h2.astype(BF16)
    h_hi32 = h_hi.astype(F32)
    yield

    h_lo = (h2 - h_hi32).astype(BF16)
    logits = _dot(h_hi, wrh_ref[...]) + _dot(h_lo, wrh_ref[...]) + _dot(h_hi, wrl_ref[...]) + br_ref[...]
    lane = lax.broadcasted_iota(jnp.int32, logits.shape, 1)
    big = jnp.int32(LANES)
    neg = -jnp.inf
    yield
    g_log = jnp.where(lane < N_GROUPS, logits, neg)
    g_max = jnp.max(g_log, axis=-1, keepdims=True)
    g_idx = jnp.min(jnp.where(g_log == g_max, lane, big), axis=-1, keepdims=True)
    g_val = 1.0 / jnp.sum(jnp.exp(g_log - g_max), axis=-1, keepdims=True)
    yield
    e_lo = N_GROUPS + g_idx * EXPERTS_PER_GROUP
    e_log = jnp.where((lane >= e_lo) & (lane < e_lo + EXPERTS_PER_GROUP), logits, neg)
    m1 = jnp.max(e_log, axis=-1, keepdims=True)
    i1 = jnp.min(jnp.where(e_log == m1, lane, big), axis=-1, keepdims=True)
    yield
    e_log2 = jnp.where(lane == i1, neg, e_log)
    m2 = jnp.max(e_log2, axis=-1, keepdims=True)
    i2 = jnp.min(jnp.where(e_log2 == m2, lane, big), axis=-1, keepdims=True)
    r2 = jnp.exp(m2 - m1)
    w1 = g_val / (1.0 + r2)
    w2 = g_val * r2 / (1.0 + r2)
    yield
    rows = logits.shape[0]
    pick0 = lane == i1 - N_GROUPS
    pick1 = lane == i2 - N_GROUPS
    earlier = (lax.broadcasted_iota(jnp.int32, (rows, rows), 1)
               < lax.broadcasted_iota(jnp.int32, (rows, rows), 0)).astype(BF16)
    cnt0 = jnp.sum(pick0.astype(F32), axis=0, keepdims=True)
    cnt1 = jnp.sum(pick1.astype(F32), axis=0, keepdims=True)
    rank0 = jnp.sum(jnp.where(pick0, _dot(earlier, pick0.astype(BF16)), 0.0), axis=-1, keepdims=True)
    rank1 = jnp.sum(jnp.where(pick1, _dot(earlier, pick1.astype(BF16)) + cnt0, 0.0), axis=-1, keepdims=True)
    yield
    columns = ((i1 - N_GROUPS).astype(F32), (i2 - N_GROUPS).astype(F32), w1, w2, rank0, rank1)
    route = jnp.zeros_like(logits)
    for c, value in enumerate(columns):
        route = jnp.where(lane == c, value, route)
    route_ref[rs, :] = route
    hs = slice(r0 // PROJ_ROWS * SUBLANES, (r0 // PROJ_ROWS + 1) * SUBLANES)
    sub = lax.broadcasted_iota(jnp.int32, (SUBLANES, LANES), 0)
    hist_ref[hs, :] = jnp.where(sub == 0, cnt0 + cnt1, 0.0)


N_MERGE_STREAMS = 7


def _merge_kernel(*refs, n_a):
    side_a = refs[0:N_MERGE_STREAMS]
    side_b = refs[N_MERGE_STREAMS:2 * N_MERGE_STREAMS]
    rest = refs[2 * N_MERGE_STREAMS:]

    def block(side):
        _run_round_robin([_merge_tile(r0, *side, *rest) for r0 in range(0, MERGE_ROWS, PROJ_ROWS)])

    @pl.when(pl.program_id(0) < n_a)
    def _():
        block(side_a)

    @pl.when(pl.program_id(0) >= n_a)
    def _():
        block(side_b)


def _merge(streams_a, streams_b, m_norm, hg_norm, w_out, norm2, w_rg, b_rg, w_re, b_re):
    D = D_MODEL
    rows = MERGE_ROWS
    n_a = streams_a[0].shape[0] // rows
    n_b = streams_b[0].shape[0] // rows
    n_all = (n_a + n_b) * rows
    n_log = N_GROUPS + N_EXPERTS
    wr = jnp.pad(jnp.concatenate([w_rg, w_re], axis=1), ((0, 0), (0, LANES - n_log)))
    br = jnp.pad(jnp.concatenate([b_rg, b_re]), (0, LANES - n_log))[None, :]
    wr_hi = wr.astype(BF16)
    wr_lo = (wr - wr_hi.astype(F32)).astype(BF16)
    consts = [m_norm[None, :], hg_norm[None, :], w_out.astype(BF16), norm2[None, :], wr_hi, wr_lo, br]

    def full(arr):
        nd = arr.ndim
        return pl.BlockSpec(arr.shape, lambda i: (0,) * nd)

    def side_a(arr):
        return pl.BlockSpec((rows, arr.shape[1]), lambda i: (jnp.minimum(i, n_a - 1), 0))

    def side_b(arr):
        return pl.BlockSpec((rows, arr.shape[1]), lambda i: (jnp.maximum(i - n_a, 0), 0))

    def out(width):
        return pl.BlockSpec((rows, width), lambda i: (i, 0))

    return pl.pallas_call(
        functools.partial(_merge_kernel, n_a=n_a),
        grid=(n_a + n_b,),
        in_specs=[side_a(s) for s in streams_a] + [side_b(s) for s in streams_b] + [full(c) for c in consts],
        out_specs=[out(D), out(D // 2), out(LANES), pl.BlockSpec((rows // PROJ_ROWS * SUBLANES, LANES), lambda i: (i, 0))],
        out_shape=[jax.ShapeDtypeStruct((n_all, D), F32), jax.ShapeDtypeStruct((n_all, D // 2), jnp.uint32),
                   jax.ShapeDtypeStruct((n_all, LANES), F32),
                   jax.ShapeDtypeStruct((n_all // PROJ_ROWS * SUBLANES, LANES), F32)],
        compiler_params=pltpu.CompilerParams(
            dimension_semantics=("arbitrary",), vmem_limit_bytes=VMEM_LIMIT),
        name="merge",
    )(*streams_a, *streams_b, *consts)


def _sc_row_mover(src, idx, n_out, scatter, name):
    n_moved = idx.shape[0]
    D = src.shape[1]
    n_sub = SC_CORES * SC_SUBCORES
    per = n_moved // n_sub
    window = SC_WINDOW_BYTES // (D * src.dtype.itemsize)
    assert per * n_sub == n_moved and per % window == 0, (n_moved, per, window)
    assert not scatter or src.shape[0] % per == 0, (src.shape, per)
    mesh = plsc.VectorSubcoreMesh(core_axis_name="c", subcore_axis_name="s",
                                  num_cores=SC_CORES, num_subcores=SC_SUBCORES)

    def body(src_hbm, idx_hbm, out_hbm, idx_v, buf):
        base = (lax.axis_index("c") * SC_SUBCORES + lax.axis_index("s")) * per
        pltpu.sync_copy(idx_hbm.at[pl.ds(base, per)], idx_v)

        @pl.loop(0, per // window)
        def _(j):
            linear = pl.ds(base + j * window, window)
            indexed = idx_v.at[pl.ds(j * window, window)]
            if scatter:
                pltpu.sync_copy(src_hbm.at[pl.ds(lax.rem(base, src.shape[0]) + j * window, window)], buf)
                pltpu.sync_copy(buf, out_hbm.at[indexed])
            else:
                pltpu.sync_copy(src_hbm.at[indexed], buf)
                pltpu.sync_copy(buf, out_hbm.at[linear])

    return pl.kernel(
        body,
        out_type=jax.ShapeDtypeStruct((n_out, D), src.dtype),
        mesh=mesh,
        scratch_types=[pltpu.VMEM((per,), jnp.int32), pltpu.VMEM((window, D), src.dtype)],
        name=name,
    )(src, idx)


def _sc_gather_rows(src, idx):
    return _sc_row_mover(src, idx, idx.shape[0], False, "sc_gather_rows")


def _sc_scatter_rows(src, idx, n_out):
    return _sc_row_mover(src, idx, n_out, True, "sc_scatter_rows")


def _expert_kernel(be_ref, nu_ref, x_ref, w1_ref, w3_ref, w2_ref, o_ref, w1_bf, w3_bf, w2_bf):
    i = pl.program_id(0)
    active = i < nu_ref[0]
    new_expert = (i == 0) | (be_ref[i] != be_ref[jnp.maximum(i - 1, 0)])

    @pl.when(active & new_expert)
    def _():
        w1_bf[...] = w1_ref[0].astype(BF16)
        w3_bf[...] = w3_ref[0].astype(BF16)
        w2_bf[...] = w2_ref[0].astype(BF16)

    @pl.when(active)
    def _():
        half = D_MODEL // 2
        x_lo, x_hi = (part.astype(BF16) for part in _unpack_bf16_pairs(x_ref[...]))
        a = _dot(x_lo, w1_bf[0:half, :]) + _dot(x_hi, w1_bf[half:, :])
        b = _dot(x_lo, w3_bf[0:half, :]) + _dot(x_hi, w3_bf[half:, :])
        o_ref[...] = _pack_bf16_pairs(_dot((_silu(a) * b).astype(BF16), w2_bf[...]))


def _experts(xs, block_e, n_used, w1, w3, w2):
    rows = EXPERT_ROWS
    n_blocks = xs.shape[0] // rows
    D = D_MODEL

    def blk(i, be, nu):
        return jnp.minimum(i, nu[0] - 1)

    grid_spec = pltpu.PrefetchScalarGridSpec(
        num_scalar_prefetch=2,
        grid=(n_blocks,),
        in_specs=[
            pl.BlockSpec((rows, D // 2), lambda i, be, nu: (blk(i, be, nu), 0)),
            pl.BlockSpec((1, D, EXPERT_FF), lambda i, be, nu: (be[blk(i, be, nu)], 0, 0)),
            pl.BlockSpec((1, D, EXPERT_FF), lambda i, be, nu: (be[blk(i, be, nu)], 0, 0)),
            pl.BlockSpec((1, EXPERT_FF, D), lambda i, be, nu: (be[blk(i, be, nu)], 0, 0)),
        ],
        out_specs=pl.BlockSpec((rows, D // 2), lambda i, be, nu: (blk(i, be, nu), 0)),
        scratch_shapes=[pltpu.VMEM((D, EXPERT_FF), BF16), pltpu.VMEM((D, EXPERT_FF), BF16),
                        pltpu.VMEM((EXPERT_FF, D), BF16)],
    )
    return pl.pallas_call(
        _expert_kernel,
        grid_spec=grid_spec,
        out_shape=jax.ShapeDtypeStruct((xs.shape[0], D // 2), jnp.uint32),
        compiler_params=pltpu.CompilerParams(
            dimension_semantics=("arbitrary",), vmem_limit_bytes=VMEM_LIMIT),
        name="experts",
    )(block_e, n_used, xs, w1, w3, w2)


def _combine_kernel(y0_ref, y1_ref, x1_ref, route_ref, nf_ref, y_ref):
    route = route_ref[...]
    r0 = jnp.concatenate(_unpack_bf16_pairs(y0_ref[...]), axis=1)
    r1 = jnp.concatenate(_unpack_bf16_pairs(y1_ref[...]), axis=1)
    y_ref[...] = _rms(x1_ref[...] + route[:, 2:3] * r0 + route[:, 3:4] * r1, nf_ref[...])


def _combine(x1, route, y_rows, norm_f, tok0):
    D = x1.shape[1]
    n = y_rows.shape[0] // TOP_K
    rows = COMBINE_ROWS
    assert n % rows == 0 and tok0 % rows == 0, (n, tok0, rows)
    nt = n // rows
    first = tok0 // rows

    def tok(width, offset=0):
        return pl.BlockSpec((rows, width), lambda i: (i + offset, 0))

    return pl.pallas_call(
        _combine_kernel,
        grid=(nt,),
        in_specs=[tok(D // 2), tok(D // 2, nt), tok(D, first), tok(LANES, first),
                  pl.BlockSpec((1, D), lambda i: (0, 0))],
        out_specs=tok(D),
        out_shape=jax.ShapeDtypeStruct((n, D), F32),
        compiler_params=pltpu.CompilerParams(
            dimension_semantics=("parallel",), vmem_limit_bytes=VMEM_LIMIT),
        name="combine",
    )(y_rows, y_rows, x1, route, norm_f[None, :])


def _plan_kernel(route_ref, table_ref, dest_ref):
    lane = lax.broadcasted_iota(jnp.int32, (PROJ_ROWS, LANES), 1)
    lane_f = lane.astype(F32)
    for tile in range(route_ref.shape[0] // PROJ_ROWS):
        rs = slice(tile * PROJ_ROWS, (tile + 1) * PROJ_ROWS)
        route = route_ref[rs, :]
        first = table_ref[tile * SUBLANES:tile * SUBLANES + 1, :]
        d0 = jnp.sum(jnp.where(lane_f == route[:, 0:1], first, 0.0), axis=-1, keepdims=True) + route[:, 4:5]
        d1 = jnp.sum(jnp.where(lane_f == route[:, 1:2], first, 0.0), axis=-1, keepdims=True) + route[:, 5:6]
        dest_ref[rs, :] = jnp.where(lane == 0, d0, jnp.where(lane == 1, d1, 0.0)).astype(jnp.int32)


def _dispatch_plan(route, hist):
    N = route.shape[0]
    rows = PROJ_ROWS
    n_tiles = N // rows
    blk = EXPERT_ROWS
    tile_counts = hist.reshape(n_tiles, SUBLANES, LANES)[:, 0, 0:N_EXPERTS].astype(jnp.int32)
    tile_first = jnp.cumsum(tile_counts, axis=0) - tile_counts
    counts = jnp.sum(tile_counts, axis=0)
    padded = ((counts + blk - 1) // blk) * blk
    pad_end = jnp.cumsum(padded)
    pad_start = pad_end - padded
    table = jnp.zeros((n_tiles, SUBLANES, LANES), F32).at[:, 0, 0:N_EXPERTS].set(
        (pad_start[None, :] + tile_first).astype(F32)).reshape(n_tiles * SUBLANES, LANES)
    per_step = PLAN_TILES
    assert n_tiles % per_step == 0, (n_tiles, per_step)
    dest_cols = pl.pallas_call(
        _plan_kernel,
        grid=(n_tiles // per_step,),
        in_specs=[pl.BlockSpec((per_step * rows, LANES), lambda i: (i, 0)),
                  pl.BlockSpec((per_step * SUBLANES, LANES), lambda i: (i, 0))],
        out_specs=pl.BlockSpec((per_step * rows, LANES), lambda i: (i, 0)),
        out_shape=jax.ShapeDtypeStruct((N, LANES), jnp.int32),
        compiler_params=pltpu.CompilerParams(dimension_semantics=("parallel",)),
        name="plan",
    )(route, table)
    dest_cols = dest_cols[:, 0:TOP_K]
    dest = dest_cols.T.reshape(TOP_K * N)
    n_blocks = (TOP_K * N + N_EXPERTS * (blk - 1) + blk - 1) // blk
    block_start = jnp.arange(n_blocks, dtype=jnp.int32) * blk
    block_e = jnp.sum((pad_end[None, :] <= block_start[:, None]).astype(jnp.int32), axis=1)
    block_e = jnp.minimum(block_e, N_EXPERTS - 1)
    n_used = (pad_end[-1] // blk).astype(jnp.int32).reshape(1)
    return dest, dest_cols, block_e, n_used, n_blocks * blk


def _token_mixer(x, norm1, w_in, b_in, conv_w, conv_b, m_fgate_bias, hg_lb_logits):
    B, T, D = x.shape
    q, k, v, mo, gcol, grow, hq, g_f, g_b, hv, hgg = _in_proj(
        x, norm1, w_in, b_in, conv_w, conv_b, m_fgate_bias, hg_lb_logits)
    h_f, o_f, h_b, o_b = _mixer(q, k, v, gcol, grow, hq, g_f, g_b, hv)
    return [a.reshape(B * T, a.shape[-1]) for a in (h_f, h_b, o_f, o_b, mo, hgg, x)]


def kernel(x_prompt, x_sample, norm1, w_in, b_in, conv_w, conv_b, m_fgate_bias, m_norm, hg_lb_logits, hg_norm,
           w_out, norm2, w_router_group, b_router_group, w_router_expert, b_router_expert, w1, w3, w2, norm_f):
    layer = 0
    mixer_args = (norm1[layer], w_in[layer], b_in[layer], conv_w[layer], conv_b[layer], m_fgate_bias[layer],
                  hg_lb_logits)
    streams_p = _token_mixer(x_prompt, *mixer_args)
    streams_s = _token_mixer(x_sample, *mixer_args)
    x1, h2, route, hist = _merge(streams_p, streams_s, m_norm[layer], hg_norm[layer], w_out[layer], norm2[layer],
                                 w_router_group[layer], b_router_group[layer], w_router_expert[layer],
                                 b_router_expert[layer])
    dest, dest_cols, block_e, n_used, n_rows = _dispatch_plan(route, hist)
    xs = _sc_scatter_rows(h2, dest, n_rows)
    out_rows = _experts(xs, block_e, n_used, w1[layer], w3[layer], w2[layer])
    outs = []
    tok0 = 0
    for x in (x_prompt, x_sample):
        n = x.shape[0] * x.shape[1]
        y_rows = _sc_gather_rows(out_rows, dest_cols[tok0:tok0 + n].T.reshape(TOP_K * n))
        outs.append(_combine(x1, route, y_rows, norm_f, tok0).reshape(x.shape))
        tok0 += n
    return tuple(outs)
```

```python
import functools

import jax
import jax.numpy as jnp
from jax import lax
from jax.experimental import pallas as pl
from jax.experimental.pallas import tpu as pltpu
from jax.experimental.pallas import tpu_sc as plsc

F32 = jnp.float32
BF16 = jnp.bfloat16

D_MODEL = 1024
N_HEADS = 4
D_HEAD = 128
WIDTH = N_HEADS * D_HEAD
CONV_K = 5
CONV_PAD = CONV_K // 2
N_GROUPS = 4
EXPERTS_PER_GROUP = 8
N_EXPERTS = N_GROUPS * EXPERTS_PER_GROUP
TOP_K = 2
EXPERT_FF = D_MODEL // 2
NORM_EPS = 1e-6

LANES = 128
SUBLANES = 8
CHUNK = 128
MIXER_CHUNKS = 2
PROJ_ROWS = 256
COMBINE_ROWS = 512
MERGE_ROWS = 512
PLAN_TILES = 8
IN_PROJ_ROWS = 512
IN_PROJ_STREAM = 256
HALO = SUBLANES
EXPERT_ROWS = 512
N_GATES = 4 * N_HEADS
SC_CORES = 2
SC_SUBCORES = 16
SC_WINDOW_BYTES = 128 * 1024
VMEM_LIMIT = 56 * 1024 * 1024


def _dot(a, b):
    return jnp.dot(a, b, preferred_element_type=F32)


def _dot_nt(a, b):
    return lax.dot_general(a, b, (((1,), (1,)), ((), ())), preferred_element_type=F32)


def _dot_tn(a, b):
    return lax.dot_general(a, b, (((0,), (0,)), ((), ())), preferred_element_type=F32)


def _split3(x):
    hi = x.astype(BF16)
    r1 = x - hi.astype(F32)
    mid = r1.astype(BF16)
    lo = (r1 - mid.astype(F32)).astype(BF16)
    return hi, mid, lo


def _pack_bf16_pairs(x):
    half = x.shape[1] // 2
    bits = lax.bitcast_convert_type(x.astype(BF16).astype(F32), jnp.uint32)
    return (bits[:, half:] & jnp.uint32(0xFFFF0000)) | (bits[:, :half] >> 16)


def _unpack_bf16_pairs(words):
    lo = lax.bitcast_convert_type(words << 16, F32)
    hi = lax.bitcast_convert_type(words & jnp.uint32(0xFFFF0000), F32)
    return lo, hi


def _silu(x):
    return x * jax.nn.sigmoid(x)


def _log_sigmoid(x):
    return -(jnp.maximum(-x, 0.0) + jnp.log1p(jnp.exp(-jnp.abs(x))))


def _rms(x, gain):
    return x * lax.rsqrt(jnp.mean(x * x, axis=-1, keepdims=True) + NORM_EPS) * gain


def _run_round_robin(generators):
    live = list(generators)
    while live:
        for gen in list(live):
            try:
                next(gen)
            except StopIteration:
                live.remove(gen)


def _in_proj_kernel(x_ref, xp_ref, xn_ref, n1_ref, wa_ref, ba_ref, wg_ref, bg_ref, wgt_ref, bgt_ref,
                    fbrow_ref, fbcol_ref, wh_ref, bh_ref, cw_ref, cb_ref, lbl_ref,
                    q_ref, k_ref, v_ref, mo_ref, gcol_ref, grow_ref, hq_ref, gf_ref, gb_ref, hv_ref, hgg_ref,
                    ext_ref):
    t = pl.program_id(1)
    nt = pl.num_programs(1)
    rows = x_ref.shape[1]
    gain = n1_ref[...]

    lbl = lbl_ref[...]
    lmax = jnp.max(lbl, axis=0, keepdims=True)
    le = jnp.exp(lbl - lmax)
    lb = le[0:1, :] / jnp.sum(le, axis=0, keepdims=True)

    wqk = wa_ref[:, 0:2 * WIDTH]
    bqk = ba_ref[:, 0:2 * WIDTH]
    hp = _rms(xp_ref[0], gain).astype(BF16)
    hn = _rms(xn_ref[0], gain).astype(BF16)
    ext_ref[0:HALO, :] = (_dot(hp, wqk) + bqk) * (t > 0).astype(F32)
    ext_ref[HALO + rows:2 * HALO + rows, :] = (_dot(hn, wqk) + bqk) * (t < nt - 1).astype(F32)

    def stream(r0, n):
        rs = slice(r0, r0 + n)
        h = _rms(x_ref[0, rs, :], gain).astype(BF16)

        def proj(w_ref, b_ref, lo, hi):
            return _dot(h, w_ref[:, lo:hi]) + b_ref[:, lo:hi]

        ext_ref[HALO + r0:HALO + r0 + n, :] = proj(wa_ref, ba_ref, 0, 2 * WIDTH)
        hq_pre = proj(wh_ref, bh_ref, 0, WIDTH)
        yield
        acc = cb_ref[...] + ext_ref[pl.ds(HALO - CONV_PAD + r0, n), :] * cw_ref[0:1, :]
        for j in range(1, CONV_K):
            acc = acc + ext_ref[pl.ds(HALO - CONV_PAD + j + r0, n), :] * cw_ref[j:j + 1, :]
        qk = _silu(acc)
        q_ref[0, rs, :] = (qk[:, 0:WIDTH] * (D_HEAD ** -0.5)).astype(BF16)
        k_ref[0, rs, :] = qk[:, WIDTH:2 * WIDTH]
        v_ref[0, rs, :] = proj(wa_ref, ba_ref, 2 * WIDTH, 3 * WIDTH).astype(BF16)
        yield
        hq_ref[0, rs, :] = _silu(hq_pre)
        mo_ref[0, rs, :] = jax.nn.sigmoid(proj(wa_ref, ba_ref, 3 * WIDTH, 4 * WIDTH))
        yield
        gf_ref[0, rs, :] = lb + (1.0 - lb) * jax.nn.sigmoid(proj(wh_ref, bh_ref, WIDTH, 2 * WIDTH))
        yield
        gb_ref[0, rs, :] = lb + (1.0 - lb) * jax.nn.sigmoid(proj(wh_ref, bh_ref, 2 * WIDTH, 3 * WIDTH))
        hv_ref[0, rs, :] = proj(wh_ref, bh_ref, 3 * WIDTH, 4 * WIDTH).astype(BF16)
        yield
        hgg_ref[0, rs, :] = _silu(proj(wh_ref, bh_ref, 4 * WIDTH, 5 * WIDTH))
        gc = _dot(h, wg_ref[...]) + bg_ref[...]
        lane = lax.broadcasted_iota(jnp.int32, gc.shape, 1)
        is_f = (lane >= 2 * N_HEADS) & (lane < N_GATES)
        gcol_ref[0, rs, :] = jnp.where(is_f, _log_sigmoid(gc + fbrow_ref[...]), gc)
        gr = _dot_nt(wgt_ref[...], h) + bgt_ref[...]
        sub = lax.broadcasted_iota(jnp.int32, gr.shape, 0)
        grow_ref[0, :, rs] = jnp.where(sub >= 2 * N_HEADS, _log_sigmoid(gr + fbcol_ref[...]), gr)

    _run_round_robin([stream(r0, IN_PROJ_STREAM) for r0 in range(0, rows, IN_PROJ_STREAM)])


def _in_proj(x, norm1, w_in, b_in, conv_w, conv_b, fgate_bias, lb_logits):
    B, T, D = x.shape
    rows = IN_PROJ_ROWS
    assert T % rows == 0, (T, rows)
    nt = T // rows
    a_w = 4 * WIDTH
    wa = w_in[:, 0:a_w].astype(BF16)
    ba = b_in[None, 0:a_w]
    wg32 = jnp.pad(w_in[:, a_w:a_w + N_GATES], ((0, 0), (0, LANES - N_GATES)))
    bg = jnp.pad(b_in[a_w:a_w + N_GATES], (0, LANES - N_GATES))[None, :]
    wg = wg32.astype(BF16)
    wgt = w_in[:, a_w:a_w + N_GATES].T.astype(BF16)
    bgt = b_in[a_w:a_w + N_GATES][:, None]
    fb = fgate_bias.reshape(2 * N_HEADS)
    fbrow = jnp.zeros((1, LANES), F32).at[0, 2 * N_HEADS:N_GATES].set(fb)
    fbcol = jnp.zeros((N_GATES, 1), F32).at[2 * N_HEADS:N_GATES, 0].set(fb)
    wh = w_in[:, a_w + N_GATES:].astype(BF16)
    bh = b_in[None, a_w + N_GATES:]

    tiles_per_halo = rows // HALO
    n_halo = T // HALO

    def full(arr):
        nd = arr.ndim
        return pl.BlockSpec(arr.shape, lambda b, t: (0,) * nd)

    def tok(width):
        return pl.BlockSpec((1, rows, width), lambda b, t: (b, t, 0))

    in_specs = [
        tok(D),
        pl.BlockSpec((1, HALO, D), lambda b, t: (b, jnp.maximum(t * tiles_per_halo - 1, 0), 0)),
        pl.BlockSpec((1, HALO, D), lambda b, t: (b, jnp.minimum((t + 1) * tiles_per_halo, n_halo - 1), 0)),
    ]
    consts = [norm1[None, :], wa, ba, wg, bg, wgt, bgt, fbrow, fbcol, wh, bh, conv_w, conv_b[None, :], lb_logits]
    in_specs += [full(c) for c in consts]
    tok_out = jax.ShapeDtypeStruct((B, T, WIDTH), F32)
    tok_bf = jax.ShapeDtypeStruct((B, T, WIDTH), BF16)
    out_shape = [tok_bf, tok_out, tok_bf, tok_out,
                 jax.ShapeDtypeStruct((B, T, LANES), F32),
                 jax.ShapeDtypeStruct((B, N_GATES, T), F32),
                 tok_out, tok_out, tok_out, tok_bf, tok_out]
    out_specs = [tok(WIDTH)] * 4 + [tok(LANES), pl.BlockSpec((1, N_GATES, rows), lambda b, t: (b, 0, t))] + [tok(WIDTH)] * 5
    return pl.pallas_call(
        _in_proj_kernel,
        grid=(B, nt),
        in_specs=in_specs,
        out_specs=out_specs,
        out_shape=out_shape,
        scratch_shapes=[pltpu.VMEM((rows + 2 * HALO, 2 * WIDTH), F32)],
        compiler_params=pltpu.CompilerParams(
            dimension_semantics=("parallel", "parallel"), vmem_limit_bytes=VMEM_LIMIT),
        name="in_proj",
    )(x, x, x, *consts)


def _cumsum_rows(tri_bf, x):
    hi, mid, lo = _split3(x)
    return _dot(tri_bf, hi) + _dot(tri_bf, mid) + _dot(tri_bf, lo)


def _cumsum_lanes(x, tri_bf):
    hi, mid, lo = _split3(x)
    return _dot(hi, tri_bf) + _dot(mid, tri_bf) + _dot(lo, tri_bf)


def _mlstm_chunk(q, k, vext, i_col, b_col, i_row, b_row, seen, last, c_ref, m_ref, out_ref, rs, sl):
    m_prev = m_ref[:, 0:1]
    c_prev = c_ref[...]
    q_bf = q
    log_d = jnp.where(seen, b_col - b_row + i_row, -jnp.inf)
    m_inter = b_col + m_prev
    m_t = jnp.maximum(m_inter, jnp.max(log_d, axis=-1, keepdims=True))
    qk = _dot_nt(q_bf, k.astype(BF16))
    yield
    scores = (qk * jnp.exp(log_d - m_t)).astype(BF16)
    inter_scale = jnp.exp(m_inter - m_t)
    b_last = b_col[last:last + 1, :]
    log_w = b_last - b_col + i_col
    m_new = jnp.maximum(b_last + m_prev, jnp.max(log_w, axis=0, keepdims=True))
    w = jnp.exp(log_w - m_new)
    decay = jnp.exp(b_last + m_prev - m_new)
    kw = (k * w).astype(BF16)
    yield
    numden = _dot(scores, vext) + inter_scale * _dot(q_bf, c_prev.astype(BF16))
    update = _dot_tn(kw, vext)
    yield
    num = numden[:, 0:D_HEAD]
    den = numden[:, D_HEAD:2 * D_HEAD]
    out_ref[0, rs, sl] = num / jnp.maximum(jnp.abs(den), jnp.exp(-m_t))
    c_ref[...] = decay * c_prev + update
    m_ref[...] = jnp.broadcast_to(m_new, (1, LANES))


def _hgrn2_level_small(q3, k3, pre3, suf3, half, rev, sub_iota):
    upper = (sub_iota & half) != 0
    second = jnp.logical_not(upper) if rev else upper
    end = 0 if rev else half - 1
    y = jnp.where((sub_iota & (half - 1)) == end, pre3, 0.0)
    step = 1 if rev else -1
    span = 1
    while span < half:
        y = y + pltpu.roll(y, (step * span) % SUBLANES, 1)
        span *= 2
    if 2 * half == SUBLANES:
        other = pltpu.roll(y, half, 1)
    else:
        other = jnp.where(upper, pltpu.roll(y, half, 1), pltpu.roll(y, SUBLANES - half, 1))
    z = jnp.where(second, q3 * pre3, k3 * suf3)
    return z, pre3 * jnp.where(second, other, 1.0), suf3 * jnp.where(second, 1.0, other)


def _hgrn2_level_big(q, k, pre, suf, half, rev):
    L, width = q.shape
    shape = (L // (2 * half), 2, half, width)
    q4, k4, pre4, suf4 = (a.reshape(shape) for a in (q, k, pre, suf))
    first = 1 if rev else 0
    second = 1 - first
    end = 0 if rev else half - 1
    total_first = pre4[:, first, end:end + 1, :]
    total_second = pre4[:, second, end:end + 1, :]

    def join(at_first, at_second):
        parts = (at_second, at_first) if rev else (at_first, at_second)
        return jnp.stack(parts, axis=1).reshape(L, width)

    z = join(k4[:, first] * suf4[:, first], q4[:, second] * pre4[:, second])
    pre_new = join(pre4[:, first], pre4[:, second] * total_first)
    suf_new = join(suf4[:, first] * total_second, suf4[:, second])
    return z, pre_new, suf_new


def _hgrn2_chunk(q, g, v_bf, rev, level, diag, sub_iota, st_refs, out_ref, rs, sl):
    L, width = q.shape
    heads = [slice(h * D_HEAD, (h + 1) * D_HEAD) for h in range(width // D_HEAD)]
    k = 1.0 - g
    q_bf = q.astype(BF16)
    k_bf = k.astype(BF16)
    att = [jnp.where(diag, _dot_nt(q_bf[:, s], k_bf[:, s]).astype(BF16), jnp.zeros((), BF16)) for s in heads]
    small = (L // SUBLANES, SUBLANES, width)
    q3, k3, pre, suf = q.reshape(small), k.reshape(small), g.reshape(small), jnp.ones(small, F32)
    half = 1
    bit = 0
    while half < L:
        if half == SUBLANES:
            pre, suf = pre.reshape(L, width), suf.reshape(L, width)
        if half < SUBLANES:
            z, pre, suf = _hgrn2_level_small(q3, k3, pre, suf, half, rev, sub_iota)
            z = z.reshape(L, width)
        else:
            z, pre, suf = _hgrn2_level_big(q, k, pre, suf, half, rev)
        z = z.astype(BF16)
        att = [jnp.where(level == bit, _dot_nt(z[:, s], z[:, s]).astype(BF16), a) for a, s in zip(att, heads)]
        half *= 2
        bit += 1
        yield
    last = 0 if rev else L - 1
    q_dec = (q * pre).astype(BF16)
    k_dec = (k * suf).astype(BF16)
    outs = []
    for h, s in enumerate(heads):
        st_prev = st_refs[h][...]
        outs.append(_dot_nt(q_dec[:, s], st_prev.astype(BF16)) + _dot(att[h], v_bf[:, s]))
        st_refs[h][...] = st_prev * pre[last:last + 1, s] + _dot_tn(v_bf[:, s], k_dec[:, s])
    out_ref[0, rs, sl] = jnp.concatenate(outs, axis=1)


def _mixer_kernel(qf_ref, kf_ref, vf_ref, gcf_ref, grf_ref, hqf_ref, hgf_ref, hvf_ref,
                  qb_ref, kb_ref, vb_ref, gcb_ref, grb_ref, hqb_ref, hgb_ref, hvb_ref,
                  hf_ref, of_ref, hb_ref, ob_ref, *state_refs):
    L = CHUNK
    n_state = 2 * N_HEADS
    c_refs, m_refs, st_refs = (state_refs[i * n_state:(i + 1) * n_state] for i in range(3))

    @pl.when(pl.program_id(1) == 0)
    def _():
        for ref in state_refs:
            ref[...] = jnp.zeros_like(ref)

    row = lax.broadcasted_iota(jnp.int32, (L, L), 0)
    col = lax.broadcasted_iota(jnp.int32, (L, L), 1)
    sub_iota = lax.broadcasted_iota(jnp.int32, (L // SUBLANES, SUBLANES, LANES), 1)
    diag = row == col
    diff = row ^ col
    high_bit = jnp.zeros((L, L), jnp.int32)
    half = 2
    while half < L:
        high_bit = high_bit + (diff >= half).astype(jnp.int32)
        half *= 2
    ones = jnp.ones((L, D_HEAD), BF16)

    dirs = (
        (0, qf_ref, kf_ref, vf_ref, gcf_ref, grf_ref, hqf_ref, hgf_ref, hvf_ref, hf_ref, of_ref),
        (1, qb_ref, kb_ref, vb_ref, gcb_ref, grb_ref, hqb_ref, hgb_ref, hvb_ref, hb_ref, ob_ref),
    )
    masks = []
    for rev in (False, True):
        seen = (col >= row) if rev else (col <= row)
        before = (col > row) if rev else (col < row)
        level = jnp.where(before, high_bit, -1)
        tri = seen.astype(BF16)
        tri_t = (row >= col if rev else row <= col).astype(BF16)
        masks.append((seen, level, tri, tri_t))
    n_sub = qf_ref.shape[1] // L
    for step in range(n_sub):
        stages = []
        for d, q_ref, k_ref, v_ref, gc_ref, gr_ref, hq_ref, hg_ref, hv_ref, h_out, o_out in dirs:
            rev = d == 1
            seen, level, tri, tri_t = masks[d]
            last = 0 if rev else L - 1
            sub_chunk = n_sub - 1 - step if rev else step
            rs = slice(sub_chunk * L, (sub_chunk + 1) * L)
            gc = gc_ref[0, rs, :]
            gr = gr_ref[0, :, rs]
            gc_cum = _cumsum_rows(tri, gc)
            gr_cum = _cumsum_lanes(gr, tri_t)
            for hd in range(N_HEADS):
                sl = slice(hd * D_HEAD, (hd + 1) * D_HEAD)
                gi = d * N_HEADS + hd
                gf = 2 * N_HEADS + gi
                idx = d * N_HEADS + hd
                vext = jnp.concatenate([v_ref[0, rs, sl], ones], axis=1)
                stages.append(_mlstm_chunk(
                    q_ref[0, rs, sl], k_ref[0, rs, sl], vext,
                    gc[:, gi:gi + 1], gc_cum[:, gf:gf + 1], gr[gi:gi + 1, :], gr_cum[gf:gf + 1, :],
                    seen, last, c_refs[idx], m_refs[idx], h_out, rs, sl))
                stages.append(_hgrn2_chunk(
                    hq_ref[0, rs, sl], hg_ref[0, rs, sl], hv_ref[0, rs, sl],
                    rev, level, diag, sub_iota, [st_refs[idx]], o_out, rs, sl))
        group = 2 * N_HEADS
        for start in range(0, len(stages), group):
            _run_round_robin(stages[start:start + group])


def _mixer(q, k, v, gcol, grow, hq, g_f, g_b, hv):
    B, T, _ = q.shape
    L = CHUNK * MIXER_CHUNKS
    nc = T // L

    def fwd(width):
        return pl.BlockSpec((1, L, width), lambda b, c: (b, c, 0))

    def bwd(width):
        return pl.BlockSpec((1, L, width), lambda b, c: (b, nc - 1 - c, 0))

    grow_f = pl.BlockSpec((1, N_GATES, L), lambda b, c: (b, 0, c))
    grow_b = pl.BlockSpec((1, N_GATES, L), lambda b, c: (b, 0, nc - 1 - c))
    in_specs = ([fwd(WIDTH)] * 3 + [fwd(LANES), grow_f] + [fwd(WIDTH)] * 3
                + [bwd(WIDTH)] * 3 + [bwd(LANES), grow_b] + [bwd(WIDTH)] * 3)
    out = jax.ShapeDtypeStruct((B, T, WIDTH), F32)
    n_state = 2 * N_HEADS
    return pl.pallas_call(
        _mixer_kernel,
        grid=(B, nc),
        in_specs=in_specs,
        out_specs=[fwd(WIDTH), fwd(WIDTH), bwd(WIDTH), bwd(WIDTH)],
        out_shape=[out, out, out, out],
        scratch_shapes=([pltpu.VMEM((D_HEAD, 2 * D_HEAD), F32)] * n_state
                        + [pltpu.VMEM((1, LANES), F32)] * n_state
                        + [pltpu.VMEM((D_HEAD, D_HEAD), F32)] * n_state),
        compiler_params=pltpu.CompilerParams(
            dimension_semantics=("parallel", "arbitrary"), vmem_limit_bytes=VMEM_LIMIT),
        name="mixer",
    )(q, k, v, gcol, grow, hq, g_f, hv, q, k, v, gcol, grow, hq, g_b, hv)


def _head_norm(hsum, gain):
    parts = []
    for hd in range(N_HEADS):
        hh = hsum[:, hd * D_HEAD:(hd + 1) * D_HEAD]
        parts.append(hh * lax.rsqrt(jnp.mean(hh * hh, axis=-1, keepdims=True) + NORM_EPS))
    return jnp.concatenate(parts, axis=1) * gain


def _merge_tile(r0, hf_ref, hb_ref, of_ref, ob_ref, mo_ref, hgg_ref, x_ref, mn_ref, hn_ref, wo_ref,
                n2_ref, wrh_ref, wrl_ref, br_ref, x1_ref, h2_ref, route_ref, hist_ref):
    rs = slice(r0, r0 + PROJ_ROWS)
    m_out = _head_norm(hf_ref[rs, :] + hb_ref[rs, :], mn_ref[...]) * mo_ref[rs, :]
    hg_out = _head_norm(of_ref[rs, :] + ob_ref[rs, :], hn_ref[...]) * hgg_ref[rs, :]
    mixed = jnp.concatenate([m_out, hg_out], axis=1).astype(BF16)
    yield
    x1 = x_ref[rs, :] + _dot(mixed, wo_ref[...])
    x1_ref[rs, :] = x1
    h2 = _rms(x1, n2_ref[...])
    h2_ref[rs, :] = _pack_bf16_pairs(h2)
    h_hi = h2.astype(BF16)
    h_hi32 = h_hi.astype(F32)
    yield

    h_lo = (h2 - h_hi32).astype(BF16)
    logits = _dot(h_hi, wrh_ref[...]) + _dot(h_lo, wrh_ref[...]) + _dot(h_hi, wrl_ref[...]) + br_ref[...]
    lane = lax.broadcasted_iota(jnp.int32, logits.shape, 1)
    big = jnp.int32(LANES)
    neg = -jnp.inf
    yield
    g_log = jnp.where(lane < N_GROUPS, logits, neg)
    g_max = jnp.max(g_log, axis=-1, keepdims=True)
    g_idx = jnp.min(jnp.where(g_log == g_max, lane, big), axis=-1, keepdims=True)
    g_val = 1.0 / jnp.sum(jnp.exp(g_log - g_max), axis=-1, keepdims=True)
    yield
    e_lo = N_GROUPS + g_idx * EXPERTS_PER_GROUP
    e_log = jnp.where((lane >= e_lo) & (lane < e_lo + EXPERTS_PER_GROUP), logits, neg)
    m1 = jnp.max(e_log, axis=-1, keepdims=True)
    i1 = jnp.min(jnp.where(e_log == m1, lane, big), axis=-1, keepdims=True)
    yield
    e_log2 = jnp.where(lane == i1, neg, e_log)
    m2 = jnp.max(e_log2, axis=-1, keepdims=True)
    i2 = jnp.min(jnp.where(e_log2 == m2, lane, big), axis=-1, keepdims=True)
    r2 = jnp.exp(m2 - m1)
    w1 = g_val / (1.0 + r2)
    w2 = g_val * r2 / (1.0 + r2)
    yield
    rows = logits.shape[0]
    pick0 = lane == i1 - N_GROUPS
    pick1 = lane == i2 - N_GROUPS
    earlier = (lax.broadcasted_iota(jnp.int32, (rows, rows), 1)
               < lax.broadcasted_iota(jnp.int32, (rows, rows), 0)).astype(BF16)
    cnt0 = jnp.sum(pick0.astype(F32), axis=0, keepdims=True)
    cnt1 = jnp.sum(pick1.astype(F32), axis=0, keepdims=True)
    rank0 = jnp.sum(jnp.where(pick0, _dot(earlier, pick0.astype(BF16)), 0.0), axis=-1, keepdims=True)
    rank1 = jnp.sum(jnp.where(pick1, _dot(earlier, pick1.astype(BF16)) + cnt0, 0.0), axis=-1, keepdims=True)
    yield
    columns = ((i1 - N_GROUPS).astype(F32), (i2 - N_GROUPS).astype(F32), w1, w2, rank0, rank1)
    route = jnp.zeros_like(logits)
    for c, value in enumerate(columns):
        route = jnp.where(lane == c, value, route)
    route_ref[rs, :] = route
    hs = slice(r0 // PROJ_ROWS * SUBLANES, (r0 // PROJ_ROWS + 1) * SUBLANES)
    sub = lax.broadcasted_iota(jnp.int32, (SUBLANES, LANES), 0)
    hist_ref[hs, :] = jnp.where(sub == 0, cnt0 + cnt1, 0.0)


N_MERGE_STREAMS = 7


def _merge_kernel(*refs, n_a):
    side_a = refs[0:N_MERGE_STREAMS]
    side_b = refs[N_MERGE_STREAMS:2 * N_MERGE_STREAMS]
    rest = refs[2 * N_MERGE_STREAMS:]

    def block(side):
        _run_round_robin([_merge_tile(r0, *side, *rest) for r0 in range(0, MERGE_ROWS, PROJ_ROWS)])

    @pl.when(pl.program_id(0) < n_a)
    def _():
        block(side_a)

    @pl.when(pl.program_id(0) >= n_a)
    def _():
        block(side_b)


def _merge(streams_a, streams_b, m_norm, hg_norm, w_out, norm2, w_rg, b_rg, w_re, b_re):
    D = D_MODEL
    rows = MERGE_ROWS
    n_a = streams_a[0].shape[0] // rows
    n_b = streams_b[0].shape[0] // rows
    n_all = (n_a + n_b) * rows
    n_log = N_GROUPS + N_EXPERTS
    wr = jnp.pad(jnp.concatenate([w_rg, w_re], axis=1), ((0, 0), (0, LANES - n_log)))
    br = jnp.pad(jnp.concatenate([b_rg, b_re]), (0, LANES - n_log))[None, :]
    wr_hi = wr.astype(BF16)
    wr_lo = (wr - wr_hi.astype(F32)).astype(BF16)
    consts = [m_norm[None, :], hg_norm[None, :], w_out.astype(BF16), norm2[None, :], wr_hi, wr_lo, br]

    def full(arr):
        nd = arr.ndim
        return pl.BlockSpec(arr.shape, lambda i: (0,) * nd)

    def side_a(arr):
        return pl.BlockSpec((rows, arr.shape[1]), lambda i: (jnp.minimum(i, n_a - 1), 0))

    def side_b(arr):
        return pl.BlockSpec((rows, arr.shape[1]), lambda i: (jnp.maximum(i - n_a, 0), 0))

    def out(width):
        return pl.BlockSpec((rows, width), lambda i: (i, 0))

    return pl.pallas_call(
        functools.partial(_merge_kernel, n_a=n_a),
        grid=(n_a + n_b,),
        in_specs=[side_a(s) for s in streams_a] + [side_b(s) for s in streams_b] + [full(c) for c in consts],
        out_specs=[out(D), out(D // 2), out(LANES), pl.BlockSpec((rows // PROJ_ROWS * SUBLANES, LANES), lambda i: (i, 0))],
        out_shape=[jax.ShapeDtypeStruct((n_all, D), F32), jax.ShapeDtypeStruct((n_all, D // 2), jnp.uint32),
                   jax.ShapeDtypeStruct((n_all, LANES), F32),
                   jax.ShapeDtypeStruct((n_all // PROJ_ROWS * SUBLANES, LANES), F32)],
        compiler_params=pltpu.CompilerParams(
            dimension_semantics=("arbitrary",), vmem_limit_bytes=VMEM_LIMIT),
        name="merge",
    )(*streams_a, *streams_b, *consts)


def _sc_row_mover(src, idx, n_out, scatter, name):
    n_moved = idx.shape[0]
    D = src.shape[1]
    n_sub = SC_CORES * SC_SUBCORES
    per = n_moved // n_sub
    window = SC_WINDOW_BYTES // (D * src.dtype.itemsize)
    assert per * n_sub == n_moved and per % window == 0, (n_moved, per, window)
    assert not scatter or src.shape[0] % per == 0, (src.shape, per)
    mesh = plsc.VectorSubcoreMesh(core_axis_name="c", subcore_axis_name="s",
                                  num_cores=SC_CORES, num_subcores=SC_SUBCORES)

    def body(src_hbm, idx_hbm, out_hbm, idx_v, buf):
        base = (lax.axis_index("c") * SC_SUBCORES + lax.axis_index("s")) * per
        pltpu.sync_copy(idx_hbm.at[pl.ds(base, per)], idx_v)

        @pl.loop(0, per // window)
        def _(j):
            linear = pl.ds(base + j * window, window)
            indexed = idx_v.at[pl.ds(j * window, window)]
            if scatter:
                pltpu.sync_copy(src_hbm.at[pl.ds(lax.rem(base, src.shape[0]) + j * window, window)], buf)
                pltpu.sync_copy(buf, out_hbm.at[indexed])
            else:
                pltpu.sync_copy(src_hbm.at[indexed], buf)
                pltpu.sync_copy(buf, out_hbm.at[linear])

    return pl.kernel(
        body,
        out_type=jax.ShapeDtypeStruct((n_out, D), src.dtype),
        mesh=mesh,
        scratch_types=[pltpu.VMEM((per,), jnp.int32), pltpu.VMEM((window, D), src.dtype)],
        name=name,
    )(src, idx)


def _sc_gather_rows(src, idx):
    return _sc_row_mover(src, idx, idx.shape[0], False, "sc_gather_rows")


def _sc_scatter_rows(src, idx, n_out):
    return _sc_row_mover(src, idx, n_out, True, "sc_scatter_rows")


def _expert_kernel(be_ref, nu_ref, x_ref, w1_ref, w3_ref, w2_ref, o_ref, w1_bf, w3_bf, w2_bf):
    i = pl.program_id(0)
    active = i < nu_ref[0]
    new_expert = (i == 0) | (be_ref[i] != be_ref[jnp.maximum(i - 1, 0)])

    @pl.when(active & new_expert)
    def _():
        w1_bf[...] = w1_ref[0].astype(BF16)
        w3_bf[...] = w3_ref[0].astype(BF16)
        w2_bf[...] = w2_ref[0].astype(BF16)

    @pl.when(active)
    def _():
        half = D_MODEL // 2
        x_lo, x_hi = (part.astype(BF16) for part in _unpack_bf16_pairs(x_ref[...]))
        a = _dot(x_lo, w1_bf[0:half, :]) + _dot(x_hi, w1_bf[half:, :])
        b = _dot(x_lo, w3_bf[0:half, :]) + _dot(x_hi, w3_bf[half:, :])
        o_ref[...] = _pack_bf16_pairs(_dot((_silu(a) * b).astype(BF16), w2_bf[...]))


def _experts(xs, block_e, n_used, w1, w3, w2):
    rows = EXPERT_ROWS
    n_blocks = xs.shape[0] // rows
    D = D_MODEL

    def blk(i, be, nu):
        return jnp.minimum(i, nu[0] - 1)

    grid_spec = pltpu.PrefetchScalarGridSpec(
        num_scalar_prefetch=2,
        grid=(n_blocks,),
        in_specs=[
            pl.BlockSpec((rows, D // 2), lambda i, be, nu: (blk(i, be, nu), 0)),
            pl.BlockSpec((1, D, EXPERT_FF), lambda i, be, nu: (be[blk(i, be, nu)], 0, 0)),
            pl.BlockSpec((1, D, EXPERT_FF), lambda i, be, nu: (be[blk(i, be, nu)], 0, 0)),
            pl.BlockSpec((1, EXPERT_FF, D), lambda i, be, nu: (be[blk(i, be, nu)], 0, 0)),
        ],
        out_specs=pl.BlockSpec((rows, D // 2), lambda i, be, nu: (blk(i, be, nu), 0)),
        scratch_shapes=[pltpu.VMEM((D, EXPERT_FF), BF16), pltpu.VMEM((D, EXPERT_FF), BF16),
                        pltpu.VMEM((EXPERT_FF, D), BF16)],
    )
    return pl.pallas_call(
        _expert_kernel,
        grid_spec=grid_spec,
        out_shape=jax.ShapeDtypeStruct((xs.shape[0], D // 2), jnp.uint32),
        compiler_params=pltpu.CompilerParams(
            dimension_semantics=("arbitrary",), vmem_limit_bytes=VMEM_LIMIT),
        name="experts",
    )(block_e, n_used, xs, w1, w3, w2)


def _combine_kernel(y0_ref, y1_ref, x1_ref, route_ref, nf_ref, y_ref):
    route = route_ref[...]
    r0 = jnp.concatenate(_unpack_bf16_pairs(y0_ref[...]), axis=1)
    r1 = jnp.concatenate(_unpack_bf16_pairs(y1_ref[...]), axis=1)
    y_ref[...] = _rms(x1_ref[...] + route[:, 2:3] * r0 + route[:, 3:4] * r1, nf_ref[...])


def _combine(x1, route, y_rows, norm_f, tok0):
    D = x1.shape[1]
    n = y_rows.shape[0] // TOP_K
    rows = COMBINE_ROWS
    assert n % rows == 0 and tok0 % rows == 0, (n, tok0, rows)
    nt = n // rows
    first = tok0 // rows

    def tok(width, offset=0):
        return pl.BlockSpec((rows, width), lambda i: (i + offset, 0))

    return pl.pallas_call(
        _combine_kernel,
        grid=(nt,),
        in_specs=[tok(D // 2), tok(D // 2, nt), tok(D, first), tok(LANES, first),
                  pl.BlockSpec((1, D), lambda i: (0, 0))],
        out_specs=tok(D),
        out_shape=jax.ShapeDtypeStruct((n, D), F32),
        compiler_params=pltpu.CompilerParams(
            dimension_semantics=("parallel",), vmem_limit_bytes=VMEM_LIMIT),
        name="combine",
    )(y_rows, y_rows, x1, route, norm_f[None, :])


def _plan_kernel(route_ref, table_ref, dest_ref):
    lane =lax.broadcasted_iota(jnp.int32, (PROJ_ROWS, LANES), 1)
    lane_f = lane.astype(F32)
    for tile in range(route_ref.shape[0] // PROJ_ROWS):
        rs = slice(tile * PROJ_ROWS, (tile + 1) * PROJ_ROWS)
        route = route_ref[rs, :]
        first = table_ref[tile * SUBLANES:tile * SUBLANES + 1, :]
        d0 = jnp.sum(jnp.where(lane_f == route[:, 0:1], first, 0.0), axis=-1, keepdims=True) + route[:, 4:5]
        d1 = jnp.sum(jnp.where(lane_f == route[:, 1:2], first, 0.0), axis=-1, keepdims=True) + route[:, 5:6]
        cols = jnp.where(lane == 0, d0, jnp.where(lane == 1, d1, 0.0))
        dest_ref[:, rs] = cols.T[0:SUBLANES, :].astype(jnp.int32)


def _dispatch_plan(route, hist):
    N = route.shape[0]
    rows = PROJ_ROWS
    n_tiles = N // rows
    blk = EXPERT_ROWS
    tile_counts = hist.reshape(n_tiles, SUBLANES, LANES)[:, 0, 0:N_EXPERTS].astype(jnp.int32)
    tile_first = jnp.cumsum(tile_counts, axis=0) - tile_counts
    counts = jnp.sum(tile_counts, axis=0)
    padded = ((counts + blk - 1) // blk) * blk
    pad_end = jnp.cumsum(padded)
    pad_start = pad_end - padded
    table = jnp.zeros((n_tiles, SUBLANES, LANES), F32).at[:, 0, 0:N_EXPERTS].set(
        (pad_start[None, :] + tile_first).astype(F32)).reshape(n_tiles * SUBLANES, LANES)
    per_step = PLAN_TILES
    assert n_tiles % per_step == 0, (n_tiles, per_step)
    dest_rows = pl.pallas_call(
        _plan_kernel,
        grid=(n_tiles // per_step,),
        in_specs=[pl.BlockSpec((per_step * rows, LANES), lambda i: (i, 0)),
                  pl.BlockSpec((per_step * SUBLANES, LANES), lambda i: (i, 0))],
        out_specs=pl.BlockSpec((SUBLANES, per_step * rows), lambda i: (0, i)),
        out_shape=jax.ShapeDtypeStruct((SUBLANES, N), jnp.int32),
        compiler_params=pltpu.CompilerParams(dimension_semantics=("parallel",)),
        name="plan",
    )(route, table)
    dest_rows = dest_rows[0:TOP_K]
    n_blocks = (TOP_K * N + N_EXPERTS * (blk - 1) + blk - 1) // blk
    block_start = jnp.arange(n_blocks, dtype=jnp.int32) * blk
    block_e = jnp.sum((pad_end[None, :] <= block_start[:, None]).astype(jnp.int32), axis=1)
    block_e = jnp.minimum(block_e, N_EXPERTS - 1)
    n_used = (pad_end[-1] // blk).astype(jnp.int32).reshape(1)
    return dest_rows, block_e, n_used, n_blocks * blk


def _token_mixer(x, norm1, w_in, b_in, conv_w, conv_b, m_fgate_bias, hg_lb_logits):
    B, T, D = x.shape
    q, k, v, mo, gcol, grow, hq, g_f, g_b, hv, hgg = _in_proj(
        x, norm1, w_in, b_in, conv_w, conv_b, m_fgate_bias, hg_lb_logits)
    h_f, o_f, h_b, o_b = _mixer(q, k, v, gcol, grow, hq, g_f, g_b, hv)
    return [a.reshape(B * T, a.shape[-1]) for a in (h_f, h_b, o_f, o_b, mo, hgg, x)]


def kernel(x_prompt, x_sample, norm1, w_in, b_in, conv_w, conv_b, m_fgate_bias, m_norm, hg_lb_logits, hg_norm,
           w_out, norm2, w_router_group, b_router_group, w_router_expert, b_router_expert, w1, w3, w2, norm_f):
    layer = 0
    mixer_args = (norm1[layer], w_in[layer], b_in[layer], conv_w[layer], conv_b[layer], m_fgate_bias[layer],
                  hg_lb_logits)
    streams_p = _token_mixer(x_prompt, *mixer_args)
    streams_s = _token_mixer(x_sample, *mixer_args)
    x1, h2, route, hist = _merge(streams_p, streams_s, m_norm[layer], hg_norm[layer], w_out[layer], norm2[layer],
                                 w_router_group[layer], b_router_group[layer], w_router_expert[layer],
                                 b_router_expert[layer])
    dest, block_e, n_used, n_rows = _dispatch_plan(route, hist)
    xs = _sc_scatter_rows(h2, dest.reshape(TOP_K * h2.shape[0]), n_rows)
    out_rows = _experts(xs, block_e, n_used, w1[layer], w3[layer], w2[layer])
    outs = []
    tok0 = 0
    for x in (x_prompt, x_sample):
        n = x.shape[0] * x.shape[1]
        y_rows = _sc_gather_rows(out_rows, dest[:, tok0:tok0 + n].reshape(TOP_K * n))
        outs.append(_combine(x1, route, y_rows, norm_f, tok0).reshape(x.shape))
        tok0 += n
    return tuple(outs)
```

```python
import functools

import jax
import jax.numpy as jnp
from jax import lax
from jax.experimental import pallas as pl
from jax.experimental.pallas import tpu as pltpu
from jax.experimental.pallas import tpu_sc as plsc

F32 = jnp.float32
BF16 = jnp.bfloat16

D_MODEL = 1024
N_HEADS = 4
D_HEAD = 128
WIDTH = N_HEADS * D_HEAD
CONV_K = 5
CONV_PAD = CONV_K // 2
N_GROUPS = 4
EXPERTS_PER_GROUP = 8
N_EXPERTS = N_GROUPS * EXPERTS_PER_GROUP
TOP_K = 2
EXPERT_FF = D_MODEL // 2
NORM_EPS = 1e-6

LANES = 128
SUBLANES = 8
CHUNK = 128
MIXER_CHUNKS = 2
PROJ_ROWS = 256
COMBINE_ROWS = 512
MERGE_ROWS = 512
PLAN_TILES = 8
IN_PROJ_ROWS = 512
IN_PROJ_STREAM = 256
HALO = SUBLANES
EXPERT_ROWS = 512
N_GATES = 4 * N_HEADS
SC_CORES = 2
SC_SUBCORES = 16
SC_WINDOW_BYTES = 128 * 1024
VMEM_LIMIT = 56 * 1024 * 1024


def _dot(a, b):
    return jnp.dot(a, b, preferred_element_type=F32)


def _dot_nt(a, b):
    return lax.dot_general(a, b, (((1,), (1,)), ((), ())), preferred_element_type=F32)


def _dot_tn(a, b):
    return lax.dot_general(a, b, (((0,), (0,)), ((), ())), preferred_element_type=F32)


def _split3(x):
    hi = x.astype(BF16)
    r1 = x - hi.astype(F32)
    mid = r1.astype(BF16)
    lo = (r1 - mid.astype(F32)).astype(BF16)
    return hi, mid, lo


def _pack_bf16_pairs(x):
    half = x.shape[1] // 2
    bits = lax.bitcast_convert_type(x.astype(BF16).astype(F32), jnp.uint32)
    return (bits[:, half:] & jnp.uint32(0xFFFF0000)) | (bits[:, :half] >> 16)


def _unpack_bf16_pairs(words):
    lo = lax.bitcast_convert_type(words << 16, F32)
    hi = lax.bitcast_convert_type(words & jnp.uint32(0xFFFF0000), F32)
    return lo, hi


def _silu(x):
    return x * jax.nn.sigmoid(x)


def _log_sigmoid(x):
    return -(jnp.maximum(-x, 0.0) + jnp.log1p(jnp.exp(-jnp.abs(x))))


def _rms(x, gain):
    return x * lax.rsqrt(jnp.mean(x * x, axis=-1, keepdims=True) + NORM_EPS) * gain


def _run_round_robin(generators):
    live = list(generators)
    while live:
        for gen in list(live):
            try:
                next(gen)
            except StopIteration:
                live.remove(gen)


def _in_proj_kernel(x_ref, xp_ref, xn_ref, n1_ref, wa_ref, ba_ref, wg_ref, bg_ref, wgt_ref, bgt_ref,
                    fbrow_ref, fbcol_ref, wh_ref, bh_ref, cw_ref, cb_ref, lbl_ref,
                    q_ref, k_ref, v_ref, mo_ref, gcol_ref, grow_ref, hq_ref, gf_ref, gb_ref, hv_ref, hgg_ref,
                    ext_ref):
    t = pl.program_id(1)
    nt = pl.num_programs(1)
    rows = x_ref.shape[1]
    gain = n1_ref[...]

    lbl = lbl_ref[...]
    lmax = jnp.max(lbl, axis=0, keepdims=True)
    le = jnp.exp(lbl - lmax)
    lb = le[0:1, :] / jnp.sum(le, axis=0, keepdims=True)

    wqk = wa_ref[:, 0:2 * WIDTH]
    bqk = ba_ref[:, 0:2 * WIDTH]
    hp = _rms(xp_ref[0], gain).astype(BF16)
    hn = _rms(xn_ref[0], gain).astype(BF16)
    ext_ref[0:HALO, :] = (_dot(hp, wqk) + bqk) * (t > 0).astype(F32)
    ext_ref[HALO + rows:2 * HALO + rows, :] = (_dot(hn, wqk) + bqk) * (t < nt - 1).astype(F32)

    def stream(r0, n):
        rs = slice(r0, r0 + n)
        h = _rms(x_ref[0, rs, :], gain).astype(BF16)

        def proj(w_ref, b_ref, lo, hi):
            return _dot(h, w_ref[:, lo:hi]) + b_ref[:, lo:hi]

        ext_ref[HALO + r0:HALO + r0 + n, :] = proj(wa_ref, ba_ref, 0, 2 * WIDTH)
        hq_pre = proj(wh_ref, bh_ref, 0, WIDTH)
        yield
        acc = cb_ref[...] + ext_ref[pl.ds(HALO - CONV_PAD + r0, n), :] * cw_ref[0:1, :]
        for j in range(1, CONV_K):
            acc = acc + ext_ref[pl.ds(HALO - CONV_PAD + j + r0, n), :] * cw_ref[j:j + 1, :]
        qk = _silu(acc)
        q_ref[0, rs, :] = (qk[:, 0:WIDTH] * (D_HEAD ** -0.5)).astype(BF16)
        k_ref[0, rs, :] = qk[:, WIDTH:2 * WIDTH]
        v_ref[0, rs, :] = proj(wa_ref, ba_ref, 2 * WIDTH, 3 * WIDTH).astype(BF16)
        yield
        hq_ref[0, rs, :] = _silu(hq_pre)
        mo_ref[0, rs, :] = jax.nn.sigmoid(proj(wa_ref, ba_ref, 3 * WIDTH, 4 * WIDTH))
        yield
        gf_ref[0, rs, :] = lb + (1.0 - lb) * jax.nn.sigmoid(proj(wh_ref, bh_ref, WIDTH, 2 * WIDTH))
        yield
        gb_ref[0, rs, :] = lb + (1.0 - lb) * jax.nn.sigmoid(proj(wh_ref, bh_ref, 2 * WIDTH, 3 * WIDTH))
        hv_ref[0, rs, :] = proj(wh_ref, bh_ref, 3 * WIDTH, 4 * WIDTH).astype(BF16)
        yield
        hgg_ref[0, rs, :] = _silu(proj(wh_ref, bh_ref, 4 * WIDTH, 5 * WIDTH))
        gc = _dot(h, wg_ref[...]) + bg_ref[...]
        lane = lax.broadcasted_iota(jnp.int32, gc.shape, 1)
        is_f = (lane >= 2 * N_HEADS) & (lane < N_GATES)
        gcol_ref[0, rs, :] = jnp.where(is_f, _log_sigmoid(gc + fbrow_ref[...]), gc)
        gr = _dot_nt(wgt_ref[...], h) + bgt_ref[...]
        sub = lax.broadcasted_iota(jnp.int32, gr.shape, 0)
        grow_ref[0, :, rs] = jnp.where(sub >= 2 * N_HEADS, _log_sigmoid(gr + fbcol_ref[...]), gr)

    _run_round_robin([stream(r0, IN_PROJ_STREAM) for r0 in range(0, rows, IN_PROJ_STREAM)])


def _in_proj(x, norm1, w_in, b_in, conv_w, conv_b, fgate_bias, lb_logits):
    B, T, D = x.shape
    rows = IN_PROJ_ROWS
    assert T % rows == 0, (T, rows)
    nt = T // rows
    a_w = 4 * WIDTH
    wa = w_in[:, 0:a_w].astype(BF16)
    ba = b_in[None, 0:a_w]
    wg32 = jnp.pad(w_in[:, a_w:a_w + N_GATES], ((0, 0), (0, LANES - N_GATES)))
    bg = jnp.pad(b_in[a_w:a_w + N_GATES], (0, LANES - N_GATES))[None, :]
    wg = wg32.astype(BF16)
    wgt = w_in[:, a_w:a_w + N_GATES].T.astype(BF16)
    bgt = b_in[a_w:a_w + N_GATES][:, None]
    fb = fgate_bias.reshape(2 * N_HEADS)
    fbrow = jnp.zeros((1, LANES), F32).at[0, 2 * N_HEADS:N_GATES].set(fb)
    fbcol = jnp.zeros((N_GATES, 1), F32).at[2 * N_HEADS:N_GATES, 0].set(fb)
    wh = w_in[:, a_w + N_GATES:].astype(BF16)
    bh = b_in[None, a_w + N_GATES:]

    tiles_per_halo = rows // HALO
    n_halo = T // HALO

    def full(arr):
        nd = arr.ndim
        return pl.BlockSpec(arr.shape, lambda b, t: (0,) * nd)

    def tok(width):
        return pl.BlockSpec((1, rows, width), lambda b, t: (b, t, 0))

    in_specs = [
        tok(D),
        pl.BlockSpec((1, HALO, D), lambda b, t: (b, jnp.maximum(t * tiles_per_halo - 1, 0), 0)),
        pl.BlockSpec((1, HALO, D), lambda b, t: (b, jnp.minimum((t + 1) * tiles_per_halo, n_halo - 1), 0)),
    ]
    consts = [norm1[None, :], wa, ba, wg, bg, wgt, bgt, fbrow, fbcol, wh, bh, conv_w, conv_b[None, :], lb_logits]
    in_specs += [full(c) for c in consts]
    tok_out = jax.ShapeDtypeStruct((B, T, WIDTH), F32)
    tok_bf = jax.ShapeDtypeStruct((B, T, WIDTH), BF16)
    out_shape = [tok_bf, tok_out, tok_bf, tok_out,
                 jax.ShapeDtypeStruct((B, T, LANES), F32),
                 jax.ShapeDtypeStruct((B, N_GATES, T), F32),
                 tok_out, tok_out, tok_out, tok_bf, tok_out]
    out_specs = [tok(WIDTH)] * 4 + [tok(LANES), pl.BlockSpec((1, N_GATES, rows), lambda b, t: (b, 0, t))] + [tok(WIDTH)] * 5
    return pl.pallas_call(
        _in_proj_kernel,
        grid=(B, nt),
        in_specs=in_specs,
        out_specs=out_specs,
        out_shape=out_shape,
        scratch_shapes=[pltpu.VMEM((rows + 2 * HALO, 2 * WIDTH), F32)],
        compiler_params=pltpu.CompilerParams(
            dimension_semantics=("parallel", "parallel"), vmem_limit_bytes=VMEM_LIMIT),
        name="in_proj",
    )(x, x, x, *consts)


def _cumsum_rows(tri_bf, x):
    hi, mid, lo = _split3(x)
    return _dot(tri_bf, hi) + _dot(tri_bf, mid) + _dot(tri_bf, lo)


def _cumsum_lanes(x, tri_bf):
    hi, mid, lo = _split3(x)
    return _dot(hi, tri_bf) + _dot(mid, tri_bf) + _dot(lo, tri_bf)


def _mlstm_chunk(q, k, vext, i_col, b_col, i_row, b_row, seen, last, c_ref, m_ref, out_ref, rs, sl):
    m_prev = m_ref[:, 0:1]
    c_prev = c_ref[...]
    q_bf = q
    log_d = jnp.where(seen, b_col - b_row + i_row, -jnp.inf)
    m_inter = b_col + m_prev
    m_t = jnp.maximum(m_inter, jnp.max(log_d, axis=-1, keepdims=True))
    qk = _dot_nt(q_bf, k.astype(BF16))
    yield
    scores = (qk * jnp.exp(log_d - m_t)).astype(BF16)
    inter_scale = jnp.exp(m_inter - m_t)
    b_last = b_col[last:last + 1, :]
    log_w = b_last - b_col + i_col
    m_new = jnp.maximum(b_last + m_prev, jnp.max(log_w, axis=0, keepdims=True))
    w = jnp.exp(log_w - m_new)
    decay = jnp.exp(b_last + m_prev - m_new)
    kw = (k * w).astype(BF16)
    yield
    numden = _dot(scores, vext) + inter_scale * _dot(q_bf, c_prev.astype(BF16))
    update = _dot_tn(kw, vext)
    yield
    num = numden[:, 0:D_HEAD]
    den = numden[:, D_HEAD:2 * D_HEAD]
    out_ref[0, rs, sl] = num / jnp.maximum(jnp.abs(den), jnp.exp(-m_t))
    c_ref[...] = decay * c_prev + update
    m_ref[...] = jnp.broadcast_to(m_new, (1, LANES))


def _hgrn2_level_small(q3, k3, pre3, suf3, half, rev, sub_iota):
    upper = (sub_iota & half) != 0
    second = jnp.logical_not(upper) if rev else upper
    end = 0 if rev else half - 1
    y = jnp.where((sub_iota & (half - 1)) == end, pre3, 0.0)
    step = 1 if rev else -1
    span = 1
    while span < half:
        y = y + pltpu.roll(y, (step * span) % SUBLANES, 1)
        span *= 2
    if 2 * half == SUBLANES:
        other = pltpu.roll(y, half, 1)
    else:
        other = jnp.where(upper, pltpu.roll(y, half, 1), pltpu.roll(y, SUBLANES - half, 1))
    z = jnp.where(second, q3 * pre3, k3 * suf3)
    return z, pre3 * jnp.where(second, other, 1.0), suf3 * jnp.where(second, 1.0, other)


def _hgrn2_level_big(q, k, pre, suf, half, rev):
    L, width = q.shape
    shape = (L // (2 * half), 2, half, width)
    q4, k4, pre4, suf4 = (a.reshape(shape) for a in (q, k, pre, suf))
    first = 1 if rev else 0
    second = 1 - first
    end = 0 if rev else half - 1
    total_first = pre4[:, first, end:end + 1, :]
    total_second = pre4[:, second, end:end + 1, :]

    def join(at_first, at_second):
        parts = (at_second, at_first) if rev else (at_first, at_second)
        return jnp.stack(parts, axis=1).reshape(L, width)

    z = join(k4[:, first] * suf4[:, first], q4[:, second] * pre4[:, second])
    pre_new = join(pre4[:, first], pre4[:, second] * total_first)
    suf_new = join(suf4[:, first] * total_second, suf4[:, second])
    return z, pre_new, suf_new


def _hgrn2_chunk(q, g, v_bf, rev, level, diag, sub_iota, st_refs, out_ref, rs, sl):
    L, width = q.shape
    heads = [slice(h * D_HEAD, (h + 1) * D_HEAD) for h in range(width // D_HEAD)]
    k = 1.0 - g
    q_bf = q.astype(BF16)
    k_bf = k.astype(BF16)
    att = [jnp.where(diag, _dot_nt(q_bf[:, s], k_bf[:, s]).astype(BF16), jnp.zeros((), BF16)) for s in heads]
    small = (L // SUBLANES, SUBLANES, width)
    q3, k3, pre, suf = q.reshape(small), k.reshape(small), g.reshape(small), jnp.ones(small, F32)
    half = 1
    bit = 0
    while half < L:
        if half == SUBLANES:
            pre, suf = pre.reshape(L, width), suf.reshape(L, width)
        if half < SUBLANES:
            z, pre, suf = _hgrn2_level_small(q3, k3, pre, suf, half, rev, sub_iota)
            z = z.reshape(L, width)
        else:
            z, pre, suf = _hgrn2_level_big(q, k, pre, suf, half, rev)
        z = z.astype(BF16)
        att = [jnp.where(level == bit, _dot_nt(z[:, s], z[:, s]).astype(BF16), a) for a, s in zip(att, heads)]
        half *= 2
        bit += 1
        yield
    last = 0 if rev else L - 1
    q_dec = (q * pre).astype(BF16)
    k_dec = (k * suf).astype(BF16)
    outs = []
    for h, s in enumerate(heads):
        st_prev = st_refs[h][...]
        outs.append(_dot_nt(q_dec[:, s], st_prev.astype(BF16)) + _dot(att[h], v_bf[:, s]))
        st_refs[h][...] = st_prev * pre[last:last + 1, s] + _dot_tn(v_bf[:, s], k_dec[:, s])
    out_ref[0, rs, sl] = jnp.concatenate(outs, axis=1)


def _mixer_kernel(qf_ref, kf_ref, vf_ref, gcf_ref, grf_ref, hqf_ref, hgf_ref, hvf_ref,
                  qb_ref, kb_ref, vb_ref, gcb_ref, grb_ref, hqb_ref, hgb_ref, hvb_ref,
                  hf_ref, of_ref, hb_ref, ob_ref, *state_refs):
    L = CHUNK
    n_state = 2 * N_HEADS
    c_refs, m_refs, st_refs = (state_refs[i * n_state:(i + 1) * n_state] for i in range(3))

    @pl.when(pl.program_id(1) == 0)
    def _():
        for ref in state_refs:
            ref[...] = jnp.zeros_like(ref)

    row = lax.broadcasted_iota(jnp.int32, (L, L), 0)
    col = lax.broadcasted_iota(jnp.int32, (L, L), 1)
    sub_iota = lax.broadcasted_iota(jnp.int32, (L // SUBLANES, SUBLANES, LANES), 1)
    diag = row == col
    diff = row ^ col
    high_bit = jnp.zeros((L, L), jnp.int32)
    half = 2
    while half < L:
        high_bit = high_bit + (diff >= half).astype(jnp.int32)
        half *= 2
    ones = jnp.ones((L, D_HEAD), BF16)

    dirs = (
        (0, qf_ref, kf_ref, vf_ref, gcf_ref, grf_ref, hqf_ref, hgf_ref, hvf_ref, hf_ref, of_ref),
        (1, qb_ref, kb_ref, vb_ref, gcb_ref, grb_ref, hqb_ref, hgb_ref, hvb_ref, hb_ref, ob_ref),
    )
    masks = []
    for rev in (False, True):
        seen = (col >= row) if rev else (col <= row)
        before = (col > row) if rev else (col < row)
        level = jnp.where(before, high_bit, -1)
        tri = seen.astype(BF16)
        tri_t = (row >= col if rev else row <= col).astype(BF16)
        masks.append((seen, level, tri, tri_t))
    n_sub = qf_ref.shape[1] // L
    for step in range(n_sub):
        stages = []
        for d, q_ref, k_ref, v_ref, gc_ref, gr_ref, hq_ref, hg_ref, hv_ref, h_out, o_out in dirs:
            rev = d == 1
            seen, level, tri, tri_t = masks[d]
            last = 0 if rev else L - 1
            sub_chunk = n_sub - 1 - step if rev else step
            rs = slice(sub_chunk * L, (sub_chunk + 1) * L)
            gc = gc_ref[0, rs, :]
            gr = gr_ref[0, :, rs]
            gc_cum = _cumsum_rows(tri, gc)
            gr_cum = _cumsum_lanes(gr, tri_t)
            for hd in range(N_HEADS):
                sl = slice(hd * D_HEAD, (hd + 1) * D_HEAD)
                gi = d * N_HEADS + hd
                gf = 2 * N_HEADS + gi
                idx = d * N_HEADS + hd
                vext = jnp.concatenate([v_ref[0, rs, sl], ones], axis=1)
                stages.append(_mlstm_chunk(
                    q_ref[0, rs, sl], k_ref[0, rs, sl], vext,
                    gc[:, gi:gi + 1], gc_cum[:, gf:gf + 1], gr[gi:gi + 1, :], gr_cum[gf:gf + 1, :],
                    seen, last, c_refs[idx], m_refs[idx], h_out, rs, sl))
                stages.append(_hgrn2_chunk(
                    hq_ref[0, rs, sl], hg_ref[0, rs, sl], hv_ref[0, rs, sl],
                    rev, level, diag, sub_iota, [st_refs[idx]], o_out, rs, sl))
        group = 2 * N_HEADS
        for start in range(0, len(stages), group):
            _run_round_robin(stages[start:start + group])


def _mixer(q, k, v, gcol, grow, hq, g_f, g_b, hv):
    B, T, _ = q.shape
    L = CHUNK * MIXER_CHUNKS
    nc = T // L

    def fwd(width):
        return pl.BlockSpec((1, L, width), lambda b, c: (b, c, 0))

    def bwd(width):
        return pl.BlockSpec((1, L, width), lambda b, c: (b, nc - 1 - c, 0))

    grow_f = pl.BlockSpec((1, N_GATES, L), lambda b, c: (b, 0, c))
    grow_b = pl.BlockSpec((1, N_GATES, L), lambda b, c: (b, 0, nc - 1 - c))
    in_specs = ([fwd(WIDTH)] * 3 + [fwd(LANES), grow_f] + [fwd(WIDTH)] * 3
                + [bwd(WIDTH)] * 3 + [bwd(LANES), grow_b] + [bwd(WIDTH)] * 3)
    out = jax.ShapeDtypeStruct((B, T, WIDTH), F32)
    n_state = 2 * N_HEADS
    return pl.pallas_call(
        _mixer_kernel,
        grid=(B, nc),
        in_specs=in_specs,
        out_specs=[fwd(WIDTH), fwd(WIDTH), bwd(WIDTH), bwd(WIDTH)],
        out_shape=[out, out, out, out],
        scratch_shapes=([pltpu.VMEM((D_HEAD, 2 * D_HEAD), F32)] * n_state
                        + [pltpu.VMEM((1, LANES), F32)] * n_state
                        + [pltpu.VMEM((D_HEAD, D_HEAD), F32)] * n_state),
        compiler_params=pltpu.CompilerParams(
            dimension_semantics=("parallel", "arbitrary"), vmem_limit_bytes=VMEM_LIMIT),
        name="mixer",
    )(q, k, v, gcol, grow, hq, g_f, hv, q, k, v, gcol, grow, hq, g_b, hv)


def _head_norm(hsum, gain):
    parts = []
    for hd in range(N_HEADS):
        hh = hsum[:, hd * D_HEAD:(hd + 1) * D_HEAD]
        parts.append(hh * lax.rsqrt(jnp.mean(hh * hh, axis=-1, keepdims=True) + NORM_EPS))
    return jnp.concatenate(parts, axis=1) * gain


def _merge_tile(r0, hf_ref, hb_ref, of_ref, ob_ref, mo_ref, hgg_ref, x_ref, mn_ref, hn_ref, wo_ref,
                n2_ref, wrh_ref, wrl_ref, br_ref, x1_ref, h2_ref, route_ref, hist_ref):
    rs = slice(r0, r0 + PROJ_ROWS)
    m_out = _head_norm(hf_ref[rs, :] + hb_ref[rs, :], mn_ref[...]) * mo_ref[rs, :]
    hg_out = _head_norm(of_ref[rs, :] + ob_ref[rs, :], hn_ref[...]) * hgg_ref[rs, :]
    mixed = jnp.concatenate([m_out, hg_out], axis=1).astype(BF16)
    yield
    x1 = x_ref[rs, :] + _dot(mixed, wo_ref[...])
    x1_ref[rs, :] = x1
    h2 = _rms(x1, n2_ref[...])
    h2_ref[rs, :] = _pack_bf16_pairs(h2)
    h_hi = h2.astype(BF16)
    h_hi32 = h_hi.astype(F32)
    yield

    h_lo = (h2 - h_hi32).astype(BF16)
    logits = _dot(h_hi, wrh_ref[...]) + _dot(h_lo, wrh_ref[...]) + _dot(h_hi, wrl_ref[...]) + br_ref[...]
    lane = lax.broadcasted_iota(jnp.int32, logits.shape, 1)
    big = jnp.int32(LANES)
    neg = -jnp.inf
    yield
    g_log = jnp.where(lane < N_GROUPS, logits, neg)
    g_max = jnp.max(g_log, axis=-1, keepdims=True)
    g_idx = jnp.min(jnp.where(g_log == g_max, lane, big), axis=-1, keepdims=True)
    g_val = 1.0 / jnp.sum(jnp.exp(g_log - g_max), axis=-1, keepdims=True)
    yield
    e_lo = N_GROUPS + g_idx * EXPERTS_PER_GROUP
    e_log = jnp.where((lane >= e_lo) & (lane < e_lo + EXPERTS_PER_GROUP), logits, neg)
    m1 = jnp.max(e_log, axis=-1, keepdims=True)
    i1 = jnp.min(jnp.where(e_log == m1, lane, big), axis=-1, keepdims=True)
    yield
    e_log2 = jnp.where(lane == i1, neg, e_log)
    m2 = jnp.max(e_log2, axis=-1, keepdims=True)
    i2 = jnp.min(jnp.where(e_log2 == m2, lane, big), axis=-1, keepdims=True)
    r2 = jnp.exp(m2 - m1)
    w1 = g_val / (1.0 + r2)
    w2 = g_val * r2 / (1.0 + r2)
    yield
    rows = logits.shape[0]
    pick0 = lane == i1 - N_GROUPS
    pick1 = lane == i2 - N_GROUPS
    earlier = (lax.broadcasted_iota(jnp.int32, (rows, rows), 1)
               < lax.broadcasted_iota(jnp.int32, (rows, rows), 0)).astype(BF16)
    cnt0 = jnp.sum(pick0.astype(F32), axis=0, keepdims=True)
    cnt1 = jnp.sum(pick1.astype(F32), axis=0, keepdims=True)
    rank0 = jnp.sum(jnp.where(pick0, _dot(earlier, pick0.astype(BF16)), 0.0), axis=-1, keepdims=True)
    rank1 = jnp.sum(jnp.where(pick1, _dot(earlier, pick1.astype(BF16)) + cnt0, 0.0), axis=-1, keepdims=True)
    yield
    columns = ((i1 - N_GROUPS).astype(F32), (i2 - N_GROUPS).astype(F32), w1, w2, rank0, rank1)
    route = jnp.zeros_like(logits)
    for c, value in enumerate(columns):
        route = jnp.where(lane == c, value, route)
    route_ref[rs, :] = route
    hs = slice(r0 // PROJ_ROWS * SUBLANES, (r0 // PROJ_ROWS + 1) * SUBLANES)
    sub = lax.broadcasted_iota(jnp.int32, (SUBLANES, LANES), 0)
    hist_ref[hs, :] = jnp.where(sub == 0, cnt0 + cnt1, 0.0)


N_MERGE_STREAMS = 7


def _merge_kernel(*refs, n_a):
    side_a = refs[0:N_MERGE_STREAMS]
    side_b = refs[N_MERGE_STREAMS:2 * N_MERGE_STREAMS]
    rest = refs[2 * N_MERGE_STREAMS:]

    def block(side):
        _run_round_robin([_merge_tile(r0, *side, *rest) for r0 in range(0, MERGE_ROWS, PROJ_ROWS)])

    @pl.when(pl.program_id(0) < n_a)
    def _():
        block(side_a)

    @pl.when(pl.program_id(0) >= n_a)
    def _():
        block(side_b)


def _merge(streams_a, streams_b, m_norm, hg_norm, w_out, norm2, w_rg, b_rg, w_re, b_re):
    D = D_MODEL
    rows = MERGE_ROWS
    n_a = streams_a[0].shape[0] // rows
    n_b = streams_b[0].shape[0] // rows
    n_all = (n_a + n_b) * rows
    n_log = N_GROUPS + N_EXPERTS
    wr = jnp.pad(jnp.concatenate([w_rg, w_re], axis=1), ((0, 0), (0, LANES - n_log)))
    br = jnp.pad(jnp.concatenate([b_rg, b_re]), (0, LANES - n_log))[None, :]
    wr_hi = wr.astype(BF16)
    wr_lo = (wr - wr_hi.astype(F32)).astype(BF16)
    consts = [m_norm[None, :], hg_norm[None, :], w_out.astype(BF16), norm2[None, :], wr_hi, wr_lo, br]

    def full(arr):
        nd = arr.ndim
        return pl.BlockSpec(arr.shape, lambda i: (0,) * nd)

    def side_a(arr):
        return pl.BlockSpec((rows, arr.shape[1]), lambda i: (jnp.minimum(i, n_a - 1), 0))

    def side_b(arr):
        return pl.BlockSpec((rows, arr.shape[1]), lambda i: (jnp.maximum(i - n_a, 0), 0))

    def out(width):
        return pl.BlockSpec((rows, width), lambda i: (i, 0))

    return pl.pallas_call(
        functools.partial(_merge_kernel, n_a=n_a),
        grid=(n_a + n_b,),
        in_specs=[side_a(s) for s in streams_a] + [side_b(s) for s in streams_b] + [full(c) for c in consts],
        out_specs=[out(D), out(D // 2), out(LANES), pl.BlockSpec((rows // PROJ_ROWS * SUBLANES, LANES), lambda i: (i, 0))],
        out_shape=[jax.ShapeDtypeStruct((n_all, D), F32), jax.ShapeDtypeStruct((n_all, D // 2), jnp.uint32),
                   jax.ShapeDtypeStruct((n_all, LANES), F32),
                   jax.ShapeDtypeStruct((n_all // PROJ_ROWS * SUBLANES, LANES), F32)],
        compiler_params=pltpu.CompilerParams(
            dimension_semantics=("arbitrary",), vmem_limit_bytes=VMEM_LIMIT),
        name="merge",
    )(*streams_a, *streams_b, *consts)


def _sc_row_mover(src, idx, n_out, scatter, name):
    n_moved = idx.shape[0]
    D = src.shape[1]
    n_sub = SC_CORES * SC_SUBCORES
    per = n_moved // n_sub
    window = SC_WINDOW_BYTES // (D * src.dtype.itemsize)
    assert per * n_sub == n_moved and per % window == 0, (n_moved, per, window)
    assert not scatter or src.shape[0] % per == 0, (src.shape, per)
    mesh = plsc.VectorSubcoreMesh(core_axis_name="c", subcore_axis_name="s",
                                  num_cores=SC_CORES, num_subcores=SC_SUBCORES)

    def body(src_hbm, idx_hbm, out_hbm, idx_v, buf):
        base = (lax.axis_index("c") * SC_SUBCORES + lax.axis_index("s")) * per
        pltpu.sync_copy(idx_hbm.at[pl.ds(base, per)], idx_v)

        @pl.loop(0, per // window)
        def _(j):
            linear = pl.ds(base + j * window, window)
            indexed = idx_v.at[pl.ds(j * window, window)]
            if scatter:
                pltpu.sync_copy(src_hbm.at[pl.ds(lax.rem(base, src.shape[0]) + j * window, window)], buf)
                pltpu.sync_copy(buf, out_hbm.at[indexed])
            else:
                pltpu.sync_copy(src_hbm.at[indexed], buf)
                pltpu.sync_copy(buf, out_hbm.at[linear])

    return pl.kernel(
        body,
        out_type=jax.ShapeDtypeStruct((n_out, D), src.dtype),
        mesh=mesh,
        scratch_types=[pltpu.VMEM((per,), jnp.int32), pltpu.VMEM((window, D), src.dtype)],
        name=name,
    )(src, idx)


def _sc_gather_rows(src, idx):
    return _sc_row_mover(src, idx, idx.shape[0], False, "sc_gather_rows")


def _sc_scatter_rows(src, idx, n_out):
    return _sc_row_mover(src, idx, n_out, True, "sc_scatter_rows")


def _expert_kernel(be_ref, nu_ref, x_ref, w1_ref, w3_ref, w2_ref, o_ref, w1_bf, w3_bf, w2_bf):
    i = pl.program_id(0)
    active = i < nu_ref[0]
    new_expert = (i == 0) | (be_ref[i] != be_ref[jnp.maximum(i - 1, 0)])

    @pl.when(active & new_expert)
    def _():
        w1_bf[...] = w1_ref[0].astype(BF16)
        w3_bf[...] = w3_ref[0].astype(BF16)
        w2_bf[...] = w2_ref[0].astype(BF16)

    @pl.when(active)
    def _():
        half = D_MODEL // 2
        x_lo, x_hi = (part.astype(BF16) for part in _unpack_bf16_pairs(x_ref[...]))
        a = _dot(x_lo, w1_bf[0:half, :]) + _dot(x_hi, w1_bf[half:, :])
        b = _dot(x_lo, w3_bf[0:half, :]) + _dot(x_hi, w3_bf[half:, :])
        o_ref[...] = _pack_bf16_pairs(_dot((_silu(a) * b).astype(BF16), w2_bf[...]))


def _experts(xs, block_e, n_used, w1, w3, w2):
    rows = EXPERT_ROWS
    n_blocks = xs.shape[0] // rows
    D = D_MODEL

    def blk(i, be, nu):
        return jnp.minimum(i, nu[0] - 1)

    grid_spec = pltpu.PrefetchScalarGridSpec(
        num_scalar_prefetch=2,
        grid=(n_blocks,),
        in_specs=[
            pl.BlockSpec((rows, D // 2), lambda i, be, nu: (blk(i, be, nu), 0)),
            pl.BlockSpec((1, D, EXPERT_FF), lambda i, be, nu: (be[blk(i, be, nu)], 0, 0)),
            pl.BlockSpec((1, D, EXPERT_FF), lambda i, be, nu: (be[blk(i, be, nu)], 0, 0)),
            pl.BlockSpec((1, EXPERT_FF, D), lambda i, be, nu: (be[blk(i, be, nu)], 0, 0)),
        ],
        out_specs=pl.BlockSpec((rows, D // 2), lambda i, be, nu: (blk(i, be, nu), 0)),
        scratch_shapes=[pltpu.VMEM((D, EXPERT_FF), BF16), pltpu.VMEM((D, EXPERT_FF), BF16),
                        pltpu.VMEM((EXPERT_FF, D), BF16)],
    )
    return pl.pallas_call(
        _expert_kernel,
        grid_spec=grid_spec,
        out_shape=jax.ShapeDtypeStruct((xs.shape[0], D // 2), jnp.uint32),
        compiler_params=pltpu.CompilerParams(
            dimension_semantics=("arbitrary",), vmem_limit_bytes=VMEM_LIMIT),
        name="experts",
    )(block_e, n_used, xs, w1, w3, w2)


def _combine_kernel(y0_ref, y1_ref, x1_ref, route_ref, nf_ref, y_ref):
    route = route_ref[...]
    r0 = jnp.concatenate(_unpack_bf16_pairs(y0_ref[...]), axis=1)
    r1 = jnp.concatenate(_unpack_bf16_pairs(y1_ref[...]), axis=1)
    y_ref[...] = _rms(x1_ref[...] + route[:, 2:3] * r0 + route[:, 3:4] * r1, nf_ref[...])


def _combine(x1, route, y_rows, norm_f, tok0):
    D = x1.shape[1]
    n = y_rows.shape[0] // TOP_K
    rows = COMBINE_ROWS
    assert n % rows == 0 and tok0 % rows == 0, (n, tok0, rows)
    nt = n // rows
    first = tok0 // rows

    def tok(width, offset=0):
        return pl.BlockSpec((rows, width), lambda i: (i + offset, 0))

    return pl.pallas_call(
        _combine_kernel,
        grid=(nt,),
        in_specs=[tok(D // 2), tok(D // 2, nt), tok(D, first), tok(LANES, first),
                  pl.BlockSpec((1, D), lambda i: (0, 0))],
        out_specs=tok(D),
        out_shape=jax.ShapeDtypeStruct((n, D), F32),
        compiler_params=pltpu.CompilerParams(
            dimension_semantics=("parallel",), vmem_limit_bytes=VMEM_LIMIT),
        name="combine",
    )(y_rows, y_rows, x1, route, norm_f[None, :])


def _plan_kernel(route_ref, table_ref, dest_ref):
    lane = lax.broadcasted_iota(jnp.int32, (PROJ_ROWS, LANES), 1)
    lane_f = lane.astype(F32)
    for tile in range(route_ref.shape[0] // PROJ_ROWS):
        rs = slice(tile * PROJ_ROWS, (tile + 1) * PROJ_ROWS)
        route = route_ref[rs, :]
        first = table_ref[tile * SUBLANES:tile * SUBLANES + 1, :]
        d0 = jnp.sum(jnp.where(lane_f == route[:, 0:1], first, 0.0), axis=-1, keepdims=True) + route[:, 4:5]
        d1 = jnp.sum(jnp.where(lane_f == route[:, 1:2], first, 0.0), axis=-1, keepdims=True) + route[:, 5:6]
        cols = jnp.where(lane == 0, d0, jnp.where(lane == 1, d1, 0.0))
        dest_ref[:, rs] = cols.T[0:SUBLANES, :].astype(jnp.int32)


def _dispatch_plan(route, hist):
    N = route.shape[0]
    rows = PROJ_ROWS
    n_tiles = N // rows
    blk = EXPERT_ROWS
    tile_counts = hist.reshape(n_tiles, SUBLANES, LANES)[:, 0, 0:N_EXPERTS].astype(jnp.int32)
    tile_first = jnp.cumsum(tile_counts, axis=0) - tile_counts
    counts = jnp.sum(tile_counts, axis=0)
    padded = ((counts + blk - 1) // blk) * blk
    pad_end = jnp.cumsum(padded)
    pad_start = pad_end - padded
    table = jnp.zeros((n_tiles, SUBLANES, LANES), F32).at[:, 0, 0:N_EXPERTS].set(
        (pad_start[None, :] + tile_first).astype(F32)).reshape(n_tiles * SUBLANES, LANES)
    per_step = PLAN_TILES
    assert n_tiles % per_step == 0, (n_tiles, per_step)
    dest_rows = pl.pallas_call(
        _plan_kernel,
        grid=(n_tiles // per_step,),
        in_specs=[pl.BlockSpec((per_step * rows, LANES), lambda i: (i, 0)),
                  pl.BlockSpec((per_step * SUBLANES, LANES), lambda i: (i, 0))],
        out_specs=pl.BlockSpec((SUBLANES, per_step * rows), lambda i: (0, i)),
        out_shape=jax.ShapeDtypeStruct((SUBLANES, N), jnp.int32),
        compiler_params=pltpu.CompilerParams(dimension_semantics=("parallel",)),
        name="plan",
    )(route, table)
    dest_rows = dest_rows[0:TOP_K]
    n_blocks = (TOP_K * N + N_EXPERTS * (blk - 1) + blk - 1) // blk
    block_start = jnp.arange(n_blocks, dtype=jnp.int32) * blk
    block_e = jnp.sum((pad_end[None, :] <= block_start[:, None]).astype(jnp.int32), axis=1)
    block_e = jnp.minimum(block_e, N_EXPERTS - 1)
    n_used = (pad_end[-1] // blk).astype(jnp.int32).reshape(1)
    return dest_rows, block_e, n_used, n_blocks * blk


def _token_mixer(x, norm1, w_in, b_in, conv_w, conv_b, m_fgate_bias, hg_lb_logits):
    B, T, D = x.shape
    q, k, v, mo, gcol, grow, hq, g_f, g_b, hv, hgg = _in_proj(
        x, norm1, w_in, b_in, conv_w, conv_b, m_fgate_bias, hg_lb_logits)
    h_f, o_f, h_b, o_b = _mixer(q, k, v, gcol, grow, hq, g_f, g_b, hv)
    return [a.reshape(B * T, a.shape[-1]) for a in (h_f, h_b, o_f, o_b, mo, hgg, x)]


def kernel(x_prompt, x_sample, norm1, w_in, b_in, conv_w, conv_b, m_fgate_bias, m_norm, hg_lb_logits, hg_norm,
           w_out, norm2, w_router_group, b_router_group, w_router_expert, b_router_expert, w1, w3, w2, norm_f):
    layer = 0
    mixer_args = (norm1[layer], w_in[layer], b_in[layer], conv_w[layer], conv_b[layer], m_fgate_bias[layer],
                  hg_lb_logits)
    streams_p = _token_mixer(x_prompt, *mixer_args)
    streams_s = _token_mixer(x_sample, *mixer_args)
    x1, h2, route, hist = _merge(streams_p, streams_s, m_norm[layer], hg_norm[layer], w_out[layer], norm2[layer],
                                 w_router_group[layer], b_router_group[layer], w_router_expert[layer],
                                 b_router_expert[layer])
    dest, block_e, n_used, n_rows = _dispatch_plan(route, hist)
    xs = _sc_scatter_rows(h2, dest.reshape(TOP_K * h2.shape[0]), n_rows)
    out_rows = _experts(xs, block_e, n_used, w1[layer], w3[layer], w2[layer])
    outs = []
    tok0 = 0
    for x in (x_prompt, x_sample):
        n = x.shape[0] * x.shape[1]
        y_rows = _sc_gather_rows(out_rows, dest[:, tok0:tok0 + n].reshape(TOP_K * n))
        outs.append(_combine(x1, route, y_rows, norm_f, tok0).reshape(x.shape))
        tok0 += n
    return tuple(outs)
```

```python
import functools

import jax
import jax.numpy as jnp
from jax import lax
from jax.experimental import pallas as pl
from jax.experimental.pallas import tpu as pltpu
from jax.experimental.pallas import tpu_sc as plsc

F32 = jnp.float32
BF16 = jnp.bfloat16

D_MODEL = 1024
N_HEADS = 4
D_HEAD = 128
WIDTH = N_HEADS * D_HEAD
CONV_K = 5
CONV_PAD = CONV_K // 2
N_GROUPS = 4
EXPERTS_PER_GROUP = 8
N_EXPERTS = N_GROUPS * EXPERTS_PER_GROUP
TOP_K = 2
EXPERT_FF = D_MODEL // 2
NORM_EPS = 1e-6

LANES = 128
SUBLANES = 8
CHUNK = 128
MIXER_CHUNKS = 2
PROJ_ROWS = 256
COMBINE_ROWS = 512
MERGE_ROWS = 512
PLAN_TILES = 8
IN_PROJ_ROWS = 512
IN_PROJ_STREAM = 256
HALO = SUBLANES
EXPERT_ROWS = 512
N_GATES = 4 * N_HEADS
SC_CORES = 2
SC_SUBCORES = 16
SC_WINDOW_BYTES = 128 * 1024
VMEM_LIMIT = 56 * 1024 * 1024


def _dot(a, b):
    return jnp.dot(a, b, preferred_element_type=F32)


def _dot_nt(a, b):
    return lax.dot_general(a, b, (((1,), (1,)), ((), ())), preferred_element_type=F32)


def _dot_tn(a, b):
    return lax.dot_general(a, b, (((0,), (0,)), ((), ())), preferred_element_type=F32)


def _split3(x):
    hi = x.astype(BF16)
    r1 = x - hi.astype(F32)
    mid = r1.astype(BF16)
    lo = (r1 - mid.astype(F32)).astype(BF16)
    return hi, mid, lo


def _pack_bf16_pairs(x):
    half = x.shape[1] // 2
    bits = lax.bitcast_convert_type(x.astype(BF16).astype(F32), jnp.uint32)
    return (bits[:, half:] & jnp.uint32(0xFFFF0000)) | (bits[:, :half] >> 16)


def _unpack_bf16_pairs(words):
    lo = lax.bitcast_convert_type(words << 16, F32)
    hi = lax.bitcast_convert_type(words & jnp.uint32(0xFFFF0000), F32)
    return lo, hi


def _silu(x):
    return x * jax.nn.sigmoid(x)


def _log_sigmoid(x):
    return -(jnp.maximum(-x, 0.0) + jnp.log1p(jnp.exp(-jnp.abs(x))))


def _rms(x, gain):
    return x * lax.rsqrt(jnp.mean(x * x, axis=-1, keepdims=True) + NORM_EPS) * gain


def _run_round_robin(generators):
    live = list(generators)
    while live:
        for gen in list(live):
            try:
                next(gen)
            except StopIteration:
                live.remove(gen)


def _in_proj_kernel(x_ref, xp_ref, xn_ref, n1_ref, wa_ref, ba_ref, wg_ref, bg_ref, wgt_ref, bgt_ref,
                    fbrow_ref, fbcol_ref, wh_ref, bh_ref, cw_ref, cb_ref, lbl_ref,
                    q_ref, k_ref, v_ref, mo_ref, gcol_ref, grow_ref, hq_ref, gf_ref, gb_ref, hv_ref, hgg_ref,
                    ext_ref):
    t = pl.program_id(1)
    nt = pl.num_programs(1)
    rows = x_ref.shape[1]
    gain = n1_ref[...]

    lbl = lbl_ref[...]
    lmax = jnp.max(lbl, axis=0, keepdims=True)
    le = jnp.exp(lbl - lmax)
    lb = le[0:1, :] / jnp.sum(le, axis=0, keepdims=True)

    wqk = wa_ref[:, 0:2 * WIDTH]
    bqk = ba_ref[:, 0:2 * WIDTH]
    hp = _rms(xp_ref[0], gain).astype(BF16)
    hn = _rms(xn_ref[0], gain).astype(BF16)
    ext_ref[0:HALO, :] = (_dot(hp, wqk) + bqk) * (t > 0).astype(F32)
    ext_ref[HALO + rows:2 * HALO + rows, :] = (_dot(hn, wqk) + bqk) * (t < nt - 1).astype(F32)

    def stream(r0, n):
        rs = slice(r0, r0 + n)
        h = _rms(x_ref[0, rs, :], gain).astype(BF16)

        def proj(w_ref, b_ref, lo, hi):
            return _dot(h, w_ref[:, lo:hi]) + b_ref[:, lo:hi]

        ext_ref[HALO + r0:HALO + r0 + n, :] = proj(wa_ref, ba_ref, 0, 2 * WIDTH)
        hq_pre = proj(wh_ref, bh_ref, 0, WIDTH)
        yield
        acc = cb_ref[...] + ext_ref[pl.ds(HALO - CONV_PAD + r0, n), :] * cw_ref[0:1, :]
        for j in range(1, CONV_K):
            acc = acc + ext_ref[pl.ds(HALO - CONV_PAD + j + r0, n), :] * cw_ref[j:j + 1, :]
        qk = _silu(acc)
        q_ref[0, rs, :] = (qk[:, 0:WIDTH] * (D_HEAD ** -0.5)).astype(BF16)
        k_ref[0, rs, :] = qk[:, WIDTH:2 * WIDTH]
        v_ref[0, rs, :] = proj(wa_ref, ba_ref, 2 * WIDTH, 3 * WIDTH).astype(BF16)
        yield
        hq_ref[0, rs, :] = _silu(hq_pre)
        mo_ref[0, rs, :] = jax.nn.sigmoid(proj(wa_ref, ba_ref, 3 * WIDTH, 4 * WIDTH))
        yield
        gf_ref[0, rs, :] = lb + (1.0 - lb) * jax.nn.sigmoid(proj(wh_ref, bh_ref, WIDTH, 2 * WIDTH))
        yield
        gb_ref[0, rs, :] = lb + (1.0 - lb) * jax.nn.sigmoid(proj(wh_ref, bh_ref, 2 * WIDTH, 3 * WIDTH))
        hv_ref[0, rs, :] = proj(wh_ref, bh_ref, 3 * WIDTH, 4 * WIDTH).astype(BF16)
        yield
        hgg_ref[0, rs, :] = _silu(proj(wh_ref, bh_ref, 4 * WIDTH, 5 * WIDTH))
        gc = _dot(h, wg_ref[...]) + bg_ref[...]
        lane = lax.broadcasted_iota(jnp.int32, gc.shape, 1)
        is_f = (lane >= 2 * N_HEADS) & (lane < N_GATES)
        gcol_ref[0, rs, :] = jnp.where(is_f, _log_sigmoid(gc + fbrow_ref[...]), gc)
        gr = _dot_nt(wgt_ref[...], h) + bgt_ref[...]
        sub = lax.broadcasted_iota(jnp.int32, gr.shape, 0)
        grow_ref[0, :, rs] = jnp.where(sub >= 2 * N_HEADS, _log_sigmoid(gr + fbcol_ref[...]), gr)

    _run_round_robin([stream(r0, IN_PROJ_STREAM) for r0 in range(0, rows, IN_PROJ_STREAM)])


def _in_proj(x, norm1, w_in, b_in, conv_w, conv_b, fgate_bias, lb_logits):
    B, T, D = x.shape
    rows = IN_PROJ_ROWS
    assert T % rows == 0, (T, rows)
    nt = T // rows
    a_w = 4 * WIDTH
    wa = w_in[:, 0:a_w].astype(BF16)
    ba = b_in[None, 0:a_w]
    wg32 = jnp.pad(w_in[:, a_w:a_w + N_GATES], ((0, 0), (0, LANES - N_GATES)))
    bg = jnp.pad(b_in[a_w:a_w + N_GATES], (0, LANES - N_GATES))[None, :]
    wg = wg32.astype(BF16)
    wgt = w_in[:, a_w:a_w + N_GATES].T.astype(BF16)
    bgt = b_in[a_w:a_w + N_GATES][:, None]
    fb = fgate_bias.reshape(2 * N_HEADS)
    fbrow = jnp.zeros((1, LANES), F32).at[0, 2 * N_HEADS:N_GATES].set(fb)
    fbcol = jnp.zeros((N_GATES, 1), F32).at[2 * N_HEADS:N_GATES, 0].set(fb)
    wh = w_in[:, a_w + N_GATES:].astype(BF16)
    bh = b_in[None, a_w + N_GATES:]

    tiles_per_halo = rows // HALO
    n_halo = T // HALO

    def full(arr):
        nd = arr.ndim
        return pl.BlockSpec(arr.shape, lambda b, t: (0,) * nd)

    def tok(width):
        return pl.BlockSpec((1, rows, width), lambda b, t: (b, t, 0))

    in_specs = [
        tok(D),
        pl.BlockSpec((1, HALO, D), lambda b, t: (b, jnp.maximum(t * tiles_per_halo - 1, 0), 0)),
        pl.BlockSpec((1, HALO, D), lambda b, t: (b, jnp.minimum((t + 1) * tiles_per_halo, n_halo - 1), 0)),
    ]
    consts = [norm1[None, :], wa, ba, wg, bg, wgt, bgt, fbrow, fbcol, wh, bh, conv_w, conv_b[None, :], lb_logits]
    in_specs += [full(c) for c in consts]
    tok_out = jax.ShapeDtypeStruct((B, T, WIDTH), F32)
    tok_bf = jax.ShapeDtypeStruct((B, T, WIDTH), BF16)
    out_shape = [tok_bf, tok_out, tok_bf, tok_out,
                 jax.ShapeDtypeStruct((B, T, LANES), F32),
                 jax.ShapeDtypeStruct((B, N_GATES, T), F32),
                 tok_out, tok_out, tok_out, tok_bf, tok_out]
    out_specs = [tok(WIDTH)] * 4 + [tok(LANES), pl.BlockSpec((1, N_GATES, rows), lambda b, t: (b, 0, t))] + [tok(WIDTH)] * 5
    return pl.pallas_call(
        _in_proj_kernel,
        grid=(B, nt),
        in_specs=in_specs,
        out_specs=out_specs,
        out_shape=out_shape,
        scratch_shapes=[pltpu.VMEM((rows + 2 * HALO, 2 * WIDTH), F32)],
        compiler_params=pltpu.CompilerParams(
            dimension_semantics=("parallel", "parallel"), vmem_limit_bytes=VMEM_LIMIT),
        name="in_proj",
    )(x, x, x, *consts)


def _cumsum_rows(tri_bf, x):
    hi, mid, lo = _split3(x)
    return _dot(tri_bf, hi) + _dot(tri_bf, mid) + _dot(tri_bf, lo)


def _cumsum_lanes(x, tri_bf):
    hi, mid, lo = _split3(x)
    return _dot(hi, tri_bf) + _dot(mid, tri_bf) + _dot(lo, tri_bf)


def _mlstm_chunk(q, k, vext, i_col, b_col, i_row, b_row, seen, last, c_ref, m_ref, out_ref, rs, sl):
    m_prev = m_ref[:, 0:1]
    c_prev = c_ref[...]
    q_bf = q
    log_d = jnp.where(seen, b_col - b_row + i_row, -jnp.inf)
    m_inter = b_col + m_prev
    m_t = jnp.maximum(m_inter, jnp.max(log_d, axis=-1, keepdims=True))
    qk = _dot_nt(q_bf, k.astype(BF16))
    yield
    scores = (qk * jnp.exp(log_d - m_t)).astype(BF16)
    inter_scale = jnp.exp(m_inter - m_t)
    b_last = b_col[last:last + 1, :]
    log_w = b_last - b_col + i_col
    m_new = jnp.maximum(b_last + m_prev, jnp.max(log_w, axis=0, keepdims=True))
    w = jnp.exp(log_w - m_new)
    decay = jnp.exp(b_last + m_prev - m_new)
    kw = (k * w).astype(BF16)
    yield
    numden = _dot(scores, vext) + inter_scale * _dot(q_bf, c_prev.astype(BF16))
    update = _dot_tn(kw, vext)
    yield
    num = numden[:, 0:D_HEAD]
    den = numden[:, D_HEAD:2 * D_HEAD]
    out_ref[0, rs, sl] = num / jnp.maximum(jnp.abs(den), jnp.exp(-m_t))
    c_ref[...] = decay * c_prev + update
    m_ref[...] = jnp.broadcast_to(m_new, (1, LANES))


def _hgrn2_level_small(q3, k3, pre3, suf3, half, rev, sub_iota):
    upper = (sub_iota & half) != 0
    second = jnp.logical_not(upper) if rev else upper
    end = 0 if rev else half - 1
    y = jnp.where((sub_iota & (half - 1)) == end, pre3, 0.0)
    step = 1 if rev else -1
    span = 1
    while span < half:
        y = y + pltpu.roll(y, (step * span) % SUBLANES, 1)
        span *= 2
    if 2 * half == SUBLANES:
        other = pltpu.roll(y, half, 1)
    else:
        other = jnp.where(upper, pltpu.roll(y, half, 1), pltpu.roll(y, SUBLANES - half, 1))
    z = jnp.where(second, q3 * pre3, k3 * suf3)
    return z, pre3 * jnp.where(second, other, 1.0), suf3 * jnp.where(second, 1.0, other)


def _hgrn2_level_big(q, k, pre, suf, half, rev):
    L, width = q.shape
    shape = (L // (2 * half), 2, half, width)
    q4, k4, pre4, suf4 = (a.reshape(shape) for a in (q, k, pre, suf))
    first = 1 if rev else 0
    second = 1 - first
    end = 0 if rev else half - 1
    total_first = pre4[:, first, end:end + 1, :]
    total_second = pre4[:, second, end:end + 1, :]

    def join(at_first, at_second):
        parts = (at_second, at_first) if rev else (at_first, at_second)
        return jnp.stack(parts, axis=1).reshape(L, width)

    z = join(k4[:, first] * suf4[:, first], q4[:, second] * pre4[:, second])
    pre_new = join(pre4[:, first], pre4[:, second] * total_first)
    suf_new = join(suf4[:, first] * total_second, suf4[:, second])
    return z, pre_new, suf_new


def _hgrn2_chunk(q, g, v_bf, rev, level, diag, sub_iota, st_refs, out_ref, rs, sl):
    L, width = q.shape
    heads = [slice(h * D_HEAD, (h + 1) * D_HEAD) for h in range(width // D_HEAD)]
    k = 1.0 - g
    q_bf = q.astype(BF16)
    k_bf = k.astype(BF16)
    att = [jnp.where(diag, _dot_nt(q_bf[:, s], k_bf[:, s]).astype(BF16), jnp.zeros((), BF16)) for s in heads]
    small = (L // SUBLANES, SUBLANES, width)
    q3, k3, pre, suf = q.reshape(small), k.reshape(small), g.reshape(small), jnp.ones(small, F32)
    half = 1
    bit = 0
    while half < L:
        if half == SUBLANES:
            pre, suf = pre.reshape(L, width), suf.reshape(L, width)
        if half < SUBLANES:
            z, pre, suf = _hgrn2_level_small(q3, k3, pre, suf, half, rev, sub_iota)
            z = z.reshape(L, width)
        else:
            z, pre, suf = _hgrn2_level_big(q, k, pre, suf, half, rev)
        z = z.astype(BF16)
        att = [jnp.where(level == bit, _dot_nt(z[:, s], z[:, s]).astype(BF16), a) for a, s in zip(att, heads)]
        half *= 2
        bit += 1
        yield
    last = 0 if rev else L - 1
    q_dec = (q * pre).astype(BF16)
    k_dec = (k * suf).astype(BF16)
    outs = []
    for h, s in enumerate(heads):
        st_prev = st_refs[h][...]
        outs.append(_dot_nt(q_dec[:, s], st_prev.astype(BF16)) + _dot(att[h], v_bf[:, s]))
        st_refs[h][...] = st_prev * pre[last:last + 1, s] + _dot_tn(v_bf[:, s], k_dec[:, s])
    out_ref[0, rs, sl] = jnp.concatenate(outs, axis=1)


def _mixer_kernel(qf_ref, kf_ref, vf_ref, gcf_ref, grf_ref, hqf_ref, hgf_ref, hvf_ref,
                  qb_ref, kb_ref, vb_ref, gcb_ref, grb_ref, hqb_ref, hgb_ref, hvb_ref,
                  hf_ref, of_ref, hb_ref, ob_ref, *state_refs):
    L = CHUNK
    n_state = 2 * N_HEADS
    c_refs, m_refs, st_refs = (state_refs[i * n_state:(i + 1) * n_state] for i in range(3))

    @pl.when(pl.program_id(1) == 0)
    def _():
        for ref in state_refs:
            ref[...] = jnp.zeros_like(ref)

    row = lax.broadcasted_iota(jnp.int32, (L, L), 0)
    col = lax.broadcasted_iota(jnp.int32, (L, L), 1)
    sub_iota = lax.broadcasted_iota(jnp.int32, (L // SUBLANES, SUBLANES, LANES), 1)
    diag = row == col
    diff = row ^ col
    high_bit = jnp.zeros((L, L), jnp.int32)
    half = 2
    while half < L:
        high_bit = high_bit + (diff >= half).astype(jnp.int32)
        half *= 2
    ones = jnp.ones((L, D_HEAD), BF16)

    dirs = (
        (0, qf_ref, kf_ref, vf_ref, gcf_ref, grf_ref, hqf_ref, hgf_ref, hvf_ref, hf_ref, of_ref),
        (1, qb_ref, kb_ref, vb_ref, gcb_ref, grb_ref, hqb_ref, hgb_ref, hvb_ref, hb_ref, ob_ref),
    )
    masks = []
    for rev in (False, True):
        seen = (col >= row) if rev else (col <= row)
        before = (col > row) if rev else (col < row)
        level = jnp.where(before, high_bit, -1)
        tri = seen.astype(BF16)
        tri_t = (row >= col if rev else row <= col).astype(BF16)
        masks.append((seen, level, tri, tri_t))
    n_sub = qf_ref.shape[1] // L
    for step in range(n_sub):
        stages = []
        for d, q_ref, k_ref, v_ref, gc_ref, gr_ref, hq_ref, hg_ref, hv_ref, h_out, o_out in dirs:
            rev = d == 1
            seen, level, tri, tri_t = masks[d]
            last = 0 if rev else L - 1
            sub_chunk = n_sub - 1 - step if rev else step
            rs = slice(sub_chunk * L, (sub_chunk + 1) * L)
            gc = gc_ref[0, rs, :]
            gr = gr_ref[0, :, rs]
            gc_cum = _cumsum_rows(tri, gc)
            gr_cum = _cumsum_lanes(gr, tri_t)
            for hd in range(N_HEADS):
                sl = slice(hd * D_HEAD, (hd + 1) * D_HEAD)
                gi = d * N_HEADS + hd
                gf = 2 * N_HEADS + gi
                idx = d * N_HEADS + hd
                vext = jnp.concatenate([v_ref[0, rs, sl], ones], axis=1)
                stages.append(_mlstm_chunk(
                    q_ref[0, rs, sl], k_ref[0, rs, sl], vext,
                    gc[:, gi:gi + 1], gc_cum[:, gf:gf + 1], gr[gi:gi + 1, :], gr_cum[gf:gf + 1, :],
                    seen, last, c_refs[idx], m_refs[idx], h_out, rs, sl))
                stages.append(_hgrn2_chunk(
                    hq_ref[0, rs, sl], hg_ref[0, rs, sl], hv_ref[0, rs, sl],
                    rev, level, diag, sub_iota, [st_refs[idx]], o_out, rs, sl))
        group = 2 * N_HEADS
        for start in range(0, len(stages), group):
            _run_round_robin(stages[start:start + group])


def _mixer(q, k, v, gcol, grow, hq, g_f, g_b, hv):
    B, T, _ = q.shape
    L = CHUNK * MIXER_CHUNKS
    nc = T // L

    def fwd(width):
        return pl.BlockSpec((1, L, width), lambda b, c: (b, c, 0))

    def bwd(width):
        return pl.BlockSpec((1, L, width), lambda b, c: (b, nc - 1 - c, 0))

    grow_f = pl.BlockSpec((1, N_GATES, L), lambda b, c: (b, 0, c))
    grow_b = pl.BlockSpec((1, N_GATES, L), lambda b, c: (b, 0, nc - 1 - c))
    in_specs = ([fwd(WIDTH)] * 3 + [fwd(LANES), grow_f] + [fwd(WIDTH)] * 3
                + [bwd(WIDTH)] * 3 + [bwd(LANES), grow_b] + [bwd(WIDTH)] * 3)
    out = jax.ShapeDtypeStruct((B, T, WIDTH), F32)
    n_state = 2 * N_HEADS
    return pl.pallas_call(
        _mixer_kernel,
        grid=(B, nc),
        in_specs=in_specs,
        out_specs=[fwd(WIDTH), fwd(WIDTH), bwd(WIDTH), bwd(WIDTH)],
        out_shape=[out, out, out, out],
        scratch_shapes=([pltpu.VMEM((D_HEAD, 2 * D_HEAD), F32)] * n_state
                        + [pltpu.VMEM((1, LANES), F32)] * n_state
                        + [pltpu.VMEM((D_HEAD, D_HEAD), F32)] * n_state),
        compiler_params=pltpu.CompilerParams(
            dimension_semantics=("parallel", "arbitrary"), vmem_limit_bytes=VMEM_LIMIT),
        name="mixer",
    )(q, k, v, gcol, grow, hq, g_f, hv, q, k, v, gcol, grow, hq, g_b, hv)


def _head_norm(hsum, gain):
    parts = []
    for hd in range(N_HEADS):
        hh = hsum[:, hd * D_HEAD:(hd + 1) * D_HEAD]
        parts.append(hh * lax.rsqrt(jnp.mean(hh * hh, axis=-1, keepdims=True) + NORM_EPS))
    return jnp.concatenate(parts, axis=1) * gain


def _merge_tile(r0, hf_ref, hb_ref, of_ref, ob_ref, mo_ref, hgg_ref, x_ref, mn_ref, hn_ref, wo_ref,
                n2_ref, wrh_ref, wrl_ref, br_ref, x1_ref, h2_ref, route_ref, hist_ref):
    rs = slice(r0, r0 + PROJ_ROWS)
    m_out = _head_norm(hf_ref[rs, :] + hb_ref[rs, :], mn_ref[...]) * mo_ref[rs, :]
    hg_out = _head_norm(of_ref[rs, :] + ob_ref[rs, :], hn_ref[...]) * hgg_ref[rs, :]
    mixed = jnp.concatenate([m_out, hg_out], axis=1).astype(BF16)
    yield
    x1 = x_ref[rs, :] + _dot(mixed, wo_ref[...])
    x1_ref[rs, :] = x1
    h2 = _rms(x1, n2_ref[...])
    h2_ref[rs, :] = _pack_bf16_pairs(h2)
    h_hi = h2.astype(BF16)
    h_hi32 = h_hi.astype(F32)
    yield

    h_lo = (h2 - h_hi32).astype(BF16)
    logits = _dot(h_hi, wrh_ref[...]) + _dot(h_lo, wrh_ref[...]) + _dot(h_hi, wrl_ref[...]) + br_ref[...]
    lane = lax.broadcasted_iota(jnp.int32, logits.shape, 1)
    big = jnp.int32(LANES)
    neg = -jnp.inf
    yield
    g_log = jnp.where(lane < N_GROUPS, logits, neg)
    g_max = jnp.max(g_log, axis=-1, keepdims=True)
    g_idx = jnp.min(jnp.where(g_log == g_max, lane, big), axis=-1, keepdims=True)
    g_val = 1.0 / jnp.sum(jnp.exp(g_log - g_max), axis=-1, keepdims=True)
    yield
    e_lo = N_GROUPS + g_idx * EXPERTS_PER_GROUP
    e_log = jnp.where((lane >= e_lo) & (lane < e_lo + EXPERTS_PER_GROUP), logits, neg)
    m1 = jnp.max(e_log, axis=-1, keepdims=True)
    i1 = jnp.min(jnp.where(e_log == m1, lane, big), axis=-1, keepdims=True)
    yield
    e_log2 = jnp.where(lane == i1, neg, e_log)
    m2 = jnp.max(e_log2, axis=-1, keepdims=True)
    i2 = jnp.min(jnp.where(e_log2 == m2, lane, big), axis=-1, keepdims=True)
    r2 = jnp.exp(m2 - m1)
    w1 = g_val / (1.0 + r2)
    w2 = g_val * r2 / (1.0 + r2)
    yield
    rows = logits.shape[0]
    pick0 = lane == i1 - N_GROUPS
    pick1 = lane == i2 - N_GROUPS
    earlier = (lax.broadcasted_iota(jnp.int32, (rows, rows), 1)
               < lax.broadcasted_iota(jnp.int32, (rows, rows), 0)).astype(BF16)
    cnt0 = jnp.sum(pick0.astype(F32), axis=0, keepdims=True)
    cnt1 = jnp.sum(pick1.astype(F32), axis=0, keepdims=True)
    rank0 = jnp.sum(jnp.where(pick0, _dot(earlier, pick0.astype(BF16)), 0.0), axis=-1, keepdims=True)
    rank1 = jnp.sum(jnp.where(pick1, _dot(earlier, pick1.astype(BF16)) + cnt0, 0.0), axis=-1, keepdims=True)
    yield
    columns = ((i1 - N_GROUPS).astype(F32), (i2 - N_GROUPS).astype(F32), w1, w2, rank0, rank1)
    route = jnp.zeros_like(logits)
    for c, value in enumerate(columns):
        route = jnp.where(lane == c, value, route)
    route_ref[rs, :] = route
    hs = slice(r0 // PROJ_ROWS * SUBLANES, (r0 // PROJ_ROWS + 1) * SUBLANES)
    sub = lax.broadcasted_iota(jnp.int32, (SUBLANES, LANES), 0)
    hist_ref[hs, :] = jnp.where(sub == 0, cnt0 + cnt1, 0.0)


N_MERGE_STREAMS = 7


def _merge_kernel(*refs, n_a):
    side_a = refs[0:N_MERGE_STREAMS]
    n_sides = 1 if n_a is None else 2
    side_b = refs[N_MERGE_STREAMS:n_sides * N_MERGE_STREAMS]
    rest = refs[n_sides * N_MERGE_STREAMS:]

    def block(side):
        _run_round_robin([_merge_tile(r0, *side, *rest) for r0 in range(0, MERGE_ROWS, PROJ_ROWS)])

    if n_a is None:
        block(side_a)
        return

    @pl.when(pl.program_id(0) < n_a)
    def _():
        block(side_a)

    @pl.when(pl.program_id(0) >= n_a)
    def _():
        block(side_b)


def _merge(streams_a, streams_b, m_norm, hg_norm, w_out, norm2, w_rg, b_rg, w_re, b_re):
    D = D_MODEL
    rows = MERGE_ROWS
    n_a = streams_a[0].shape[0] // rows
    n_b = streams_b[0].shape[0] // rows if streams_b else 0
    n_all = (n_a + n_b) * rows
    n_log = N_GROUPS + N_EXPERTS
    wr = jnp.pad(jnp.concatenate([w_rg, w_re], axis=1), ((0, 0), (0, LANES - n_log)))
    br = jnp.pad(jnp.concatenate([b_rg, b_re]), (0, LANES - n_log))[None, :]
    wr_hi = wr.astype(BF16)
    wr_lo = (wr - wr_hi.astype(F32)).astype(BF16)
    consts = [m_norm[None, :], hg_norm[None, :], w_out.astype(BF16), norm2[None, :], wr_hi, wr_lo, br]

    def full(arr):
        nd = arr.ndim
        return pl.BlockSpec(arr.shape, lambda i: (0,) * nd)

    def side_a(arr):
        return pl.BlockSpec((rows, arr.shape[1]), lambda i: (jnp.minimum(i, n_a - 1), 0))

    def side_b(arr):
        return pl.BlockSpec((rows, arr.shape[1]), lambda i: (jnp.maximum(i - n_a, 0), 0))

    def out(width):
        return pl.BlockSpec((rows, width), lambda i: (i, 0))

    return pl.pallas_call(
        functools.partial(_merge_kernel, n_a=n_a if streams_b else None),
        grid=(n_a + n_b,),
        in_specs=[side_a(s) for s in streams_a] + [side_b(s) for s in (streams_b or ())] + [full(c) for c in consts],
        out_specs=[out(D), out(D // 2), out(LANES), pl.BlockSpec((rows // PROJ_ROWS * SUBLANES, LANES), lambda i: (i, 0))],
        out_shape=[jax.ShapeDtypeStruct((n_all, D), F32), jax.ShapeDtypeStruct((n_all, D // 2), jnp.uint32),
                   jax.ShapeDtypeStruct((n_all, LANES), F32),
                   jax.ShapeDtypeStruct((n_all // PROJ_ROWS * SUBLANES, LANES), F32)],
        compiler_params=pltpu.CompilerParams(
            dimension_semantics=("arbitrary",), vmem_limit_bytes=VMEM_LIMIT),
        name="merge",
    )(*streams_a, *(streams_b or ()), *consts)


def _sc_row_mover(src, idx, n_out, scatter, name):
    n_moved = idx.shape[0]
    D = src.shape[1]
    n_sub = SC_CORES * SC_SUBCORES
    per = n_moved // n_sub
    window = SC_WINDOW_BYTES // (D * src.dtype.itemsize)
    assert per * n_sub == n_moved and per % window == 0, (n_moved, per, window)
    assert not scatter or src.shape[0] % per == 0, (src.shape, per)
    mesh = plsc.VectorSubcoreMesh(core_axis_name="c", subcore_axis_name="s",
                                  num_cores=SC_CORES, num_subcores=SC_SUBCORES)

    def body(src_hbm, idx_hbm, out_hbm, idx_v, buf):
        base = (lax.axis_index("c") * SC_SUBCORES + lax.axis_index("s")) * per
        pltpu.sync_copy(idx_hbm.at[pl.ds(base, per)], idx_v)

        @pl.loop(0, per // window)
        def _(j):
            linear = pl.ds(base + j * window, window)
            indexed = idx_v.at[pl.ds(j * window, window)]
            if scatter:
                pltpu.sync_copy(src_hbm.at[pl.ds(lax.rem(base, src.shape[0]) + j * window, window)], buf)
                pltpu.sync_copy(buf, out_hbm.at[indexed])
            else:
                pltpu.sync_copy(src_hbm.at[indexed], buf)
                pltpu.sync_copy(buf, out_hbm.at[linear])

    return pl.kernel(
        body,
        out_type=jax.ShapeDtypeStruct((n_out, D), src.dtype),
        mesh=mesh,
        scratch_types=[pltpu.VMEM((per,), jnp.int32), pltpu.VMEM((window, D), src.dtype)],
        name=name,
    )(src, idx)


def _sc_gather_rows(src, idx):
    return _sc_row_mover(src, idx, idx.shape[0], False, "sc_gather_rows")


def _sc_scatter_rows(src, idx, n_out):
    return _sc_row_mover(src, idx, n_out, True, "sc_scatter_rows")


def _expert_kernel(be_ref, nu_ref, x_ref, w1_ref, w3_ref, w2_ref, o_ref, w1_bf, w3_bf, w2_bf):
    i = pl.program_id(0)
    active = i < nu_ref[0]
    new_expert = (i == 0) | (be_ref[i] != be_ref[jnp.maximum(i - 1, 0)])

    @pl.when(active & new_expert)
    def _():
        w1_bf[...] = w1_ref[0].astype(BF16)
        w3_bf[...] = w3_ref[0].astype(BF16)
        w2_bf[...] = w2_ref[0].astype(BF16)

    @pl.when(active)
    def _():
        half = D_MODEL // 2
        x_lo, x_hi = (part.astype(BF16) for part in _unpack_bf16_pairs(x_ref[...]))
        a = _dot(x_lo, w1_bf[0:half, :]) + _dot(x_hi, w1_bf[half:, :])
        b = _dot(x_lo, w3_bf[0:half, :]) + _dot(x_hi, w3_bf[half:, :])
        o_ref[...] = _pack_bf16_pairs(_dot((_silu(a) * b).astype(BF16), w2_bf[...]))


def _experts(xs, block_e, n_used, w1, w3, w2):
    rows = EXPERT_ROWS
    n_blocks = xs.shape[0] // rows
    D = D_MODEL

    def blk(i, be, nu):
        return jnp.minimum(i, nu[0] - 1)

    grid_spec = pltpu.PrefetchScalarGridSpec(
        num_scalar_prefetch=2,
        grid=(n_blocks,),
        in_specs=[
            pl.BlockSpec((rows, D // 2), lambda i, be, nu: (blk(i, be, nu), 0)),
            pl.BlockSpec((1, D, EXPERT_FF), lambda i, be, nu: (be[blk(i, be, nu)], 0, 0)),
            pl.BlockSpec((1, D, EXPERT_FF), lambda i, be, nu: (be[blk(i, be, nu)], 0, 0)),
            pl.BlockSpec((1, EXPERT_FF, D), lambda i, be, nu: (be[blk(i, be, nu)], 0, 0)),
        ],
        out_specs=pl.BlockSpec((rows, D // 2), lambda i, be, nu: (blk(i, be, nu), 0)),
        scratch_shapes=[pltpu.VMEM((D, EXPERT_FF), BF16), pltpu.VMEM((D, EXPERT_FF), BF16),
                        pltpu.VMEM((EXPERT_FF, D), BF16)],
    )
    return pl.pallas_call(
        _expert_kernel,
        grid_spec=grid_spec,
        out_shape=jax.ShapeDtypeStruct((xs.shape[0], D // 2), jnp.uint32),
        compiler_params=pltpu.CompilerParams(
            dimension_semantics=("arbitrary",), vmem_limit_bytes=VMEM_LIMIT),
        name="experts",
    )(block_e, n_used, xs, w1, w3, w2)


def _combine_kernel(y0_ref, y1_ref, x1_ref, route_ref, nf_ref, y_ref):
    route = route_ref[...]
    r0 = jnp.concatenate(_unpack_bf16_pairs(y0_ref[...]), axis=1)
    r1 = jnp.concatenate(_unpack_bf16_pairs(y1_ref[...]), axis=1)
    y_ref[...] = _rms(x1_ref[...] + route[:, 2:3] * r0 + route[:, 3:4] * r1, nf_ref[...])


def _combine(x1, route, y_rows, norm_f, tok0):
    D = x1.shape[1]
    n = y_rows.shape[0] // TOP_K
    rows = COMBINE_ROWS
    assert n % rows == 0 and tok0 % rows == 0, (n, tok0, rows)
    nt = n // rows
    first = tok0 // rows

    def tok(width, offset=0):
        return pl.BlockSpec((rows, width), lambda i: (i + offset, 0))

    return pl.pallas_call(
        _combine_kernel,
        grid=(nt,),
        in_specs=[tok(D // 2), tok(D // 2, nt), tok(D, first), tok(LANES, first),
                  pl.BlockSpec((1, D), lambda i: (0, 0))],
        out_specs=tok(D),
        out_shape=jax.ShapeDtypeStruct((n, D), F32),
        compiler_params=pltpu.CompilerParams(
            dimension_semantics=("parallel",), vmem_limit_bytes=VMEM_LIMIT),
        name="combine",
    )(y_rows, y_rows, x1, route, norm_f[None, :])


def _plan_kernel(route_ref, table_ref, dest_ref):
    lane = lax.broadcasted_iota(jnp.int32, (PROJ_ROWS, LANES), 1)
    lane_f = lane.astype(F32)
    for tile in range(route_ref.shape[0] // PROJ_ROWS):
        rs = slice(tile * PROJ_ROWS, (tile + 1) * PROJ_ROWS)
        route = route_ref[rs, :]
        first = table_ref[tile * SUBLANES:tile * SUBLANES + 1, :]
        d0 = jnp.sum(jnp.where(lane_f == route[:, 0:1], first, 0.0), axis=-1, keepdims=True) + route[:, 4:5]
        d1 = jnp.sum(jnp.where(lane_f == route[:, 1:2], first, 0.0), axis=-1, keepdims=True) + route[:, 5:6]
        cols = jnp.where(lane == 0, d0, jnp.where(lane == 1, d1, 0.0))
        dest_ref[:, rs] = cols.T[0:SUBLANES, :].astype(jnp.int32)


def _dispatch_plan(route, hist):
    N = route.shape[0]
    rows = PROJ_ROWS
    n_tiles = N // rows
    blk = EXPERT_ROWS
    tile_counts = hist.reshape(n_tiles, SUBLANES, LANES)[:, 0, 0:N_EXPERTS].astype(jnp.int32)
    tile_first = jnp.cumsum(tile_counts, axis=0) - tile_counts
    counts = jnp.sum(tile_counts, axis=0)
    padded = ((counts + blk - 1) // blk) * blk
    pad_end = jnp.cumsum(padded)
    pad_start = pad_end - padded
    table = jnp.zeros((n_tiles, SUBLANES, LANES), F32).at[:, 0, 0:N_EXPERTS].set(
        (pad_start[None, :] + tile_first).astype(F32)).reshape(n_tiles * SUBLANES, LANES)
    per_step = min(PLAN_TILES, n_tiles)
    assert n_tiles % per_step == 0, (n_tiles, per_step)
    dest_rows = pl.pallas_call(
        _plan_kernel,
        grid=(n_tiles // per_step,),
        in_specs=[pl.BlockSpec((per_step * rows, LANES), lambda i: (i, 0)),
                  pl.BlockSpec((per_step * SUBLANES, LANES), lambda i: (i, 0))],
        out_specs=pl.BlockSpec((SUBLANES, per_step * rows), lambda i: (0, i)),
        out_shape=jax.ShapeDtypeStruct((SUBLANES, N), jnp.int32),
        compiler_params=pltpu.CompilerParams(dimension_semantics=("parallel",)),
        name="plan",
    )(route, table)
    dest_rows = dest_rows[0:TOP_K]
    n_blocks = (TOP_K * N + N_EXPERTS * (blk - 1) + blk - 1) // blk
    block_start = jnp.arange(n_blocks, dtype=jnp.int32) * blk
    block_e = jnp.sum((pad_end[None, :] <= block_start[:, None]).astype(jnp.int32), axis=1)
    block_e = jnp.minimum(block_e, N_EXPERTS - 1)
    n_used = (pad_end[-1] // blk).astype(jnp.int32).reshape(1)
    return dest_rows, block_e, n_used, n_blocks * blk


def _token_mixer(x, norm1, w_in, b_in, conv_w, conv_b, m_fgate_bias, hg_lb_logits):
    B, T, D = x.shape
    q, k, v, mo, gcol, grow, hq, g_f, g_b, hv, hgg = _in_proj(
        x, norm1, w_in, b_in, conv_w, conv_b, m_fgate_bias, hg_lb_logits)
    h_f, o_f, h_b, o_b = _mixer(q, k, v, gcol, grow, hq, g_f, g_b, hv)
    return [a.reshape(B * T, a.shape[-1]) for a in (h_f, h_b, o_f, o_b, mo, hgg, x)]


def kernel(x_prompt, x_sample, norm1, w_in, b_in, conv_w, conv_b, m_fgate_bias, m_norm, hg_lb_logits, hg_norm,
           w_out, norm2, w_router_group, b_router_group, w_router_expert, b_router_expert, w1, w3, w2, norm_f):
    layer = 0
    mixer_args = (norm1[layer], w_in[layer], b_in[layer], conv_w[layer], conv_b[layer], m_fgate_bias[layer],
                  hg_lb_logits)
    outs = []
    for x in (x_prompt, x_sample):
        streams = _token_mixer(x, *mixer_args)
        x1, h2, route, hist = _merge(streams, None, m_norm[layer], hg_norm[layer], w_out[layer], norm2[layer],
                                     w_router_group[layer], b_router_group[layer], w_router_expert[layer],
                                     b_router_expert[layer])
        dest, block_e, n_used, n_rows = _dispatch_plan(route, hist)
        n = h2.shape[0]
        xs = _sc_scatter_rows(h2, dest.reshape(TOP_K * n), n_rows)
        out_rows = _experts(xs, block_e, n_used, w1[layer], w3[layer], w2[layer])
        y_rows = _sc_gather_rows(out_rows, dest.reshape(TOP_K * n))
        outs.append(_combine(x1, route, y_rows, norm_f, 0).reshape(x.shape))
    return tuple(outs)
```

```python
import functools

import jax
import jax.numpy as jnp
from jax import lax
from jax.experimental import pallas as pl
from jax.experimental.pallas import tpu as pltpu
from jax.experimental.pallas import tpu_sc as plsc

F32 = jnp.float32
BF16 = jnp.bfloat16

D_MODEL = 1024
N_HEADS = 4
D_HEAD = 128
WIDTH = N_HEADS * D_HEAD
CONV_K = 5
CONV_PAD = CONV_K // 2
N_GROUPS = 4
EXPERTS_PER_GROUP = 8
N_EXPERTS = N_GROUPS * EXPERTS_PER_GROUP
TOP_K = 2
EXPERT_FF = D_MODEL // 2
NORM_EPS = 1e-6

LANES = 128
SUBLANES = 8
CHUNK = 128
MIXER_CHUNKS = 2
PROJ_ROWS = 256
COMBINE_ROWS = 512
MERGE_ROWS = 512
PLAN_TILES = 8
IN_PROJ_ROWS = 512
IN_PROJ_STREAM = 256
HALO = SUBLANES
EXPERT_ROWS = 512
N_GATES = 4 * N_HEADS
SC_CORES = 2
SC_SUBCORES = 16
SC_WINDOW_BYTES = 128 * 1024
VMEM_LIMIT = 56 * 1024 * 1024


def _dot(a, b):
    return jnp.dot(a, b, preferred_element_type=F32)


def _dot_nt(a, b):
    return lax.dot_general(a, b, (((1,), (1,)), ((), ())), preferred_element_type=F32)


def _dot_tn(a, b):
    return lax.dot_general(a, b, (((0,), (0,)), ((), ())), preferred_element_type=F32)


def _split3(x):
    hi = x.astype(BF16)
    r1 = x - hi.astype(F32)
    mid = r1.astype(BF16)
    lo = (r1 - mid.astype(F32)).astype(BF16)
    return hi, mid, lo


def _pack_bf16_pairs(x):
    half = x.shape[1] // 2
    bits = lax.bitcast_convert_type(x.astype(BF16).astype(F32), jnp.uint32)
    return (bits[:, half:] & jnp.uint32(0xFFFF0000)) | (bits[:, :half] >> 16)


def _unpack_bf16_pairs(words):
    lo = lax.bitcast_convert_type(words << 16, F32)
    hi = lax.bitcast_convert_type(words & jnp.uint32(0xFFFF0000), F32)
    return lo, hi


def _silu(x):
    return x * jax.nn.sigmoid(x)


def _log_sigmoid(x):
    return -(jnp.maximum(-x, 0.0) + jnp.log1p(jnp.exp(-jnp.abs(x))))


def _rms(x, gain):
    return x * lax.rsqrt(jnp.mean(x * x, axis=-1, keepdims=True) + NORM_EPS) * gain


def _run_round_robin(generators):
    live = list(generators)
    while live:
        for gen in list(live):
            try:
                next(gen)
            except StopIteration:
                live.remove(gen)


def _in_proj_kernel(x_ref, xp_ref, xn_ref, n1_ref, wa_ref, ba_ref, wg_ref, bg_ref, wgt_ref, bgt_ref,
                    fbrow_ref, fbcol_ref, wh_ref, bh_ref, cw_ref, cb_ref, lbl_ref,
                    q_ref, k_ref, v_ref, mo_ref, gcol_ref, grow_ref, hq_ref, gf_ref, gb_ref, hv_ref, hgg_ref,
                    ext_ref):
    t = pl.program_id(1)
    nt = pl.num_programs(1)
    rows = x_ref.shape[1]
    gain = n1_ref[...]

    lbl = lbl_ref[...]
    lmax = jnp.max(lbl, axis=0, keepdims=True)
    le = jnp.exp(lbl - lmax)
    lb = le[0:1, :] / jnp.sum(le, axis=0, keepdims=True)

    wqk = wa_ref[:, 0:2 * WIDTH]
    bqk = ba_ref[:, 0:2 * WIDTH]
    hp = _rms(xp_ref[0], gain).astype(BF16)
    hn = _rms(xn_ref[0], gain).astype(BF16)
    ext_ref[0:HALO, :] = (_dot(hp, wqk) + bqk) * (t > 0).astype(F32)
    ext_ref[HALO + rows:2 * HALO + rows, :] = (_dot(hn, wqk) + bqk) * (t < nt - 1).astype(F32)

    def stream(r0, n):
        rs = slice(r0, r0 + n)
        h = _rms(x_ref[0, rs, :], gain).astype(BF16)

        def proj(w_ref, b_ref, lo, hi):
            return _dot(h, w_ref[:, lo:hi]) + b_ref[:, lo:hi]

        ext_ref[HALO + r0:HALO + r0 + n, :] = proj(wa_ref, ba_ref, 0, 2 * WIDTH)
        hq_pre = proj(wh_ref, bh_ref, 0, WIDTH)
        yield
        acc = cb_ref[...] + ext_ref[pl.ds(HALO - CONV_PAD + r0, n), :] * cw_ref[0:1, :]
        for j in range(1, CONV_K):
            acc = acc + ext_ref[pl.ds(HALO - CONV_PAD + j + r0, n), :] * cw_ref[j:j + 1, :]
        qk = _silu(acc)
        q_ref[0, rs, :] = (qk[:, 0:WIDTH] * (D_HEAD ** -0.5)).astype(BF16)
        k_ref[0, rs, :] = qk[:, WIDTH:2 * WIDTH]
        v_ref[0, rs, :] = proj(wa_ref, ba_ref, 2 * WIDTH, 3 * WIDTH).astype(BF16)
        yield
        hq_ref[0, rs, :] = _silu(hq_pre)
        mo_ref[0, rs, :] = jax.nn.sigmoid(proj(wa_ref, ba_ref, 3 * WIDTH, 4 * WIDTH))
        yield
        gf_ref[0, rs, :] = lb + (1.0 - lb) * jax.nn.sigmoid(proj(wh_ref, bh_ref, WIDTH, 2 * WIDTH))
        yield
        gb_ref[0, rs, :] = lb + (1.0 - lb) * jax.nn.sigmoid(proj(wh_ref, bh_ref, 2 * WIDTH, 3 * WIDTH))
        hv_ref[0, rs, :] = proj(wh_ref, bh_ref, 3 * WIDTH, 4 * WIDTH).astype(BF16)
        yield
        hgg_ref[0, rs, :] = _silu(proj(wh_ref, bh_ref, 4 * WIDTH, 5 * WIDTH))
        gc = _dot(h, wg_ref[...]) + bg_ref[...]
        lane = lax.broadcasted_iota(jnp.int32, gc.shape, 1)
        is_f = (lane >= 2 * N_HEADS) & (lane < N_GATES)
        gcol_ref[0, rs, :] = jnp.where(is_f, _log_sigmoid(gc + fbrow_ref[...]), gc)
        gr = _dot_nt(wgt_ref[...], h) + bgt_ref[...]
        sub = lax.broadcasted_iota(jnp.int32, gr.shape, 0)
        grow_ref[0, :, rs] = jnp.where(sub >= 2 * N_HEADS, _log_sigmoid(gr + fbcol_ref[...]), gr)

    _run_round_robin([stream(r0, IN_PROJ_STREAM) for r0 in range(0, rows, IN_PROJ_STREAM)])


def _in_proj(x, norm1, w_in, b_in, conv_w, conv_b, fgate_bias, lb_logits):
    B, T, D = x.shape
    rows = IN_PROJ_ROWS
    assert T % rows == 0, (T, rows)
    nt = T // rows
    a_w = 4 * WIDTH
    wa = w_in[:, 0:a_w].astype(BF16)
    ba = b_in[None, 0:a_w]
    wg32 = jnp.pad(w_in[:, a_w:a_w + N_GATES], ((0, 0), (0, LANES - N_GATES)))
    bg = jnp.pad(b_in[a_w:a_w + N_GATES], (0, LANES - N_GATES))[None, :]
    wg = wg32.astype(BF16)
    wgt = w_in[:, a_w:a_w + N_GATES].T.astype(BF16)
    bgt = b_in[a_w:a_w + N_GATES][:, None]
    fb = fgate_bias.reshape(2 * N_HEADS)
    fbrow = jnp.zeros((1, LANES), F32).at[0, 2 * N_HEADS:N_GATES].set(fb)
    fbcol = jnp.zeros((N_GATES, 1), F32).at[2 * N_HEADS:N_GATES, 0].set(fb)
    wh = w_in[:, a_w + N_GATES:].astype(BF16)
    bh = b_in[None, a_w + N_GATES:]

    tiles_per_halo = rows // HALO
    n_halo = T // HALO

    def full(arr):
        nd = arr.ndim
        return pl.BlockSpec(arr.shape, lambda b, t: (0,) * nd)

    def tok(width):
        return pl.BlockSpec((1, rows, width), lambda b, t: (b, t, 0))

    in_specs = [
        tok(D),
        pl.BlockSpec((1, HALO, D), lambda b, t: (b, jnp.maximum(t * tiles_per_halo - 1, 0), 0)),
        pl.BlockSpec((1, HALO, D), lambda b, t: (b, jnp.minimum((t + 1) * tiles_per_halo, n_halo - 1), 0)),
    ]
    consts = [norm1[None, :], wa, ba, wg, bg, wgt, bgt, fbrow, fbcol, wh, bh, conv_w, conv_b[None, :], lb_logits]
    in_specs += [full(c) for c in consts]
    tok_out = jax.ShapeDtypeStruct((B, T, WIDTH), F32)
    tok_bf = jax.ShapeDtypeStruct((B, T, WIDTH), BF16)
    out_shape = [tok_bf, tok_out, tok_bf, tok_out,
                 jax.ShapeDtypeStruct((B, T, LANES), F32),
                 jax.ShapeDtypeStruct((B, N_GATES, T), F32),
                 tok_out, tok_out, tok_out, tok_bf, tok_out]
    out_specs = [tok(WIDTH)] * 4 + [tok(LANES), pl.BlockSpec((1, N_GATES, rows), lambda b, t: (b, 0, t))] + [tok(WIDTH)] * 5
    return pl.pallas_call(
        _in_proj_kernel,
        grid=(B, nt),
        in_specs=in_specs,
        out_specs=out_specs,
        out_shape=out_shape,
        scratch_shapes=[pltpu.VMEM((rows + 2 * HALO, 2 * WIDTH), F32)],
        compiler_params=pltpu.CompilerParams(
            dimension_semantics=("parallel", "parallel"), vmem_limit_bytes=VMEM_LIMIT),
        name="in_proj",
    )(x, x, x, *consts)


def _cumsum_rows(tri_bf, x):
    hi, mid, lo = _split3(x)
    return _dot(tri_bf, hi) + _dot(tri_bf, mid) + _dot(tri_bf, lo)


def _cumsum_lanes(x, tri_bf):
    hi, mid, lo = _split3(x)
    return _dot(hi, tri_bf) + _dot(mid, tri_bf) + _dot(lo, tri_bf)


def _mlstm_chunk(q, k, vext, i_col, b_col, i_row, b_row, seen, last, c_ref, m_ref, out_ref, rs, sl):
    m_prev = m_ref[:, 0:1]
    c_prev = c_ref[...]
    q_bf = q
    log_d = jnp.where(seen, b_col - b_row + i_row, -jnp.inf)
    m_inter = b_col + m_prev
    m_t = jnp.maximum(m_inter, jnp.max(log_d, axis=-1, keepdims=True))
    qk = _dot_nt(q_bf, k.astype(BF16))
    yield
    scores = (qk * jnp.exp(log_d - m_t)).astype(BF16)
    inter_scale = jnp.exp(m_inter - m_t)
    b_last = b_col[last:last + 1, :]
    log_w = b_last - b_col + i_col
    m_new = jnp.maximum(b_last + m_prev, jnp.max(log_w, axis=0, keepdims=True))
    w = jnp.exp(log_w - m_new)
    decay = jnp.exp(b_last + m_prev - m_new)
    kw = (k * w).astype(BF16)
    yield
    numden = _dot(scores, vext) + inter_scale * _dot(q_bf, c_prev.astype(BF16))
    update = _dot_tn(kw, vext)
    yield
    num = numden[:, 0:D_HEAD]
    den = numden[:, D_HEAD:2 * D_HEAD]
    out_ref[0, rs, sl] = num / jnp.maximum(jnp.abs(den), jnp.exp(-m_t))
    c_ref[...] = decay * c_prev + update
    m_ref[...] = jnp.broadcast_to(m_new, (1, LANES))


def _hgrn2_level_small(q3, k3, pre3, suf3, half, rev, sub_iota):
    upper = (sub_iota & half) != 0
    second = jnp.logical_not(upper) if rev else upper
    end = 0 if rev else half - 1
    y = jnp.where((sub_iota & (half - 1)) == end, pre3, 0.0)
    step = 1 if rev else -1
    span = 1
    while span < half:
        y = y + pltpu.roll(y, (step * span) % SUBLANES, 1)
        span *= 2
    if 2 * half == SUBLANES:
        other = pltpu.roll(y, half, 1)
    else:
        other = jnp.where(upper, pltpu.roll(y, half, 1), pltpu.roll(y, SUBLANES - half, 1))
    z = jnp.where(second, q3 * pre3, k3 * suf3)
    return z, pre3 * jnp.where(second, other, 1.0), suf3 * jnp.where(second, 1.0, other)


def _hgrn2_level_big(q, k, pre, suf, half, rev):
    L, width = q.shape
    shape = (L // (2 * half), 2, half, width)
    q4, k4, pre4, suf4 = (a.reshape(shape) for a in (q, k, pre, suf))
    first = 1 if rev else 0
    second = 1 - first
    end = 0 if rev else half - 1
    total_first = pre4[:, first, end:end + 1, :]
    total_second = pre4[:, second, end:end + 1, :]

    def join(at_first, at_second):
        parts = (at_second, at_first) if rev else (at_first, at_second)
        return jnp.stack(parts, axis=1).reshape(L, width)

    z = join(k4[:, first] * suf4[:, first], q4[:, second] * pre4[:, second])
    pre_new = join(pre4[:, first], pre4[:, second] * total_first)
    suf_new = join(suf4[:, first] * total_second, suf4[:, second])
    return z, pre_new, suf_new


def _hgrn2_chunk(q, g, v_bf, rev, level, diag, sub_iota, st_refs, out_ref, rs, sl):
    L, width = q.shape
    heads = [slice(h * D_HEAD, (h + 1) * D_HEAD) for h in range(width // D_HEAD)]
    k = 1.0 - g
    q_bf = q.astype(BF16)
    k_bf = k.astype(BF16)
    att = [jnp.where(diag, _dot_nt(q_bf[:, s], k_bf[:, s]).astype(BF16), jnp.zeros((), BF16)) for s in heads]
    small = (L // SUBLANES, SUBLANES, width)
    q3, k3, pre, suf = q.reshape(small), k.reshape(small), g.reshape(small), jnp.ones(small, F32)
    half = 1
    bit = 0
    while half < L:
        if half == SUBLANES:
            pre, suf = pre.reshape(L, width), suf.reshape(L, width)
        if half < SUBLANES:
            z, pre, suf = _hgrn2_level_small(q3, k3, pre, suf, half, rev, sub_iota)
            z = z.reshape(L, width)
        else:
            z, pre, suf = _hgrn2_level_big(q, k, pre, suf, half, rev)
        z = z.astype(BF16)
        att = [jnp.where(level == bit, _dot_nt(z[:, s], z[:, s]).astype(BF16), a) for a, s in zip(att, heads)]
        half *= 2
        bit += 1
        yield
    last = 0 if rev else L - 1
    q_dec = (q * pre).astype(BF16)
    k_dec = (k * suf).astype(BF16)
    outs = []
    for h, s in enumerate(heads):
        st_prev = st_refs[h][...]
        outs.append(_dot_nt(q_dec[:, s], st_prev.astype(BF16)) + _dot(att[h], v_bf[:, s]))
        st_refs[h][...] = st_prev * pre[last:last + 1, s] + _dot_tn(v_bf[:, s], k_dec[:, s])
    out_ref[0, rs, sl] = jnp.concatenate(outs, axis=1)


def _mixer_kernel(qf_ref, kf_ref, vf_ref, gcf_ref, grf_ref, hqf_ref, hgf_ref, hvf_ref,
                  qb_ref, kb_ref, vb_ref, gcb_ref, grb_ref, hqb_ref, hgb_ref, hvb_ref,
                  hf_ref, of_ref, hb_ref, ob_ref, *state_refs):
    L = CHUNK
    n_state = 2 * N_HEADS
    c_refs, m_refs, st_refs = (state_refs[i * n_state:(i + 1) * n_state] for i in range(3))

    @pl.when(pl.program_id(1) == 0)
    def _():
        for ref in state_refs:
            ref[...] = jnp.zeros_like(ref)

    row = lax.broadcasted_iota(jnp.int32, (L, L), 0)
    col = lax.broadcasted_iota(jnp.int32, (L, L), 1)
    sub_iota = lax.broadcasted_iota(jnp.int32, (L // SUBLANES, SUBLANES, LANES), 1)
    diag = row == col
    diff = row ^ col
    high_bit = jnp.zeros((L, L), jnp.int32)
    half = 2
    while half < L:
        high_bit = high_bit + (diff >= half).astype(jnp.int32)
        half *= 2
    ones = jnp.ones((L, D_HEAD), BF16)

    dirs = (
        (0, qf_ref, kf_ref, vf_ref, gcf_ref, grf_ref, hqf_ref, hgf_ref, hvf_ref, hf_ref, of_ref),
        (1, qb_ref, kb_ref, vb_ref, gcb_ref, grb_ref, hqb_ref, hgb_ref, hvb_ref, hb_ref, ob_ref),
    )
    masks = []
    for rev in (False, True):
        seen = (col >= row) if rev else (col <= row)
        before = (col > row) if rev else (col < row)
        level = jnp.where(before, high_bit, -1)
        tri = seen.astype(BF16)
        tri_t = (row >= col if rev else row <= col).astype(BF16)
        masks.append((seen, level, tri, tri_t))
    n_sub = qf_ref.shape[1] // L
    for step in range(n_sub):
        stages = []
        for d, q_ref, k_ref, v_ref, gc_ref, gr_ref, hq_ref, hg_ref, hv_ref, h_out, o_out in dirs:
            rev = d == 1
            seen, level, tri, tri_t = masks[d]
            last = 0 if rev else L - 1
            sub_chunk = n_sub - 1 - step if rev else step
            rs = slice(sub_chunk * L, (sub_chunk + 1) * L)
            gc = gc_ref[0, rs, :]
            gr = gr_ref[0, :, rs]
            gc_cum = _cumsum_rows(tri, gc)
            gr_cum = _cumsum_lanes(gr, tri_t)
            for hd in range(N_HEADS):
                sl = slice(hd * D_HEAD, (hd + 1) * D_HEAD)
                gi = d * N_HEADS + hd
                gf = 2 * N_HEADS + gi
                idx = d * N_HEADS + hd
                vext = jnp.concatenate([v_ref[0, rs, sl], ones], axis=1)
                stages.append(_mlstm_chunk(
                    q_ref[0, rs, sl], k_ref[0, rs, sl], vext,
                    gc[:, gi:gi + 1], gc_cum[:, gf:gf + 1], gr[gi:gi + 1, :], gr_cum[gf:gf + 1, :],
                    seen, last, c_refs[idx], m_refs[idx], h_out, rs, sl))
                stages.append(_hgrn2_chunk(
                    hq_ref[0, rs, sl], hg_ref[0, rs, sl], hv_ref[0, rs, sl],
                    rev, level, diag, sub_iota, [st_refs[idx]], o_out, rs, sl))
        group = 2 * N_HEADS
        for start in range(0, len(stages), group):
            _run_round_robin(stages[start:start + group])


def _mixer(q, k, v, gcol, grow, hq, g_f, g_b, hv):
    B, T, _ = q.shape
    L = CHUNK * MIXER_CHUNKS
    nc = T // L

    def fwd(width):
        return pl.BlockSpec((1, L, width), lambda b, c: (b, c, 0))

    def bwd(width):
        return pl.BlockSpec((1, L, width), lambda b, c: (b, nc - 1 - c, 0))

    grow_f = pl.BlockSpec((1, N_GATES, L), lambda b, c: (b, 0, c))
    grow_b = pl.BlockSpec((1, N_GATES, L), lambda b, c: (b, 0, nc - 1 - c))
    in_specs = ([fwd(WIDTH)] * 3 + [fwd(LANES), grow_f] + [fwd(WIDTH)] * 3
                + [bwd(WIDTH)] * 3 + [bwd(LANES), grow_b] + [bwd(WIDTH)] * 3)
    out = jax.ShapeDtypeStruct((B, T, WIDTH), F32)
    n_state = 2 * N_HEADS
    return pl.pallas_call(
        _mixer_kernel,
        grid=(B, nc),
        in_specs=in_specs,
        out_specs=[fwd(WIDTH), fwd(WIDTH), bwd(WIDTH), bwd(WIDTH)],
        out_shape=[out, out, out, out],
        scratch_shapes=([pltpu.VMEM((D_HEAD, 2 * D_HEAD), F32)] * n_state
                        + [pltpu.VMEM((1, LANES), F32)] * n_state
                        + [pltpu.VMEM((D_HEAD, D_HEAD), F32)] * n_state),
        compiler_params=pltpu.CompilerParams(
            dimension_semantics=("parallel", "arbitrary"), vmem_limit_bytes=VMEM_LIMIT),
        name="mixer",
    )(q, k, v, gcol, grow, hq, g_f, hv, q, k, v, gcol, grow, hq, g_b, hv)


def _head_norm(hsum, gain):
    parts = []
    for hd in range(N_HEADS):
        hh = hsum[:, hd * D_HEAD:(hd + 1) * D_HEAD]
        parts.append(hh * lax.rsqrt(jnp.mean(hh * hh, axis=-1, keepdims=True) + NORM_EPS))
    return jnp.concatenate(parts, axis=1) * gain


def _merge_tile(r0, hf_ref, hb_ref, of_ref, ob_ref, mo_ref, hgg_ref, x_ref, mn_ref, hn_ref, wo_ref,
                n2_ref, wrh_ref, wrl_ref, br_ref, x1_ref, h2_ref, route_ref, hist_ref):
    rs = slice(r0, r0 + PROJ_ROWS)
    m_out = _head_norm(hf_ref[rs, :] + hb_ref[rs, :], mn_ref[...]) * mo_ref[rs, :]
    hg_out = _head_norm(of_ref[rs, :] + ob_ref[rs, :], hn_ref[...]) * hgg_ref[rs, :]
    mixed = jnp.concatenate([m_out, hg_out], axis=1).astype(BF16)
    yield
    x1 = x_ref[rs, :] + _dot(mixed, wo_ref[...])
    x1_ref[rs, :] = x1
    h2 = _rms(x1, n2_ref[...])
    h2_ref[rs, :] = _pack_bf16_pairs(h2)
    h_hi = h2.astype(BF16)
    h_hi32 = h_hi.astype(F32)
    yield

    h_lo = (h2 - h_hi32).astype(BF16)
    logits = _dot(h_hi, wrh_ref[...]) + _dot(h_lo, wrh_ref[...]) + _dot(h_hi, wrl_ref[...]) + br_ref[...]
    lane = lax.broadcasted_iota(jnp.int32, logits.shape, 1)
    big = jnp.int32(LANES)
    neg = -jnp.inf
    yield
    g_log = jnp.where(lane < N_GROUPS, logits, neg)
    g_max = jnp.max(g_log, axis=-1, keepdims=True)
    g_idx = jnp.min(jnp.where(g_log == g_max, lane, big), axis=-1, keepdims=True)
    g_val = 1.0 / jnp.sum(jnp.exp(g_log - g_max), axis=-1, keepdims=True)
    yield
    e_lo = N_GROUPS + g_idx * EXPERTS_PER_GROUP
    e_log = jnp.where((lane >= e_lo) & (lane < e_lo + EXPERTS_PER_GROUP), logits, neg)
    m1 = jnp.max(e_log, axis=-1, keepdims=True)
    i1 = jnp.min(jnp.where(e_log == m1, lane, big), axis=-1, keepdims=True)
    yield
    e_log2 = jnp.where(lane == i1, neg, e_log)
    m2 = jnp.max(e_log2, axis=-1, keepdims=True)
    i2 = jnp.min(jnp.where(e_log2 == m2, lane, big), axis=-1, keepdims=True)
    r2 = jnp.exp(m2 - m1)
    w1 = g_val / (1.0 + r2)
    w2 = g_val * r2 / (1.0 + r2)
    yield
    rows = logits.shape[0]
    pick0 = lane == i1 - N_GROUPS
    pick1 = lane == i2 - N_GROUPS
    earlier = (lax.broadcasted_iota(jnp.int32, (rows, rows), 1)
               < lax.broadcasted_iota(jnp.int32, (rows, rows), 0)).astype(BF16)
    cnt0 = jnp.sum(pick0.astype(F32), axis=0, keepdims=True)
    cnt1 = jnp.sum(pick1.astype(F32), axis=0, keepdims=True)
    rank0 = jnp.sum(jnp.where(pick0, _dot(earlier, pick0.astype(BF16)), 0.0), axis=-1, keepdims=True)
    rank1 = jnp.sum(jnp.where(pick1, _dot(earlier, pick1.astype(BF16)) + cnt0, 0.0), axis=-1, keepdims=True)
    yield
    columns = ((i1 - N_GROUPS).astype(F32), (i2 - N_GROUPS).astype(F32), w1, w2, rank0, rank1)
    route = jnp.zeros_like(logits)
    for c, value in enumerate(columns):
        route = jnp.where(lane == c, value, route)
    route_ref[rs, :] = route
    hs = slice(r0 // PROJ_ROWS * SUBLANES, (r0 // PROJ_ROWS + 1) * SUBLANES)
    sub = lax.broadcasted_iota(jnp.int32, (SUBLANES, LANES), 0)
    hist_ref[hs, :] = jnp.where(sub == 0, cnt0 + cnt1, 0.0)


N_MERGE_STREAMS = 7


def _merge_kernel(*refs, n_a):
    side_a = refs[0:N_MERGE_STREAMS]
    n_sides = 1 if n_a is None else 2
    side_b = refs[N_MERGE_STREAMS:n_sides * N_MERGE_STREAMS]
    rest = refs[n_sides * N_MERGE_STREAMS:]

    def block(side):
        _run_round_robin([_merge_tile(r0, *side, *rest) for r0 in range(0, MERGE_ROWS, PROJ_ROWS)])

    if n_a is None:
        block(side_a)
        return

    @pl.when(pl.program_id(0) < n_a)
    def _():
        block(side_a)

    @pl.when(pl.program_id(0) >= n_a)
    def _():
        block(side_b)


def _merge(streams_a, streams_b, m_norm, hg_norm, w_out, norm2, w_rg, b_rg, w_re, b_re):
    D = D_MODEL
    rows = MERGE_ROWS
    n_a = streams_a[0].shape[0] // rows
    n_b = streams_b[0].shape[0] // rows if streams_b else 0
    n_all = (n_a + n_b) * rows
    n_log = N_GROUPS + N_EXPERTS
    wr = jnp.pad(jnp.concatenate([w_rg, w_re], axis=1), ((0, 0), (0, LANES - n_log)))
    br = jnp.pad(jnp.concatenate([b_rg, b_re]), (0, LANES - n_log))[None, :]
    wr_hi = wr.astype(BF16)
    wr_lo = (wr - wr_hi.astype(F32)).astype(BF16)
    consts = [m_norm[None, :], hg_norm[None, :], w_out.astype(BF16), norm2[None, :], wr_hi, wr_lo, br]

    def full(arr):
        nd = arr.ndim
        return pl.BlockSpec(arr.shape, lambda i: (0,) * nd)

    def side_a(arr):
        return pl.BlockSpec((rows, arr.shape[1]), lambda i: (jnp.minimum(i, n_a - 1), 0))

    def side_b(arr):
        return pl.BlockSpec((rows, arr.shape[1]), lambda i: (jnp.maximum(i - n_a, 0), 0))

    def out(width):
        return pl.BlockSpec((rows, width), lambda i: (i, 0))

    return pl.pallas_call(
        functools.partial(_merge_kernel, n_a=n_a if streams_b else None),
        grid=(n_a + n_b,),
        in_specs=[side_a(s) for s in streams_a] + [side_b(s) for s in (streams_b or ())] + [full(c) for c in consts],
        out_specs=[out(D), out(D // 2), out(LANES), pl.BlockSpec((rows // PROJ_ROWS * SUBLANES, LANES), lambda i: (i, 0))],
        out_shape=[jax.ShapeDtypeStruct((n_all, D), F32), jax.ShapeDtypeStruct((n_all, D // 2), jnp.uint32),
                   jax.ShapeDtypeStruct((n_all, LANES), F32),
                   jax.ShapeDtypeStruct((n_all // PROJ_ROWS * SUBLANES, LANES), F32)],
        compiler_params=pltpu.CompilerParams(
            dimension_semantics=("arbitrary",), vmem_limit_bytes=VMEM_LIMIT),
        name="merge",
    )(*streams_a, *(streams_b or ()), *consts)


def _sc_row_mover(src, idx, n_out, scatter, name):
    n_moved = idx.shape[0]
    D = src.shape[1]
    n_sub = SC_CORES * SC_SUBCORES
    per = n_moved // n_sub
    window = SC_WINDOW_BYTES // (D * src.dtype.itemsize)
    assert per * n_sub == n_moved and per % window == 0, (n_moved, per, window)
    assert not scatter or src.shape[0] % per == 0, (src.shape, per)
    mesh = plsc.VectorSubcoreMesh(core_axis_name="c", subcore_axis_name="s",
                                  num_cores=SC_CORES, num_subcores=SC_SUBCORES)

    def body(src_hbm, idx_hbm, out_hbm, idx_v, buf):
        base = (lax.axis_index("c") * SC_SUBCORES + lax.axis_index("s")) * per
        pltpu.sync_copy(idx_hbm.at[pl.ds(base, per)], idx_v)

        @pl.loop(0, per // window)
        def _(j):
            linear = pl.ds(base + j * window, window)
            indexed = idx_v.at[pl.ds(j * window, window)]
            if scatter:
                pltpu.sync_copy(src_hbm.at[pl.ds(lax.rem(base, src.shape[0]) + j * window, window)], buf)
                pltpu.sync_copy(buf, out_hbm.at[indexed])
            else:
                pltpu.sync_copy(src_hbm.at[indexed], buf)
                pltpu.sync_copy(buf, out_hbm.at[linear])

    return pl.kernel(
        body,
        out_type=jax.ShapeDtypeStruct((n_out, D), src.dtype),
        mesh=mesh,
        scratch_types=[pltpu.VMEM((per,), jnp.int32), pltpu.VMEM((window, D), src.dtype)],
        name=name,
    )(src, idx)


def _sc_gather_rows(src, idx):
    return _sc_row_mover(src, idx, idx.shape[0], False, "sc_gather_rows")


def _sc_scatter_rows(src, idx, n_out):
    return _sc_row_mover(src, idx, n_out, True, "sc_scatter_rows")


def _expert_kernel(be_ref, nu_ref, x_ref, w1_ref, w3_ref, w2_ref, o_ref):
    @pl.when(pl.program_id(0) < nu_ref[0])
    def _():
        half = D_MODEL // 2
        x_lo, x_hi = (part.astype(BF16) for part in _unpack_bf16_pairs(x_ref[...]))
        a = _dot(x_lo, w1_ref[0, 0:half, :]) + _dot(x_hi, w1_ref[0, half:, :])
        b = _dot(x_lo, w3_ref[0, 0:half, :]) + _dot(x_hi, w3_ref[0, half:, :])
        o_ref[...] = _pack_bf16_pairs(_dot((_silu(a) * b).astype(BF16), w2_ref[0]))


def _experts(xs, block_e, n_used, w1, w3, w2):
    rows = EXPERT_ROWS
    n_blocks = xs.shape[0] // rows
    D = D_MODEL

    def blk(i, be, nu):
        return jnp.minimum(i, nu[0] - 1)

    grid_spec = pltpu.PrefetchScalarGridSpec(
        num_scalar_prefetch=2,
        grid=(n_blocks,),
        in_specs=[
            pl.BlockSpec((rows, D // 2), lambda i, be, nu: (blk(i, be, nu), 0)),
            pl.BlockSpec((1, D, EXPERT_FF), lambda i, be, nu: (be[blk(i, be, nu)], 0, 0)),
            pl.BlockSpec((1, D, EXPERT_FF), lambda i, be, nu: (be[blk(i, be, nu)], 0, 0)),
            pl.BlockSpec((1, EXPERT_FF, D), lambda i, be, nu: (be[blk(i, be, nu)], 0, 0)),
        ],
        out_specs=pl.BlockSpec((rows, D // 2), lambda i, be, nu: (blk(i, be, nu), 0)),
    )
    return pl.pallas_call(
        _expert_kernel,
        grid_spec=grid_spec,
        out_shape=jax.ShapeDtypeStruct((xs.shape[0], D // 2), jnp.uint32),
        compiler_params=pltpu.CompilerParams(
            dimension_semantics=("arbitrary",), vmem_limit_bytes=VMEM_LIMIT),
        name="experts",
    )(block_e, n_used, xs, w1, w3, w2)


def _combine_kernel(y0_ref, y1_ref, x1_ref, route_ref, nf_ref, y_ref):
    route = route_ref[...]
    r0 = jnp.concatenate(_unpack_bf16_pairs(y0_ref[...]), axis=1)
    r1 = jnp.concatenate(_unpack_bf16_pairs(y1_ref[...]), axis=1)
    y_ref[...] = _rms(x1_ref[...] + route[:, 2:3] * r0 + route[:, 3:4] * r1, nf_ref[...])


def _combine(x1, route, y_rows, norm_f, tok0):
    D = x1.shape[1]
    n = y_rows.shape[0] // TOP_K
    rows = COMBINE_ROWS
    assert n % rows == 0 and tok0 % rows == 0, (n, tok0, rows)
    nt = n // rows
    first = tok0 // rows

    def tok(width, offset=0):
        return pl.BlockSpec((rows, width), lambda i: (i + offset, 0))

    return pl.pallas_call(
        _combine_kernel,
        grid=(nt,),
        in_specs=[tok(D // 2), tok(D // 2, nt), tok(D, first), tok(LANES, first),
                  pl.BlockSpec((1, D), lambda i: (0, 0))],
        out_specs=tok(D),
        out_shape=jax.ShapeDtypeStruct((n, D), F32),
        compiler_params=pltpu.CompilerParams(
            dimension_semantics=("parallel",), vmem_limit_bytes=VMEM_LIMIT),
        name="combine",
    )(y_rows, y_rows, x1, route, norm_f[None, :])


def _plan_kernel(route_ref, table_ref, dest_ref):
    lane = lax.broadcasted_iota(jnp.int32, (PROJ_ROWS, LANES), 1)
    lane_f = lane.astype(F32)
    for tile in range(route_ref.shape[0] // PROJ_ROWS):
        rs = slice(tile * PROJ_ROWS, (tile + 1) * PROJ_ROWS)
        route = route_ref[rs, :]
        first = table_ref[tile * SUBLANES:tile * SUBLANES + 1, :]
        d0 = jnp.sum(jnp.where(lane_f == route[:, 0:1], first, 0.0), axis=-1, keepdims=True) + route[:, 4:5]
        d1 = jnp.sum(jnp.where(lane_f == route[:, 1:2], first, 0.0), axis=-1, keepdims=True) + route[:, 5:6]
        cols = jnp.where(lane == 0, d0, jnp.where(lane == 1, d1, 0.0))
        dest_ref[:, rs] = cols.T[0:SUBLANES, :].astype(jnp.int32)


def _dispatch_plan(route, hist):
    N = route.shape[0]
    rows = PROJ_ROWS
    n_tiles = N // rows
    blk = EXPERT_ROWS
    tile_counts = hist.reshape(n_tiles, SUBLANES, LANES)[:, 0, 0:N_EXPERTS].astype(jnp.int32)
    tile_first = jnp.cumsum(tile_counts, axis=0) - tile_counts
    counts = jnp.sum(tile_counts, axis=0)
    padded = ((counts + blk - 1) // blk) * blk
    pad_end = jnp.cumsum(padded)
    pad_start = pad_end - padded
    table = jnp.zeros((n_tiles, SUBLANES, LANES), F32).at[:, 0, 0:N_EXPERTS].set(
        (pad_start[None, :] + tile_first).astype(F32)).reshape(n_tiles * SUBLANES, LANES)
    per_step = min(PLAN_TILES, n_tiles)
    assert n_tiles % per_step == 0, (n_tiles, per_step)
    dest_rows = pl.pallas_call(
        _plan_kernel,
        grid=(n_tiles // per_step,),
        in_specs=[pl.BlockSpec((per_step * rows, LANES), lambda i: (i, 0)),
                  pl.BlockSpec((per_step * SUBLANES, LANES), lambda i: (i, 0))],
        out_specs=pl.BlockSpec((SUBLANES, per_step * rows), lambda i: (0, i)),
        out_shape=jax.ShapeDtypeStruct((SUBLANES, N), jnp.int32),
        compiler_params=pltpu.CompilerParams(dimension_semantics=("parallel",)),
        name="plan",
    )(route, table)
    dest_rows = dest_rows[0:TOP_K]
    n_blocks = (TOP_K * N + N_EXPERTS * (blk - 1) + blk - 1) // blk
    block_start = jnp.arange(n_blocks, dtype=jnp.int32) * blk
    block_e = jnp.sum((pad_end[None, :] <= block_start[:, None]).astype(jnp.int32), axis=1)
    block_e = jnp.minimum(block_e, N_EXPERTS - 1)
    n_used = (pad_end[-1] // blk).astype(jnp.int32).reshape(1)
    return dest_rows, block_e, n_used, n_blocks * blk


def _token_mixer(x, norm1, w_in, b_in, conv_w, conv_b, m_fgate_bias, hg_lb_logits):
    B, T, D = x.shape
    q, k, v, mo, gcol, grow, hq, g_f, g_b, hv, hgg = _in_proj(
        x, norm1, w_in, b_in, conv_w, conv_b, m_fgate_bias, hg_lb_logits)
    h_f, o_f, h_b, o_b = _mixer(q, k, v, gcol, grow, hq, g_f, g_b, hv)
    return [a.reshape(B * T, a.shape[-1]) for a in (h_f, h_b, o_f, o_b, mo, hgg, x)]


def kernel(x_prompt, x_sample, norm1, w_in, b_in, conv_w, conv_b, m_fgate_bias, m_norm, hg_lb_logits, hg_norm,
           w_out, norm2, w_router_group, b_router_group, w_router_expert, b_router_expert, w1, w3, w2, norm_f):
    layer = 0
    mixer_args = (norm1[layer], w_in[layer], b_in[layer], conv_w[layer], conv_b[layer], m_fgate_bias[layer],
                  hg_lb_logits)
    expert_w = [w[layer].astype(BF16) for w in (w1, w3, w2)]
    outs = []
    for x in (x_prompt, x_sample):
        streams = _token_mixer(x, *mixer_args)
        x1, h2, route, hist = _merge(streams, None, m_norm[layer], hg_norm[layer], w_out[layer], norm2[layer],
                                     w_router_group[layer], b_router_group[layer], w_router_expert[layer],
                                     b_router_expert[layer])
        dest, block_e, n_used, n_rows = _dispatch_plan(route, hist)
        n = h2.shape[0]
        xs = _sc_scatter_rows(h2, dest.reshape(TOP_K * n), n_rows)
        out_rows = _experts(xs, block_e, n_used, *expert_w)
        y_rows = _sc_gather_rows(out_rows, dest.reshape(TOP_K * n))
        outs.append(_combine(x1, route, y_rows, norm_f, 0).reshape(x.shape))
    return tuple(outs)
```

```python
import jax
import jax.numpy as jnp
from jax import lax
from jax.experimental import pallas as pl
from jax.experimental.pallas import tpu as pltpu
from jax.experimental.pallas import tpu_sc as plsc

F32 = jnp.float32
BF16 = jnp.bfloat16

D_MODEL = 1024
N_HEADS = 4
D_HEAD = 128
WIDTH = N_HEADS * D_HEAD
CONV_K = 5
CONV_PAD = CONV_K // 2
N_GROUPS = 4
EXPERTS_PER_GROUP = 8
N_EXPERTS = N_GROUPS * EXPERTS_PER_GROUP
TOP_K = 2
EXPERT_FF = D_MODEL // 2
NORM_EPS = 1e-6

LANES = 128
SUBLANES = 8
CHUNK = 128
MIXER_CHUNKS = 2
PROJ_ROWS = 256
COMBINE_ROWS = 512
MERGE_ROWS = 512
PLAN_TILES = 8
IN_PROJ_ROWS = 512
IN_PROJ_STREAM = 256
HALO = SUBLANES
EXPERT_ROWS = 512
N_GATES = 4 * N_HEADS
SC_CORES = 2
SC_SUBCORES = 16
SC_WINDOW_BYTES = 128 * 1024
VMEM_LIMIT = 56 * 1024 * 1024


def _dot(a, b):
    return jnp.dot(a, b, preferred_element_type=F32)


def _dot_nt(a, b):
    return lax.dot_general(a, b, (((1,), (1,)), ((), ())), preferred_element_type=F32)


def _dot_tn(a, b):
    return lax.dot_general(a, b, (((0,), (0,)), ((), ())), preferred_element_type=F32)


def _split3(x):
    hi = x.astype(BF16)
    r1 = x - hi.astype(F32)
    mid = r1.astype(BF16)
    lo = (r1 - mid.astype(F32)).astype(BF16)
    return hi, mid, lo


def _pack_bf16_pairs(x):
    half = x.shape[1] // 2
    bits = lax.bitcast_convert_type(x.astype(BF16).astype(F32), jnp.uint32)
    return (bits[:, half:] & jnp.uint32(0xFFFF0000)) | (bits[:, :half] >> 16)


def _unpack_bf16_pairs(words):
    lo = lax.bitcast_convert_type(words << 16, F32)
    hi = lax.bitcast_convert_type(words & jnp.uint32(0xFFFF0000), F32)
    return lo, hi


def _silu(x):
    return x * jax.nn.sigmoid(x)


def _log_sigmoid(x):
    return -(jnp.maximum(-x, 0.0) + jnp.log1p(jnp.exp(-jnp.abs(x))))


def _rms(x, gain):
    return x * lax.rsqrt(jnp.mean(x * x, axis=-1, keepdims=True) + NORM_EPS) * gain


def _run_round_robin(generators):
    live = list(generators)
    while live:
        for gen in list(live):
            try:
                next(gen)
            except StopIteration:
                live.remove(gen)


def _in_proj_kernel(x_ref, xp_ref, xn_ref, n1_ref, wa_ref, ba_ref, wg_ref, bg_ref, wgt_ref, bgt_ref,
                    fbrow_ref, fbcol_ref, wh_ref, bh_ref, cw_ref, cb_ref, lbl_ref,
                    q_ref, k_ref, v_ref, mo_ref, gcol_ref, grow_ref, hq_ref, gf_ref, gb_ref, hv_ref, hgg_ref,
                    ext_ref):
    t = pl.program_id(1)
    nt = pl.num_programs(1)
    rows = x_ref.shape[1]
    gain = n1_ref[...]

    lbl = lbl_ref[...]
    lmax = jnp.max(lbl, axis=0, keepdims=True)
    le = jnp.exp(lbl - lmax)
    lb = le[0:1, :] / jnp.sum(le, axis=0, keepdims=True)

    wqk = wa_ref[:, 0:2 * WIDTH]
    bqk = ba_ref[:, 0:2 * WIDTH]
    hp = _rms(xp_ref[0], gain).astype(BF16)
    hn = _rms(xn_ref[0], gain).astype(BF16)
    ext_ref[0:HALO, :] = (_dot(hp, wqk) + bqk) * (t > 0).astype(F32)
    ext_ref[HALO + rows:2 * HALO + rows, :] = (_dot(hn, wqk) + bqk) * (t < nt - 1).astype(F32)

    def stream(r0, n):
        rs = slice(r0, r0 + n)
        h = _rms(x_ref[0, rs, :], gain).astype(BF16)

        def proj(w_ref, b_ref, lo, hi):
            return _dot(h, w_ref[:, lo:hi]) + b_ref[:, lo:hi]

        ext_ref[HALO + r0:HALO + r0 + n, :] = proj(wa_ref, ba_ref, 0, 2 * WIDTH)
        hq_pre = proj(wh_ref, bh_ref, 0, WIDTH)
        yield
        acc = cb_ref[...] + ext_ref[pl.ds(HALO - CONV_PAD + r0, n), :] * cw_ref[0:1, :]
        for j in range(1, CONV_K):
            acc = acc + ext_ref[pl.ds(HALO - CONV_PAD + j + r0, n), :] * cw_ref[j:j + 1, :]
        qk = _silu(acc)
        q_ref[0, rs, :] = (qk[:, 0:WIDTH] * (D_HEAD ** -0.5)).astype(BF16)
        k_ref[0, rs, :] = qk[:, WIDTH:2 * WIDTH]
        v_ref[0, rs, :] = proj(wa_ref, ba_ref, 2 * WIDTH, 3 * WIDTH).astype(BF16)
        yield
        hq_ref[0, rs, :] = _silu(hq_pre)
        mo_ref[0, rs, :] = jax.nn.sigmoid(proj(wa_ref, ba_ref, 3 * WIDTH, 4 * WIDTH))
        yield
        gf_ref[0, rs, :] = lb + (1.0 - lb) * jax.nn.sigmoid(proj(wh_ref, bh_ref, WIDTH, 2 * WIDTH))
        yield
        gb_ref[0, rs, :] = lb + (1.0 - lb) * jax.nn.sigmoid(proj(wh_ref, bh_ref, 2 * WIDTH, 3 * WIDTH))
        hv_ref[0, rs, :] = proj(wh_ref, bh_ref, 3 * WIDTH, 4 * WIDTH).astype(BF16)
        yield
        hgg_ref[0, rs, :] = _silu(proj(wh_ref, bh_ref, 4 * WIDTH, 5 * WIDTH))
        gc = _dot(h, wg_ref[...]) + bg_ref[...]
        lane = lax.broadcasted_iota(jnp.int32, gc.shape, 1)
        is_f = (lane >= 2 * N_HEADS) & (lane < N_GATES)
        gcol_ref[0, rs, :] = jnp.where(is_f, _log_sigmoid(gc + fbrow_ref[...]), gc)
        gr = _dot_nt(wgt_ref[...], h) + bgt_ref[...]
        sub = lax.broadcasted_iota(jnp.int32, gr.shape, 0)
        grow_ref[0, :, rs] = jnp.where(sub >= 2 * N_HEADS, _log_sigmoid(gr + fbcol_ref[...]), gr)

    _run_round_robin([stream(r0, IN_PROJ_STREAM) for r0 in range(0, rows, IN_PROJ_STREAM)])


def _in_proj(x, norm1, w_in, b_in, conv_w, conv_b, fgate_bias, lb_logits):
    B, T, D = x.shape
    rows = IN_PROJ_ROWS
    assert T % rows == 0, (T, rows)
    nt = T // rows
    a_w = 4 * WIDTH
    wa = w_in[:, 0:a_w].astype(BF16)
    ba = b_in[None, 0:a_w]
    wg32 = jnp.pad(w_in[:, a_w:a_w + N_GATES], ((0, 0), (0, LANES - N_GATES)))
    bg = jnp.pad(b_in[a_w:a_w + N_GATES], (0, LANES - N_GATES))[None, :]
    wg = wg32.astype(BF16)
    wgt = w_in[:, a_w:a_w + N_GATES].T.astype(BF16)
    bgt = b_in[a_w:a_w + N_GATES][:, None]
    fb = fgate_bias.reshape(2 * N_HEADS)
    fbrow = jnp.zeros((1, LANES), F32).at[0, 2 * N_HEADS:N_GATES].set(fb)
    fbcol = jnp.zeros((N_GATES, 1), F32).at[2 * N_HEADS:N_GATES, 0].set(fb)
    wh = w_in[:, a_w + N_GATES:].astype(BF16)
    bh = b_in[None, a_w + N_GATES:]

    tiles_per_halo = rows // HALO
    n_halo = T // HALO

    def full(arr):
        nd = arr.ndim
        return pl.BlockSpec(arr.shape, lambda b, t: (0,) * nd)

    def tok(width):
        return pl.BlockSpec((1, rows, width), lambda b, t: (b, t, 0))

    in_specs = [
        tok(D),
        pl.BlockSpec((1, HALO, D), lambda b, t: (b, jnp.maximum(t * tiles_per_halo - 1, 0), 0)),
        pl.BlockSpec((1, HALO, D), lambda b, t: (b, jnp.minimum((t + 1) * tiles_per_halo, n_halo - 1), 0)),
    ]
    consts = [norm1[None, :], wa, ba, wg, bg, wgt, bgt, fbrow, fbcol, wh, bh, conv_w, conv_b[None, :], lb_logits]
    in_specs += [full(c) for c in consts]
    tok_out = jax.ShapeDtypeStruct((B, T, WIDTH), F32)
    tok_bf = jax.ShapeDtypeStruct((B, T, WIDTH), BF16)
    out_shape = [tok_bf, tok_out, tok_bf, tok_out,
                 jax.ShapeDtypeStruct((B, T, LANES), F32),
                 jax.ShapeDtypeStruct((B, N_GATES, T), F32),
                 tok_out, tok_out, tok_out, tok_bf, tok_out]
    out_specs = [tok(WIDTH)] * 4 + [tok(LANES), pl.BlockSpec((1, N_GATES, rows), lambda b, t: (b, 0, t))] + [tok(WIDTH)] * 5
    return pl.pallas_call(
        _in_proj_kernel,
        grid=(B, nt),
        in_specs=in_specs,
        out_specs=out_specs,
        out_shape=out_shape,
        scratch_shapes=[pltpu.VMEM((rows + 2 * HALO, 2 * WIDTH), F32)],
        compiler_params=pltpu.CompilerParams(
            dimension_semantics=("parallel", "parallel"), vmem_limit_bytes=VMEM_LIMIT),
        name="in_proj",
    )(x, x, x, *consts)


def _cumsum_rows(tri_bf, x):
    hi, mid, lo = _split3(x)
    return _dot(tri_bf, hi) + _dot(tri_bf, mid) + _dot(tri_bf, lo)


def _cumsum_lanes(x, tri_bf):
    hi, mid, lo = _split3(x)
    return _dot(hi, tri_bf) + _dot(mid, tri_bf) + _dot(lo, tri_bf)


def _mlstm_chunk(q, k, vext, i_col, b_col, i_row, b_row, seen, last, c_ref, m_ref, out_ref, rs, sl):
    m_prev = m_ref[:, 0:1]
    c_prev = c_ref[...]
    q_bf = q
    log_d = jnp.where(seen, b_col - b_row + i_row, -jnp.inf)
    m_inter = b_col + m_prev
    m_t = jnp.maximum(m_inter, jnp.max(log_d, axis=-1, keepdims=True))
    qk = _dot_nt(q_bf, k.astype(BF16))
    yield
    scores = (qk * jnp.exp(log_d - m_t)).astype(BF16)
    inter_scale = jnp.exp(m_inter - m_t)
    b_last = b_col[last:last + 1, :]
    log_w = b_last - b_col + i_col
    m_new = jnp.maximum(b_last + m_prev, jnp.max(log_w, axis=0, keepdims=True))
    w = jnp.exp(log_w - m_new)
    decay = jnp.exp(b_last + m_prev - m_new)
    kw = (k * w).astype(BF16)
    yield
    numden = _dot(scores, vext) + inter_scale * _dot(q_bf, c_prev.astype(BF16))
    update = _dot_tn(kw, vext)
    yield
    num = numden[:, 0:D_HEAD]
    den = numden[:, D_HEAD:2 * D_HEAD]
    out_ref[0, rs, sl] = num / jnp.maximum(jnp.abs(den), jnp.exp(-m_t))
    c_ref[...] = decay * c_prev + update
    m_ref[...] = jnp.broadcast_to(m_new, (1, LANES))


def _hgrn2_level_small(q3, k3, pre3, suf3, half, rev, sub_iota):
    upper = (sub_iota & half) != 0
    second = jnp.logical_not(upper) if rev else upper
    end = 0 if rev else half - 1
    y = jnp.where((sub_iota & (half - 1)) == end, pre3, 0.0)
    step = 1 if rev else -1
    span = 1
    while span < half:
        y = y + pltpu.roll(y, (step * span) % SUBLANES, 1)
        span *= 2
    if 2 * half == SUBLANES:
        other = pltpu.roll(y, half, 1)
    else:
        other = jnp.where(upper, pltpu.roll(y, half, 1), pltpu.roll(y, SUBLANES - half, 1))
    z = jnp.where(second, q3 * pre3, k3 * suf3)
    return z, pre3 * jnp.where(second, other, 1.0), suf3 * jnp.where(second, 1.0, other)


def _hgrn2_level_big(q, k, pre, suf, half, rev):
    L, width = q.shape
    shape = (L // (2 * half), 2, half, width)
    q4, k4, pre4, suf4 = (a.reshape(shape) for a in (q, k, pre, suf))
    first = 1 if rev else 0
    second = 1 - first
    end = 0 if rev else half - 1
    total_first = pre4[:, first, end:end + 1, :]
    total_second = pre4[:, second, end:end + 1, :]

    def join(at_first, at_second):
        parts = (at_second, at_first) if rev else (at_first, at_second)
        return jnp.stack(parts, axis=1).reshape(L, width)

    z = join(k4[:, first] * suf4[:, first], q4[:, second] * pre4[:, second])
    pre_new = join(pre4[:, first], pre4[:, second] * total_first)
    suf_new = join(suf4[:, first] * total_second, suf4[:, second])
    return z, pre_new, suf_new


def _hgrn2_chunk(q, g, v_bf, rev, level, diag, sub_iota, st_refs, out_ref, rs, sl):
    L, width = q.shape
    heads = [slice(h * D_HEAD, (h + 1) * D_HEAD) for h in range(width // D_HEAD)]
    k = 1.0 - g
    q_bf = q.astype(BF16)
    k_bf = k.astype(BF16)
    att = [jnp.where(diag, _dot_nt(q_bf[:, s], k_bf[:, s]).astype(BF16), jnp.zeros((), BF16)) for s in heads]
    small = (L // SUBLANES, SUBLANES, width)
    q3, k3, pre, suf = q.reshape(small), k.reshape(small), g.reshape(small), jnp.ones(small, F32)
    half = 1
    bit = 0
    while half < L:
        if half == SUBLANES:
            pre, suf = pre.reshape(L, width), suf.reshape(L, width)
        if half < SUBLANES:
            z, pre, suf = _hgrn2_level_small(q3, k3, pre, suf, half, rev, sub_iota)
            z = z.reshape(L, width)
        else:
            z, pre, suf = _hgrn2_level_big(q, k, pre, suf, half, rev)
        z = z.astype(BF16)
        att = [jnp.where(level == bit, _dot_nt(z[:, s], z[:, s]).astype(BF16), a) for a, s in zip(att, heads)]
        half *= 2
        bit += 1
        yield
    last = 0 if rev else L - 1
    q_dec = (q * pre).astype(BF16)
    k_dec = (k * suf).astype(BF16)
    outs = []
    for h, s in enumerate(heads):
        st_prev = st_refs[h][...]
        outs.append(_dot_nt(q_dec[:, s], st_prev.astype(BF16)) + _dot(att[h], v_bf[:, s]))
        st_refs[h][...] = st_prev * pre[last:last + 1, s] + _dot_tn(v_bf[:, s], k_dec[:, s])
    out_ref[0, rs, sl] = jnp.concatenate(outs, axis=1)


def _mixer_kernel(qf_ref, kf_ref, vf_ref, gcf_ref, grf_ref, hqf_ref, hgf_ref, hvf_ref,
                  qb_ref, kb_ref, vb_ref, gcb_ref, grb_ref, hqb_ref, hgb_ref, hvb_ref,
                  hf_ref, of_ref, hb_ref, ob_ref, *state_refs):
    L = CHUNK
    n_state = 2 * N_HEADS
    c_refs, m_refs, st_refs = (state_refs[i * n_state:(i + 1) * n_state] for i in range(3))

    @pl.when(pl.program_id(1) == 0)
    def _():
        for ref in state_refs:
            ref[...] = jnp.zeros_like(ref)

    row = lax.broadcasted_iota(jnp.int32, (L, L), 0)
    col = lax.broadcasted_iota(jnp.int32, (L, L), 1)
    sub_iota = lax.broadcasted_iota(jnp.int32, (L // SUBLANES, SUBLANES, LANES), 1)
    diag = row == col
    diff = row ^ col
    high_bit = jnp.zeros((L, L), jnp.int32)
    half = 2
    while half < L:
        high_bit = high_bit + (diff >= half).astype(jnp.int32)
        half *= 2
    ones = jnp.ones((L, D_HEAD), BF16)

    dirs = (
        (0, qf_ref, kf_ref, vf_ref, gcf_ref, grf_ref, hqf_ref, hgf_ref, hvf_ref, hf_ref, of_ref),
        (1, qb_ref, kb_ref, vb_ref, gcb_ref, grb_ref, hqb_ref, hgb_ref, hvb_ref, hb_ref, ob_ref),
    )
    masks = []
    for rev in (False, True):
        seen = (col >= row) if rev else (col <= row)
        before = (col > row) if rev else (col < row)
        level = jnp.where(before, high_bit, -1)
        tri = seen.astype(BF16)
        tri_t = (row >= col if rev else row <= col).astype(BF16)
        masks.append((seen, level, tri, tri_t))
    n_sub = qf_ref.shape[1] // L
    for step in range(n_sub):
        stages = []
        for d, q_ref, k_ref, v_ref, gc_ref, gr_ref, hq_ref, hg_ref, hv_ref, h_out, o_out in dirs:
            rev = d == 1
            seen, level, tri, tri_t = masks[d]
            last = 0 if rev else L - 1
            sub_chunk = n_sub - 1 - step if rev else step
            rs = slice(sub_chunk * L, (sub_chunk + 1) * L)
            gc = gc_ref[0, rs, :]
            gr = gr_ref[0, :, rs]
            gc_cum = _cumsum_rows(tri, gc)
            gr_cum = _cumsum_lanes(gr, tri_t)
            for hd in range(N_HEADS):
                sl = slice(hd * D_HEAD, (hd + 1) * D_HEAD)
                gi = d * N_HEADS + hd
                gf = 2 * N_HEADS + gi
                idx = d * N_HEADS + hd
                vext = jnp.concatenate([v_ref[0, rs, sl], ones], axis=1)
                stages.append(_mlstm_chunk(
                    q_ref[0, rs, sl], k_ref[0, rs, sl], vext,
                    gc[:, gi:gi + 1], gc_cum[:, gf:gf + 1], gr[gi:gi + 1, :], gr_cum[gf:gf + 1, :],
                    seen, last, c_refs[idx], m_refs[idx], h_out, rs, sl))
                stages.append(_hgrn2_chunk(
                    hq_ref[0, rs, sl], hg_ref[0, rs, sl], hv_ref[0, rs, sl],
                    rev, level, diag, sub_iota, [st_refs[idx]], o_out, rs, sl))
        group = 2 * N_HEADS
        for start in range(0, len(stages), group):
            _run_round_robin(stages[start:start + group])


def _mixer(q, k, v, gcol, grow, hq, g_f, g_b, hv):
    B, T, _ = q.shape
    L = CHUNK * MIXER_CHUNKS
    nc = T // L

    def fwd(width):
        return pl.BlockSpec((1, L, width), lambda b, c: (b, c, 0))

    def bwd(width):
        return pl.BlockSpec((1, L, width), lambda b, c: (b, nc - 1 - c, 0))

    grow_f = pl.BlockSpec((1, N_GATES, L), lambda b, c: (b, 0, c))
    grow_b = pl.BlockSpec((1, N_GATES, L), lambda b, c: (b, 0, nc - 1 - c))
    in_specs = ([fwd(WIDTH)] * 3 + [fwd(LANES), grow_f] + [fwd(WIDTH)] * 3
                + [bwd(WIDTH)] * 3 + [bwd(LANES), grow_b] + [bwd(WIDTH)] * 3)
    out = jax.ShapeDtypeStruct((B, T, WIDTH), F32)
    n_state = 2 * N_HEADS
    return pl.pallas_call(
        _mixer_kernel,
        grid=(B, nc),
        in_specs=in_specs,
        out_specs=[fwd(WIDTH), fwd(WIDTH), bwd(WIDTH), bwd(WIDTH)],
        out_shape=[out, out, out, out],
        scratch_shapes=([pltpu.VMEM((D_HEAD, 2 * D_HEAD), F32)] * n_state
                        + [pltpu.VMEM((1, LANES), F32)] * n_state
                        + [pltpu.VMEM((D_HEAD, D_HEAD), F32)] * n_state),
        compiler_params=pltpu.CompilerParams(
            dimension_semantics=("parallel", "arbitrary"), vmem_limit_bytes=VMEM_LIMIT),
        name="mixer",
    )(q, k, v, gcol, grow, hq, g_f, hv, q, k, v, gcol, grow, hq, g_b, hv)


def _head_norm(hsum, gain):
    parts = []
    for hd in range(N_HEADS):
        hh = hsum[:, hd * D_HEAD:(hd + 1) * D_HEAD]
        parts.append(hh * lax.rsqrt(jnp.mean(hh * hh, axis=-1, keepdims=True) + NORM_EPS))
    return jnp.concatenate(parts, axis=1) * gain


def _merge_tile(r0, hf_ref, hb_ref, of_ref, ob_ref, mo_ref, hgg_ref, x_ref, mn_ref, hn_ref, wo_ref,
                n2_ref, wrh_ref, wrl_ref, br_ref, x1_ref, h2_ref, route_ref, hist_ref):
    rs = slice(r0, r0 + PROJ_ROWS)
    m_out = _head_norm(hf_ref[rs, :] + hb_ref[rs, :], mn_ref[...]) * mo_ref[rs, :]
    hg_out = _head_norm(of_ref[rs, :] + ob_ref[rs, :], hn_ref[...]) * hgg_ref[rs, :]
    mixed = jnp.concatenate([m_out, hg_out], axis=1).astype(BF16)
    yield
    x1 = x_ref[rs, :] + _dot(mixed, wo_ref[...])
    x1_ref[rs, :] = x1
    h2 = _rms(x1, n2_ref[...])
    h2_ref[rs, :] = _pack_bf16_pairs(h2)
    h_hi = h2.astype(BF16)
    h_hi32 = h_hi.astype(F32)
    yield

    h_lo = (h2 - h_hi32).astype(BF16)
    logits = _dot(h_hi, wrh_ref[...]) + _dot(h_lo, wrh_ref[...]) + _dot(h_hi, wrl_ref[...]) + br_ref[...]
    lane = lax.broadcasted_iota(jnp.int32, logits.shape, 1)
    big = jnp.int32(LANES)
    neg = -jnp.inf
    yield
    g_log = jnp.where(lane < N_GROUPS, logits, neg)
    g_max = jnp.max(g_log, axis=-1, keepdims=True)
    g_idx = jnp.min(jnp.where(g_log == g_max, lane, big), axis=-1, keepdims=True)
    g_val = 1.0 / jnp.sum(jnp.exp(g_log - g_max), axis=-1, keepdims=True)
    yield
    e_lo = N_GROUPS + g_idx * EXPERTS_PER_GROUP
    e_log = jnp.where((lane >= e_lo) & (lane < e_lo + EXPERTS_PER_GROUP), logits, neg)
    m1 = jnp.max(e_log, axis=-1, keepdims=True)
    i1 = jnp.min(jnp.where(e_log == m1, lane, big), axis=-1, keepdims=True)
    yield
    e_log2 = jnp.where(lane == i1, neg, e_log)
    m2 = jnp.max(e_log2, axis=-1, keepdims=True)
    i2 = jnp.min(jnp.where(e_log2 == m2, lane, big), axis=-1, keepdims=True)
    r2 = jnp.exp(m2 - m1)
    w1 = g_val / (1.0 + r2)
    w2 = g_val * r2 / (1.0 + r2)
    yield
    rows = logits.shape[0]
    pick0 = lane == i1 - N_GROUPS
    pick1 = lane == i2 - N_GROUPS
    earlier = (lax.broadcasted_iota(jnp.int32, (rows, rows), 1)
               < lax.broadcasted_iota(jnp.int32, (rows, rows), 0)).astype(BF16)
    cnt0 = jnp.sum(pick0.astype(F32), axis=0, keepdims=True)
    cnt1 = jnp.sum(pick1.astype(F32), axis=0, keepdims=True)
    rank0 = jnp.sum(jnp.where(pick0, _dot(earlier, pick0.astype(BF16)), 0.0), axis=-1, keepdims=True)
    rank1 = jnp.sum(jnp.where(pick1, _dot(earlier, pick1.astype(BF16)) + cnt0, 0.0), axis=-1, keepdims=True)
    yield
    columns = ((i1 - N_GROUPS).astype(F32), (i2 - N_GROUPS).astype(F32), w1, w2, rank0, rank1)
    route = jnp.zeros_like(logits)
    for c, value in enumerate(columns):
        route = jnp.where(lane == c, value, route)
    route_ref[rs, :] = route
    hs = slice(r0 // PROJ_ROWS * SUBLANES, (r0 // PROJ_ROWS + 1) * SUBLANES)
    sub = lax.broadcasted_iota(jnp.int32, (SUBLANES, LANES), 0)
    hist_ref[hs, :] = jnp.where(sub == 0, cnt0 + cnt1, 0.0)


N_MERGE_STREAMS = 7


def _merge_kernel(*refs):
    _run_round_robin([_merge_tile(r0, *refs) for r0 in range(0, MERGE_ROWS, PROJ_ROWS)])


def _merge(streams, m_norm, hg_norm, w_out, norm2, w_rg, b_rg, w_re, b_re):
    D = D_MODEL
    rows = MERGE_ROWS
    n_all = streams[0].shape[0]
    assert n_all % rows == 0, (n_all, rows)
    n_log = N_GROUPS + N_EXPERTS
    wr = jnp.pad(jnp.concatenate([w_rg, w_re], axis=1), ((0, 0), (0, LANES - n_log)))
    br = jnp.pad(jnp.concatenate([b_rg, b_re]), (0, LANES - n_log))[None, :]
    wr_hi = wr.astype(BF16)
    wr_lo = (wr - wr_hi.astype(F32)).astype(BF16)
    consts = [m_norm[None, :], hg_norm[None, :], w_out.astype(BF16), norm2[None, :], wr_hi, wr_lo, br]

    def full(arr):
        nd = arr.ndim
        return pl.BlockSpec(arr.shape, lambda i: (0,) * nd)

    def tok(width):
        return pl.BlockSpec((rows, width), lambda i: (i, 0))

    return pl.pallas_call(
        _merge_kernel,
        grid=(n_all // rows,),
        in_specs=[tok(s.shape[1]) for s in streams] + [full(c) for c in consts],
        out_specs=[tok(D), tok(D // 2), tok(LANES), pl.BlockSpec((rows // PROJ_ROWS * SUBLANES, LANES), lambda i: (i, 0))],
        out_shape=[jax.ShapeDtypeStruct((n_all, D), F32), jax.ShapeDtypeStruct((n_all, D // 2), jnp.uint32),
                   jax.ShapeDtypeStruct((n_all, LANES), F32),
                   jax.ShapeDtypeStruct((n_all // PROJ_ROWS * SUBLANES, LANES), F32)],
        compiler_params=pltpu.CompilerParams(
            dimension_semantics=("parallel",), vmem_limit_bytes=VMEM_LIMIT),
        name="merge",
    )(*streams, *consts)


def _sc_row_mover(src, idx, n_out, scatter, name):
    n_moved = idx.shape[0]
    D = src.shape[1]
    n_sub = SC_CORES * SC_SUBCORES
    per = n_moved // n_sub
    window = SC_WINDOW_BYTES // (D * src.dtype.itemsize)
    assert per * n_sub == n_moved and per % window == 0, (n_moved, per, window)
    assert not scatter or src.shape[0] % per == 0, (src.shape, per)
    mesh = plsc.VectorSubcoreMesh(core_axis_name="c", subcore_axis_name="s",
                                  num_cores=SC_CORES, num_subcores=SC_SUBCORES)

    def body(src_hbm, idx_hbm, out_hbm, idx_v, buf):
        base = (lax.axis_index("c") * SC_SUBCORES + lax.axis_index("s")) * per
        pltpu.sync_copy(idx_hbm.at[pl.ds(base, per)], idx_v)

        @pl.loop(0, per // window)
        def _(j):
            linear = pl.ds(base + j * window, window)
            indexed = idx_v.at[pl.ds(j * window, window)]
            if scatter:
                pltpu.sync_copy(src_hbm.at[pl.ds(lax.rem(base, src.shape[0]) + j * window, window)], buf)
                pltpu.sync_copy(buf, out_hbm.at[indexed])
            else:
                pltpu.sync_copy(src_hbm.at[indexed], buf)
                pltpu.sync_copy(buf, out_hbm.at[linear])

    return pl.kernel(
        body,
        out_type=jax.ShapeDtypeStruct((n_out, D), src.dtype),
        mesh=mesh,
        scratch_types=[pltpu.VMEM((per,), jnp.int32), pltpu.VMEM((window, D), src.dtype)],
        name=name,
    )(src, idx)


def _sc_gather_rows(src, idx):
    return _sc_row_mover(src, idx, idx.shape[0], False, "sc_gather_rows")


def _sc_scatter_rows(src, idx, n_out):
    return _sc_row_mover(src, idx, n_out, True, "sc_scatter_rows")


def _expert_kernel(be_ref, nu_ref, x_ref, w1_ref, w3_ref, w2_ref, o_ref, w1_bf, w3_bf, w2_bf):
    i = pl.program_id(0)
    active = i < nu_ref[0]
    new_expert = (i == 0) | (be_ref[i] != be_ref[jnp.maximum(i - 1, 0)])

    @pl.when(active & new_expert)
    def _():
        w1_bf[...] = w1_ref[0].astype(BF16)
        w3_bf[...] = w3_ref[0].astype(BF16)
        w2_bf[...] = w2_ref[0].astype(BF16)

    @pl.when(active)
    def _():
        half = D_MODEL // 2
        x_lo, x_hi = (part.astype(BF16) for part in _unpack_bf16_pairs(x_ref[...]))
        a = _dot(x_lo, w1_bf[0:half, :]) + _dot(x_hi, w1_bf[half:, :])
        b = _dot(x_lo, w3_bf[0:half, :]) + _dot(x_hi, w3_bf[half:, :])
        o_ref[...] = _pack_bf16_pairs(_dot((_silu(a) * b).astype(BF16), w2_bf[...]))


def _experts(xs, block_e, n_used, w1, w3, w2):
    rows = EXPERT_ROWS
    n_blocks = xs.shape[0] // rows
    D = D_MODEL

    def blk(i, be, nu):
        return jnp.minimum(i, nu[0] - 1)

    grid_spec = pltpu.PrefetchScalarGridSpec(
        num_scalar_prefetch=2,
        grid=(n_blocks,),
        in_specs=[
            pl.BlockSpec((rows, D // 2), lambda i, be, nu: (blk(i, be, nu), 0)),
            pl.BlockSpec((1, D, EXPERT_FF), lambda i, be, nu: (be[blk(i, be, nu)], 0, 0)),
            pl.BlockSpec((1, D, EXPERT_FF), lambda i, be, nu: (be[blk(i, be, nu)], 0, 0)),
            pl.BlockSpec((1, EXPERT_FF, D), lambda i, be, nu: (be[blk(i, be, nu)], 0, 0)),
        ],
        out_specs=pl.BlockSpec((rows, D // 2), lambda i, be, nu: (blk(i, be, nu), 0)),
        scratch_shapes=[pltpu.VMEM((D, EXPERT_FF), BF16), pltpu.VMEM((D, EXPERT_FF), BF16),
                        pltpu.VMEM((EXPERT_FF, D), BF16)],
    )
    return pl.pallas_call(
        _expert_kernel,
        grid_spec=grid_spec,
        out_shape=jax.ShapeDtypeStruct((xs.shape[0], D // 2), jnp.uint32),
        compiler_params=pltpu.CompilerParams(
            dimension_semantics=("arbitrary",), vmem_limit_bytes=VMEM_LIMIT),
        name="experts",
    )(block_e, n_used, xs, w1, w3, w2)


def _combine_kernel(y0_ref, y1_ref, x1_ref, route_ref, nf_ref, y_ref):
    route = route_ref[...]
    r0 = jnp.concatenate(_unpack_bf16_pairs(y0_ref[...]), axis=1)
    r1 = jnp.concatenate(_unpack_bf16_pairs(y1_ref[...]), axis=1)
    y_ref[...] = _rms(x1_ref[...] + route[:, 2:3] * r0 + route[:, 3:4] * r1, nf_ref[...])


def _combine(x1, route, y_rows, norm_f, tok0):
    D = x1.shape[1]
    n = y_rows.shape[0] // TOP_K
    rows = COMBINE_ROWS
    assert n % rows == 0 and tok0 % rows == 0, (n, tok0, rows)
    nt = n // rows
    first = tok0 // rows

    def tok(width, offset=0):
        return pl.BlockSpec((rows, width), lambda i: (i + offset, 0))

    return pl.pallas_call(
        _combine_kernel,
        grid=(nt,),
        in_specs=[tok(D // 2), tok(D // 2, nt), tok(D, first), tok(LANES, first),
                  pl.BlockSpec((1, D), lambda i: (0, 0))],
        out_specs=tok(D),
        out_shape=jax.ShapeDtypeStruct((n, D), F32),
        compiler_params=pltpu.CompilerParams(
            dimension_semantics=("parallel",), vmem_limit_bytes=VMEM_LIMIT),
        name="combine",
    )(y_rows, y_rows, x1, route, norm_f[None, :])


def _plan_kernel(route_ref, table_ref, dest_ref):
    lane = lax.broadcasted_iota(jnp.int32, (PROJ_ROWS, LANES), 1)
    lane_f = lane.astype(F32)
    for tile in range(route_ref.shape[0] // PROJ_ROWS):
        rs = slice(tile * PROJ_ROWS, (tile + 1) * PROJ_ROWS)
        route = route_ref[rs, :]
        first = table_ref[tile * SUBLANES:tile * SUBLANES + 1, :]
        d0 = jnp.sum(jnp.where(lane_f == route[:, 0:1], first, 0.0), axis=-1, keepdims=True) + route[:, 4:5]
        d1 = jnp.sum(jnp.where(lane_f == route[:, 1:2], first, 0.0), axis=-1, keepdims=True) + route[:, 5:6]
        cols = jnp.where(lane == 0, d0, jnp.where(lane == 1, d1, 0.0))
        dest_ref[:, rs] = cols.T[0:SUBLANES, :].astype(jnp.int32)


def _dispatch_plan(route, hist):
    N = route.shape[0]
    rows = PROJ_ROWS
    n_tiles = N // rows
    blk = EXPERT_ROWS
    tile_counts = hist.reshape(n_tiles, SUBLANES, LANES)[:, 0, 0:N_EXPERTS].astype(jnp.int32)
    tile_first = jnp.cumsum(tile_counts, axis=0) - tile_counts
    counts = jnp.sum(tile_counts, axis=0)
    padded = ((counts + blk - 1) // blk) * blk
    pad_end = jnp.cumsum(padded)
    pad_start = pad_end - padded
    table = jnp.zeros((n_tiles, SUBLANES, LANES), F32).at[:, 0, 0:N_EXPERTS].set(
        (pad_start[None, :] + tile_first).astype(F32)).reshape(n_tiles * SUBLANES, LANES)
    per_step = min(PLAN_TILES, n_tiles)
    assert n_tiles % per_step == 0, (n_tiles, per_step)
    dest_rows = pl.pallas_call(
        _plan_kernel,
        grid=(n_tiles // per_step,),
        in_specs=[pl.BlockSpec((per_step * rows, LANES), lambda i: (i, 0)),
                  pl.BlockSpec((per_step * SUBLANES, LANES), lambda i: (i, 0))],
        out_specs=pl.BlockSpec((SUBLANES, per_step * rows), lambda i: (0, i)),
        out_shape=jax.ShapeDtypeStruct((SUBLANES, N), jnp.int32),
        compiler_params=pltpu.CompilerParams(dimension_semantics=("parallel",)),
        name="plan",
    )(route, table)
    dest_rows = dest_rows[0:TOP_K]
    n_blocks = (TOP_K * N + N_EXPERTS * (blk - 1) + blk - 1) // blk
    block_start = jnp.arange(n_blocks, dtype=jnp.int32) * blk
    block_e = jnp.sum((pad_end[None, :] <= block_start[:, None]).astype(jnp.int32), axis=1)
    block_e = jnp.minimum(block_e, N_EXPERTS - 1)
    n_used = (pad_end[-1] // blk).astype(jnp.int32).reshape(1)
    return dest_rows, block_e, n_used, n_blocks * blk


def _token_mixer(x, norm1, w_in, b_in, conv_w, conv_b, m_fgate_bias, hg_lb_logits):
    B, T, D = x.shape
    q, k, v, mo, gcol, grow, hq, g_f, g_b, hv, hgg = _in_proj(
        x, norm1, w_in, b_in, conv_w, conv_b, m_fgate_bias, hg_lb_logits)
    h_f, o_f, h_b, o_b = _mixer(q, k, v, gcol, grow, hq, g_f, g_b, hv)
    return [a.reshape(B * T, a.shape[-1]) for a in (h_f, h_b, o_f, o_b, mo, hgg, x)]


def kernel(x_prompt, x_sample, norm1, w_in, b_in, conv_w, conv_b, m_fgate_bias, m_norm, hg_lb_logits, hg_norm,
           w_out, norm2, w_router_group, b_router_group, w_router_expert, b_router_expert, w1, w3, w2, norm_f):
    layer = 0
    mixer_args = (norm1[layer], w_in[layer], b_in[layer], conv_w[layer], conv_b[layer], m_fgate_bias[layer],
                  hg_lb_logits)
    outs = []
    for x in (x_prompt, x_sample):
        streams = _token_mixer(x, *mixer_args)
        x1, h2, route, hist = _merge(streams, m_norm[layer], hg_norm[layer], w_out[layer], norm2[layer],
                                     w_router_group[layer], b_router_group[layer], w_router_expert[layer],
                                     b_router_expert[layer])
        dest, block_e, n_used, n_rows = _dispatch_plan(route, hist)
        n = h2.shape[0]
        xs = _sc_scatter_rows(h2, dest.reshape(TOP_K * n), n_rows)
        out_rows = _experts(xs, block_e, n_used, w1[layer], w3[layer], w2[layer])
        y_rows = _sc_gather_rows(out_rows, dest.reshape(TOP_K * n))
        outs.append(_combine(x1, route, y_rows, norm_f, 0).reshape(x.shape))
    return tuple(outs)
```
